```python
import math
import jax
import jax.numpy as jnp
from jax import lax
import numpy as np

D_MODEL = 1024
BATCH = 8
SEQ = 4096
DEPTH = 4

GRID_W = 64
CTX_LEN = 256
EPS = 1e-6

MIX_W = D_MODEL
S5_W = D_MODEL // 4
S5_GROUP = 16
S5_GROUPS = S5_W // S5_GROUP
S5_STATE = 64
MLA_V = 64
MLA_W = D_MODEL // 2
MLA_HEADS = MLA_W // MLA_V
MLA_NOPE = 64
MLA_ROPE = 32
MLA_Q_RANK = 384
MLA_KV_RANK = 256
MLA_SCALE = 1.0 / math.sqrt(MLA_NOPE + MLA_ROPE)
ROPE_BASE = 10000.0
Q_BLOCK = 128
HY_W = D_MODEL // 4
HY_ORDER = 2
HY_POS_EMB = 33
HY_FILTER_W = 64
HY_MIN_DECAY = math.log(1e-2) / 1.5
HY_MAX_DECAY = math.log(1e-2) / 0.3
P_IN = S5_W + MLA_Q_RANK + MLA_KV_RANK + MLA_ROPE + 3 * HY_W
MOE_GROUPS = 4
MOE_PER_GROUP = 8
MOE_EXPERTS = MOE_GROUPS * MOE_PER_GROUP
MOE_TOP_K = 2
MOE_HIDDEN = 512
MOE_BLOCK = 256

kernel_name = 'hybrid_s5_mla_hyena_hmoe_diffusion'


def _rms_norm(x, g):
    xf = x.astype(jnp.float32)
    y = xf * lax.rsqrt(jnp.mean(xf * xf, axis=-1, keepdims=True) + EPS)
    return (y * g.astype(jnp.float32)).astype(x.dtype)


def _modulate(x, g, shift, scale):
    return _rms_norm(x, g) * (1.0 + scale) + shift


def _split_projection(p):
    o1 = S5_W
    o2 = o1 + MLA_Q_RANK
    o3 = o2 + MLA_KV_RANK
    o4 = o3 + MLA_ROPE
    return p[..., :o1], p[..., o1:o2], p[..., o2:o3], p[..., o3:o4], p[..., o4:]


def _s5_discretise(lam_re, lam_im, log_dt, b_re, b_im):
    lam = lax.complex(lam_re.astype(jnp.float32), lam_im.astype(jnp.float32))
    dt = jnp.exp(log_dt.astype(jnp.float32))[..., None]
    lam_bar = jnp.exp(lam * dt)
    b = lax.complex(b_re.astype(jnp.float32), b_im.astype(jnp.float32))
    b_bar = ((lam_bar - 1.0) / lam)[..., None] * b
    return lam_bar, b_bar


def _diag_scan(lam_bar, bu, reverse):
    a = jnp.broadcast_to(lam_bar, bu.shape)

    def combine(e1, e2):
        a1, b1 = e1
        a2, b2 = e2
        return a1 * a2, a2 * b1 + b2

    return lax.associative_scan(combine, (a, bu), reverse=reverse, axis=1)[1]


def _s5_mixer(u_l, u_c, lam_re, lam_im, log_dt, b_re, b_im, c_re, c_im, d, glu_w, ctx_out):
    lam_bar, b_bar = _s5_discretise(lam_re, lam_im, log_dt, b_re, b_im)
    c = lax.complex(c_re.astype(jnp.float32), c_im.astype(jnp.float32))

    def drive(u):
        ug = u.astype(jnp.float32).reshape(u.shape[0], u.shape[1], S5_GROUPS, S5_GROUP).astype(jnp.complex64)
        return [jnp.einsum('blgh,gph->blgp', ug, b_bar[k]) for k in range(2)]

    bu_c = drive(u_c)
    bu_l = drive(u_l)
    hc_f = _diag_scan(lam_bar[0], bu_c[0], False)
    hc_b = _diag_scan(lam_bar[1], bu_c[1], True)
    hl_f = _diag_scan(lam_bar[0], bu_l[0].at[:, 0].add(lam_bar[0] * hc_f[:, -1]), False)
    hl_b = _diag_scan(lam_bar[1], bu_l[1].at[:, -1].add(lam_bar[1] * hc_b[:, 0]), True)

    def readout(u, h_f, h_b):
        y = jnp.real(jnp.einsum('blgp,ghp->blgh', h_f, c[0]) + jnp.einsum('blgp,ghp->blgh', h_b, c[1]))
        y = y.reshape(u.shape) + d.astype(jnp.float32) * u.astype(jnp.float32)
        y = jax.nn.gelu(y)
        return (y * jax.nn.sigmoid(y @ glu_w.astype(jnp.float32))).astype(u.dtype)

    y_l = readout(u_l, hl_f, hl_b)
    y_c = readout(u_c, hc_f, hc_b) if ctx_out else None
    return y_l, y_c


def _axial_rope_tables(n_tokens):
    rows = n_tokens // GRID_W
    row = jnp.repeat(jnp.arange(rows), GRID_W).astype(jnp.float32)
    col = jnp.tile(jnp.arange(GRID_W), rows).astype(jnp.float32)
    half = MLA_ROPE // 2
    inv = ROPE_BASE ** (-jnp.arange(0, half, 2, dtype=jnp.float32) / half)
    ang_r = row[:, None] * inv
    ang_c = col[:, None] * inv
    return jnp.cos(ang_r), jnp.sin(ang_r), jnp.cos(ang_c), jnp.sin(ang_c)


def _rotate(x, cos, sin):
    m = x.shape[-1] // 2
    x1, x2 = x[..., :m], x[..., m:]
    return jnp.concatenate([x1 * cos - x2 * sin, x1 * sin + x2 * cos], axis=-1)


def _axial_rope(x, tables):
    cr, sr, cc, sc = (t[:, None, :].astype(x.dtype) for t in tables)
    half = MLA_ROPE // 2
    return jnp.concatenate([_rotate(x[..., :half], cr, sr), _rotate(x[..., half:], cc, sc)], axis=-1)


def _mla_attend(q_nope, q_rope, k_nope, k_rope, v):
    s = jnp.einsum('bqhd,bkhd->bhqk', q_nope, k_nope) + jnp.einsum('bqhr,bkr->bhqk', q_rope, k_rope)
    p = jax.nn.softmax(s.astype(jnp.float32) * MLA_SCALE, axis=-1).astype(v.dtype)
    return jnp.einsum('bhqk,bkhd->bqhd', p, v)


def _mla_mixer(cq_l, ckv_l, kr_l, cq_c, ckv_c, kr_c, q_norm_g, kv_norm_g, w_uq, w_ukv, ctx_out):
    def queries(cq):
        q = (_rms_norm(cq, q_norm_g) @ w_uq).reshape(cq.shape[0], cq.shape[1], MLA_HEADS, MLA_NOPE + MLA_ROPE)
        return q[..., :MLA_NOPE], q[..., MLA_NOPE:]

    def keys_values(ckv):
        kv = (_rms_norm(ckv, kv_norm_g) @ w_ukv).reshape(ckv.shape[0], ckv.shape[1], MLA_HEADS, MLA_NOPE + MLA_V)
        return kv[..., :MLA_NOPE], kv[..., MLA_NOPE:]

    bsz, n = cq_l.shape[0], cq_l.shape[1]
    tables = _axial_rope_tables(n)
    kn_c, v_c = keys_values(ckv_c)
    kn_l, v_l = keys_values(ckv_l)
    kr_lat = _axial_rope(kr_l[:, :, None, :], tables)[:, :, 0]
    qn_l, qr_l = queries(cq_l)
    qr_l = _axial_rope(qr_l, tables)
    k_nope = jnp.concatenate([kn_c, kn_l], axis=1)
    k_rope = jnp.concatenate([kr_c, kr_lat], axis=1)
    v = jnp.concatenate([v_c, v_l], axis=1)
    nblk = n // Q_BLOCK

    def to_blocks(q):
        return q.reshape(bsz, nblk, Q_BLOCK, q.shape[2], q.shape[3]).swapaxes(0, 1)

    o = lax.map(lambda qb: _mla_attend(qb[0], qb[1], k_nope, k_rope, v), (to_blocks(qn_l), to_blocks(qr_l)))
    y_l = o.swapaxes(0, 1).reshape(bsz, n, MLA_W)
    y_c = None
    if ctx_out:
        qn_c, qr_c = queries(cq_c)
        y_c = _mla_attend(qn_c, qr_c, kn_c, kr_c, v_c).reshape(bsz, kr_c.shape[1], MLA_W)
    return y_l, y_c


def _short_conv(z, w, b):
    n = z.shape[1]
    zp = jnp.pad(z, ((0, 0), (1, 1), (0, 0)))
    return zp[:, :n] * w[0] + zp[:, 1:n + 1] * w[1] + zp[:, 2:] * w[2] + b


def _hyena_filter_spectrum(n, w1, b1, w2, b2, w3, b3, freq):
    t = jnp.linspace(0.0, 1.0, n, dtype=jnp.float32)[:, None]
    bands = (HY_POS_EMB - 1) // 2
    f = jnp.linspace(1e-4, bands - 1, bands, dtype=jnp.float32)
    ang = (2.0 * math.pi * jnp.arange(n, dtype=jnp.float32) / n)[:, None] * f
    z = jnp.concatenate([t, jnp.cos(ang), -jnp.sin(ang)], axis=-1)
    h = jnp.sin(freq[0] * (z @ w1 + b1))
    h = jnp.sin(freq[1] * (h @ w2 + b2))
    h = (h @ w3 + b3).astype(jnp.float32).reshape(n, 2, HY_ORDER, HY_W)
    deltas = jnp.abs(jnp.linspace(HY_MIN_DECAY, HY_MAX_DECAY, HY_W, dtype=jnp.float32))
    h = h * jnp.exp(-t[:, :, None, None] * deltas)
    fwd, bwd = h[:, 0], h[:, 1]
    k = jnp.concatenate([fwd, jnp.zeros((1, HY_ORDER, HY_W), jnp.float32), bwd[:0:-1]], axis=0)
    return jnp.fft.rfft(k, axis=0)


def _fft_long_conv(y, k_f):
    n = y.shape[1]
    yf = jnp.fft.rfft(y, n=2 * n, axis=1)
    return jnp.fft.irfft(yf * k_f, n=2 * n, axis=1)[:, :n]


def _hyena_mixer(z, short_w, short_b, filt, bias):
    n = z.shape[1]
    z = _short_conv(z, short_w, short_b)
    v, x1, x2 = jnp.split(z, 3, axis=-1)
    k_f = _hyena_filter_spectrum(n, *filt)
    y = v.astype(jnp.float32)
    for o, gate in enumerate((x1, x2)):
        y = gate.astype(jnp.float32) * (_fft_long_conv(y, k_f[:, o]) + y * bias[o].astype(jnp.float32))
    return y.astype(z.dtype)


def _merge_groups(y_s5, y_mla, y_hy, g):
    return jnp.concatenate([
        _rms_norm(y_s5, g[:S5_W]),
        _rms_norm(y_mla, g[S5_W:S5_W + MLA_W]),
        _rms_norm(y_hy, g[S5_W + MLA_W:]),
    ], axis=-1)


def _hier_moe(h, w_group, w_expert, w1, w3, w2):
    t, d = h.shape
    g_prob = jax.nn.softmax((h @ w_group).astype(jnp.float32), axis=-1)
    g_w, g_idx = lax.top_k(g_prob, 1)
    e_logits = (h @ w_expert).astype(jnp.float32).reshape(t, MOE_GROUPS, MOE_PER_GROUP)
    in_group = jnp.take_along_axis(e_logits, g_idx[:, :, None], axis=1)[:, 0]
    top_v, top_i = lax.top_k(in_group, MOE_TOP_K)
    gate = jax.nn.softmax(top_v, axis=-1) * g_w
    eid = g_idx * MOE_PER_GROUP + top_i
    n_assign = t * MOE_TOP_K
    flat_e = eid.reshape(n_assign)
    order = jnp.argsort(flat_e)
    se = flat_e[order]
    st = (order // MOE_TOP_K).astype(jnp.int32)
    sw = gate.reshape(n_assign)[order]
    counts = jnp.bincount(flat_e, length=MOE_EXPERTS)
    padded = (counts + MOE_BLOCK - 1) // MOE_BLOCK * MOE_BLOCK
    start = jnp.cumsum(counts) - counts
    pend = jnp.cumsum(padded)
    pstart = pend - padded
    dest = pstart[se] + jnp.arange(n_assign) - start[se]
    n_blocks = -(-n_assign // MOE_BLOCK) + MOE_EXPERTS
    n_rows = n_blocks * MOE_BLOCK
    row_tok = jnp.zeros((n_rows,), jnp.int32).at[dest].set(st)
    row_w = jnp.zeros((n_rows,), jnp.float32).at[dest].set(sw)
    block_e = jnp.minimum(jnp.searchsorted(pend, jnp.arange(n_blocks) * MOE_BLOCK, side='right'), MOE_EXPERTS - 1)

    def expert_block(args):
        xb, e = args
        return (jax.nn.silu(xb @ w1[e]) * (xb @ w3[e])) @ w2[e]

    ys = lax.map(expert_block, (h[row_tok].reshape(n_blocks, MOE_BLOCK, d), block_e))
    ys = ys.reshape(n_rows, d) * row_w[:, None].astype(h.dtype)
    return jnp.zeros_like(h).at[row_tok].add(ys)


def setup_inputs(seed: int = 0) -> dict:
    key = jax.random.key(seed)
    ks = iter(jax.random.split(key, 48))

    def nrm(shape, scale):
        return scale * jax.random.normal(next(ks), shape, jnp.float32)

    D = D_MODEL
    G, P, H = S5_GROUPS, S5_STATE, S5_GROUP
    inp = {}
    inp['x'] = nrm((BATCH, SEQ, D), 1.0)
    inp['c'] = nrm((BATCH, D), 1.0)
    inp['ctx'] = nrm((BATCH, CTX_LEN, D), 1.0)
    inp['c_ctx'] = nrm((D,), 1.0)
    inp['ada_w'] = nrm((DEPTH, D, 6 * D), 0.5 * D ** -0.5)
    inp['ada_b'] = nrm((DEPTH, 6 * D), 0.02)
    inp['norm1_g'] = 1.0 + nrm((DEPTH, D), 0.02)
    inp['norm2_g'] = 1.0 + nrm((DEPTH, D), 0.02)
    inp['w_in'] = nrm((DEPTH, D, P_IN), D ** -0.5)
    inp['s5_lambda_re'] = -0.5 + nrm((DEPTH, 2, G, P), 0.01)
    inp['s5_lambda_im'] = math.pi * jnp.arange(P, dtype=jnp.float32) + nrm((DEPTH, 2, G, P), 0.01)
    inp['s5_log_dt'] = jax.random.uniform(next(ks), (DEPTH, 2, G), jnp.float32, math.log(1e-3), math.log(1e-1))
    inp['s5_b_re'] = nrm((DEPTH, 2, G, P, H), (2 * H) ** -0.5)
    inp['s5_b_im'] = nrm((DEPTH, 2, G, P, H), (2 * H) ** -0.5)
    inp['s5_c_re'] = nrm((DEPTH, 2, G, H, P), (2 * P) ** -0.5)
    inp['s5_c_im'] = nrm((DEPTH, 2, G, H, P), (2 * P) ** -0.5)
    inp['s5_d'] = nrm((DEPTH, S5_W), 1.0)
    inp['s5_glu_w'] = nrm((DEPTH, S5_W, S5_W), S5_W ** -0.5)
    inp['mla_q_norm_g'] = 1.0 + nrm((DEPTH, MLA_Q_RANK), 0.02)
    inp['mla_kv_norm_g'] = 1.0 + nrm((DEPTH, MLA_KV_RANK), 0.02)
    inp['mla_w_uq'] = nrm((DEPTH, MLA_Q_RANK, MLA_HEADS * (MLA_NOPE + MLA_ROPE)), MLA_Q_RANK ** -0.5)
    inp['mla_w_ukv'] = nrm((DEPTH, MLA_KV_RANK, MLA_HEADS * (MLA_NOPE + MLA_V)), MLA_KV_RANK ** -0.5)
    inp['hy_short_w'] = nrm((DEPTH, 3, 3 * HY_W), 3 ** -0.5)
    inp['hy_short_b'] = nrm((DEPTH, 3 * HY_W), 0.02)
    inp['hy_f_w1'] = nrm((DEPTH, HY_POS_EMB, HY_FILTER_W), HY_POS_EMB ** -0.5)
    inp['hy_f_b1'] = nrm((DEPTH, HY_FILTER_W), 0.02)
    inp['hy_f_w2'] = nrm((DEPTH, HY_FILTER_W, HY_FILTER_W), HY_FILTER_W ** -0.5)
    inp['hy_f_b2'] = nrm((DEPTH, HY_FILTER_W), 0.02)
    inp['hy_f_w3'] = nrm((DEPTH, HY_FILTER_W, 2 * HY_ORDER * HY_W), 0.1 * HY_FILTER_W ** -0.5)
    inp['hy_f_b3'] = nrm((DEPTH, 2 * HY_ORDER * HY_W), 0.002)
    inp['hy_f_freq'] = 1.0 + nrm((DEPTH, 2, HY_FILTER_W), 0.02)
    inp['hy_bias'] = nrm((DEPTH, HY_ORDER, HY_W), 1.0)
    inp['mix_norm_g'] = 1.0 + nrm((DEPTH, MIX_W), 0.02)
    inp['w_out'] = nrm((DEPTH, MIX_W, D), MIX_W ** -0.5)
    inp['moe_w_group'] = nrm((DEPTH, D, MOE_GROUPS), D ** -0.5)
    inp['moe_w_expert'] = nrm((DEPTH, D, MOE_EXPERTS), D ** -0.5)
    inp['moe_w1'] = nrm((DEPTH, MOE_EXPERTS, D, MOE_HIDDEN), D ** -0.5)
    inp['moe_w3'] = nrm((DEPTH, MOE_EXPERTS, D, MOE_HIDDEN), D ** -0.5)
    inp['moe_w2'] = nrm((DEPTH, MOE_EXPERTS, MOE_HIDDEN, D), MOE_HIDDEN ** -0.5)
    inp['final_g'] = 1.0 + nrm((D,), 0.02)
    return inp


def reference(x, c, ctx, c_ctx, ada_w, ada_b, norm1_g, norm2_g, w_in,
              s5_lambda_re, s5_lambda_im, s5_log_dt, s5_b_re, s5_b_im, s5_c_re, s5_c_im, s5_d, s5_glu_w,
              mla_q_norm_g, mla_kv_norm_g, mla_w_uq, mla_w_ukv,
              hy_short_w, hy_short_b, hy_f_w1, hy_f_b1, hy_f_w2, hy_f_b2, hy_f_w3, hy_f_b3, hy_f_freq, hy_bias,
              mix_norm_g, w_out, moe_w_group, moe_w_expert, moe_w1, moe_w3, moe_w2, final_g):
    bsz, n, d = x.shape
    n_ctx = ctx.shape[1]
    xl, xc = x, ctx
    act_l = jax.nn.silu(c)
    act_c = jax.nn.silu(c_ctx)
    for i in range(DEPTH):
        ctx_out = i < DEPTH - 1
        mod_l = jnp.split((act_l @ ada_w[i] + ada_b[i])[:, None, :], 6, axis=-1)
        mod_c = jnp.split((act_c @ ada_w[i] + ada_b[i])[None, None, :], 6, axis=-1)
        hl = _modulate(xl, norm1_g[i], mod_l[0], mod_l[1])
        hc = _modulate(xc, norm1_g[i], mod_c[0], mod_c[1])
        u_l, cq_l, ckv_l, kr_l, hz_l = _split_projection(hl @ w_in[i])
        u_c, cq_c, ckv_c, kr_c, hz_c = _split_projection(hc @ w_in[i])
        s5_l, s5_c = _s5_mixer(u_l, u_c, s5_lambda_re[i], s5_lambda_im[i], s5_log_dt[i], s5_b_re[i], s5_b_im[i],
                               s5_c_re[i], s5_c_im[i], s5_d[i], s5_glu_w[i], ctx_out)
        mla_l, mla_c = _mla_mixer(cq_l, ckv_l, kr_l, cq_c, ckv_c, kr_c, mla_q_norm_g[i], mla_kv_norm_g[i],
                                  mla_w_uq[i], mla_w_ukv[i], ctx_out)
        filt = (hy_f_w1[i], hy_f_b1[i], hy_f_w2[i], hy_f_b2[i], hy_f_w3[i], hy_f_b3[i], hy_f_freq[i])
        hy_l = _hyena_mixer(hz_l, hy_short_w[i], hy_short_b[i], filt, hy_bias[i])
        xl = xl + mod_l[2] * (_merge_groups(s5_l, mla_l, hy_l, mix_norm_g[i]) @ w_out[i])
        fl = _modulate(xl, norm2_g[i], mod_l[3], mod_l[4]).reshape(bsz * n, d)
        if ctx_out:
            hy_c = _hyena_mixer(hz_c, hy_short_w[i], hy_short_b[i], filt, hy_bias[i])
            xc = xc + mod_c[2] * (_merge_groups(s5_c, mla_c, hy_c, mix_norm_g[i]) @ w_out[i])
            fc = _modulate(xc, norm2_g[i], mod_c[3], mod_c[4]).reshape(bsz * n_ctx, d)
            y = _hier_moe(jnp.concatenate([fl, fc], axis=0), moe_w_group[i], moe_w_expert[i],
                          moe_w1[i], moe_w3[i], moe_w2[i])
            xl = xl + mod_l[5] * y[:bsz * n].reshape(bsz, n, d)
            xc = xc + mod_c[5] * y[bsz * n:].reshape(bsz, n_ctx, d)
        else:
            y = _hier_moe(fl, moe_w_group[i], moe_w_expert[i], moe_w1[i], moe_w3[i], moe_w2[i])
            xl = xl + mod_l[5] * y.reshape(bsz, n, d)
    return _rms_norm(xl, final_g)
```

```python
import functools
import math

import jax
import jax.numpy as jnp
from jax import lax
from jax.experimental import pallas as pl
from jax.experimental.pallas import tpu as pltpu

D_MODEL = 1024
BATCH = 8
SEQ = 4096
DEPTH = 4

GRID_W = 64
CTX_LEN = 256
EPS = 1e-6

MIX_W = D_MODEL
S5_W = D_MODEL // 4
S5_GROUP = 16
S5_GROUPS = S5_W // S5_GROUP
S5_STATE = 64
MLA_V = 64
MLA_W = D_MODEL // 2
MLA_HEADS = MLA_W // MLA_V
MLA_NOPE = 64
MLA_ROPE = 32
MLA_Q_RANK = 384
MLA_KV_RANK = 256
MLA_SCALE = 1.0 / math.sqrt(MLA_NOPE + MLA_ROPE)
ROPE_BASE = 10000.0
Q_BLOCK = 128
HY_W = D_MODEL // 4
HY_ORDER = 2
HY_POS_EMB = 33
HY_FILTER_W = 64
HY_MIN_DECAY = math.log(1e-2) / 1.5
HY_MAX_DECAY = math.log(1e-2) / 0.3
P_IN = S5_W + MLA_Q_RANK + MLA_KV_RANK + MLA_ROPE + 3 * HY_W
MOE_GROUPS = 4
MOE_PER_GROUP = 8
MOE_EXPERTS = MOE_GROUPS * MOE_PER_GROUP
MOE_TOP_K = 2
MOE_HIDDEN = 512
MOE_BLOCK = 256


def _rms_norm(x, g):
    xf = x.astype(jnp.float32)
    y = xf * lax.rsqrt(jnp.mean(xf * xf, axis=-1, keepdims=True) + EPS)
    return (y * g.astype(jnp.float32)).astype(x.dtype)


def _modulate(x, g, shift, scale):
    return _rms_norm(x, g) * (1.0 + scale) + shift


def _split_projection(p):
    o1 = S5_W
    o2 = o1 + MLA_Q_RANK
    o3 = o2 + MLA_KV_RANK
    o4 = o3 + MLA_ROPE
    return p[..., :o1], p[..., o1:o2], p[..., o2:o3], p[..., o3:o4], p[..., o4:]


def _s5_discretise(lam_re, lam_im, log_dt, b_re, b_im):
    lam = lax.complex(lam_re.astype(jnp.float32), lam_im.astype(jnp.float32))
    dt = jnp.exp(log_dt.astype(jnp.float32))[..., None]
    lam_bar = jnp.exp(lam * dt)
    b = lax.complex(b_re.astype(jnp.float32), b_im.astype(jnp.float32))
    b_bar = ((lam_bar - 1.0) / lam)[..., None] * b
    return lam_bar, b_bar


def _diag_scan(lam_bar, bu, reverse):
    a = jnp.broadcast_to(lam_bar, bu.shape)

    def combine(e1, e2):
        a1, b1 = e1
        a2, b2 = e2
        return a1 * a2, a2 * b1 + b2

    return lax.associative_scan(combine, (a, bu), reverse=reverse, axis=1)[1]


def _s5_mixer(u_l, u_c, lam_re, lam_im, log_dt, b_re, b_im, c_re, c_im, d, glu_w, ctx_out):
    lam_bar, b_bar = _s5_discretise(lam_re, lam_im, log_dt, b_re, b_im)
    c = lax.complex(c_re.astype(jnp.float32), c_im.astype(jnp.float32))

    def drive(u):
        ug = u.astype(jnp.float32).reshape(u.shape[0], u.shape[1], S5_GROUPS, S5_GROUP).astype(jnp.complex64)
        return [jnp.einsum('blgh,gph->blgp', ug, b_bar[k]) for k in range(2)]

    bu_c = drive(u_c)
    bu_l = drive(u_l)
    hc_f = _diag_scan(lam_bar[0], bu_c[0], False)
    hc_b = _diag_scan(lam_bar[1], bu_c[1], True)
    hl_f = _diag_scan(lam_bar[0], bu_l[0].at[:, 0].add(lam_bar[0] * hc_f[:, -1]), False)
    hl_b = _diag_scan(lam_bar[1], bu_l[1].at[:, -1].add(lam_bar[1] * hc_b[:, 0]), True)

    def readout(u, h_f, h_b):
        y = jnp.real(jnp.einsum('blgp,ghp->blgh', h_f, c[0]) + jnp.einsum('blgp,ghp->blgh', h_b, c[1]))
        y = y.reshape(u.shape) + d.astype(jnp.float32) * u.astype(jnp.float32)
        y = jax.nn.gelu(y)
        return (y * jax.nn.sigmoid(y @ glu_w.astype(jnp.float32))).astype(u.dtype)

    y_l = readout(u_l, hl_f, hl_b)
    y_c = readout(u_c, hc_f, hc_b) if ctx_out else None
    return y_l, y_c


def _axial_rope_tables(n_tokens):
    rows = n_tokens // GRID_W
    row = jnp.repeat(jnp.arange(rows), GRID_W).astype(jnp.float32)
    col = jnp.tile(jnp.arange(GRID_W), rows).astype(jnp.float32)
    half = MLA_ROPE // 2
    inv = ROPE_BASE ** (-jnp.arange(0, half, 2, dtype=jnp.float32) / half)
    ang_r = row[:, None] * inv
    ang_c = col[:, None] * inv
    return jnp.cos(ang_r), jnp.sin(ang_r), jnp.cos(ang_c), jnp.sin(ang_c)


def _rotate(x, cos, sin):
    m = x.shape[-1] // 2
    x1, x2 = x[..., :m], x[..., m:]
    return jnp.concatenate([x1 * cos - x2 * sin, x1 * sin + x2 * cos], axis=-1)


def _axial_rope(x, tables):
    cr, sr, cc, sc = (t[:, None, :].astype(x.dtype) for t in tables)
    half = MLA_ROPE // 2
    return jnp.concatenate([_rotate(x[..., :half], cr, sr), _rotate(x[..., half:], cc, sc)], axis=-1)


def _mla_attend(q_nope, q_rope, k_nope, k_rope, v):
    s = jnp.einsum('bqhd,bkhd->bhqk', q_nope, k_nope) + jnp.einsum('bqhr,bkr->bhqk', q_rope, k_rope)
    p = jax.nn.softmax(s.astype(jnp.float32) * MLA_SCALE, axis=-1).astype(v.dtype)
    return jnp.einsum('bhqk,bkhd->bqhd', p, v)


def _mla_mixer(cq_l, ckv_l, kr_l, cq_c, ckv_c, kr_c, q_norm_g, kv_norm_g, w_uq, w_ukv, ctx_out):
    def queries(cq):
        q = (_rms_norm(cq, q_norm_g) @ w_uq).reshape(cq.shape[0], cq.shape[1], MLA_HEADS, MLA_NOPE + MLA_ROPE)
        return q[..., :MLA_NOPE], q[..., MLA_NOPE:]

    def keys_values(ckv):
        kv = (_rms_norm(ckv, kv_norm_g) @ w_ukv).reshape(ckv.shape[0], ckv.shape[1], MLA_HEADS, MLA_NOPE + MLA_V)
        return kv[..., :MLA_NOPE], kv[..., MLA_NOPE:]

    bsz, n = cq_l.shape[0], cq_l.shape[1]
    tables = _axial_rope_tables(n)
    kn_c, v_c = keys_values(ckv_c)
    kn_l, v_l = keys_values(ckv_l)
    kr_lat = _axial_rope(kr_l[:, :, None, :], tables)[:, :, 0]
    qn_l, qr_l = queries(cq_l)
    qr_l = _axial_rope(qr_l, tables)
    k_nope = jnp.concatenate([kn_c, kn_l], axis=1)
    k_rope = jnp.concatenate([kr_c, kr_lat], axis=1)
    v = jnp.concatenate([v_c, v_l], axis=1)
    nblk = n // Q_BLOCK

    def to_blocks(q):
        return q.reshape(bsz, nblk, Q_BLOCK, q.shape[2], q.shape[3]).swapaxes(0, 1)

    o = lax.map(lambda qb: _mla_attend(qb[0], qb[1], k_nope, k_rope, v), (to_blocks(qn_l), to_blocks(qr_l)))
    y_l = o.swapaxes(0, 1).reshape(bsz, n, MLA_W)
    y_c = None
    if ctx_out:
        qn_c, qr_c = queries(cq_c)
        y_c = _mla_attend(qn_c, qr_c, kn_c, kr_c, v_c).reshape(bsz, kr_c.shape[1], MLA_W)
    return y_l, y_c


def _short_conv(z, w, b):
    n = z.shape[1]
    zp = jnp.pad(z, ((0, 0), (1, 1), (0, 0)))
    return zp[:, :n] * w[0] + zp[:, 1:n + 1] * w[1] + zp[:, 2:] * w[2] + b


def _hyena_filter_spectrum(n, w1, b1, w2, b2, w3, b3, freq):
    t = jnp.linspace(0.0, 1.0, n, dtype=jnp.float32)[:, None]
    bands = (HY_POS_EMB - 1) // 2
    f = jnp.linspace(1e-4, bands - 1, bands, dtype=jnp.float32)
    ang = (2.0 * math.pi * jnp.arange(n, dtype=jnp.float32) / n)[:, None] * f
    z = jnp.concatenate([t, jnp.cos(ang), -jnp.sin(ang)], axis=-1)
    h = jnp.sin(freq[0] * (z @ w1 + b1))
    h = jnp.sin(freq[1] * (h @ w2 + b2))
    h = (h @ w3 + b3).astype(jnp.float32).reshape(n, 2, HY_ORDER, HY_W)
    deltas = jnp.abs(jnp.linspace(HY_MIN_DECAY, HY_MAX_DECAY, HY_W, dtype=jnp.float32))
    h = h * jnp.exp(-t[:, :, None, None] * deltas)
    fwd, bwd = h[:, 0], h[:, 1]
    k = jnp.concatenate([fwd, jnp.zeros((1, HY_ORDER, HY_W), jnp.float32), bwd[:0:-1]], axis=0)
    return jnp.fft.rfft(k, axis=0)


def _fft_long_conv(y, k_f):
    n = y.shape[1]
    yf = jnp.fft.rfft(y, n=2 * n, axis=1)
    return jnp.fft.irfft(yf * k_f, n=2 * n, axis=1)[:, :n]


def _hyena_mixer(z, short_w, short_b, filt, bias):
    n = z.shape[1]
    z = _short_conv(z, short_w, short_b)
    v, x1, x2 = jnp.split(z, 3, axis=-1)
    k_f = _hyena_filter_spectrum(n, *filt)
    y = v.astype(jnp.float32)
    for o, gate in enumerate((x1, x2)):
        y = gate.astype(jnp.float32) * (_fft_long_conv(y, k_f[:, o]) + y * bias[o].astype(jnp.float32))
    return y.astype(z.dtype)


def _merge_groups(y_s5, y_mla, y_hy, g):
    return jnp.concatenate([
        _rms_norm(y_s5, g[:S5_W]),
        _rms_norm(y_mla, g[S5_W:S5_W + MLA_W]),
        _rms_norm(y_hy, g[S5_W + MLA_W:]),
    ], axis=-1)


def _hier_moe(h, w_group, w_expert, w1, w3, w2):
    t, d = h.shape
    g_prob = jax.nn.softmax((h @ w_group).astype(jnp.float32), axis=-1)
    g_w, g_idx = lax.top_k(g_prob, 1)
    e_logits = (h @ w_expert).astype(jnp.float32).reshape(t, MOE_GROUPS, MOE_PER_GROUP)
    in_group = jnp.take_along_axis(e_logits, g_idx[:, :, None], axis=1)[:, 0]
    top_v, top_i = lax.top_k(in_group, MOE_TOP_K)
    gate = jax.nn.softmax(top_v, axis=-1) * g_w
    eid = g_idx * MOE_PER_GROUP + top_i
    n_assign = t * MOE_TOP_K
    flat_e = eid.reshape(n_assign)
    order = jnp.argsort(flat_e)
    se = flat_e[order]
    st = (order // MOE_TOP_K).astype(jnp.int32)
    sw = gate.reshape(n_assign)[order]
    counts = jnp.bincount(flat_e, length=MOE_EXPERTS)
    padded = (counts + MOE_BLOCK - 1) // MOE_BLOCK * MOE_BLOCK
    start = jnp.cumsum(counts) - counts
    pend = jnp.cumsum(padded)
    pstart = pend - padded
    dest = pstart[se] + jnp.arange(n_assign) - start[se]
    n_blocks = -(-n_assign // MOE_BLOCK) + MOE_EXPERTS
    n_rows = n_blocks * MOE_BLOCK
    row_tok = jnp.zeros((n_rows,), jnp.int32).at[dest].set(st)
    row_w = jnp.zeros((n_rows,), jnp.float32).at[dest].set(sw)
    block_e = jnp.minimum(jnp.searchsorted(pend, jnp.arange(n_blocks) * MOE_BLOCK, side='right'), MOE_EXPERTS - 1)

    def expert_block(args):
        xb, e = args
        return (jax.nn.silu(xb @ w1[e]) * (xb @ w3[e])) @ w2[e]

    ys = lax.map(expert_block, (h[row_tok].reshape(n_blocks, MOE_BLOCK, d), block_e))
    ys = ys.reshape(n_rows, d) * row_w[:, None].astype(h.dtype)
    return jnp.zeros_like(h).at[row_tok].add(ys)


def _final_norm_kernel(x_ref, g_ref, o_ref):
    x = x_ref[...]
    y = x * lax.rsqrt(jnp.mean(x * x, axis=-1, keepdims=True) + EPS)
    o_ref[...] = y * g_ref[...]


def _final_norm(x, g):
    t, d = x.shape
    blk = 1024
    return pl.pallas_call(
        _final_norm_kernel,
        grid=(t // blk,),
        in_specs=[pl.BlockSpec((blk, d), lambda i: (i, 0)), pl.BlockSpec((1, d), lambda i: (0, 0))],
        out_specs=pl.BlockSpec((blk, d), lambda i: (i, 0)),
        out_shape=jax.ShapeDtypeStruct((t, d), jnp.float32),
    )(x, g.reshape(1, d))


def kernel(x, c, ctx, c_ctx, ada_w, ada_b, norm1_g, norm2_g, w_in,
           s5_lambda_re, s5_lambda_im, s5_log_dt, s5_b_re, s5_b_im, s5_c_re, s5_c_im, s5_d, s5_glu_w,
           mla_q_norm_g, mla_kv_norm_g, mla_w_uq, mla_w_ukv,
           hy_short_w, hy_short_b, hy_f_w1, hy_f_b1, hy_f_w2, hy_f_b2, hy_f_w3, hy_f_b3, hy_f_freq, hy_bias,
           mix_norm_g, w_out, moe_w_group, moe_w_expert, moe_w1, moe_w3, moe_w2, final_g):
    bsz, n, d = x.shape
    n_ctx = ctx.shape[1]
    xl, xc = x, ctx
    act_l = jax.nn.silu(c)
    act_c = jax.nn.silu(c_ctx)
    for i in range(DEPTH):
        ctx_out = i < DEPTH - 1
        mod_l = jnp.split((act_l @ ada_w[i] + ada_b[i])[:, None, :], 6, axis=-1)
        mod_c = jnp.split((act_c @ ada_w[i] + ada_b[i])[None, None, :], 6, axis=-1)
        hl = _modulate(xl, norm1_g[i], mod_l[0], mod_l[1])
        hc = _modulate(xc, norm1_g[i], mod_c[0], mod_c[1])
        u_l, cq_l, ckv_l, kr_l, hz_l = _split_projection(hl @ w_in[i])
        u_c, cq_c, ckv_c, kr_c, hz_c = _split_projection(hc @ w_in[i])
        s5_l, s5_c = _s5_mixer(u_l, u_c, s5_lambda_re[i], s5_lambda_im[i], s5_log_dt[i], s5_b_re[i], s5_b_im[i],
                               s5_c_re[i], s5_c_im[i], s5_d[i], s5_glu_w[i], ctx_out)
        mla_l, mla_c = _mla_mixer(cq_l, ckv_l, kr_l, cq_c, ckv_c, kr_c, mla_q_norm_g[i], mla_kv_norm_g[i],
                                  mla_w_uq[i], mla_w_ukv[i], ctx_out)
        filt = (hy_f_w1[i], hy_f_b1[i], hy_f_w2[i], hy_f_b2[i], hy_f_w3[i], hy_f_b3[i], hy_f_freq[i])
        hy_l = _hyena_mixer(hz_l, hy_short_w[i], hy_short_b[i], filt, hy_bias[i])
        xl = xl + mod_l[2] * (_merge_groups(s5_l, mla_l, hy_l, mix_norm_g[i]) @ w_out[i])
        fl = _modulate(xl, norm2_g[i], mod_l[3], mod_l[4]).reshape(bsz * n, d)
        if ctx_out:
            hy_c = _hyena_mixer(hz_c, hy_short_w[i], hy_short_b[i], filt, hy_bias[i])
            xc = xc + mod_c[2] * (_merge_groups(s5_c, mla_c, hy_c, mix_norm_g[i]) @ w_out[i])
            fc = _modulate(xc, norm2_g[i], mod_c[3], mod_c[4]).reshape(bsz * n_ctx, d)
            y = _hier_moe(jnp.concatenate([fl, fc], axis=0), moe_w_group[i], moe_w_expert[i],
                          moe_w1[i], moe_w3[i], moe_w2[i])
            xl = xl + mod_l[5] * y[:bsz * n].reshape(bsz, n, d)
            xc = xc + mod_c[5] * y[bsz * n:].reshape(bsz, n_ctx, d)
        else:
            y = _hier_moe(fl, moe_w_group[i], moe_w_expert[i], moe_w1[i], moe_w3[i], moe_w2[i])
            xl = xl + mod_l[5] * y.reshape(bsz, n, d)
    return _final_norm(xl.reshape(bsz * n, d), final_g).reshape(bsz, n, d)
```

```python
import functools
import math

import jax
import jax.numpy as jnp
from jax import lax
from jax.experimental import pallas as pl
from jax.experimental.pallas import tpu as pltpu

D_MODEL = 1024
DEPTH = 4
GRID_W = 64
EPS = 1e-6

MIX_W = D_MODEL
S5_W = D_MODEL // 4
S5_GROUP = 16
S5_GROUPS = S5_W // S5_GROUP
S5_STATE = 64
S5_N = S5_GROUPS * S5_STATE
MLA_V = 64
MLA_W = D_MODEL // 2
MLA_HEADS = MLA_W // MLA_V
MLA_NOPE = 64
MLA_ROPE = 32
MLA_QK = MLA_NOPE + MLA_ROPE
MLA_Q_RANK = 384
MLA_KV_RANK = 256
MLA_SCALE = 1.0 / math.sqrt(MLA_NOPE + MLA_ROPE)
ROPE_BASE = 10000.0
HY_W = D_MODEL // 4
HY_ORDER = 2
HY_POS_EMB = 33
HY_FILTER_W = 64
HY_MIN_DECAY = math.log(1e-2) / 1.5
HY_MAX_DECAY = math.log(1e-2) / 0.3
MOE_GROUPS = 4
MOE_PER_GROUP = 8
MOE_EXPERTS = MOE_GROUPS * MOE_PER_GROUP
MOE_TOP_K = 2
MOE_HIDDEN = 512
MOE_BLOCK = 256

SUBLANES = 8
S5_CHUNK = 128
MM_ROWS = 256
ATTN_Q_BLOCK = 256
VMEM_LIMIT = 48 * 1024 * 1024

F32 = jnp.float32
BF16 = jnp.bfloat16


def _rms_norm(x, g):
    xf = x.astype(F32)
    y = xf * lax.rsqrt(jnp.mean(xf * xf, axis=-1, keepdims=True) + EPS)
    return (y * g.astype(F32)).astype(x.dtype)


def _modulate(x, g, shift, scale):
    return _rms_norm(x, g) * (1.0 + scale) + shift


def _split_projection(p):
    o1 = S5_W
    o2 = o1 + MLA_Q_RANK
    o3 = o2 + MLA_KV_RANK
    o4 = o3 + MLA_ROPE
    return p[..., :o1], p[..., o1:o2], p[..., o2:o3], p[..., o3:o4], p[..., o4:]


def _s5_matrices(lam_re, lam_im, log_dt, b_re, b_im, c_re, c_im):
    g, p, h = S5_GROUPS, S5_STATE, S5_GROUP
    dt = jnp.exp(log_dt)[..., None]
    mag = jnp.exp(lam_re * dt)
    ar, ai = mag * jnp.cos(lam_im * dt), mag * jnp.sin(lam_im * dt)
    den = lam_re * lam_re + lam_im * lam_im
    fr = ((ar - 1.0) * lam_re + ai * lam_im) / den
    fi = (ai * lam_re - (ar - 1.0) * lam_im) / den
    bbr = fr[..., None] * b_re - fi[..., None] * b_im
    bbi = fr[..., None] * b_im + fi[..., None] * b_re
    eye = jnp.eye(g, dtype=F32)

    def block_in(m):
        return jnp.einsum('kgph,gj->kghjp', m, eye).reshape(2, g * h, g * p)

    def block_out(m):
        return jnp.einsum('kghp,gj->kgpjh', m, eye).reshape(2, g * p, g * h)

    bmat = jnp.concatenate([block_in(bbr), block_in(bbi)], axis=-1)
    cmat = jnp.concatenate([block_out(c_re), -block_out(c_im)], axis=1)
    lam = jnp.stack([ar[0], ai[0], ar[1], ai[1]]).reshape(4, 1, g * p)
    lam = jnp.broadcast_to(lam, (4, SUBLANES, g * p)).reshape(4 * SUBLANES, g * p)
    return bmat.astype(BF16), cmat.astype(BF16), lam


def _s5_kernel(uf_ref, ub_ref, bmat_ref, cmat_ref, lam_ref, yf_ref, yb_ref, buf_f, buf_b, state):
    rows = uf_ref.shape[0]
    steps = rows // SUBLANES
    n = S5_N
    s = SUBLANES

    @pl.when(pl.program_id(0) == 0)
    def _():
        state[...] = jnp.zeros_like(state)

    def drive(r, carry):
        r0 = pl.multiple_of(r * MM_ROWS, MM_ROWS)
        buf_f[pl.ds(r0, MM_ROWS), :] = jnp.dot(uf_ref[pl.ds(r0, MM_ROWS), :].astype(BF16), bmat_ref[0],
                                               preferred_element_type=F32)
        buf_b[pl.ds(r0, MM_ROWS), :] = jnp.dot(ub_ref[pl.ds(r0, MM_ROWS), :].astype(BF16), bmat_ref[1],
                                               preferred_element_type=F32)
        return carry

    lax.fori_loop(0, rows // MM_ROWS, drive, 0)

    def step(j, carry):
        fr, fi, br, bi = carry
        rf = pl.multiple_of(j * s, s)
        rb = pl.multiple_of((steps - 1 - j) * s, s)
        lfr, lfi = lam_ref[0:s, :], lam_ref[s:2 * s, :]
        lbr, lbi = lam_ref[2 * s:3 * s, :], lam_ref[3 * s:4 * s, :]
        nfr = lfr * fr - lfi * fi + buf_f[pl.ds(rf, s), 0:n]
        nfi = lfr * fi + lfi * fr + buf_f[pl.ds(rf, s), n:2 * n]
        nbr = lbr * br - lbi * bi + buf_b[pl.ds(rb, s), 0:n]
        nbi = lbr * bi + lbi * br + buf_b[pl.ds(rb, s), n:2 * n]
        buf_f[pl.ds(rf, s), 0:n] = nfr
        buf_f[pl.ds(rf, s), n:2 * n] = nfi
        buf_b[pl.ds(rb, s), 0:n] = nbr
        buf_b[pl.ds(rb, s), n:2 * n] = nbi
        return nfr, nfi, nbr, nbi

    init = (state[0:s, :], state[s:2 * s, :], state[2 * s:3 * s, :], state[3 * s:4 * s, :])
    fr, fi, br, bi = lax.fori_loop(0, steps, step, init, unroll=2)
    state[0:s, :] = fr
    state[s:2 * s, :] = fi
    state[2 * s:3 * s, :] = br
    state[3 * s:4 * s, :] = bi

    def readout(r, carry):
        r0 = pl.multiple_of(r * MM_ROWS, MM_ROWS)
        yf_ref[pl.ds(r0, MM_ROWS), :] = jnp.dot(buf_f[pl.ds(r0, MM_ROWS), :].astype(BF16), cmat_ref[0],
                                                preferred_element_type=F32)
        yb_ref[pl.ds(r0, MM_ROWS), :] = jnp.dot(buf_b[pl.ds(r0, MM_ROWS), :].astype(BF16), cmat_ref[1],
                                                preferred_element_type=F32)
        return carry

    lax.fori_loop(0, rows // MM_ROWS, readout, 0)


def _s5_scan(u_tm, bmat, cmat, lam, n_ctx_chunks, chunk):
    rows_all, w = u_tm.shape
    rows = chunk * SUBLANES
    n_chunks = rows_all // rows
    n2 = 2 * S5_N

    def bwd_chunk(i):
        return jnp.where(i < n_ctx_chunks, n_ctx_chunks - 1 - i, n_chunks - 1 + n_ctx_chunks - i)

    return pl.pallas_call(
        _s5_kernel,
        grid=(n_chunks,),
        in_specs=[
            pl.BlockSpec((rows, w), lambda i: (i, 0)),
            pl.BlockSpec((rows, w), lambda i: (bwd_chunk(i), 0)),
            pl.BlockSpec((2, w, n2), lambda i: (0, 0, 0)),
            pl.BlockSpec((2, n2, w), lambda i: (0, 0, 0)),
            pl.BlockSpec((4 * SUBLANES, S5_N), lambda i: (0, 0)),
        ],
        out_specs=[
            pl.BlockSpec((rows, w), lambda i: (i, 0)),
            pl.BlockSpec((rows, w), lambda i: (bwd_chunk(i), 0)),
        ],
        out_shape=[jax.ShapeDtypeStruct((rows_all, w), F32)] * 2,
        scratch_shapes=[
            pltpu.VMEM((rows, n2), F32),
            pltpu.VMEM((rows, n2), F32),
            pltpu.VMEM((4 * SUBLANES, S5_N), F32),
        ],
        compiler_params=pltpu.CompilerParams(dimension_semantics=("arbitrary",), vmem_limit_bytes=VMEM_LIMIT),
        name="s5_scan",
    )(u_tm, u_tm, bmat, cmat, lam)


def _s5_mixer(u_l, u_c, lam_re, lam_im, log_dt, b_re, b_im, c_re, c_im, d, glu_w, ctx_out):
    bsz, n, w = u_l.shape
    n_ctx = u_c.shape[1]
    assert bsz == SUBLANES and n % S5_CHUNK == 0 and n_ctx % S5_CHUNK == 0
    bmat, cmat, lam = _s5_matrices(lam_re, lam_im, log_dt, b_re, b_im, c_re, c_im)
    u_all = jnp.concatenate([u_c, u_l], axis=1)
    u_tm = u_all.swapaxes(0, 1).reshape((n_ctx + n) * bsz, w)
    y_f, y_b = _s5_scan(u_tm, bmat, cmat, lam, n_ctx // S5_CHUNK, S5_CHUNK)
    y = (y_f + y_b).reshape(n_ctx + n, bsz, w).swapaxes(0, 1)
    y = y + d * u_all
    y = jax.nn.gelu(y)
    y = y * jax.nn.sigmoid(y @ glu_w)
    return y[:, n_ctx:], (y[:, :n_ctx] if ctx_out else None)


def _axial_rope_tables(n_tokens):
    rows = n_tokens // GRID_W
    row = jnp.repeat(jnp.arange(rows), GRID_W).astype(F32)
    col = jnp.tile(jnp.arange(GRID_W), rows).astype(F32)
    half = MLA_ROPE // 2
    inv = ROPE_BASE ** (-jnp.arange(0, half, 2, dtype=F32) / half)
    ang_r = row[:, None] * inv
    ang_c = col[:, None] * inv
    return jnp.cos(ang_r), jnp.sin(ang_r), jnp.cos(ang_c), jnp.sin(ang_c)


def _rotate(x, cos, sin):
    m = x.shape[-1] // 2
    x1, x2 = x[..., :m], x[..., m:]
    return jnp.concatenate([x1 * cos - x2 * sin, x1 * sin + x2 * cos], axis=-1)


def _axial_rope(x, tables):
    cr, sr, cc, sc = (t[:, None, :].astype(x.dtype) for t in tables)
    half = MLA_ROPE // 2
    return jnp.concatenate([_rotate(x[..., :half], cr, sr), _rotate(x[..., half:], cc, sc)], axis=-1)


def _attn_kernel(q_ref, k_ref, v_ref, o_ref):
    s = lax.dot_general(q_ref[0, 0], k_ref[0, 0], (((1,), (1,)), ((), ())), preferred_element_type=F32)
    m = jnp.max(s, axis=-1, keepdims=True)
    p = jnp.exp(s - m)
    l = jnp.sum(p, axis=-1, keepdims=True)
    o = jnp.dot(p.astype(BF16), v_ref[0, 0], preferred_element_type=F32)
    o_ref[0, 0] = o / l


def _attention(q, k, v):
    bsz, heads, nq, dqk = q.shape
    nk, dv = k.shape[2], v.shape[3]
    tq = min(ATTN_Q_BLOCK, nq)
    return pl.pallas_call(
        _attn_kernel,
        grid=(bsz, heads, nq // tq),
        in_specs=[
            pl.BlockSpec((1, 1, tq, dqk), lambda b, h, i: (b, h, i, 0)),
            pl.BlockSpec((1, 1, nk, dqk), lambda b, h, i: (b, h, 0, 0)),
            pl.BlockSpec((1, 1, nk, dv), lambda b, h, i: (b, h, 0, 0)),
        ],
        out_specs=pl.BlockSpec((1, 1, tq, dv), lambda b, h, i: (b, h, i, 0)),
        out_shape=jax.ShapeDtypeStruct((bsz, heads, nq, dv), F32),
        compiler_params=pltpu.CompilerParams(
            dimension_semantics=("arbitrary", "arbitrary", "arbitrary"), vmem_limit_bytes=VMEM_LIMIT),
        name="mla_attention",
    )(q, k, v)


def _mla_mixer(cq_l, ckv_l, kr_l, cq_c, ckv_c, kr_c, q_norm_g, kv_norm_g, w_uq, w_ukv, ctx_out):
    def queries(cq):
        q = (_rms_norm(cq, q_norm_g) @ w_uq).reshape(cq.shape[0], cq.shape[1], MLA_HEADS, MLA_QK)
        return q[..., :MLA_NOPE], q[..., MLA_NOPE:]

    def keys_values(ckv):
        kv = (_rms_norm(ckv, kv_norm_g) @ w_ukv).reshape(ckv.shape[0], ckv.shape[1], MLA_HEADS, MLA_NOPE + MLA_V)
        return kv[..., :MLA_NOPE], kv[..., MLA_NOPE:]

    def head_major_q(qn, qr):
        q = jnp.concatenate([qn, qr], axis=-1) * MLA_SCALE
        return q.swapaxes(1, 2).astype(BF16)

    def head_major_k(kn, kr):
        kr = jnp.broadcast_to(kr[:, :, None, :], kn.shape[:3] + (MLA_ROPE,))
        return jnp.concatenate([kn, kr], axis=-1).swapaxes(1, 2).astype(BF16)

    bsz, n = cq_l.shape[0], cq_l.shape[1]
    n_ctx = kr_c.shape[1]
    tables = _axial_rope_tables(n)
    kn_c, v_c = keys_values(ckv_c)
    kn_l, v_l = keys_values(ckv_l)
    kr_lat = _axial_rope(kr_l[:, :, None, :], tables)[:, :, 0]
    qn_l, qr_l = queries(cq_l)
    qr_l = _axial_rope(qr_l, tables)
    k_nope = jnp.concatenate([kn_c, kn_l], axis=1)
    k_rope = jnp.concatenate([kr_c, kr_lat], axis=1)
    v = jnp.concatenate([v_c, v_l], axis=1)
    k_hm = head_major_k(k_nope, k_rope)
    v_hm = v.swapaxes(1, 2).astype(BF16)
    o = _attention(head_major_q(qn_l, qr_l), k_hm, v_hm)
    y_l = o.swapaxes(1, 2).reshape(bsz, n, MLA_W)
    y_c = None
    if ctx_out:
        qn_c, qr_c = queries(cq_c)
        o_c = _attention(head_major_q(qn_c, qr_c), k_hm[:, :, :n_ctx], v_hm[:, :, :n_ctx])
        y_c = o_c.swapaxes(1, 2).reshape(bsz, n_ctx, MLA_W)
    return y_l, y_c


def _short_conv(z, w, b):
    n = z.shape[1]
    zp = jnp.pad(z, ((0, 0), (1, 1), (0, 0)))
    return zp[:, :n] * w[0] + zp[:, 1:n + 1] * w[1] + zp[:, 2:] * w[2] + b


def _hyena_filter_spectrum(n, w1, b1, w2, b2, w3, b3, freq):
    t = jnp.linspace(0.0, 1.0, n, dtype=F32)[:, None]
    bands = (HY_POS_EMB - 1) // 2
    f = jnp.linspace(1e-4, bands - 1, bands, dtype=F32)
    ang = (2.0 * math.pi * jnp.arange(n, dtype=F32) / n)[:, None] * f
    z = jnp.concatenate([t, jnp.cos(ang), -jnp.sin(ang)], axis=-1)
    h = jnp.sin(freq[0] * (z @ w1 + b1))
    h = jnp.sin(freq[1] * (h @ w2 + b2))
    h = (h @ w3 + b3).astype(F32).reshape(n, 2, HY_ORDER, HY_W)
    deltas = jnp.abs(jnp.linspace(HY_MIN_DECAY, HY_MAX_DECAY, HY_W, dtype=F32))
    h = h * jnp.exp(-t[:, :, None, None] * deltas)
    fwd, bwd = h[:, 0], h[:, 1]
    k = jnp.concatenate([fwd, jnp.zeros((1, HY_ORDER, HY_W), F32), bwd[:0:-1]], axis=0)
    return jnp.fft.rfft(k, axis=0)


def _fft_long_conv(y, k_f):
    n = y.shape[1]
    yf = jnp.fft.rfft(y, n=2 * n, axis=1)
    return jnp.fft.irfft(yf * k_f, n=2 * n, axis=1)[:, :n]


def _hyena_mixer(z, short_w, short_b, filt, bias):
    n = z.shape[1]
    z = _short_conv(z, short_w, short_b)
    v, x1, x2 = jnp.split(z, 3, axis=-1)
    k_f = _hyena_filter_spectrum(n, *filt)
    y = v.astype(F32)
    for o, gate in enumerate((x1, x2)):
        y = gate.astype(F32) * (_fft_long_conv(y, k_f[:, o]) + y * bias[o].astype(F32))
    return y.astype(z.dtype)


def _merge_groups(y_s5, y_mla, y_hy, g):
    return jnp.concatenate([
        _rms_norm(y_s5, g[:S5_W]),
        _rms_norm(y_mla, g[S5_W:S5_W + MLA_W]),
        _rms_norm(y_hy, g[S5_W + MLA_W:]),
    ], axis=-1)


def _moe_kernel(be_ref, x_ref, w1_ref, w3_ref, w2_ref, rw_ref, o_ref):
    del be_ref
    x = x_ref[...]
    a = jnp.dot(x, w1_ref[0], preferred_element_type=F32)
    b = jnp.dot(x, w3_ref[0], preferred_element_type=F32)
    h = (a * jax.nn.sigmoid(a)) * b
    y = jnp.dot(h.astype(BF16), w2_ref[0], preferred_element_type=F32)
    o_ref[...] = y * rw_ref[...]


def _moe_experts(xg, block_e, w1, w3, w2, row_w):
    n_rows, d = xg.shape
    hid = w1.shape[2]
    n_blocks = n_rows // MOE_BLOCK
    return pl.pallas_call(
        _moe_kernel,
        grid_spec=pltpu.PrefetchScalarGridSpec(
            num_scalar_prefetch=1,
            grid=(n_blocks,),
            in_specs=[
                pl.BlockSpec((MOE_BLOCK, d), lambda i, be: (i, 0)),
                pl.BlockSpec((1, d, hid), lambda i, be: (be[i], 0, 0)),
                pl.BlockSpec((1, d, hid), lambda i, be: (be[i], 0, 0)),
                pl.BlockSpec((1, hid, d), lambda i, be: (be[i], 0, 0)),
                pl.BlockSpec((MOE_BLOCK, 1), lambda i, be: (i, 0)),
            ],
            out_specs=pl.BlockSpec((MOE_BLOCK, d), lambda i, be: (i, 0)),
        ),
        out_shape=jax.ShapeDtypeStruct((n_rows, d), F32),
        compiler_params=pltpu.CompilerParams(dimension_semantics=("arbitrary",), vmem_limit_bytes=VMEM_LIMIT),
        name="moe_experts",
    )(block_e, xg, w1, w3, w2, row_w)


def _hier_moe(h, w_group, w_expert, w1, w3, w2):
    t, d = h.shape
    g_prob = jax.nn.softmax((h @ w_group).astype(F32), axis=-1)
    g_w, g_idx = lax.top_k(g_prob, 1)
    e_logits = (h @ w_expert).astype(F32).reshape(t, MOE_GROUPS, MOE_PER_GROUP)
    in_group = jnp.take_along_axis(e_logits, g_idx[:, :, None], axis=1)[:, 0]
    top_v, top_i = lax.top_k(in_group, MOE_TOP_K)
    gate = jax.nn.softmax(top_v, axis=-1) * g_w
    eid = g_idx * MOE_PER_GROUP + top_i
    n_assign = t * MOE_TOP_K
    flat_e = eid.reshape(n_assign)
    order = jnp.argsort(flat_e)
    se = flat_e[order]
    st = (order // MOE_TOP_K).astype(jnp.int32)
    sw = gate.reshape(n_assign)[order]
    counts = jnp.bincount(flat_e, length=MOE_EXPERTS)
    padded = (counts + MOE_BLOCK - 1) // MOE_BLOCK * MOE_BLOCK
    start = jnp.cumsum(counts) - counts
    pend = jnp.cumsum(padded)
    pstart = pend - padded
    dest = (pstart[se] + jnp.arange(n_assign) - start[se]).astype(jnp.int32)
    n_blocks = -(-n_assign // MOE_BLOCK) + MOE_EXPERTS
    n_rows = n_blocks * MOE_BLOCK
    row_tok = jnp.zeros((n_rows,), jnp.int32).at[dest].set(st)
    row_w = jnp.zeros((n_rows,), F32).at[dest].set(sw)
    block_e = jnp.minimum(jnp.searchsorted(pend, jnp.arange(n_blocks) * MOE_BLOCK, side='right'),
                          MOE_EXPERTS - 1).astype(jnp.int32)
    xg = h.astype(BF16)[row_tok]
    ys = _moe_experts(xg, block_e, w1.astype(BF16), w3.astype(BF16), w2.astype(BF16), row_w[:, None])
    slot = jnp.zeros((n_assign,), jnp.int32).at[order].set(dest).reshape(t, MOE_TOP_K)
    return ys[slot[:, 0]] + ys[slot[:, 1]]


def _final_norm_kernel(x_ref, g_ref, o_ref):
    x = x_ref[...]
    y = x * lax.rsqrt(jnp.mean(x * x, axis=-1, keepdims=True) + EPS)
    o_ref[...] = y * g_ref[...]


def _final_norm(x, g):
    t, d = x.shape
    blk = 1024
    return pl.pallas_call(
        _final_norm_kernel,
        grid=(t // blk,),
        in_specs=[pl.BlockSpec((blk, d), lambda i: (i, 0)), pl.BlockSpec((1, d), lambda i: (0, 0))],
        out_specs=pl.BlockSpec((blk, d), lambda i: (i, 0)),
        out_shape=jax.ShapeDtypeStruct((t, d), F32),
        name="final_norm",
    )(x, g.reshape(1, d))


def kernel(x, c, ctx, c_ctx, ada_w, ada_b, norm1_g, norm2_g, w_in,
           s5_lambda_re, s5_lambda_im, s5_log_dt, s5_b_re, s5_b_im, s5_c_re, s5_c_im, s5_d, s5_glu_w,
           mla_q_norm_g, mla_kv_norm_g, mla_w_uq, mla_w_ukv,
           hy_short_w, hy_short_b, hy_f_w1, hy_f_b1, hy_f_w2, hy_f_b2, hy_f_w3, hy_f_b3, hy_f_freq, hy_bias,
           mix_norm_g, w_out, moe_w_group, moe_w_expert, moe_w1, moe_w3, moe_w2, final_g):
    bsz, n, d = x.shape
    n_ctx = ctx.shape[1]
    xl, xc = x, ctx
    act_l = jax.nn.silu(c)
    act_c = jax.nn.silu(c_ctx)
    for i in range(DEPTH):
        ctx_out = i < DEPTH - 1
        mod_l = jnp.split((act_l @ ada_w[i] + ada_b[i])[:, None, :], 6, axis=-1)
        mod_c = jnp.split((act_c @ ada_w[i] + ada_b[i])[None, None, :], 6, axis=-1)
        hl = _modulate(xl, norm1_g[i], mod_l[0], mod_l[1])
        hc = _modulate(xc, norm1_g[i], mod_c[0], mod_c[1])
        u_l, cq_l, ckv_l, kr_l, hz_l = _split_projection(hl @ w_in[i])
        u_c, cq_c, ckv_c, kr_c, hz_c = _split_projection(hc @ w_in[i])
        s5_l, s5_c = _s5_mixer(u_l, u_c, s5_lambda_re[i], s5_lambda_im[i], s5_log_dt[i], s5_b_re[i], s5_b_im[i],
                               s5_c_re[i], s5_c_im[i], s5_d[i], s5_glu_w[i], ctx_out)
        mla_l, mla_c = _mla_mixer(cq_l, ckv_l, kr_l, cq_c, ckv_c, kr_c, mla_q_norm_g[i], mla_kv_norm_g[i],
                                  mla_w_uq[i], mla_w_ukv[i], ctx_out)
        filt = (hy_f_w1[i], hy_f_b1[i], hy_f_w2[i], hy_f_b2[i], hy_f_w3[i], hy_f_b3[i], hy_f_freq[i])
        hy_l = _hyena_mixer(hz_l, hy_short_w[i], hy_short_b[i], filt, hy_bias[i])
        xl = xl + mod_l[2] * (_merge_groups(s5_l, mla_l, hy_l, mix_norm_g[i]) @ w_out[i])
        fl = _modulate(xl, norm2_g[i], mod_l[3], mod_l[4]).reshape(bsz * n, d)
        if ctx_out:
            hy_c = _hyena_mixer(hz_c, hy_short_w[i], hy_short_b[i], filt, hy_bias[i])
            xc = xc + mod_c[2] * (_merge_groups(s5_c, mla_c, hy_c, mix_norm_g[i]) @ w_out[i])
            fc = _modulate(xc, norm2_g[i], mod_c[3], mod_c[4]).reshape(bsz * n_ctx, d)
            y = _hier_moe(jnp.concatenate([fl, fc], axis=0), moe_w_group[i], moe_w_expert[i],
                          moe_w1[i], moe_w3[i], moe_w2[i])
            xl = xl + mod_l[5] * y[:bsz * n].reshape(bsz, n, d)
            xc = xc + mod_c[5] * y[bsz * n:].reshape(bsz, n_ctx, d)
        else:
            y = _hier_moe(fl, moe_w_group[i], moe_w_expert[i], moe_w1[i], moe_w3[i], moe_w2[i])
            xl = xl + mod_l[5] * y.reshape(bsz, n, d)
    return _final_norm(xl.reshape(bsz * n, d), final_g).reshape(bsz, n, d)
```

```python
import functools
import math

import jax
import jax.numpy as jnp
from jax import lax
from jax.experimental import pallas as pl
from jax.experimental.pallas import tpu as pltpu

D_MODEL = 1024
DEPTH = 4
GRID_W = 64
EPS = 1e-6

MIX_W = D_MODEL
S5_W = D_MODEL // 4
S5_GROUP = 16
S5_GROUPS = S5_W // S5_GROUP
S5_STATE = 64
S5_N = S5_GROUPS * S5_STATE
MLA_V = 64
MLA_W = D_MODEL // 2
MLA_HEADS = MLA_W // MLA_V
MLA_NOPE = 64
MLA_ROPE = 32
MLA_QK = MLA_NOPE + MLA_ROPE
MLA_Q_RANK = 384
MLA_KV_RANK = 256
MLA_SCALE = 1.0 / math.sqrt(MLA_NOPE + MLA_ROPE)
ROPE_BASE = 10000.0
HY_W = D_MODEL // 4
HY_ORDER = 2
HY_POS_EMB = 33
HY_FILTER_W = 64
HY_MIN_DECAY = math.log(1e-2) / 1.5
HY_MAX_DECAY = math.log(1e-2) / 0.3
MOE_GROUPS = 4
MOE_PER_GROUP = 8
MOE_EXPERTS = MOE_GROUPS * MOE_PER_GROUP
MOE_TOP_K = 2
MOE_HIDDEN = 512
MOE_BLOCK = 256

SUBLANES = 8
S5_CHUNK = 128
MM_ROWS = 256
ATTN_Q_BLOCK = 256
VMEM_LIMIT = 48 * 1024 * 1024

F32 = jnp.float32
BF16 = jnp.bfloat16


def _rms_norm(x, g):
    xf = x.astype(F32)
    y = xf * lax.rsqrt(jnp.mean(xf * xf, axis=-1, keepdims=True) + EPS)
    return (y * g.astype(F32)).astype(x.dtype)


def _modulate(x, g, shift, scale):
    return _rms_norm(x, g) * (1.0 + scale) + shift


def _split_projection(p):
    o1 = S5_W
    o2 = o1 + MLA_Q_RANK
    o3 = o2 + MLA_KV_RANK
    o4 = o3 + MLA_ROPE
    return p[..., :o1], p[..., o1:o2], p[..., o2:o3], p[..., o3:o4], p[..., o4:]


def _s5_matrices(lam_re, lam_im, log_dt, b_re, b_im, c_re, c_im):
    g, p, h = S5_GROUPS, S5_STATE, S5_GROUP
    dt = jnp.exp(log_dt)[..., None]
    mag = jnp.exp(lam_re * dt)
    ar, ai = mag * jnp.cos(lam_im * dt), mag * jnp.sin(lam_im * dt)
    den = lam_re * lam_re + lam_im * lam_im
    fr = ((ar - 1.0) * lam_re + ai * lam_im) / den
    fi = (ai * lam_re - (ar - 1.0) * lam_im) / den
    bbr = fr[..., None] * b_re - fi[..., None] * b_im
    bbi = fr[..., None] * b_im + fi[..., None] * b_re
    eye = jnp.eye(g, dtype=F32)

    def block_in(m):
        return jnp.einsum('kgph,gj->kghjp', m, eye).reshape(2, g * h, g * p)

    def block_out(m):
        return jnp.einsum('kghp,gj->kgpjh', m, eye).reshape(2, g * p, g * h)

    bmat = jnp.concatenate([block_in(bbr), block_in(bbi)], axis=-1)
    cmat = jnp.concatenate([block_out(c_re), -block_out(c_im)], axis=1)
    lam = jnp.stack([ar[0], ai[0], ar[1], ai[1]]).reshape(4, 1, g * p)
    lam = jnp.broadcast_to(lam, (4, SUBLANES, g * p)).reshape(4 * SUBLANES, g * p)
    return bmat.astype(BF16), cmat.astype(BF16), lam


def _s5_kernel(uf_ref, ub_ref, bmat_ref, cmat_ref, lam_ref, yf_ref, yb_ref, buf_f, buf_b, state):
    rows = uf_ref.shape[0]
    steps = rows // SUBLANES
    n = S5_N
    s = SUBLANES

    @pl.when(pl.program_id(0) == 0)
    def _():
        state[...] = jnp.zeros_like(state)

    def drive(r, carry):
        r0 = pl.multiple_of(r * MM_ROWS, MM_ROWS)
        buf_f[pl.ds(r0, MM_ROWS), :] = jnp.dot(uf_ref[pl.ds(r0, MM_ROWS), :].astype(BF16), bmat_ref[0],
                                               preferred_element_type=F32)
        buf_b[pl.ds(r0, MM_ROWS), :] = jnp.dot(ub_ref[pl.ds(r0, MM_ROWS), :].astype(BF16), bmat_ref[1],
                                               preferred_element_type=F32)
        return carry

    lax.fori_loop(0, rows // MM_ROWS, drive, 0)

    def step(j, carry):
        fr, fi, br, bi = carry
        rf = pl.multiple_of(j * s, s)
        rb = pl.multiple_of((steps - 1 - j) * s, s)
        lfr, lfi = lam_ref[0:s, :], lam_ref[s:2 * s, :]
        lbr, lbi = lam_ref[2 * s:3 * s, :], lam_ref[3 * s:4 * s, :]
        nfr = lfr * fr - lfi * fi + buf_f[pl.ds(rf, s), 0:n]
        nfi = lfr * fi + lfi * fr + buf_f[pl.ds(rf, s), n:2 * n]
        nbr = lbr * br - lbi * bi + buf_b[pl.ds(rb, s), 0:n]
        nbi = lbr * bi + lbi * br + buf_b[pl.ds(rb, s), n:2 * n]
        buf_f[pl.ds(rf, s), 0:n] = nfr
        buf_f[pl.ds(rf, s), n:2 * n] = nfi
        buf_b[pl.ds(rb, s), 0:n] = nbr
        buf_b[pl.ds(rb, s), n:2 * n] = nbi
        return nfr, nfi, nbr, nbi

    init = (state[0:s, :], state[s:2 * s, :], state[2 * s:3 * s, :], state[3 * s:4 * s, :])
    fr, fi, br, bi = lax.fori_loop(0, steps, step, init, unroll=2)
    state[0:s, :] = fr
    state[s:2 * s, :] = fi
    state[2 * s:3 * s, :] = br
    state[3 * s:4 * s, :] = bi

    def readout(r, carry):
        r0 = pl.multiple_of(r * MM_ROWS, MM_ROWS)
        yf_ref[pl.ds(r0, MM_ROWS), :] = jnp.dot(buf_f[pl.ds(r0, MM_ROWS), :].astype(BF16), cmat_ref[0],
                                                preferred_element_type=F32)
        yb_ref[pl.ds(r0, MM_ROWS), :] = jnp.dot(buf_b[pl.ds(r0, MM_ROWS), :].astype(BF16), cmat_ref[1],
                                                preferred_element_type=F32)
        return carry

    lax.fori_loop(0, rows // MM_ROWS, readout, 0)


def _s5_scan(u_tm, bmat, cmat, lam, n_ctx_chunks, chunk):
    rows_all, w = u_tm.shape
    rows = chunk * SUBLANES
    n_chunks = rows_all // rows
    n2 = 2 * S5_N

    def bwd_chunk(i):
        return jnp.where(i < n_ctx_chunks, n_ctx_chunks - 1 - i, n_chunks - 1 + n_ctx_chunks - i)

    return pl.pallas_call(
        _s5_kernel,
        grid=(n_chunks,),
        in_specs=[
            pl.BlockSpec((rows, w), lambda i: (i, 0)),
            pl.BlockSpec((rows, w), lambda i: (bwd_chunk(i), 0)),
            pl.BlockSpec((2, w, n2), lambda i: (0, 0, 0)),
            pl.BlockSpec((2, n2, w), lambda i: (0, 0, 0)),
            pl.BlockSpec((4 * SUBLANES, S5_N), lambda i: (0, 0)),
        ],
        out_specs=[
            pl.BlockSpec((rows, w), lambda i: (i, 0)),
            pl.BlockSpec((rows, w), lambda i: (bwd_chunk(i), 0)),
        ],
        out_shape=[jax.ShapeDtypeStruct((rows_all, w), F32)] * 2,
        scratch_shapes=[
            pltpu.VMEM((rows, n2), F32),
            pltpu.VMEM((rows, n2), F32),
            pltpu.VMEM((4 * SUBLANES, S5_N), F32),
        ],
        compiler_params=pltpu.CompilerParams(dimension_semantics=("arbitrary",), vmem_limit_bytes=VMEM_LIMIT),
        name="s5_scan",
    )(u_tm, u_tm, bmat, cmat, lam)


def _s5_mixer(u_l, u_c, lam_re, lam_im, log_dt, b_re, b_im, c_re, c_im, d, glu_w, ctx_out):
    bsz, n, w = u_l.shape
    n_ctx = u_c.shape[1]
    assert bsz == SUBLANES and n % S5_CHUNK == 0 and n_ctx % S5_CHUNK == 0
    bmat, cmat, lam = _s5_matrices(lam_re, lam_im, log_dt, b_re, b_im, c_re, c_im)
    u_all = jnp.concatenate([u_c, u_l], axis=1)
    u_tm = u_all.swapaxes(0, 1).reshape((n_ctx + n) * bsz, w)
    y_f, y_b = _s5_scan(u_tm, bmat, cmat, lam, n_ctx // S5_CHUNK, S5_CHUNK)
    y = (y_f + y_b).reshape(n_ctx + n, bsz, w).swapaxes(0, 1)
    y = y + d * u_all
    y = jax.nn.gelu(y)
    y = y * jax.nn.sigmoid(y @ glu_w)
    return y[:, n_ctx:], (y[:, :n_ctx] if ctx_out else None)


def _axial_rope_tables(n_tokens):
    rows = n_tokens // GRID_W
    row = jnp.repeat(jnp.arange(rows), GRID_W).astype(F32)
    col = jnp.tile(jnp.arange(GRID_W), rows).astype(F32)
    half = MLA_ROPE // 2
    inv = ROPE_BASE ** (-jnp.arange(0, half, 2, dtype=F32) / half)
    ang_r = row[:, None] * inv
    ang_c = col[:, None] * inv
    return jnp.cos(ang_r), jnp.sin(ang_r), jnp.cos(ang_c), jnp.sin(ang_c)


def _rotate(x, cos, sin):
    m = x.shape[-1] // 2
    x1, x2 = x[..., :m], x[..., m:]
    return jnp.concatenate([x1 * cos - x2 * sin, x1 * sin + x2 * cos], axis=-1)


def _axial_rope(x, tables):
    cr, sr, cc, sc = (t[:, None, :].astype(x.dtype) for t in tables)
    half = MLA_ROPE // 2
    return jnp.concatenate([_rotate(x[..., :half], cr, sr), _rotate(x[..., half:], cc, sc)], axis=-1)


def _attn_kernel(q_ref, k_ref, v_ref, o_ref):
    s = lax.dot_general(q_ref[0, 0], k_ref[0, 0], (((1,), (1,)), ((), ())), preferred_element_type=F32)
    m = jnp.max(s, axis=-1, keepdims=True)
    p = jnp.exp(s - m)
    l = jnp.sum(p, axis=-1, keepdims=True)
    o = jnp.dot(p.astype(BF16), v_ref[0, 0], preferred_element_type=F32)
    o_ref[0, 0] = o / l


def _attention(q, k, v):
    bsz, heads, nq, dqk = q.shape
    nk, dv = k.shape[2], v.shape[3]
    tq = min(ATTN_Q_BLOCK, nq)
    return pl.pallas_call(
        _attn_kernel,
        grid=(bsz, heads, nq // tq),
        in_specs=[
            pl.BlockSpec((1, 1, tq, dqk), lambda b, h, i: (b, h, i, 0)),
            pl.BlockSpec((1, 1, nk, dqk), lambda b, h, i: (b, h, 0, 0)),
            pl.BlockSpec((1, 1, nk, dv), lambda b, h, i: (b, h, 0, 0)),
        ],
        out_specs=pl.BlockSpec((1, 1, tq, dv), lambda b, h, i: (b, h, i, 0)),
        out_shape=jax.ShapeDtypeStruct((bsz, heads, nq, dv), F32),
        compiler_params=pltpu.CompilerParams(
            dimension_semantics=("arbitrary", "arbitrary", "arbitrary"), vmem_limit_bytes=VMEM_LIMIT),
        name="mla_attention",
    )(q, k, v)


def _mla_mixer(cq_l, ckv_l, kr_l, cq_c, ckv_c, kr_c, q_norm_g, kv_norm_g, w_uq, w_ukv, ctx_out):
    def queries(cq):
        q = (_rms_norm(cq, q_norm_g) @ w_uq).reshape(cq.shape[0], cq.shape[1], MLA_HEADS, MLA_QK)
        return q[..., :MLA_NOPE], q[..., MLA_NOPE:]

    def keys_values(ckv):
        kv = (_rms_norm(ckv, kv_norm_g) @ w_ukv).reshape(ckv.shape[0], ckv.shape[1], MLA_HEADS, MLA_NOPE + MLA_V)
        return kv[..., :MLA_NOPE], kv[..., MLA_NOPE:]

    def head_major_q(qn, qr):
        q = jnp.concatenate([qn, qr], axis=-1) * MLA_SCALE
        return q.swapaxes(1, 2).astype(BF16)

    def head_major_k(kn, kr):
        kr = jnp.broadcast_to(kr[:, :, None, :], kn.shape[:3] + (MLA_ROPE,))
        return jnp.concatenate([kn, kr], axis=-1).swapaxes(1, 2).astype(BF16)

    bsz, n = cq_l.shape[0], cq_l.shape[1]
    n_ctx = kr_c.shape[1]
    tables = _axial_rope_tables(n)
    kn_c, v_c = keys_values(ckv_c)
    kn_l, v_l = keys_values(ckv_l)
    kr_lat = _axial_rope(kr_l[:, :, None, :], tables)[:, :, 0]
    qn_l, qr_l = queries(cq_l)
    qr_l = _axial_rope(qr_l, tables)
    k_nope = jnp.concatenate([kn_c, kn_l], axis=1)
    k_rope = jnp.concatenate([kr_c, kr_lat], axis=1)
    v = jnp.concatenate([v_c, v_l], axis=1)
    k_hm = head_major_k(k_nope, k_rope)
    v_hm = v.swapaxes(1, 2).astype(BF16)
    o = _attention(head_major_q(qn_l, qr_l), k_hm, v_hm)
    y_l = o.swapaxes(1, 2).reshape(bsz, n, MLA_W)
    y_c = None
    if ctx_out:
        qn_c, qr_c = queries(cq_c)
        o_c = _attention(head_major_q(qn_c, qr_c), k_hm[:, :, :n_ctx], v_hm[:, :, :n_ctx])
        y_c = o_c.swapaxes(1, 2).reshape(bsz, n_ctx, MLA_W)
    return y_l, y_c


def _short_conv(z, w, b):
    n = z.shape[1]
    zp = jnp.pad(z, ((0, 0), (1, 1), (0, 0)))
    return zp[:, :n] * w[0] + zp[:, 1:n + 1] * w[1] + zp[:, 2:] * w[2] + b


def _hyena_filter_taps(n, w1, b1, w2, b2, w3, b3, freq):
    t = jnp.linspace(0.0, 1.0, n, dtype=F32)[:, None]
    bands = (HY_POS_EMB - 1) // 2
    f = jnp.linspace(1e-4, bands - 1, bands, dtype=F32)
    ang = (2.0 * math.pi * jnp.arange(n, dtype=F32) / n)[:, None] * f
    z = jnp.concatenate([t, jnp.cos(ang), -jnp.sin(ang)], axis=-1)
    h = jnp.sin(freq[0] * (z @ w1 + b1))
    h = jnp.sin(freq[1] * (h @ w2 + b2))
    h = (h @ w3 + b3).astype(F32).reshape(n, 2, HY_ORDER, HY_W)
    deltas = jnp.abs(jnp.linspace(HY_MIN_DECAY, HY_MAX_DECAY, HY_W, dtype=F32))
    h = h * jnp.exp(-t[:, :, None, None] * deltas)
    fwd, bwd = h[:, 0], h[:, 1]
    return jnp.concatenate([fwd, jnp.zeros((1, HY_ORDER, HY_W), F32), bwd[:0:-1]], axis=0)


def _fft_long_conv(y, k_f):
    n = y.shape[1]
    yf = jnp.fft.rfft(y, n=2 * n, axis=1)
    return jnp.fft.irfft(yf * k_f, n=2 * n, axis=1)[:, :n]


def _hyena_mixer(z, short_w, short_b, filt, bias):
    n = z.shape[1]
    z = _short_conv(z, short_w, short_b)
    v, x1, x2 = jnp.split(z, 3, axis=-1)
    y = v.astype(F32)
    if (2 * n) % HY_N2 == 0 and (2 * n) // HY_N2 >= 2 * SUBLANES:
        k_time = _hyena_filter_taps(n, *filt)
        tables = _dft_tables(2 * n)
        kr, ki = _hyena_spectrum(k_time.reshape(2 * n, HY_ORDER * HY_W), tables)
        for o, gate in enumerate((x1, x2)):
            cols = slice(o * HY_W, (o + 1) * HY_W)
            y = gate.astype(F32) * (_hyena_long_conv(y, kr[:, cols], ki[:, cols], tables) + y * bias[o].astype(F32))
    else:
        k_f = jnp.fft.rfft(_hyena_filter_taps(n, *filt), axis=0)
        for o, gate in enumerate((x1, x2)):
            y = gate.astype(F32) * (_fft_long_conv(y, k_f[:, o]) + y * bias[o].astype(F32))
    return y.astype(z.dtype)


HY_N2 = 128
HY_PITCH = 136
LANES = 128
HY_UNROLL = 8


def _dft_tables(nfft):
    n1 = nfft // HY_N2
    half = n1 // 2
    k1 = jnp.arange(n1, dtype=jnp.int32)
    n2 = jnp.arange(HY_N2, dtype=jnp.int32)
    t = HY_N2 * k1[None, None, :] + n2[:, None, None]
    ang = (2.0 * math.pi / nfft) * ((k1[None, :, None] * t) % nfft).astype(F32)
    gr, gi = jnp.cos(ang), -jnp.sin(ang)
    g_cplx = jnp.concatenate([jnp.concatenate([gr[..., :half], -gi[..., :half]], -1),
                              jnp.concatenate([gi[..., :half], gr[..., :half]], -1)], axis=1)
    g_real = jnp.concatenate([gr, gi], axis=1)
    hr = jnp.cos(ang).swapaxes(1, 2)[:, :half] / nfft
    hi = jnp.sin(ang).swapaxes(1, 2)[:, :half] / nfft
    g_inv = jnp.concatenate([jnp.concatenate([hr, -hi], -1), jnp.concatenate([hi, hr], -1)], axis=1)
    a2 = (2.0 * math.pi / HY_N2) * ((n2[:, None] * n2[None, :]) % HY_N2).astype(F32)
    fr, fi = jnp.cos(a2), -jnp.sin(a2)
    f2 = jnp.concatenate([jnp.concatenate([fr, -fi], -1), jnp.concatenate([fi, fr], -1)], axis=0)
    f2_inv = jnp.concatenate([jnp.concatenate([fr, fi], -1), jnp.concatenate([-fi, fr], -1)], axis=0)
    return dict(n1=n1, g_cplx=g_cplx.astype(BF16), g_real=g_real.astype(BF16), g_inv=g_inv.astype(BF16),
                f2=f2.astype(BF16), f2_inv=f2_inv.astype(BF16))


def _dft_stage1(load_rows, g_ref, s_re, s_im, n1):
    def body(n2, carry):
        a = jnp.dot(g_ref[n2], load_rows(n2).astype(BF16), preferred_element_type=F32)
        s_re[pl.ds(n2, n1, stride=HY_PITCH), :] = a[:n1]
        s_im[pl.ds(n2, n1, stride=HY_PITCH), :] = a[n1:]
        return carry

    lax.fori_loop(0, HY_N2, body, 0, unroll=HY_UNROLL)


def _dft_stage2(f2_ref, s_re, s_im, k1):
    r0 = pl.multiple_of(k1 * HY_PITCH, SUBLANES)
    sl = jnp.concatenate([s_re[pl.ds(r0, HY_N2), :], s_im[pl.ds(r0, HY_N2), :]], axis=0).astype(BF16)
    x = jnp.dot(f2_ref[...], sl, preferred_element_type=F32)
    return r0, x[:HY_N2], x[HY_N2:]


def _hyena_spectrum_kernel(x_ref, g_ref, f2_ref, kr_ref, ki_ref, s_re, s_im, *, n1):
    _dft_stage1(lambda n2: x_ref[pl.ds(n2, n1, stride=HY_PITCH), :], g_ref, s_re, s_im, n1)

    def body(k1, carry):
        _, xr, xi = _dft_stage2(f2_ref, s_re, s_im, k1)
        q0 = pl.multiple_of(k1 * HY_N2, HY_N2)
        kr_ref[pl.ds(q0, HY_N2), :] = xr
        ki_ref[pl.ds(q0, HY_N2), :] = xi
        return carry

    lax.fori_loop(0, n1, body, 0)


def _hyena_conv_kernel(xr_ref, xi_ref, g_ref, f2_ref, f2i_ref, gi_ref, kr_ref, ki_ref, yr_ref, yi_ref,
                       s_re, s_im, *, n1):
    half = n1 // 2

    def load_rows(n2):
        return jnp.concatenate([xr_ref[0, pl.ds(n2, half, stride=HY_PITCH), :],
                                xi_ref[0, pl.ds(n2, half, stride=HY_PITCH), :]], axis=0)

    _dft_stage1(load_rows, g_ref, s_re, s_im, n1)

    def spectrum_product(k1, carry):
        r0, xr, xi = _dft_stage2(f2_ref, s_re, s_im, k1)
        q0 = pl.multiple_of(k1 * HY_N2, HY_N2)
        kr, ki = kr_ref[pl.ds(q0, HY_N2), :], ki_ref[pl.ds(q0, HY_N2), :]
        y = jnp.concatenate([xr * kr - xi * ki, xr * ki + xi * kr], axis=0).astype(BF16)
        b = jnp.dot(f2i_ref[...], y, preferred_element_type=F32)
        s_re[pl.ds(r0, HY_N2), :] = b[:HY_N2]
        s_im[pl.ds(r0, HY_N2), :] = b[HY_N2:]
        return carry

    lax.fori_loop(0, n1, spectrum_product, 0, unroll=HY_UNROLL // 2)

    yr_ref[...] = jnp.zeros_like(yr_ref)
    yi_ref[...] = jnp.zeros_like(yi_ref)

    def inverse_stage1(n2, carry):
        bs = jnp.concatenate([s_re[pl.ds(n2, n1, stride=HY_PITCH), :],
                              s_im[pl.ds(n2, n1, stride=HY_PITCH), :]], axis=0).astype(BF16)
        y = jnp.dot(gi_ref[n2], bs, preferred_element_type=F32)
        yr_ref[0, pl.ds(n2, half, stride=HY_PITCH), :] = y[:half]
        yi_ref[0, pl.ds(n2, half, stride=HY_PITCH), :] = y[half:]
        return carry

    lax.fori_loop(0, HY_N2, inverse_stage1, 0, unroll=HY_UNROLL)


def _to_strided(a, rows):
    lead, c = a.shape[:-2], a.shape[-1]
    a = a.reshape(lead + (rows, HY_N2, c))
    a = jnp.pad(a, [(0, 0)] * len(lead) + [(0, 0), (0, HY_PITCH - HY_N2), (0, 0)])
    return a.reshape(lead + (rows * HY_PITCH, c))


def _from_strided(a, rows):
    lead, c = a.shape[:-2], a.shape[-1]
    return a.reshape(lead + (rows, HY_PITCH, c))[..., :HY_N2, :].reshape(lead + (rows * HY_N2, c))


def _const_spec(shape):
    return pl.BlockSpec(shape, lambda *_: (0,) * len(shape), pipeline_mode=pl.Buffered(1))


def _hyena_spectrum(k_time, tables):
    nfft, c = k_time.shape
    n1 = tables["n1"]
    spec_out = pl.BlockSpec((nfft, LANES), lambda j: (0, j))
    return pl.pallas_call(
        functools.partial(_hyena_spectrum_kernel, n1=n1),
        grid=(c // LANES,),
        in_specs=[pl.BlockSpec((n1 * HY_PITCH, LANES), lambda j: (0, j)),
                  _const_spec((HY_N2, 2 * n1, n1)), _const_spec((2 * HY_N2, 2 * HY_N2))],
        out_specs=[spec_out, spec_out],
        out_shape=[jax.ShapeDtypeStruct((nfft, c), F32)] * 2,
        scratch_shapes=[pltpu.VMEM((n1 * HY_PITCH, LANES), F32)] * 2,
        compiler_params=pltpu.CompilerParams(dimension_semantics=("arbitrary",), vmem_limit_bytes=VMEM_LIMIT),
        name="hyena_spectrum",
    )(_to_strided(k_time, n1), tables["g_real"], tables["f2"])


def _hyena_long_conv(y, kr, ki, tables):
    bsz, n, c = y.shape
    n1 = tables["n1"]
    half = n1 // 2
    pairs = bsz // 2
    xr, xi = _to_strided(y[0::2], half), _to_strided(y[1::2], half)
    spec_x = pl.BlockSpec((1, half * HY_PITCH, LANES), lambda j, p: (p, 0, j))
    spec_k = pl.BlockSpec((n1 * HY_N2, LANES), lambda j, p: (0, j), pipeline_mode=pl.Buffered(1))
    yr, yi = pl.pallas_call(
        functools.partial(_hyena_conv_kernel, n1=n1),
        grid=(c // LANES, pairs),
        in_specs=[spec_x, spec_x,
                  _const_spec((HY_N2, 2 * n1, n1)), _const_spec((2 * HY_N2, 2 * HY_N2)),
                  _const_spec((2 * HY_N2, 2 * HY_N2)), _const_spec((HY_N2, n1, 2 * n1)),
                  spec_k, spec_k],
        out_specs=[spec_x, spec_x],
        out_shape=[jax.ShapeDtypeStruct((pairs, half * HY_PITCH, c), F32)] * 2,
        scratch_shapes=[pltpu.VMEM((n1 * HY_PITCH, LANES), F32)] * 2,
        compiler_params=pltpu.CompilerParams(dimension_semantics=("arbitrary", "arbitrary"),
                                             vmem_limit_bytes=VMEM_LIMIT),
        name="hyena_conv",
    )(xr, xi, tables["g_cplx"], tables["f2"], tables["f2_inv"], tables["g_inv"], kr, ki)
    out = jnp.stack([_from_strided(yr, half), _from_strided(yi, half)], axis=1)
    return out.reshape(bsz, n, c)


def _merge_groups(y_s5, y_mla, y_hy, g):
    return jnp.concatenate([
        _rms_norm(y_s5, g[:S5_W]),
        _rms_norm(y_mla, g[S5_W:S5_W + MLA_W]),
        _rms_norm(y_hy, g[S5_W + MLA_W:]),
    ], axis=-1)


def _moe_kernel(be_ref, x_ref, w1_ref, w3_ref, w2_ref, rw_ref, o_ref):
    del be_ref
    x = x_ref[...]
    a = jnp.dot(x, w1_ref[0], preferred_element_type=F32)
    b = jnp.dot(x, w3_ref[0], preferred_element_type=F32)
    h = (a * jax.nn.sigmoid(a)) * b
    y = jnp.dot(h.astype(BF16), w2_ref[0], preferred_element_type=F32)
    o_ref[...] = y * rw_ref[...]


def _moe_experts(xg, block_e, w1, w3, w2, row_w):
    n_rows, d = xg.shape
    hid = w1.shape[2]
    n_blocks = n_rows // MOE_BLOCK
    return pl.pallas_call(
        _moe_kernel,
        grid_spec=pltpu.PrefetchScalarGridSpec(
            num_scalar_prefetch=1,
            grid=(n_blocks,),
            in_specs=[
                pl.BlockSpec((MOE_BLOCK, d), lambda i, be: (i, 0)),
                pl.BlockSpec((1, d, hid), lambda i, be: (be[i], 0, 0)),
                pl.BlockSpec((1, d, hid), lambda i, be: (be[i], 0, 0)),
                pl.BlockSpec((1, hid, d), lambda i, be: (be[i], 0, 0)),
                pl.BlockSpec((MOE_BLOCK, 1), lambda i, be: (i, 0)),
            ],
            out_specs=pl.BlockSpec((MOE_BLOCK, d), lambda i, be: (i, 0)),
        ),
        out_shape=jax.ShapeDtypeStruct((n_rows, d), F32),
        compiler_params=pltpu.CompilerParams(dimension_semantics=("arbitrary",), vmem_limit_bytes=VMEM_LIMIT),
        name="moe_experts",
    )(block_e, xg, w1, w3, w2, row_w)


def _hier_moe(h, w_group, w_expert, w1, w3, w2):
    t, d = h.shape
    g_prob = jax.nn.softmax((h @ w_group).astype(F32), axis=-1)
    g_idx = jnp.argmax(g_prob, axis=-1).astype(jnp.int32)
    g_w = jnp.max(g_prob, axis=-1, keepdims=True)
    e_logits = (h @ w_expert).astype(F32).reshape(t, MOE_GROUPS, MOE_PER_GROUP)
    in_group = jnp.take_along_axis(e_logits, g_idx[:, None, None], axis=1)[:, 0]
    i1 = jnp.argmax(in_group, axis=-1).astype(jnp.int32)
    rest = jnp.where(jnp.arange(MOE_PER_GROUP, dtype=jnp.int32)[None, :] == i1[:, None], -jnp.inf, in_group)
    i2 = jnp.argmax(rest, axis=-1).astype(jnp.int32)
    top_v = jnp.stack([jnp.max(in_group, axis=-1), jnp.max(rest, axis=-1)], axis=-1)
    top_i = jnp.stack([i1, i2], axis=-1)
    gate = jax.nn.softmax(top_v, axis=-1) * g_w
    eid = g_idx[:, None] * MOE_PER_GROUP + top_i
    n_assign = t * MOE_TOP_K
    flat_e = eid.reshape(n_assign)
    flat_w = gate.reshape(n_assign)
    se, order = lax.sort((flat_e, jnp.arange(n_assign, dtype=jnp.int32)), num_keys=1)
    counts = jnp.sum(flat_e[:, None] == jnp.arange(MOE_EXPERTS, dtype=jnp.int32)[None, :], axis=0, dtype=jnp.int32)
    padded = (counts + MOE_BLOCK - 1) // MOE_BLOCK * MOE_BLOCK
    start = jnp.cumsum(counts) - counts
    pend = jnp.cumsum(padded)
    pstart = pend - padded
    dest = (pstart[se] + jnp.arange(n_assign, dtype=jnp.int32) - start[se]).astype(jnp.int32)
    n_blocks = -(-n_assign // MOE_BLOCK) + MOE_EXPERTS
    n_rows = n_blocks * MOE_BLOCK
    block_start = jnp.arange(n_blocks, dtype=jnp.int32) * MOE_BLOCK
    block_e = jnp.minimum(jnp.sum(pend[None, :] <= block_start[:, None], axis=1), MOE_EXPERTS - 1).astype(jnp.int32)
    row_e = jnp.repeat(block_e, MOE_BLOCK)
    off = jnp.arange(n_rows, dtype=jnp.int32) - pstart[row_e]
    valid = off < counts[row_e]
    row_asg = order[jnp.where(valid, start[row_e] + off, 0)]
    row_tok = jnp.where(valid, row_asg // MOE_TOP_K, 0)
    row_w = jnp.where(valid, flat_w[row_asg], 0.0)
    xg = h.astype(BF16)[row_tok]
    ys = _moe_experts(xg, block_e, w1.astype(BF16), w3.astype(BF16), w2.astype(BF16), row_w[:, None])
    _, slot = lax.sort((order, dest), num_keys=1)
    slot = slot.reshape(t, MOE_TOP_K)
    return ys[slot[:, 0]] + ys[slot[:, 1]]


def _final_norm_kernel(x_ref, g_ref, o_ref):
    x = x_ref[...]
    y = x * lax.rsqrt(jnp.mean(x * x, axis=-1, keepdims=True) + EPS)
    o_ref[...] = y * g_ref[...]


def _final_norm(x, g):
    t, d = x.shape
    blk = 1024
    return pl.pallas_call(
        _final_norm_kernel,
        grid=(t // blk,),
        in_specs=[pl.BlockSpec((blk, d), lambda i: (i, 0)), pl.BlockSpec((1, d), lambda i: (0, 0))],
        out_specs=pl.BlockSpec((blk, d), lambda i: (i, 0)),
        out_shape=jax.ShapeDtypeStruct((t, d), F32),
        name="final_norm",
    )(x, g.reshape(1, d))


def kernel(x, c, ctx, c_ctx, ada_w, ada_b, norm1_g, norm2_g, w_in,
           s5_lambda_re, s5_lambda_im, s5_log_dt, s5_b_re, s5_b_im, s5_c_re, s5_c_im, s5_d, s5_glu_w,
           mla_q_norm_g, mla_kv_norm_g, mla_w_uq, mla_w_ukv,
           hy_short_w, hy_short_b, hy_f_w1, hy_f_b1, hy_f_w2, hy_f_b2, hy_f_w3, hy_f_b3, hy_f_freq, hy_bias,
           mix_norm_g, w_out, moe_w_group, moe_w_expert, moe_w1, moe_w3, moe_w2, final_g):
    bsz, n, d = x.shape
    n_ctx = ctx.shape[1]
    xl, xc = x, ctx
    act_l = jax.nn.silu(c)
    act_c = jax.nn.silu(c_ctx)
    for i in range(DEPTH):
        ctx_out = i < DEPTH - 1
        mod_l = jnp.split((act_l @ ada_w[i] + ada_b[i])[:, None, :], 6, axis=-1)
        mod_c = jnp.split((act_c @ ada_w[i] + ada_b[i])[None, None, :], 6, axis=-1)
        hl = _modulate(xl, norm1_g[i], mod_l[0], mod_l[1])
        hc = _modulate(xc, norm1_g[i], mod_c[0], mod_c[1])
        u_l, cq_l, ckv_l, kr_l, hz_l = _split_projection(hl @ w_in[i])
        u_c, cq_c, ckv_c, kr_c, hz_c = _split_projection(hc @ w_in[i])
        s5_l, s5_c = _s5_mixer(u_l, u_c, s5_lambda_re[i], s5_lambda_im[i], s5_log_dt[i], s5_b_re[i], s5_b_im[i],
                               s5_c_re[i], s5_c_im[i], s5_d[i], s5_glu_w[i], ctx_out)
        mla_l, mla_c = _mla_mixer(cq_l, ckv_l, kr_l, cq_c, ckv_c, kr_c, mla_q_norm_g[i], mla_kv_norm_g[i],
                                  mla_w_uq[i], mla_w_ukv[i], ctx_out)
        filt = (hy_f_w1[i], hy_f_b1[i], hy_f_w2[i], hy_f_b2[i], hy_f_w3[i], hy_f_b3[i], hy_f_freq[i])
        hy_l = _hyena_mixer(hz_l, hy_short_w[i], hy_short_b[i], filt, hy_bias[i])
        xl = xl + mod_l[2] * (_merge_groups(s5_l, mla_l, hy_l, mix_norm_g[i]) @ w_out[i])
        fl = _modulate(xl, norm2_g[i], mod_l[3], mod_l[4]).reshape(bsz * n, d)
        if ctx_out:
            hy_c = _hyena_mixer(hz_c, hy_short_w[i], hy_short_b[i], filt, hy_bias[i])
            xc = xc + mod_c[2] * (_merge_groups(s5_c, mla_c, hy_c, mix_norm_g[i]) @ w_out[i])
            fc = _modulate(xc, norm2_g[i], mod_c[3], mod_c[4]).reshape(bsz * n_ctx, d)
            y = _hier_moe(jnp.concatenate([fl, fc], axis=0), moe_w_group[i], moe_w_expert[i],
                          moe_w1[i], moe_w3[i], moe_w2[i])
            xl = xl + mod_l[5] * y[:bsz * n].reshape(bsz, n, d)
            xc = xc + mod_c[5] * y[bsz * n:].reshape(bsz, n_ctx, d)
        else:
            y = _hier_moe(fl, moe_w_group[i], moe_w_expert[i], moe_w1[i], moe_w3[i], moe_w2[i])
            xl = xl + mod_l[5] * y.reshape(bsz, n, d)
    return _final_norm(xl.reshape(bsz * n, d), final_g).reshape(bsz, n, d)
```

```python
import functools
import math

import jax
import jax.numpy as jnp
from jax import lax
from jax.experimental import pallas as pl
from jax.experimental.pallas import tpu as pltpu

D_MODEL = 1024
DEPTH = 4
GRID_W = 64
EPS = 1e-6

MIX_W = D_MODEL
S5_W = D_MODEL // 4
S5_GROUP = 16
S5_GROUPS = S5_W // S5_GROUP
S5_STATE = 64
S5_N = S5_GROUPS * S5_STATE
MLA_V = 64
MLA_W = D_MODEL // 2
MLA_HEADS = MLA_W // MLA_V
MLA_NOPE = 64
MLA_ROPE = 32
MLA_QK = MLA_NOPE + MLA_ROPE
MLA_Q_RANK = 384
MLA_KV_RANK = 256
MLA_SCALE = 1.0 / math.sqrt(MLA_NOPE + MLA_ROPE)
ROPE_BASE = 10000.0
HY_W = D_MODEL // 4
HY_ORDER = 2
HY_POS_EMB = 33
HY_FILTER_W = 64
HY_MIN_DECAY = math.log(1e-2) / 1.5
HY_MAX_DECAY = math.log(1e-2) / 0.3
MOE_GROUPS = 4
MOE_PER_GROUP = 8
MOE_EXPERTS = MOE_GROUPS * MOE_PER_GROUP
MOE_TOP_K = 2
MOE_HIDDEN = 512
MOE_BLOCK = 256

SUBLANES = 8
S5_CHUNK = 128
MM_ROWS = 256
ATTN_Q_BLOCK = 256
LANES = 128
ATTN_HEADS_PER_STEP = 4
VMEM_LIMIT = 48 * 1024 * 1024

F32 = jnp.float32
BF16 = jnp.bfloat16


def _rms_norm(x, g):
    xf = x.astype(F32)
    y = xf * lax.rsqrt(jnp.mean(xf * xf, axis=-1, keepdims=True) + EPS)
    return (y * g.astype(F32)).astype(x.dtype)


def _modulate(x, g, shift, scale):
    return _rms_norm(x, g) * (1.0 + scale) + shift


def _split_projection(p):
    o1 = S5_W
    o2 = o1 + MLA_Q_RANK
    o3 = o2 + MLA_KV_RANK
    o4 = o3 + MLA_ROPE
    return p[..., :o1], p[..., o1:o2], p[..., o2:o3], p[..., o3:o4], p[..., o4:]


def _s5_matrices(lam_re, lam_im, log_dt, b_re, b_im, c_re, c_im):
    g, p, h = S5_GROUPS, S5_STATE, S5_GROUP
    dt = jnp.exp(log_dt)[..., None]
    mag = jnp.exp(lam_re * dt)
    ar, ai = mag * jnp.cos(lam_im * dt), mag * jnp.sin(lam_im * dt)
    den = lam_re * lam_re + lam_im * lam_im
    fr = ((ar - 1.0) * lam_re + ai * lam_im) / den
    fi = (ai * lam_re - (ar - 1.0) * lam_im) / den
    bbr = fr[..., None] * b_re - fi[..., None] * b_im
    bbi = fr[..., None] * b_im + fi[..., None] * b_re
    eye = jnp.eye(g, dtype=F32)

    def block_in(m):
        return jnp.einsum('kgph,gj->kghjp', m, eye).reshape(2, g * h, g * p)

    def block_out(m):
        return jnp.einsum('kghp,gj->kgpjh', m, eye).reshape(2, g * p, g * h)

    bmat = jnp.concatenate([block_in(bbr), block_in(bbi)], axis=-1)
    cmat = jnp.concatenate([block_out(c_re), -block_out(c_im)], axis=1)
    lam = jnp.stack([ar[0], ai[0], ar[1], ai[1]]).reshape(4, 1, g * p)
    lam = jnp.broadcast_to(lam, (4, SUBLANES, g * p)).reshape(4 * SUBLANES, g * p)
    return bmat.astype(BF16), cmat.astype(BF16), lam


def _s5_kernel(uf_ref, ub_ref, bmat_ref, cmat_ref, lam_ref, yf_ref, yb_ref, buf_f, buf_b, state):
    rows = uf_ref.shape[0]
    steps = rows // SUBLANES
    n = S5_N
    s = SUBLANES

    @pl.when(pl.program_id(0) == 0)
    def _():
        state[...] = jnp.zeros_like(state)

    def drive(r, carry):
        r0 = pl.multiple_of(r * MM_ROWS, MM_ROWS)
        buf_f[pl.ds(r0, MM_ROWS), :] = jnp.dot(uf_ref[pl.ds(r0, MM_ROWS), :].astype(BF16), bmat_ref[0],
                                               preferred_element_type=F32)
        buf_b[pl.ds(r0, MM_ROWS), :] = jnp.dot(ub_ref[pl.ds(r0, MM_ROWS), :].astype(BF16), bmat_ref[1],
                                               preferred_element_type=F32)
        return carry

    lax.fori_loop(0, rows // MM_ROWS, drive, 0)

    def step(j, carry):
        fr, fi, br, bi = carry
        rf = pl.multiple_of(j * s, s)
        rb = pl.multiple_of((steps - 1 - j) * s, s)
        lfr, lfi = lam_ref[0:s, :], lam_ref[s:2 * s, :]
        lbr, lbi = lam_ref[2 * s:3 * s, :], lam_ref[3 * s:4 * s, :]
        nfr = lfr * fr - lfi * fi + buf_f[pl.ds(rf, s), 0:n]
        nfi = lfr * fi + lfi * fr + buf_f[pl.ds(rf, s), n:2 * n]
        nbr = lbr * br - lbi * bi + buf_b[pl.ds(rb, s), 0:n]
        nbi = lbr * bi + lbi * br + buf_b[pl.ds(rb, s), n:2 * n]
        buf_f[pl.ds(rf, s), 0:n] = nfr
        buf_f[pl.ds(rf, s), n:2 * n] = nfi
        buf_b[pl.ds(rb, s), 0:n] = nbr
        buf_b[pl.ds(rb, s), n:2 * n] = nbi
        return nfr, nfi, nbr, nbi

    init = (state[0:s, :], state[s:2 * s, :], state[2 * s:3 * s, :], state[3 * s:4 * s, :])
    fr, fi, br, bi = lax.fori_loop(0, steps, step, init, unroll=2)
    state[0:s, :] = fr
    state[s:2 * s, :] = fi
    state[2 * s:3 * s, :] = br
    state[3 * s:4 * s, :] = bi

    def readout(r, carry):
        r0 = pl.multiple_of(r * MM_ROWS, MM_ROWS)
        yf_ref[pl.ds(r0, MM_ROWS), :] = jnp.dot(buf_f[pl.ds(r0, MM_ROWS), :].astype(BF16), cmat_ref[0],
                                                preferred_element_type=F32)
        yb_ref[pl.ds(r0, MM_ROWS), :] = jnp.dot(buf_b[pl.ds(r0, MM_ROWS), :].astype(BF16), cmat_ref[1],
                                                preferred_element_type=F32)
        return carry

    lax.fori_loop(0, rows // MM_ROWS, readout, 0)


def _s5_scan(u_tm, bmat, cmat, lam, n_ctx_chunks, chunk):
    rows_all, w = u_tm.shape
    rows = chunk * SUBLANES
    n_chunks = rows_all // rows
    n2 = 2 * S5_N

    def bwd_chunk(i):
        return jnp.where(i < n_ctx_chunks, n_ctx_chunks - 1 - i, n_chunks - 1 + n_ctx_chunks - i)

    return pl.pallas_call(
        _s5_kernel,
        grid=(n_chunks,),
        in_specs=[
            pl.BlockSpec((rows, w), lambda i: (i, 0)),
            pl.BlockSpec((rows, w), lambda i: (bwd_chunk(i), 0)),
            pl.BlockSpec((2, w, n2), lambda i: (0, 0, 0)),
            pl.BlockSpec((2, n2, w), lambda i: (0, 0, 0)),
            pl.BlockSpec((4 * SUBLANES, S5_N), lambda i: (0, 0)),
        ],
        out_specs=[
            pl.BlockSpec((rows, w), lambda i: (i, 0)),
            pl.BlockSpec((rows, w), lambda i: (bwd_chunk(i), 0)),
        ],
        out_shape=[jax.ShapeDtypeStruct((rows_all, w), F32)] * 2,
        scratch_shapes=[
            pltpu.VMEM((rows, n2), F32),
            pltpu.VMEM((rows, n2), F32),
            pltpu.VMEM((4 * SUBLANES, S5_N), F32),
        ],
        compiler_params=pltpu.CompilerParams(dimension_semantics=("arbitrary",), vmem_limit_bytes=VMEM_LIMIT),
        name="s5_scan",
    )(u_tm, u_tm, bmat, cmat, lam)


def _s5_mixer(u_l, u_c, lam_re, lam_im, log_dt, b_re, b_im, c_re, c_im, d, glu_w, ctx_out):
    bsz, n, w = u_l.shape
    n_ctx = u_c.shape[1]
    assert bsz == SUBLANES and n % S5_CHUNK == 0 and n_ctx % S5_CHUNK == 0
    bmat, cmat, lam = _s5_matrices(lam_re, lam_im, log_dt, b_re, b_im, c_re, c_im)
    u_all = jnp.concatenate([u_c, u_l], axis=1)
    u_tm = u_all.swapaxes(0, 1).reshape((n_ctx + n) * bsz, w)
    y_f, y_b = _s5_scan(u_tm, bmat, cmat, lam, n_ctx // S5_CHUNK, S5_CHUNK)
    y = (y_f + y_b).reshape(n_ctx + n, bsz, w).swapaxes(0, 1)
    y = y + d * u_all
    y = jax.nn.gelu(y)
    y = y * jax.nn.sigmoid(y @ glu_w)
    return y[:, n_ctx:], (y[:, :n_ctx] if ctx_out else None)


def _axial_rope_tables(n_tokens):
    rows = n_tokens // GRID_W
    row = jnp.repeat(jnp.arange(rows), GRID_W).astype(F32)
    col = jnp.tile(jnp.arange(GRID_W), rows).astype(F32)
    half = MLA_ROPE // 2
    inv = ROPE_BASE ** (-jnp.arange(0, half, 2, dtype=F32) / half)
    ang_r = row[:, None] * inv
    ang_c = col[:, None] * inv
    return jnp.cos(ang_r), jnp.sin(ang_r), jnp.cos(ang_c), jnp.sin(ang_c)


def _rotate(x, cos, sin):
    m = x.shape[-1] // 2
    x1, x2 = x[..., :m], x[..., m:]
    return jnp.concatenate([x1 * cos - x2 * sin, x1 * sin + x2 * cos], axis=-1)


def _axial_rope(x, tables):
    cr, sr, cc, sc = (t[:, None, :].astype(x.dtype) for t in tables)
    half = MLA_ROPE // 2
    return jnp.concatenate([_rotate(x[..., :half], cr, sr), _rotate(x[..., half:], cc, sc)], axis=-1)


def _attn_kernel(q_ref, k_ref, v_ref, o_ref):
    dv = o_ref.shape[2] // q_ref.shape[1]
    outs = []
    for h in range(q_ref.shape[1]):
        s = lax.dot_general(q_ref[0, h], k_ref[0, h], (((1,), (1,)), ((), ())), preferred_element_type=F32)
        p = jnp.exp(s - jnp.max(s, axis=-1, keepdims=True))
        o = jnp.dot(p.astype(BF16), v_ref[0, h], preferred_element_type=F32)
        outs.append(o[:, :dv] / o[:, dv:dv + 1])
    o_ref[0] = jnp.concatenate(outs, axis=-1)


def _attention(q, k, v):
    bsz, heads, nq, dqk = q.shape
    nk, dv = k.shape[2], v.shape[3]
    tq = min(ATTN_Q_BLOCK, nq)
    hs = ATTN_HEADS_PER_STEP
    v_aug = jnp.concatenate([v, jnp.ones((bsz, heads, nk, 1), v.dtype),
                             jnp.zeros((bsz, heads, nk, LANES - dv - 1), v.dtype)], axis=-1)
    return pl.pallas_call(
        _attn_kernel,
        grid=(bsz, heads // hs, nq // tq),
        in_specs=[
            pl.BlockSpec((1, hs, tq, dqk), lambda b, h, i: (b, h, i, 0)),
            pl.BlockSpec((1, hs, nk, dqk), lambda b, h, i: (b, h, 0, 0)),
            pl.BlockSpec((1, hs, nk, LANES), lambda b, h, i: (b, h, 0, 0)),
        ],
        out_specs=pl.BlockSpec((1, tq, hs * dv), lambda b, h, i: (b, i, h)),
        out_shape=jax.ShapeDtypeStruct((bsz, nq, heads * dv), F32),
        compiler_params=pltpu.CompilerParams(
            dimension_semantics=("arbitrary", "arbitrary", "arbitrary"), vmem_limit_bytes=VMEM_LIMIT),
        name="mla_attention",
    )(q, k, v_aug)


def _mla_mixer(cq_l, ckv_l, kr_l, cq_c, ckv_c, kr_c, q_norm_g, kv_norm_g, w_uq, w_ukv, ctx_out):
    def queries(cq):
        q = (_rms_norm(cq, q_norm_g) @ w_uq).reshape(cq.shape[0], cq.shape[1], MLA_HEADS, MLA_QK)
        return q[..., :MLA_NOPE], q[..., MLA_NOPE:]

    def keys_values(ckv):
        kv = (_rms_norm(ckv, kv_norm_g) @ w_ukv).reshape(ckv.shape[0], ckv.shape[1], MLA_HEADS, MLA_NOPE + MLA_V)
        return kv[..., :MLA_NOPE], kv[..., MLA_NOPE:]

    def head_major_q(qn, qr):
        q = jnp.concatenate([qn, qr], axis=-1) * MLA_SCALE
        return q.swapaxes(1, 2).astype(BF16)

    def head_major_k(kn, kr):
        kr = jnp.broadcast_to(kr[:, :, None, :], kn.shape[:3] + (MLA_ROPE,))
        return jnp.concatenate([kn, kr], axis=-1).swapaxes(1, 2).astype(BF16)

    bsz, n = cq_l.shape[0], cq_l.shape[1]
    n_ctx = kr_c.shape[1]
    tables = _axial_rope_tables(n)
    kn_c, v_c = keys_values(ckv_c)
    kn_l, v_l = keys_values(ckv_l)
    kr_lat = _axial_rope(kr_l[:, :, None, :], tables)[:, :, 0]
    qn_l, qr_l = queries(cq_l)
    qr_l = _axial_rope(qr_l, tables)
    k_nope = jnp.concatenate([kn_c, kn_l], axis=1)
    k_rope = jnp.concatenate([kr_c, kr_lat], axis=1)
    v = jnp.concatenate([v_c, v_l], axis=1)
    k_hm = head_major_k(k_nope, k_rope)
    v_hm = v.swapaxes(1, 2).astype(BF16)
    y_l = _attention(head_major_q(qn_l, qr_l), k_hm, v_hm)
    y_c = None
    if ctx_out:
        qn_c, qr_c = queries(cq_c)
        y_c = _attention(head_major_q(qn_c, qr_c), k_hm[:, :, :n_ctx], v_hm[:, :, :n_ctx])
    return y_l, y_c


def _short_conv(z, w, b):
    n = z.shape[1]
    zp = jnp.pad(z, ((0, 0), (1, 1), (0, 0)))
    return zp[:, :n] * w[0] + zp[:, 1:n + 1] * w[1] + zp[:, 2:] * w[2] + b


def _hyena_filter_taps(n, w1, b1, w2, b2, w3, b3, freq):
    t = jnp.linspace(0.0, 1.0, n, dtype=F32)[:, None]
    bands = (HY_POS_EMB - 1) // 2
    f = jnp.linspace(1e-4, bands - 1, bands, dtype=F32)
    ang = (2.0 * math.pi * jnp.arange(n, dtype=F32) / n)[:, None] * f
    z = jnp.concatenate([t, jnp.cos(ang), -jnp.sin(ang)], axis=-1)
    h = jnp.sin(freq[0] * (z @ w1 + b1))
    h = jnp.sin(freq[1] * (h @ w2 + b2))
    h = (h @ w3 + b3).astype(F32).reshape(n, 2, HY_ORDER, HY_W)
    deltas = jnp.abs(jnp.linspace(HY_MIN_DECAY, HY_MAX_DECAY, HY_W, dtype=F32))
    h = h * jnp.exp(-t[:, :, None, None] * deltas)
    fwd, bwd = h[:, 0], h[:, 1]
    return jnp.concatenate([fwd, jnp.zeros((1, HY_ORDER, HY_W), F32), bwd[:0:-1]], axis=0)


def _fft_long_conv(y, k_f):
    n = y.shape[1]
    yf = jnp.fft.rfft(y, n=2 * n, axis=1)
    return jnp.fft.irfft(yf * k_f, n=2 * n, axis=1)[:, :n]


def _hyena_mixer(z, short_w, short_b, filt, bias):
    n = z.shape[1]
    z = _short_conv(z, short_w, short_b)
    v, x1, x2 = jnp.split(z, 3, axis=-1)
    y = v.astype(F32)
    if (2 * n) % HY_N2 == 0 and (2 * n) // HY_N2 >= 2 * SUBLANES:
        k_time = _hyena_filter_taps(n, *filt)
        tables = _dft_tables(2 * n)
        kr, ki = _hyena_spectrum(k_time.reshape(2 * n, HY_ORDER * HY_W), tables)
        for o, gate in enumerate((x1, x2)):
            cols = slice(o * HY_W, (o + 1) * HY_W)
            y = gate.astype(F32) * (_hyena_long_conv(y, kr[:, cols], ki[:, cols], tables) + y * bias[o].astype(F32))
    else:
        k_f = jnp.fft.rfft(_hyena_filter_taps(n, *filt), axis=0)
        for o, gate in enumerate((x1, x2)):
            y = gate.astype(F32) * (_fft_long_conv(y, k_f[:, o]) + y * bias[o].astype(F32))
    return y.astype(z.dtype)


HY_N2 = 128
HY_PITCH = 136
HY_UNROLL = 8


def _dft_tables(nfft):
    n1 = nfft // HY_N2
    half = n1 // 2
    k1 = jnp.arange(n1, dtype=jnp.int32)
    n2 = jnp.arange(HY_N2, dtype=jnp.int32)
    t = HY_N2 * k1[None, None, :] + n2[:, None, None]
    ang = (2.0 * math.pi / nfft) * ((k1[None, :, None] * t) % nfft).astype(F32)
    gr, gi = jnp.cos(ang), -jnp.sin(ang)
    g_cplx = jnp.concatenate([jnp.concatenate([gr[..., :half], -gi[..., :half]], -1),
                              jnp.concatenate([gi[..., :half], gr[..., :half]], -1)], axis=1)
    g_real = jnp.concatenate([gr, gi], axis=1)
    hr = jnp.cos(ang).swapaxes(1, 2)[:, :half] / nfft
    hi = jnp.sin(ang).swapaxes(1, 2)[:, :half] / nfft
    g_inv = jnp.concatenate([jnp.concatenate([hr, -hi], -1), jnp.concatenate([hi, hr], -1)], axis=1)
    a2 = (2.0 * math.pi / HY_N2) * ((n2[:, None] * n2[None, :]) % HY_N2).astype(F32)
    fr, fi = jnp.cos(a2), -jnp.sin(a2)
    f2 = jnp.concatenate([jnp.concatenate([fr, -fi], -1), jnp.concatenate([fi, fr], -1)], axis=0)
    f2_inv = jnp.concatenate([jnp.concatenate([fr, fi], -1), jnp.concatenate([-fi, fr], -1)], axis=0)
    return dict(n1=n1, g_cplx=g_cplx.astype(BF16), g_real=g_real.astype(BF16), g_inv=g_inv.astype(BF16),
                f2=f2.astype(BF16), f2_inv=f2_inv.astype(BF16))


def _dft_stage1(load_rows, g_ref, s_re, s_im, n1):
    def body(n2, carry):
        a = jnp.dot(g_ref[n2], load_rows(n2).astype(BF16), preferred_element_type=F32)
        s_re[pl.ds(n2, n1, stride=HY_PITCH), :] = a[:n1]
        s_im[pl.ds(n2, n1, stride=HY_PITCH), :] = a[n1:]
        return carry

    lax.fori_loop(0, HY_N2, body, 0, unroll=HY_UNROLL)


def _dft_stage2(f2_ref, s_re, s_im, k1):
    r0 = pl.multiple_of(k1 * HY_PITCH, SUBLANES)
    sl = jnp.concatenate([s_re[pl.ds(r0, HY_N2), :], s_im[pl.ds(r0, HY_N2), :]], axis=0).astype(BF16)
    x = jnp.dot(f2_ref[...], sl, preferred_element_type=F32)
    return r0, x[:HY_N2], x[HY_N2:]


def _hyena_spectrum_kernel(x_ref, g_ref, f2_ref, kr_ref, ki_ref, s_re, s_im, *, n1):
    _dft_stage1(lambda n2: x_ref[pl.ds(n2, n1, stride=HY_PITCH), :], g_ref, s_re, s_im, n1)

    def body(k1, carry):
        _, xr, xi = _dft_stage2(f2_ref, s_re, s_im, k1)
        q0 = pl.multiple_of(k1 * HY_N2, HY_N2)
        kr_ref[pl.ds(q0, HY_N2), :] = xr
        ki_ref[pl.ds(q0, HY_N2), :] = xi
        return carry

    lax.fori_loop(0, n1, body, 0)


def _hyena_conv_kernel(xr_ref, xi_ref, g_ref, f2_ref, f2i_ref, gi_ref, kr_ref, ki_ref, yr_ref, yi_ref,
                       s_re, s_im, *, n1):
    half = n1 // 2

    def load_rows(n2):
        return jnp.concatenate([xr_ref[0, pl.ds(n2, half, stride=HY_PITCH), :],
                                xi_ref[0, pl.ds(n2, half, stride=HY_PITCH), :]], axis=0)

    _dft_stage1(load_rows, g_ref, s_re, s_im, n1)

    def spectrum_product(k1, carry):
        r0, xr, xi = _dft_stage2(f2_ref, s_re, s_im, k1)
        q0 = pl.multiple_of(k1 * HY_N2, HY_N2)
        kr, ki = kr_ref[pl.ds(q0, HY_N2), :], ki_ref[pl.ds(q0, HY_N2), :]
        y = jnp.concatenate([xr * kr - xi * ki, xr * ki + xi * kr], axis=0).astype(BF16)
        b = jnp.dot(f2i_ref[...], y, preferred_element_type=F32)
        s_re[pl.ds(r0, HY_N2), :] = b[:HY_N2]
        s_im[pl.ds(r0, HY_N2), :] = b[HY_N2:]
        return carry

    lax.fori_loop(0, n1, spectrum_product, 0, unroll=HY_UNROLL // 2)

    yr_ref[...] = jnp.zeros_like(yr_ref)
    yi_ref[...] = jnp.zeros_like(yi_ref)

    def inverse_stage1(n2, carry):
        bs = jnp.concatenate([s_re[pl.ds(n2, n1, stride=HY_PITCH), :],
                              s_im[pl.ds(n2, n1, stride=HY_PITCH), :]], axis=0).astype(BF16)
        y = jnp.dot(gi_ref[n2], bs, preferred_element_type=F32)
        yr_ref[0, pl.ds(n2, half, stride=HY_PITCH), :] = y[:half]
        yi_ref[0, pl.ds(n2, half, stride=HY_PITCH), :] = y[half:]
        return carry

    lax.fori_loop(0, HY_N2, inverse_stage1, 0, unroll=HY_UNROLL)


def _to_strided(a, rows):
    lead, c = a.shape[:-2], a.shape[-1]
    a = a.reshape(lead + (rows, HY_N2, c))
    a = jnp.pad(a, [(0, 0)] * len(lead) + [(0, 0), (0, HY_PITCH - HY_N2), (0, 0)])
    return a.reshape(lead + (rows * HY_PITCH, c))


def _from_strided(a, rows):
    lead, c = a.shape[:-2], a.shape[-1]
    return a.reshape(lead + (rows, HY_PITCH, c))[..., :HY_N2, :].reshape(lead + (rows * HY_N2, c))


def _const_spec(shape):
    return pl.BlockSpec(shape, lambda *_: (0,) * len(shape), pipeline_mode=pl.Buffered(1))


def _hyena_spectrum(k_time, tables):
    nfft, c = k_time.shape
    n1 = tables["n1"]
    spec_out = pl.BlockSpec((nfft, LANES), lambda j: (0, j))
    return pl.pallas_call(
        functools.partial(_hyena_spectrum_kernel, n1=n1),
        grid=(c // LANES,),
        in_specs=[pl.BlockSpec((n1 * HY_PITCH, LANES), lambda j: (0, j)),
                  _const_spec((HY_N2, 2 * n1, n1)), _const_spec((2 * HY_N2, 2 * HY_N2))],
        out_specs=[spec_out, spec_out],
        out_shape=[jax.ShapeDtypeStruct((nfft, c), F32)] * 2,
        scratch_shapes=[pltpu.VMEM((n1 * HY_PITCH, LANES), F32)] * 2,
        compiler_params=pltpu.CompilerParams(dimension_semantics=("arbitrary",), vmem_limit_bytes=VMEM_LIMIT),
        name="hyena_spectrum",
    )(_to_strided(k_time, n1), tables["g_real"], tables["f2"])


def _hyena_long_conv(y, kr, ki, tables):
    bsz, n, c = y.shape
    n1 = tables["n1"]
    half = n1 // 2
    pairs = bsz // 2
    xr, xi = _to_strided(y[0::2], half), _to_strided(y[1::2], half)
    spec_x = pl.BlockSpec((1, half * HY_PITCH, LANES), lambda j, p: (p, 0, j))
    spec_k = pl.BlockSpec((n1 * HY_N2, LANES), lambda j, p: (0, j), pipeline_mode=pl.Buffered(1))
    yr, yi = pl.pallas_call(
        functools.partial(_hyena_conv_kernel, n1=n1),
        grid=(c // LANES, pairs),
        in_specs=[spec_x, spec_x,
                  _const_spec((HY_N2, 2 * n1, n1)), _const_spec((2 * HY_N2, 2 * HY_N2)),
                  _const_spec((2 * HY_N2, 2 * HY_N2)), _const_spec((HY_N2, n1, 2 * n1)),
                  spec_k, spec_k],
        out_specs=[spec_x, spec_x],
        out_shape=[jax.ShapeDtypeStruct((pairs, half * HY_PITCH, c), F32)] * 2,
        scratch_shapes=[pltpu.VMEM((n1 * HY_PITCH, LANES), F32)] * 2,
        compiler_params=pltpu.CompilerParams(dimension_semantics=("arbitrary", "arbitrary"),
                                             vmem_limit_bytes=VMEM_LIMIT),
        name="hyena_conv",
    )(xr, xi, tables["g_cplx"], tables["f2"], tables["f2_inv"], tables["g_inv"], kr, ki)
    out = jnp.stack([_from_strided(yr, half), _from_strided(yi, half)], axis=1)
    return out.reshape(bsz, n, c)


def _merge_groups(y_s5, y_mla, y_hy, g):
    return jnp.concatenate([
        _rms_norm(y_s5, g[:S5_W]),
        _rms_norm(y_mla, g[S5_W:S5_W + MLA_W]),
        _rms_norm(y_hy, g[S5_W + MLA_W:]),
    ], axis=-1)


def _moe_kernel(be_ref, x_ref, w1_ref, w3_ref, w2_ref, rw_ref, o_ref):
    del be_ref
    x = x_ref[...]
    a = jnp.dot(x, w1_ref[0], preferred_element_type=F32)
    b = jnp.dot(x, w3_ref[0], preferred_element_type=F32)
    h = (a * jax.nn.sigmoid(a)) * b
    y = jnp.dot(h.astype(BF16), w2_ref[0], preferred_element_type=F32)
    o_ref[...] = y * rw_ref[...]


def _moe_experts(xg, block_e, w1, w3, w2, row_w):
    n_rows, d = xg.shape
    hid = w1.shape[2]
    n_blocks = n_rows // MOE_BLOCK
    return pl.pallas_call(
        _moe_kernel,
        grid_spec=pltpu.PrefetchScalarGridSpec(
            num_scalar_prefetch=1,
            grid=(n_blocks,),
            in_specs=[
                pl.BlockSpec((MOE_BLOCK, d), lambda i, be: (i, 0)),
                pl.BlockSpec((1, d, hid), lambda i, be: (be[i], 0, 0)),
                pl.BlockSpec((1, d, hid), lambda i, be: (be[i], 0, 0)),
                pl.BlockSpec((1, hid, d), lambda i, be: (be[i], 0, 0)),
                pl.BlockSpec((MOE_BLOCK, 1), lambda i, be: (i, 0)),
            ],
            out_specs=pl.BlockSpec((MOE_BLOCK, d), lambda i, be: (i, 0)),
        ),
        out_shape=jax.ShapeDtypeStruct((n_rows, d), F32),
        compiler_params=pltpu.CompilerParams(dimension_semantics=("arbitrary",), vmem_limit_bytes=VMEM_LIMIT),
        name="moe_experts",
    )(block_e, xg, w1, w3, w2, row_w)


def _hier_moe(h, w_group, w_expert, w1, w3, w2):
    t, d = h.shape
    g_prob = jax.nn.softmax((h @ w_group).astype(F32), axis=-1)
    g_idx = jnp.argmax(g_prob, axis=-1).astype(jnp.int32)
    g_w = jnp.max(g_prob, axis=-1, keepdims=True)
    e_logits = (h @ w_expert).astype(F32).reshape(t, MOE_GROUPS, MOE_PER_GROUP)
    in_group = jnp.take_along_axis(e_logits, g_idx[:, None, None], axis=1)[:, 0]
    i1 = jnp.argmax(in_group, axis=-1).astype(jnp.int32)
    rest = jnp.where(jnp.arange(MOE_PER_GROUP, dtype=jnp.int32)[None, :] == i1[:, None], -jnp.inf, in_group)
    i2 = jnp.argmax(rest, axis=-1).astype(jnp.int32)
    top_v = jnp.stack([jnp.max(in_group, axis=-1), jnp.max(rest, axis=-1)], axis=-1)
    top_i = jnp.stack([i1, i2], axis=-1)
    gate = jax.nn.softmax(top_v, axis=-1) * g_w
    eid = g_idx[:, None] * MOE_PER_GROUP + top_i
    n_assign = t * MOE_TOP_K
    flat_e = eid.reshape(n_assign)
    flat_w = gate.reshape(n_assign)
    se, order = lax.sort((flat_e, jnp.arange(n_assign, dtype=jnp.int32)), num_keys=1)
    counts = jnp.sum(flat_e[None, :] == jnp.arange(MOE_EXPERTS, dtype=jnp.int32)[:, None], axis=1, dtype=jnp.int32)
    padded = (counts + MOE_BLOCK - 1) // MOE_BLOCK * MOE_BLOCK
    start = jnp.cumsum(counts) - counts
    pend = jnp.cumsum(padded)
    pstart = pend - padded
    dest = (pstart[se] + jnp.arange(n_assign, dtype=jnp.int32) - start[se]).astype(jnp.int32)
    n_blocks = -(-n_assign // MOE_BLOCK) + MOE_EXPERTS
    n_rows = n_blocks * MOE_BLOCK
    block_start = jnp.arange(n_blocks, dtype=jnp.int32) * MOE_BLOCK
    block_e = jnp.minimum(jnp.sum(pend[None, :] <= block_start[:, None], axis=1), MOE_EXPERTS - 1).astype(jnp.int32)
    row_e = jnp.repeat(block_e, MOE_BLOCK)
    off = jnp.arange(n_rows, dtype=jnp.int32) - pstart[row_e]
    valid = off < counts[row_e]
    row_asg = order[jnp.where(valid, start[row_e] + off, 0)]
    row_tok = jnp.where(valid, row_asg // MOE_TOP_K, 0)
    row_w = jnp.where(valid, flat_w[row_asg], 0.0)
    xg = h.astype(BF16)[row_tok]
    ys = _moe_experts(xg, block_e, w1.astype(BF16), w3.astype(BF16), w2.astype(BF16), row_w[:, None])
    _, slot = lax.sort((order, dest), num_keys=1)
    slot = slot.reshape(t, MOE_TOP_K)
    return ys[slot[:, 0]] + ys[slot[:, 1]]


def _final_norm_kernel(x_ref, g_ref, o_ref):
    x = x_ref[...]
    y = x * lax.rsqrt(jnp.mean(x * x, axis=-1, keepdims=True) + EPS)
    o_ref[...] = y * g_ref[...]


def _final_norm(x, g):
    t, d = x.shape
    blk = 1024
    return pl.pallas_call(
        _final_norm_kernel,
        grid=(t // blk,),
        in_specs=[pl.BlockSpec((blk, d), lambda i: (i, 0)), pl.BlockSpec((1, d), lambda i: (0, 0))],
        out_specs=pl.BlockSpec((blk, d), lambda i: (i, 0)),
        out_shape=jax.ShapeDtypeStruct((t, d), F32),
        name="final_norm",
    )(x, g.reshape(1, d))


def kernel(x, c, ctx, c_ctx, ada_w, ada_b, norm1_g, norm2_g, w_in,
           s5_lambda_re, s5_lambda_im, s5_log_dt, s5_b_re, s5_b_im, s5_c_re, s5_c_im, s5_d, s5_glu_w,
           mla_q_norm_g, mla_kv_norm_g, mla_w_uq, mla_w_ukv,
           hy_short_w, hy_short_b, hy_f_w1, hy_f_b1, hy_f_w2, hy_f_b2, hy_f_w3, hy_f_b3, hy_f_freq, hy_bias,
           mix_norm_g, w_out, moe_w_group, moe_w_expert, moe_w1, moe_w3, moe_w2, final_g):
    bsz, n, d = x.shape
    n_ctx = ctx.shape[1]
    xl, xc = x, ctx
    act_l = jax.nn.silu(c)
    act_c = jax.nn.silu(c_ctx)
    for i in range(DEPTH):
        ctx_out = i < DEPTH - 1
        mod_l = jnp.split((act_l @ ada_w[i] + ada_b[i])[:, None, :], 6, axis=-1)
        mod_c = jnp.split((act_c @ ada_w[i] + ada_b[i])[None, None, :], 6, axis=-1)
        hl = _modulate(xl, norm1_g[i], mod_l[0], mod_l[1])
        hc = _modulate(xc, norm1_g[i], mod_c[0], mod_c[1])
        u_l, cq_l, ckv_l, kr_l, hz_l = _split_projection(hl @ w_in[i])
        u_c, cq_c, ckv_c, kr_c, hz_c = _split_projection(hc @ w_in[i])
        s5_l, s5_c = _s5_mixer(u_l, u_c, s5_lambda_re[i], s5_lambda_im[i], s5_log_dt[i], s5_b_re[i], s5_b_im[i],
                               s5_c_re[i], s5_c_im[i], s5_d[i], s5_glu_w[i], ctx_out)
        mla_l, mla_c = _mla_mixer(cq_l, ckv_l, kr_l, cq_c, ckv_c, kr_c, mla_q_norm_g[i], mla_kv_norm_g[i],
                                  mla_w_uq[i], mla_w_ukv[i], ctx_out)
        filt = (hy_f_w1[i], hy_f_b1[i], hy_f_w2[i], hy_f_b2[i], hy_f_w3[i], hy_f_b3[i], hy_f_freq[i])
        hy_l = _hyena_mixer(hz_l, hy_short_w[i], hy_short_b[i], filt, hy_bias[i])
        xl = xl + mod_l[2] * (_merge_groups(s5_l, mla_l, hy_l, mix_norm_g[i]) @ w_out[i])
        fl = _modulate(xl, norm2_g[i], mod_l[3], mod_l[4]).reshape(bsz * n, d)
        if ctx_out:
            hy_c = _hyena_mixer(hz_c, hy_short_w[i], hy_short_b[i], filt, hy_bias[i])
            xc = xc + mod_c[2] * (_merge_groups(s5_c, mla_c, hy_c, mix_norm_g[i]) @ w_out[i])
            fc = _modulate(xc, norm2_g[i], mod_c[3], mod_c[4]).reshape(bsz * n_ctx, d)
            y = _hier_moe(jnp.concatenate([fl, fc], axis=0), moe_w_group[i], moe_w_expert[i],
                          moe_w1[i], moe_w3[i], moe_w2[i])
            xl = xl + mod_l[5] * y[:bsz * n].reshape(bsz, n, d)
            xc = xc + mod_c[5] * y[bsz * n:].reshape(bsz, n_ctx, d)
        else:
            y = _hier_moe(fl, moe_w_group[i], moe_w_expert[i], moe_w1[i], moe_w3[i], moe_w2[i])
            xl = xl + mod_l[5] * y.reshape(bsz, n, d)
    return _final_norm(xl.reshape(bsz * n, d), final_g).reshape(bsz, n, d)
```

```python
import functools
import math

import jax
import jax.numpy as jnp
from jax import lax
from jax.experimental import pallas as pl
from jax.experimental.pallas import tpu as pltpu

D_MODEL = 1024
DEPTH = 4
GRID_W = 64
EPS = 1e-6

MIX_W = D_MODEL
S5_W = D_MODEL // 4
S5_GROUP = 16
S5_GROUPS = S5_W // S5_GROUP
S5_STATE = 64
S5_N = S5_GROUPS * S5_STATE
MLA_V = 64
MLA_W = D_MODEL // 2
MLA_HEADS = MLA_W // MLA_V
MLA_NOPE = 64
MLA_ROPE = 32
MLA_QK = MLA_NOPE + MLA_ROPE
MLA_Q_RANK = 384
MLA_KV_RANK = 256
MLA_SCALE = 1.0 / math.sqrt(MLA_NOPE + MLA_ROPE)
ROPE_BASE = 10000.0
HY_W = D_MODEL // 4
HY_ORDER = 2
HY_POS_EMB = 33
HY_FILTER_W = 64
HY_MIN_DECAY = math.log(1e-2) / 1.5
HY_MAX_DECAY = math.log(1e-2) / 0.3
MOE_GROUPS = 4
MOE_PER_GROUP = 8
MOE_EXPERTS = MOE_GROUPS * MOE_PER_GROUP
MOE_TOP_K = 2
MOE_HIDDEN = 512
MOE_BLOCK = 256

SUBLANES = 8
S5_CHUNK = 128
MM_ROWS = 256
ATTN_Q_BLOCK = 256
LANES = 128
ATTN_HEADS_PER_STEP = 4
VMEM_LIMIT = 48 * 1024 * 1024

F32 = jnp.float32
BF16 = jnp.bfloat16


def _rms_norm(x, g):
    xf = x.astype(F32)
    y = xf * lax.rsqrt(jnp.mean(xf * xf, axis=-1, keepdims=True) + EPS)
    return (y * g.astype(F32)).astype(x.dtype)


def _modulate(x, g, shift, scale):
    return _rms_norm(x, g) * (1.0 + scale) + shift


def _split_projection(p):
    o1 = S5_W
    o2 = o1 + MLA_Q_RANK
    o3 = o2 + MLA_KV_RANK
    o4 = o3 + MLA_ROPE
    return p[..., :o1], p[..., o1:o2], p[..., o2:o3], p[..., o3:o4], p[..., o4:]


def _s5_matrices(lam_re, lam_im, log_dt, b_re, b_im, c_re, c_im):
    g, p, h = S5_GROUPS, S5_STATE, S5_GROUP
    dt = jnp.exp(log_dt)[..., None]
    mag = jnp.exp(lam_re * dt)
    ar, ai = mag * jnp.cos(lam_im * dt), mag * jnp.sin(lam_im * dt)
    den = lam_re * lam_re + lam_im * lam_im
    fr = ((ar - 1.0) * lam_re + ai * lam_im) / den
    fi = (ai * lam_re - (ar - 1.0) * lam_im) / den
    bbr = fr[..., None] * b_re - fi[..., None] * b_im
    bbi = fr[..., None] * b_im + fi[..., None] * b_re
    eye = jnp.eye(g, dtype=F32)

    def block_in(m):
        return jnp.einsum('kgph,gj->kghjp', m, eye).reshape(2, g * h, g * p)

    def block_out(m):
        return jnp.einsum('kghp,gj->kgpjh', m, eye).reshape(2, g * p, g * h)

    bmat = jnp.concatenate([block_in(bbr), block_in(bbi)], axis=-1)
    cmat = jnp.concatenate([block_out(c_re), -block_out(c_im)], axis=1)
    lam = jnp.stack([ar[0], ai[0], ar[1], ai[1]]).reshape(4, 1, g * p)
    lam = jnp.broadcast_to(lam, (4, SUBLANES, g * p)).reshape(4 * SUBLANES, g * p)
    return bmat.astype(BF16), cmat.astype(BF16), lam


def _s5_kernel(uf_ref, ub_ref, bmat_ref, cmat_ref, lam_ref, yf_ref, yb_ref, buf_f, buf_b, state):
    rows = uf_ref.shape[0]
    steps = rows // SUBLANES
    n = S5_N
    s = SUBLANES

    @pl.when(pl.program_id(0) == 0)
    def _():
        state[...] = jnp.zeros_like(state)

    def drive(r, carry):
        r0 = pl.multiple_of(r * MM_ROWS, MM_ROWS)
        buf_f[pl.ds(r0, MM_ROWS), :] = jnp.dot(uf_ref[pl.ds(r0, MM_ROWS), :].astype(BF16), bmat_ref[0],
                                               preferred_element_type=F32)
        buf_b[pl.ds(r0, MM_ROWS), :] = jnp.dot(ub_ref[pl.ds(r0, MM_ROWS), :].astype(BF16), bmat_ref[1],
                                               preferred_element_type=F32)
        return carry

    lax.fori_loop(0, rows // MM_ROWS, drive, 0)

    def step(j, carry):
        fr, fi, br, bi = carry
        rf = pl.multiple_of(j * s, s)
        rb = pl.multiple_of((steps - 1 - j) * s, s)
        lfr, lfi = lam_ref[0:s, :], lam_ref[s:2 * s, :]
        lbr, lbi = lam_ref[2 * s:3 * s, :], lam_ref[3 * s:4 * s, :]
        nfr = lfr * fr - lfi * fi + buf_f[pl.ds(rf, s), 0:n]
        nfi = lfr * fi + lfi * fr + buf_f[pl.ds(rf, s), n:2 * n]
        nbr = lbr * br - lbi * bi + buf_b[pl.ds(rb, s), 0:n]
        nbi = lbr * bi + lbi * br + buf_b[pl.ds(rb, s), n:2 * n]
        buf_f[pl.ds(rf, s), 0:n] = nfr
        buf_f[pl.ds(rf, s), n:2 * n] = nfi
        buf_b[pl.ds(rb, s), 0:n] = nbr
        buf_b[pl.ds(rb, s), n:2 * n] = nbi
        return nfr, nfi, nbr, nbi

    init = (state[0:s, :], state[s:2 * s, :], state[2 * s:3 * s, :], state[3 * s:4 * s, :])
    fr, fi, br, bi = lax.fori_loop(0, steps, step, init, unroll=2)
    state[0:s, :] = fr
    state[s:2 * s, :] = fi
    state[2 * s:3 * s, :] = br
    state[3 * s:4 * s, :] = bi

    def readout(r, carry):
        r0 = pl.multiple_of(r * MM_ROWS, MM_ROWS)
        yf_ref[pl.ds(r0, MM_ROWS), :] = jnp.dot(buf_f[pl.ds(r0, MM_ROWS), :].astype(BF16), cmat_ref[0],
                                                preferred_element_type=F32)
        yb_ref[pl.ds(r0, MM_ROWS), :] = jnp.dot(buf_b[pl.ds(r0, MM_ROWS), :].astype(BF16), cmat_ref[1],
                                                preferred_element_type=F32)
        return carry

    lax.fori_loop(0, rows // MM_ROWS, readout, 0)


def _s5_scan(u_tm, bmat, cmat, lam, n_ctx_chunks, chunk):
    rows_all, w = u_tm.shape
    rows = chunk * SUBLANES
    n_chunks = rows_all // rows
    n2 = 2 * S5_N

    def bwd_chunk(i):
        return jnp.where(i < n_ctx_chunks, n_ctx_chunks - 1 - i, n_chunks - 1 + n_ctx_chunks - i)

    return pl.pallas_call(
        _s5_kernel,
        grid=(n_chunks,),
        in_specs=[
            pl.BlockSpec((rows, w), lambda i: (i, 0)),
            pl.BlockSpec((rows, w), lambda i: (bwd_chunk(i), 0)),
            pl.BlockSpec((2, w, n2), lambda i: (0, 0, 0)),
            pl.BlockSpec((2, n2, w), lambda i: (0, 0, 0)),
            pl.BlockSpec((4 * SUBLANES, S5_N), lambda i: (0, 0)),
        ],
        out_specs=[
            pl.BlockSpec((rows, w), lambda i: (i, 0)),
            pl.BlockSpec((rows, w), lambda i: (bwd_chunk(i), 0)),
        ],
        out_shape=[jax.ShapeDtypeStruct((rows_all, w), F32)] * 2,
        scratch_shapes=[
            pltpu.VMEM((rows, n2), F32),
            pltpu.VMEM((rows, n2), F32),
            pltpu.VMEM((4 * SUBLANES, S5_N), F32),
        ],
        compiler_params=pltpu.CompilerParams(dimension_semantics=("arbitrary",), vmem_limit_bytes=VMEM_LIMIT),
        name="s5_scan",
    )(u_tm, u_tm, bmat, cmat, lam)


def _s5_mixer(u_l, u_c, lam_re, lam_im, log_dt, b_re, b_im, c_re, c_im, d, glu_w, ctx_out):
    bsz, n, w = u_l.shape
    n_ctx = u_c.shape[1]
    assert bsz == SUBLANES and n % S5_CHUNK == 0 and n_ctx % S5_CHUNK == 0
    bmat, cmat, lam = _s5_matrices(lam_re, lam_im, log_dt, b_re, b_im, c_re, c_im)
    u_all = jnp.concatenate([u_c, u_l], axis=1)
    u_tm = u_all.swapaxes(0, 1).reshape((n_ctx + n) * bsz, w)
    y_f, y_b = _s5_scan(u_tm, bmat, cmat, lam, n_ctx // S5_CHUNK, S5_CHUNK)
    y = (y_f + y_b).reshape(n_ctx + n, bsz, w).swapaxes(0, 1)
    y = y + d * u_all
    y = jax.nn.gelu(y)
    y = y * jax.nn.sigmoid(y @ glu_w)
    return y[:, n_ctx:], (y[:, :n_ctx] if ctx_out else None)


def _axial_rope_tables(n_tokens):
    rows = n_tokens // GRID_W
    row = jnp.repeat(jnp.arange(rows), GRID_W).astype(F32)
    col = jnp.tile(jnp.arange(GRID_W), rows).astype(F32)
    half = MLA_ROPE // 2
    inv = ROPE_BASE ** (-jnp.arange(0, half, 2, dtype=F32) / half)
    ang_r = row[:, None] * inv
    ang_c = col[:, None] * inv
    cos = jnp.concatenate([jnp.cos(ang_r)] * 2 + [jnp.cos(ang_c)] * 2, axis=-1)
    sin = jnp.concatenate([jnp.sin(ang_r)] * 2 + [jnp.sin(ang_c)] * 2, axis=-1)
    return cos, sin


def _rope_partner(w):
    q = MLA_ROPE // 4
    parts = []
    for g in range(2):
        x1, x2 = w[..., 2 * g * q:(2 * g + 1) * q], w[..., (2 * g + 1) * q:(2 * g + 2) * q]
        parts += [-x2, x1]
    return jnp.concatenate(parts, axis=-1)


def _attn_kernel(q_ref, kv_ref, kr_ref, o_ref, kcat, vaug):
    hs, nk = kcat.shape[0], kcat.shape[1]
    dn = MLA_NOPE

    @pl.when(pl.program_id(2) == 0)
    def _():
        lane = lax.broadcasted_iota(jnp.int32, (nk, LANES), 1)
        ones_col = jnp.where(lane == 0, 1.0, 0.0).astype(BF16)
        for h in range(hs):
            kvh = kv_ref[0, :, h * LANES:(h + 1) * LANES]
            kcat[h] = jnp.where(lane < dn, kvh, kr_ref[0])
            vaug[h] = jnp.where(lane >= dn, kvh, ones_col)

    outs = []
    for h in range(hs):
        s = lax.dot_general(q_ref[0, :, h * LANES:(h + 1) * LANES], kcat[h], (((1,), (1,)), ((), ())),
                            preferred_element_type=F32)
        p = jnp.exp(s - jnp.max(s, axis=-1, keepdims=True))
        o = jnp.dot(p.astype(BF16), vaug[h], preferred_element_type=F32)
        outs.append(o[:, dn:] / o[:, 0:1])
    o_ref[0] = jnp.concatenate(outs, axis=-1)


def _attention(q, kv, kr):
    bsz, nq, _ = q.shape
    nk = kv.shape[1]
    heads, dv = MLA_HEADS, MLA_V
    tq = min(ATTN_Q_BLOCK, nq)
    hs = ATTN_HEADS_PER_STEP
    return pl.pallas_call(
        _attn_kernel,
        grid=(bsz, heads // hs, nq // tq),
        in_specs=[
            pl.BlockSpec((1, tq, hs * LANES), lambda b, h, i: (b, i, h)),
            pl.BlockSpec((1, nk, hs * LANES), lambda b, h, i: (b, 0, h), pipeline_mode=pl.Buffered(1)),
            pl.BlockSpec((1, nk, LANES), lambda b, h, i: (b, 0, 0), pipeline_mode=pl.Buffered(1)),
        ],
        out_specs=pl.BlockSpec((1, tq, hs * dv), lambda b, h, i: (b, i, h)),
        out_shape=jax.ShapeDtypeStruct((bsz, nq, heads * dv), F32),
        scratch_shapes=[pltpu.VMEM((hs, nk, LANES), BF16), pltpu.VMEM((hs, nk, LANES), BF16)],
        compiler_params=pltpu.CompilerParams(
            dimension_semantics=("arbitrary", "arbitrary", "arbitrary"), vmem_limit_bytes=VMEM_LIMIT),
        name="mla_attention",
    )(q, kv, kr)


def _mla_mixer(cq_l, ckv_l, kr_l, krp_l, cq_c, ckv_c, kr_c, q_norm_g, kv_norm_g, w_uq, w_ukv, ctx_out):
    bsz, n = cq_l.shape[0], cq_l.shape[1]
    n_ctx = kr_c.shape[1]
    cos, sin = _axial_rope_tables(n)
    pad = LANES - MLA_QK
    w_h = w_uq.reshape(MLA_Q_RANK, MLA_HEADS, MLA_QK)
    w_rot = jnp.concatenate([jnp.zeros_like(w_h[..., :MLA_NOPE]), _rope_partner(w_h[..., MLA_NOPE:])], axis=-1)
    w_q = jnp.concatenate([jnp.pad(w, ((0, 0), (0, 0), (0, pad))).reshape(MLA_Q_RANK, MLA_HEADS * LANES)
                           for w in (w_h, w_rot)], axis=-1)
    cos_q = jnp.concatenate([jnp.ones((n, MLA_NOPE), F32), cos, jnp.ones((n, pad), F32)], axis=-1)
    sin_q = jnp.concatenate([jnp.zeros((n, MLA_NOPE), F32), sin, jnp.zeros((n, pad), F32)], axis=-1)

    def queries(cq, rotate):
        q2 = _rms_norm(cq, q_norm_g) @ (w_q if rotate else w_q[:, :MLA_HEADS * LANES])
        q2 = q2.reshape(cq.shape[0], cq.shape[1], -1, MLA_HEADS, LANES)
        q = q2[:, :, 0] * cos_q[None, :, None, :] + q2[:, :, 1] * sin_q[None, :, None, :] if rotate else q2[:, :, 0]
        return (q * MLA_SCALE).astype(BF16).reshape(cq.shape[0], cq.shape[1], MLA_HEADS * LANES)

    kv = (_rms_norm(jnp.concatenate([ckv_c, ckv_l], axis=1), kv_norm_g) @ w_ukv).astype(BF16)
    kr_all = jnp.concatenate([kr_c, kr_l * cos[None] + krp_l * sin[None]], axis=1)
    kr_pad = jnp.pad(kr_all, ((0, 0), (0, 0), (MLA_NOPE, pad))).astype(BF16)
    y_l = _attention(queries(cq_l, True), kv, kr_pad)
    y_c = _attention(queries(cq_c, False), kv[:, :n_ctx], kr_pad[:, :n_ctx]) if ctx_out else None
    return y_l, y_c


def _short_conv(z, w, b):
    n = z.shape[1]
    zp = jnp.pad(z, ((0, 0), (1, 1), (0, 0)))
    return zp[:, :n] * w[0] + zp[:, 1:n + 1] * w[1] + zp[:, 2:] * w[2] + b


def _hyena_filter_taps(n, w1, b1, w2, b2, w3, b3, freq):
    t = jnp.linspace(0.0, 1.0, n, dtype=F32)[:, None]
    bands = (HY_POS_EMB - 1) // 2
    f = jnp.linspace(1e-4, bands - 1, bands, dtype=F32)
    ang = (2.0 * math.pi * jnp.arange(n, dtype=F32) / n)[:, None] * f
    z = jnp.concatenate([t, jnp.cos(ang), -jnp.sin(ang)], axis=-1)
    h = jnp.sin(freq[0] * (z @ w1 + b1))
    h = jnp.sin(freq[1] * (h @ w2 + b2))
    h = (h @ w3 + b3).astype(F32).reshape(n, 2, HY_ORDER, HY_W)
    deltas = jnp.abs(jnp.linspace(HY_MIN_DECAY, HY_MAX_DECAY, HY_W, dtype=F32))
    h = h * jnp.exp(-t[:, :, None, None] * deltas)
    fwd, bwd = h[:, 0], h[:, 1]
    return jnp.concatenate([fwd, jnp.zeros((1, HY_ORDER, HY_W), F32), bwd[:0:-1]], axis=0)


def _fft_long_conv(y, k_f):
    n = y.shape[1]
    yf = jnp.fft.rfft(y, n=2 * n, axis=1)
    return jnp.fft.irfft(yf * k_f, n=2 * n, axis=1)[:, :n]


def _hyena_mixer(z, short_w, short_b, filt, bias):
    n = z.shape[1]
    z = _short_conv(z, short_w, short_b)
    v, x1, x2 = jnp.split(z, 3, axis=-1)
    y = v.astype(F32)
    if (2 * n) % HY_N2 == 0 and (2 * n) // HY_N2 >= 2 * SUBLANES:
        k_time = _hyena_filter_taps(n, *filt)
        tables = _dft_tables(2 * n)
        kr, ki = _hyena_spectrum(k_time.reshape(2 * n, HY_ORDER * HY_W), tables)
        for o, gate in enumerate((x1, x2)):
            cols = slice(o * HY_W, (o + 1) * HY_W)
            y = gate.astype(F32) * (_hyena_long_conv(y, kr[:, cols], ki[:, cols], tables) + y * bias[o].astype(F32))
    else:
        k_f = jnp.fft.rfft(_hyena_filter_taps(n, *filt), axis=0)
        for o, gate in enumerate((x1, x2)):
            y = gate.astype(F32) * (_fft_long_conv(y, k_f[:, o]) + y * bias[o].astype(F32))
    return y.astype(z.dtype)


HY_N2 = 128
HY_PITCH = 136
HY_UNROLL = 8


def _dft_tables(nfft):
    n1 = nfft // HY_N2
    half = n1 // 2
    k1 = jnp.arange(n1, dtype=jnp.int32)
    n2 = jnp.arange(HY_N2, dtype=jnp.int32)
    t = HY_N2 * k1[None, None, :] + n2[:, None, None]
    ang = (2.0 * math.pi / nfft) * ((k1[None, :, None] * t) % nfft).astype(F32)
    gr, gi = jnp.cos(ang), -jnp.sin(ang)
    g_cplx = jnp.concatenate([jnp.concatenate([gr[..., :half], -gi[..., :half]], -1),
                              jnp.concatenate([gi[..., :half], gr[..., :half]], -1)], axis=1)
    g_real = jnp.concatenate([gr, gi], axis=1)
    hr = jnp.cos(ang).swapaxes(1, 2)[:, :half] / nfft
    hi = jnp.sin(ang).swapaxes(1, 2)[:, :half] / nfft
    g_inv = jnp.concatenate([jnp.concatenate([hr, -hi], -1), jnp.concatenate([hi, hr], -1)], axis=1)
    a2 = (2.0 * math.pi / HY_N2) * ((n2[:, None] * n2[None, :]) % HY_N2).astype(F32)
    fr, fi = jnp.cos(a2), -jnp.sin(a2)
    f2 = jnp.concatenate([jnp.concatenate([fr, -fi], -1), jnp.concatenate([fi, fr], -1)], axis=0)
    f2_inv = jnp.concatenate([jnp.concatenate([fr, fi], -1), jnp.concatenate([-fi, fr], -1)], axis=0)
    return dict(n1=n1, g_cplx=g_cplx.astype(BF16), g_real=g_real.astype(BF16), g_inv=g_inv.astype(BF16),
                f2=f2.astype(BF16), f2_inv=f2_inv.astype(BF16))


def _dft_stage1(load_rows, g_ref, s_re, s_im, n1):
    def body(n2, carry):
        a = jnp.dot(g_ref[n2], load_rows(n2).astype(BF16), preferred_element_type=F32)
        s_re[pl.ds(n2, n1, stride=HY_PITCH), :] = a[:n1]
        s_im[pl.ds(n2, n1, stride=HY_PITCH), :] = a[n1:]
        return carry

    lax.fori_loop(0, HY_N2, body, 0, unroll=HY_UNROLL)


def _dft_stage2(f2_ref, s_re, s_im, k1):
    r0 = pl.multiple_of(k1 * HY_PITCH, SUBLANES)
    sl = jnp.concatenate([s_re[pl.ds(r0, HY_N2), :], s_im[pl.ds(r0, HY_N2), :]], axis=0).astype(BF16)
    x = jnp.dot(f2_ref[...], sl, preferred_element_type=F32)
    return r0, x[:HY_N2], x[HY_N2:]


def _hyena_spectrum_kernel(x_ref, g_ref, f2_ref, kr_ref, ki_ref, s_re, s_im, *, n1):
    _dft_stage1(lambda n2: x_ref[pl.ds(n2, n1, stride=HY_PITCH), :], g_ref, s_re, s_im, n1)

    def body(k1, carry):
        _, xr, xi = _dft_stage2(f2_ref, s_re, s_im, k1)
        q0 = pl.multiple_of(k1 * HY_N2, HY_N2)
        kr_ref[pl.ds(q0, HY_N2), :] = xr
        ki_ref[pl.ds(q0, HY_N2), :] = xi
        return carry

    lax.fori_loop(0, n1, body, 0)


def _hyena_conv_kernel(xr_ref, xi_ref, g_ref, f2_ref, f2i_ref, gi_ref, kr_ref, ki_ref, yr_ref, yi_ref,
                       s_re, s_im, *, n1):
    half = n1 // 2

    def load_rows(n2):
        return jnp.concatenate([xr_ref[0, pl.ds(n2, half, stride=HY_PITCH), :],
                                xi_ref[0, pl.ds(n2, half, stride=HY_PITCH), :]], axis=0)

    _dft_stage1(load_rows, g_ref, s_re, s_im, n1)

    def spectrum_product(k1, carry):
        r0, xr, xi = _dft_stage2(f2_ref, s_re, s_im, k1)
        q0 = pl.multiple_of(k1 * HY_N2, HY_N2)
        kr, ki = kr_ref[pl.ds(q0, HY_N2), :], ki_ref[pl.ds(q0, HY_N2), :]
        y = jnp.concatenate([xr * kr - xi * ki, xr * ki + xi * kr], axis=0).astype(BF16)
        b = jnp.dot(f2i_ref[...], y, preferred_element_type=F32)
        s_re[pl.ds(r0, HY_N2), :] = b[:HY_N2]
        s_im[pl.ds(r0, HY_N2), :] = b[HY_N2:]
        return carry

    lax.fori_loop(0, n1, spectrum_product, 0, unroll=HY_UNROLL // 2)

    yr_ref[...] = jnp.zeros_like(yr_ref)
    yi_ref[...] = jnp.zeros_like(yi_ref)

    def inverse_stage1(n2, carry):
        bs = jnp.concatenate([s_re[pl.ds(n2, n1, stride=HY_PITCH), :],
                              s_im[pl.ds(n2, n1, stride=HY_PITCH), :]], axis=0).astype(BF16)
        y = jnp.dot(gi_ref[n2], bs, preferred_element_type=F32)
        yr_ref[0, pl.ds(n2, half, stride=HY_PITCH), :] = y[:half]
        yi_ref[0, pl.ds(n2, half, stride=HY_PITCH), :] = y[half:]
        return carry

    lax.fori_loop(0, HY_N2, inverse_stage1, 0, unroll=HY_UNROLL)


def _to_strided(a, rows):
    lead, c = a.shape[:-2], a.shape[-1]
    a = a.reshape(lead + (rows, HY_N2, c))
    a = jnp.pad(a, [(0, 0)] * len(lead) + [(0, 0), (0, HY_PITCH - HY_N2), (0, 0)])
    return a.reshape(lead + (rows * HY_PITCH, c))


def _from_strided(a, rows):
    lead, c = a.shape[:-2], a.shape[-1]
    return a.reshape(lead + (rows, HY_PITCH, c))[..., :HY_N2, :].reshape(lead + (rows * HY_N2, c))


def _const_spec(shape):
    return pl.BlockSpec(shape, lambda *_: (0,) * len(shape), pipeline_mode=pl.Buffered(1))


def _hyena_spectrum(k_time, tables):
    nfft, c = k_time.shape
    n1 = tables["n1"]
    spec_out = pl.BlockSpec((nfft, LANES), lambda j: (0, j))
    return pl.pallas_call(
        functools.partial(_hyena_spectrum_kernel, n1=n1),
        grid=(c // LANES,),
        in_specs=[pl.BlockSpec((n1 * HY_PITCH, LANES), lambda j: (0, j)),
                  _const_spec((HY_N2, 2 * n1, n1)), _const_spec((2 * HY_N2, 2 * HY_N2))],
        out_specs=[spec_out, spec_out],
        out_shape=[jax.ShapeDtypeStruct((nfft, c), F32)] * 2,
        scratch_shapes=[pltpu.VMEM((n1 * HY_PITCH, LANES), F32)] * 2,
        compiler_params=pltpu.CompilerParams(dimension_semantics=("arbitrary",), vmem_limit_bytes=VMEM_LIMIT),
        name="hyena_spectrum",
    )(_to_strided(k_time, n1), tables["g_real"], tables["f2"])


def _hyena_long_conv(y, kr, ki, tables):
    bsz, n, c = y.shape
    n1 = tables["n1"]
    half = n1 // 2
    pairs = bsz // 2
    xr, xi = _to_strided(y[0::2], half), _to_strided(y[1::2], half)
    spec_x = pl.BlockSpec((1, half * HY_PITCH, LANES), lambda j, p: (p, 0, j))
    spec_k = pl.BlockSpec((n1 * HY_N2, LANES), lambda j, p: (0, j), pipeline_mode=pl.Buffered(1))
    yr, yi = pl.pallas_call(
        functools.partial(_hyena_conv_kernel, n1=n1),
        grid=(c // LANES, pairs),
        in_specs=[spec_x, spec_x,
                  _const_spec((HY_N2, 2 * n1, n1)), _const_spec((2 * HY_N2, 2 * HY_N2)),
                  _const_spec((2 * HY_N2, 2 * HY_N2)), _const_spec((HY_N2, n1, 2 * n1)),
                  spec_k, spec_k],
        out_specs=[spec_x, spec_x],
        out_shape=[jax.ShapeDtypeStruct((pairs, half * HY_PITCH, c), F32)] * 2,
        scratch_shapes=[pltpu.VMEM((n1 * HY_PITCH, LANES), F32)] * 2,
        compiler_params=pltpu.CompilerParams(dimension_semantics=("arbitrary", "arbitrary"),
                                             vmem_limit_bytes=VMEM_LIMIT),
        name="hyena_conv",
    )(xr, xi, tables["g_cplx"], tables["f2"], tables["f2_inv"], tables["g_inv"], kr, ki)
    out = jnp.stack([_from_strided(yr, half), _from_strided(yi, half)], axis=1)
    return out.reshape(bsz, n, c)


def _merge_groups(y_s5, y_mla, y_hy, g):
    return jnp.concatenate([
        _rms_norm(y_s5, g[:S5_W]),
        _rms_norm(y_mla, g[S5_W:S5_W + MLA_W]),
        _rms_norm(y_hy, g[S5_W + MLA_W:]),
    ], axis=-1)


def _moe_kernel(be_ref, used_ref, x_ref, w1_ref, w3_ref, w2_ref, rw_ref, o_ref, w13_s, w2_s):
    i = pl.program_id(0)
    hid = w2_s.shape[0]

    @pl.when(jnp.logical_or(i == 0, be_ref[i] != be_ref[jnp.maximum(i - 1, 0)]))
    def _():
        w13_s[:, :hid] = w1_ref[0].astype(BF16)
        w13_s[:, hid:] = w3_ref[0].astype(BF16)
        w2_s[...] = w2_ref[0].astype(BF16)

    @pl.when(i < used_ref[0])
    def _():
        ab = jnp.dot(x_ref[...], w13_s[...], preferred_element_type=F32)
        a, b = ab[:, :hid], ab[:, hid:]
        h = (a * jax.nn.sigmoid(a)) * b
        y = jnp.dot(h.astype(BF16), w2_s[...], preferred_element_type=F32)
        o_ref[...] = y * rw_ref[...]

    @pl.when(i >= used_ref[0])
    def _():
        o_ref[...] = jnp.zeros_like(o_ref)


def _moe_experts(xg, block_e, n_used, w1, w3, w2, row_w):
    n_rows, d = xg.shape
    hid = w1.shape[2]
    n_blocks = n_rows // MOE_BLOCK
    return pl.pallas_call(
        _moe_kernel,
        grid_spec=pltpu.PrefetchScalarGridSpec(
            num_scalar_prefetch=2,
            grid=(n_blocks,),
            in_specs=[
                pl.BlockSpec((MOE_BLOCK, d), lambda i, be, nu: (i, 0)),
                pl.BlockSpec((1, d, hid), lambda i, be, nu: (be[i], 0, 0)),
                pl.BlockSpec((1, d, hid), lambda i, be, nu: (be[i], 0, 0)),
                pl.BlockSpec((1, hid, d), lambda i, be, nu: (be[i], 0, 0)),
                pl.BlockSpec((MOE_BLOCK, 1), lambda i, be, nu: (i, 0)),
            ],
            out_specs=pl.BlockSpec((MOE_BLOCK, d), lambda i, be, nu: (i, 0)),
            scratch_shapes=[pltpu.VMEM((d, 2 * hid), BF16), pltpu.VMEM((hid, d), BF16)],
        ),
        out_shape=jax.ShapeDtypeStruct((n_rows, d), F32),
        compiler_params=pltpu.CompilerParams(dimension_semantics=("arbitrary",), vmem_limit_bytes=VMEM_LIMIT),
        name="moe_experts",
    )(block_e, n_used, xg, w1, w3, w2, row_w)


def _hier_moe(h, w_group, w_expert, w1, w3, w2):
    t, d = h.shape
    g_prob = jax.nn.softmax((h @ w_group).astype(F32), axis=-1)
    g_idx = jnp.argmax(g_prob, axis=-1).astype(jnp.int32)
    g_w = jnp.max(g_prob, axis=-1, keepdims=True)
    e_logits = (h @ w_expert).astype(F32).reshape(t, MOE_GROUPS, MOE_PER_GROUP)
    in_group = jnp.take_along_axis(e_logits, g_idx[:, None, None], axis=1)[:, 0]
    i1 = jnp.argmax(in_group, axis=-1).astype(jnp.int32)
    rest = jnp.where(jnp.arange(MOE_PER_GROUP, dtype=jnp.int32)[None, :] == i1[:, None], -jnp.inf, in_group)
    i2 = jnp.argmax(rest, axis=-1).astype(jnp.int32)
    top_v = jnp.stack([jnp.max(in_group, axis=-1), jnp.max(rest, axis=-1)], axis=-1)
    top_i = jnp.stack([i1, i2], axis=-1)
    gate = jax.nn.softmax(top_v, axis=-1) * g_w
    eid = g_idx[:, None] * MOE_PER_GROUP + top_i
    n_assign = t * MOE_TOP_K
    flat_e = eid.reshape(n_assign)
    flat_w = gate.reshape(n_assign)
    se, order = lax.sort((flat_e, jnp.arange(n_assign, dtype=jnp.int32)), num_keys=1)
    counts = jnp.sum(flat_e[None, :] == jnp.arange(MOE_EXPERTS, dtype=jnp.int32)[:, None], axis=1, dtype=jnp.int32)
    padded = (counts + MOE_BLOCK - 1) // MOE_BLOCK * MOE_BLOCK
    start = jnp.cumsum(counts) - counts
    pend = jnp.cumsum(padded)
    pstart = pend - padded
    experts = jnp.arange(MOE_EXPERTS, dtype=jnp.int32)[:, None]
    shift = jnp.sum(jnp.where(se[None, :] == experts, (pstart - start)[:, None], 0), axis=0)
    dest = (jnp.arange(n_assign, dtype=jnp.int32) + shift).astype(jnp.int32)
    n_blocks = -(-n_assign // MOE_BLOCK) + MOE_EXPERTS
    n_rows = n_blocks * MOE_BLOCK
    block_start = jnp.arange(n_blocks, dtype=jnp.int32) * MOE_BLOCK
    block_e = jnp.minimum(jnp.sum(pend[None, :] <= block_start[:, None], axis=1), MOE_EXPERTS - 1).astype(jnp.int32)
    off = (block_start - pstart[block_e])[:, None] + jnp.arange(MOE_BLOCK, dtype=jnp.int32)[None, :]
    valid = (off < counts[block_e][:, None]).reshape(n_rows)
    row_asg = order[jnp.where(valid, (start[block_e][:, None] + off).reshape(n_rows), 0)]
    row_tok = jnp.where(valid, row_asg // MOE_TOP_K, 0)
    row_w = jnp.where(valid, flat_w[row_asg], 0.0)
    xg = h.astype(BF16)[row_tok]
    n_used = (pend[-1:] // MOE_BLOCK).astype(jnp.int32)
    ys = _moe_experts(xg, block_e, n_used, w1, w3, w2, row_w[:, None])
    _, slot = lax.sort((order, dest), num_keys=1)
    slot = slot.reshape(t, MOE_TOP_K)
    return ys[slot[:, 0]] + ys[slot[:, 1]]


def _final_norm_kernel(x_ref, g_ref, o_ref):
    x = x_ref[...]
    y = x * lax.rsqrt(jnp.mean(x * x, axis=-1, keepdims=True) + EPS)
    o_ref[...] = y * g_ref[...]


def _final_norm(x, g):
    t, d = x.shape
    blk = 1024
    return pl.pallas_call(
        _final_norm_kernel,
        grid=(t // blk,),
        in_specs=[pl.BlockSpec((blk, d), lambda i: (i, 0)), pl.BlockSpec((1, d), lambda i: (0, 0))],
        out_specs=pl.BlockSpec((blk, d), lambda i: (i, 0)),
        out_shape=jax.ShapeDtypeStruct((t, d), F32),
        name="final_norm",
    )(x, g.reshape(1, d))


def kernel(x, c, ctx, c_ctx, ada_w, ada_b, norm1_g, norm2_g, w_in,
           s5_lambda_re, s5_lambda_im, s5_log_dt, s5_b_re, s5_b_im, s5_c_re, s5_c_im, s5_d, s5_glu_w,
           mla_q_norm_g, mla_kv_norm_g, mla_w_uq, mla_w_ukv,
           hy_short_w, hy_short_b, hy_f_w1, hy_f_b1, hy_f_w2, hy_f_b2, hy_f_w3, hy_f_b3, hy_f_freq, hy_bias,
           mix_norm_g, w_out, moe_w_group, moe_w_expert, moe_w1, moe_w3, moe_w2, final_g):
    bsz, n, d = x.shape
    n_ctx = ctx.shape[1]
    xl, xc = x, ctx
    act_l = jax.nn.silu(c)
    act_c = jax.nn.silu(c_ctx)
    for i in range(DEPTH):
        ctx_out = i < DEPTH - 1
        mod_l = jnp.split((act_l @ ada_w[i] + ada_b[i])[:, None, :], 6, axis=-1)
        mod_c = jnp.split((act_c @ ada_w[i] + ada_b[i])[None, None, :], 6, axis=-1)
        hl = _modulate(xl, norm1_g[i], mod_l[0], mod_l[1])
        hc = _modulate(xc, norm1_g[i], mod_c[0], mod_c[1])
        kr0 = S5_W + MLA_Q_RANK + MLA_KV_RANK
        w_in_l = jnp.concatenate([w_in[i], _rope_partner(w_in[i][:, kr0:kr0 + MLA_ROPE])], axis=1)
        proj_l = hl @ w_in_l
        u_l, cq_l, ckv_l, kr_l, hz_l = _split_projection(proj_l[..., :w_in.shape[2]])
        krp_l = proj_l[..., w_in.shape[2]:]
        u_c, cq_c, ckv_c, kr_c, hz_c = _split_projection(hc @ w_in[i])
        s5_l, s5_c = _s5_mixer(u_l, u_c, s5_lambda_re[i], s5_lambda_im[i], s5_log_dt[i], s5_b_re[i], s5_b_im[i],
                               s5_c_re[i], s5_c_im[i], s5_d[i], s5_glu_w[i], ctx_out)
        mla_l, mla_c = _mla_mixer(cq_l, ckv_l, kr_l, krp_l, cq_c, ckv_c, kr_c, mla_q_norm_g[i], mla_kv_norm_g[i],
                                  mla_w_uq[i], mla_w_ukv[i], ctx_out)
        filt = (hy_f_w1[i], hy_f_b1[i], hy_f_w2[i], hy_f_b2[i], hy_f_w3[i], hy_f_b3[i], hy_f_freq[i])
        hy_l = _hyena_mixer(hz_l, hy_short_w[i], hy_short_b[i], filt, hy_bias[i])
        xl = xl + mod_l[2] * (_merge_groups(s5_l, mla_l, hy_l, mix_norm_g[i]) @ w_out[i])
        fl = _modulate(xl, norm2_g[i], mod_l[3], mod_l[4]).reshape(bsz * n, d)
        if ctx_out:
            hy_c = _hyena_mixer(hz_c, hy_short_w[i], hy_short_b[i], filt, hy_bias[i])
            xc = xc + mod_c[2] * (_merge_groups(s5_c, mla_c, hy_c, mix_norm_g[i]) @ w_out[i])
            fc = _modulate(xc, norm2_g[i], mod_c[3], mod_c[4]).reshape(bsz * n_ctx, d)
            y = _hier_moe(jnp.concatenate([fl, fc], axis=0), moe_w_group[i], moe_w_expert[i],
                          moe_w1[i], moe_w3[i], moe_w2[i])
            xl = xl + mod_l[5] * y[:bsz * n].reshape(bsz, n, d)
            xc = xc + mod_c[5] * y[bsz * n:].reshape(bsz, n_ctx, d)
        else:
            y = _hier_moe(fl, moe_w_group[i], moe_w_expert[i], moe_w1[i], moe_w3[i], moe_w2[i])
            xl = xl + mod_l[5] * y.reshape(bsz, n, d)
    return _final_norm(xl.reshape(bsz * n, d), final_g).reshape(bsz, n, d)
```

```python
import functools
import math

import jax
import jax.numpy as jnp
from jax import lax
from jax.experimental import pallas as pl
from jax.experimental.pallas import tpu as pltpu

D_MODEL = 1024
DEPTH = 4
GRID_W = 64
EPS = 1e-6

MIX_W = D_MODEL
S5_W = D_MODEL // 4
S5_GROUP = 16
S5_GROUPS = S5_W // S5_GROUP
S5_STATE = 64
S5_N = S5_GROUPS * S5_STATE
MLA_V = 64
MLA_W = D_MODEL // 2
MLA_HEADS = MLA_W // MLA_V
MLA_NOPE = 64
MLA_ROPE = 32
MLA_QK = MLA_NOPE + MLA_ROPE
MLA_Q_RANK = 384
MLA_KV_RANK = 256
MLA_SCALE = 1.0 / math.sqrt(MLA_NOPE + MLA_ROPE)
ROPE_BASE = 10000.0
HY_W = D_MODEL // 4
HY_ORDER = 2
HY_POS_EMB = 33
HY_FILTER_W = 64
HY_MIN_DECAY = math.log(1e-2) / 1.5
HY_MAX_DECAY = math.log(1e-2) / 0.3
MOE_GROUPS = 4
MOE_PER_GROUP = 8
MOE_EXPERTS = MOE_GROUPS * MOE_PER_GROUP
MOE_TOP_K = 2
MOE_HIDDEN = 512
MOE_BLOCK = 256

SUBLANES = 8
S5_CHUNK = 128
MM_ROWS = 256
ATTN_Q_BLOCK = 256
MIX_ROWS = 512
LANES = 128
ATTN_HEADS_PER_STEP = 4
VMEM_LIMIT = 48 * 1024 * 1024

F32 = jnp.float32
BF16 = jnp.bfloat16


def _rms_norm(x, g):
    xf = x.astype(F32)
    y = xf * lax.rsqrt(jnp.mean(xf * xf, axis=-1, keepdims=True) + EPS)
    return (y * g.astype(F32)).astype(x.dtype)


def _modulate(x, g, shift, scale):
    return _rms_norm(x, g) * (1.0 + scale) + shift


def _split_projection(p):
    o1 = S5_W
    o2 = o1 + MLA_Q_RANK
    o3 = o2 + MLA_KV_RANK
    o4 = o3 + MLA_ROPE
    return p[..., :o1], p[..., o1:o2], p[..., o2:o3], p[..., o3:o4], p[..., o4:]


def _s5_matrices(lam_re, lam_im, log_dt, b_re, b_im, c_re, c_im):
    g, p, h = S5_GROUPS, S5_STATE, S5_GROUP
    dt = jnp.exp(log_dt)[..., None]
    mag = jnp.exp(lam_re * dt)
    ar, ai = mag * jnp.cos(lam_im * dt), mag * jnp.sin(lam_im * dt)
    den = lam_re * lam_re + lam_im * lam_im
    fr = ((ar - 1.0) * lam_re + ai * lam_im) / den
    fi = (ai * lam_re - (ar - 1.0) * lam_im) / den
    bbr = fr[..., None] * b_re - fi[..., None] * b_im
    bbi = fr[..., None] * b_im + fi[..., None] * b_re
    eye = jnp.eye(g, dtype=F32)

    def block_in(m):
        return jnp.einsum('kgph,gj->kghjp', m, eye).reshape(2, g * h, g * p)

    def block_out(m):
        return jnp.einsum('kghp,gj->kgpjh', m, eye).reshape(2, g * p, g * h)

    bmat = jnp.concatenate([block_in(bbr), block_in(bbi)], axis=-1)
    cmat = jnp.concatenate([block_out(c_re), -block_out(c_im)], axis=1)
    lam = jnp.stack([ar[0], ai[0], ar[1], ai[1]]).reshape(4, 1, g * p)
    lam = jnp.broadcast_to(lam, (4, SUBLANES, g * p)).reshape(4 * SUBLANES, g * p)
    return bmat.astype(BF16), cmat.astype(BF16), lam


def _s5_kernel(uf_ref, ub_ref, bmat_ref, cmat_ref, lam_ref, yf_ref, yb_ref, buf_f, buf_b, state):
    rows = uf_ref.shape[0]
    steps = rows // SUBLANES
    n = S5_N
    s = SUBLANES

    @pl.when(pl.program_id(0) == 0)
    def _():
        state[...] = jnp.zeros_like(state)

    def drive(r, carry):
        r0 = pl.multiple_of(r * MM_ROWS, MM_ROWS)
        buf_f[pl.ds(r0, MM_ROWS), :] = jnp.dot(uf_ref[pl.ds(r0, MM_ROWS), :].astype(BF16), bmat_ref[0],
                                               preferred_element_type=F32)
        buf_b[pl.ds(r0, MM_ROWS), :] = jnp.dot(ub_ref[pl.ds(r0, MM_ROWS), :].astype(BF16), bmat_ref[1],
                                               preferred_element_type=F32)
        return carry

    lax.fori_loop(0, rows // MM_ROWS, drive, 0)

    def step(j, carry):
        fr, fi, br, bi = carry
        rf = pl.multiple_of(j * s, s)
        rb = pl.multiple_of((steps - 1 - j) * s, s)
        lfr, lfi = lam_ref[0:s, :], lam_ref[s:2 * s, :]
        lbr, lbi = lam_ref[2 * s:3 * s, :], lam_ref[3 * s:4 * s, :]
        nfr = lfr * fr - lfi * fi + buf_f[pl.ds(rf, s), 0:n]
        nfi = lfr * fi + lfi * fr + buf_f[pl.ds(rf, s), n:2 * n]
        nbr = lbr * br - lbi * bi + buf_b[pl.ds(rb, s), 0:n]
        nbi = lbr * bi + lbi * br + buf_b[pl.ds(rb, s), n:2 * n]
        buf_f[pl.ds(rf, s), 0:n] = nfr
        buf_f[pl.ds(rf, s), n:2 * n] = nfi
        buf_b[pl.ds(rb, s), 0:n] = nbr
        buf_b[pl.ds(rb, s), n:2 * n] = nbi
        return nfr, nfi, nbr, nbi

    init = (state[0:s, :], state[s:2 * s, :], state[2 * s:3 * s, :], state[3 * s:4 * s, :])
    fr, fi, br, bi = lax.fori_loop(0, steps, step, init, unroll=2)
    state[0:s, :] = fr
    state[s:2 * s, :] = fi
    state[2 * s:3 * s, :] = br
    state[3 * s:4 * s, :] = bi

    def readout(r, carry):
        r0 = pl.multiple_of(r * MM_ROWS, MM_ROWS)
        yf_ref[pl.ds(r0, MM_ROWS), :] = jnp.dot(buf_f[pl.ds(r0, MM_ROWS), :].astype(BF16), cmat_ref[0],
                                                preferred_element_type=F32)
        yb_ref[pl.ds(r0, MM_ROWS), :] = jnp.dot(buf_b[pl.ds(r0, MM_ROWS), :].astype(BF16), cmat_ref[1],
                                                preferred_element_type=F32)
        return carry

    lax.fori_loop(0, rows // MM_ROWS, readout, 0)


def _s5_scan(u_tm, bmat, cmat, lam, n_ctx_chunks, chunk):
    rows_all, w = u_tm.shape
    rows = chunk * SUBLANES
    n_chunks = rows_all // rows
    n2 = 2 * S5_N

    def bwd_chunk(i):
        return jnp.where(i < n_ctx_chunks, n_ctx_chunks - 1 - i, n_chunks - 1 + n_ctx_chunks - i)

    return pl.pallas_call(
        _s5_kernel,
        grid=(n_chunks,),
        in_specs=[
            pl.BlockSpec((rows, w), lambda i: (i, 0)),
            pl.BlockSpec((rows, w), lambda i: (bwd_chunk(i), 0)),
            pl.BlockSpec((2, w, n2), lambda i: (0, 0, 0)),
            pl.BlockSpec((2, n2, w), lambda i: (0, 0, 0)),
            pl.BlockSpec((4 * SUBLANES, S5_N), lambda i: (0, 0)),
        ],
        out_specs=[
            pl.BlockSpec((rows, w), lambda i: (i, 0)),
            pl.BlockSpec((rows, w), lambda i: (bwd_chunk(i), 0)),
        ],
        out_shape=[jax.ShapeDtypeStruct((rows_all, w), F32)] * 2,
        scratch_shapes=[
            pltpu.VMEM((rows, n2), F32),
            pltpu.VMEM((rows, n2), F32),
            pltpu.VMEM((4 * SUBLANES, S5_N), F32),
        ],
        compiler_params=pltpu.CompilerParams(dimension_semantics=("arbitrary",), vmem_limit_bytes=VMEM_LIMIT),
        name="s5_scan",
    )(u_tm, u_tm, bmat, cmat, lam)


def _s5_mixer(u_tm, n_ctx, lam_re, lam_im, log_dt, b_re, b_im, c_re, c_im, d, glu_w):
    n_all, bw = u_tm.shape
    bsz, w = bw // S5_W, S5_W
    assert bsz == SUBLANES and n_all % S5_CHUNK == 0 and n_ctx % S5_CHUNK == 0
    bmat, cmat, lam = _s5_matrices(lam_re, lam_im, log_dt, b_re, b_im, c_re, c_im)
    u_rows = u_tm.reshape(n_all * bsz, w)
    y_f, y_b = _s5_scan(u_rows, bmat, cmat, lam, n_ctx // S5_CHUNK, S5_CHUNK)
    y = y_f + y_b + d * u_rows
    y = jax.nn.gelu(y)
    y = y * jax.nn.sigmoid(y @ glu_w)
    return y.reshape(n_all, bw)


def _axial_rope_tables(n_tokens):
    rows = n_tokens // GRID_W
    row = jnp.repeat(jnp.arange(rows), GRID_W).astype(F32)
    col = jnp.tile(jnp.arange(GRID_W), rows).astype(F32)
    half = MLA_ROPE // 2
    inv = ROPE_BASE ** (-jnp.arange(0, half, 2, dtype=F32) / half)
    ang_r = row[:, None] * inv
    ang_c = col[:, None] * inv
    cos = jnp.concatenate([jnp.cos(ang_r)] * 2 + [jnp.cos(ang_c)] * 2, axis=-1)
    sin = jnp.concatenate([jnp.sin(ang_r)] * 2 + [jnp.sin(ang_c)] * 2, axis=-1)
    return cos, sin


def _rope_partner(w):
    q = MLA_ROPE // 4
    parts = []
    for g in range(2):
        x1, x2 = w[..., 2 * g * q:(2 * g + 1) * q], w[..., (2 * g + 1) * q:(2 * g + 2) * q]
        parts += [-x2, x1]
    return jnp.concatenate(parts, axis=-1)


def _attn_kernel(q_ref, kv_ref, kr_ref, o_ref, kcat, vaug):
    hs, nk = kcat.shape[0], kcat.shape[1]
    dn = MLA_NOPE

    @pl.when(pl.program_id(2) == 0)
    def _():
        lane = lax.broadcasted_iota(jnp.int32, (nk, LANES), 1)
        ones_col = jnp.where(lane == 0, 1.0, 0.0).astype(BF16)
        for h in range(hs):
            kvh = kv_ref[0, :, h * LANES:(h + 1) * LANES]
            kcat[h] = jnp.where(lane < dn, kvh, kr_ref[0])
            vaug[h] = jnp.where(lane >= dn, kvh, ones_col)

    outs = []
    for h in range(hs):
        s = lax.dot_general(q_ref[0, :, h * LANES:(h + 1) * LANES], kcat[h], (((1,), (1,)), ((), ())),
                            preferred_element_type=F32)
        p = jnp.exp(s - jnp.max(s, axis=-1, keepdims=True))
        o = jnp.dot(p.astype(BF16), vaug[h], preferred_element_type=F32)
        outs.append(o[:, dn:] / o[:, 0:1])
    o_ref[0] = jnp.concatenate(outs, axis=-1)


def _attention(q, kv, kr):
    bsz, nq, _ = q.shape
    nk = kv.shape[1]
    heads, dv = MLA_HEADS, MLA_V
    tq = min(ATTN_Q_BLOCK, nq)
    hs = ATTN_HEADS_PER_STEP
    return pl.pallas_call(
        _attn_kernel,
        grid=(bsz, heads // hs, nq // tq),
        in_specs=[
            pl.BlockSpec((1, tq, hs * LANES), lambda b, h, i: (b, i, h)),
            pl.BlockSpec((1, nk, hs * LANES), lambda b, h, i: (b, 0, h), pipeline_mode=pl.Buffered(1)),
            pl.BlockSpec((1, nk, LANES), lambda b, h, i: (b, 0, 0), pipeline_mode=pl.Buffered(1)),
        ],
        out_specs=pl.BlockSpec((1, tq, hs * dv), lambda b, h, i: (b, i, h)),
        out_shape=jax.ShapeDtypeStruct((bsz, nq, heads * dv), F32),
        scratch_shapes=[pltpu.VMEM((hs, nk, LANES), BF16), pltpu.VMEM((hs, nk, LANES), BF16)],
        compiler_params=pltpu.CompilerParams(
            dimension_semantics=("arbitrary", "arbitrary", "arbitrary"), vmem_limit_bytes=VMEM_LIMIT),
        name="mla_attention",
    )(q, kv, kr)


PIN_U = 0
PIN_CQ = PIN_U + S5_W
PIN_CKV = PIN_CQ + MLA_Q_RANK
PIN_HZ = PIN_CKV + MLA_KV_RANK
PIN_KR = PIN_HZ + 3 * HY_W
PIN_KRP = PIN_KR + LANES
PIN_COLS = PIN_KRP + LANES


def _proj_in_weights(w_in, w_uq):
    o1 = S5_W
    o2 = o1 + MLA_Q_RANK
    o3 = o2 + MLA_KV_RANK
    o4 = o3 + MLA_ROPE
    pad = LANES - MLA_QK
    w_kr = w_in[:, o3:o4]
    in_slot = lambda w: jnp.pad(w, ((0, 0), (MLA_NOPE, pad)))
    w_ext = jnp.concatenate([w_in[:, :o3], w_in[:, o4:], in_slot(w_kr), in_slot(_rope_partner(w_kr))], axis=1)
    w_h = w_uq.reshape(MLA_Q_RANK, MLA_HEADS, MLA_QK)
    w_rot = jnp.concatenate([jnp.zeros_like(w_h[..., :MLA_NOPE]), _rope_partner(w_h[..., MLA_NOPE:])], axis=-1)
    w_q = jnp.concatenate([jnp.pad(w, ((0, 0), (0, 0), (0, pad))).reshape(MLA_Q_RANK, MLA_HEADS * LANES)
                           for w in (w_h, w_rot)], axis=-1)
    return w_ext.astype(BF16), w_q.astype(BF16)


def _rope_slot_tables(n_ctx, n):
    cos, sin = _axial_rope_tables(n)
    pad = LANES - MLA_QK
    cos = jnp.concatenate([jnp.ones((n, MLA_NOPE), F32), cos, jnp.ones((n, pad), F32)], axis=-1)
    sin = jnp.concatenate([jnp.zeros((n, MLA_NOPE), F32), sin, jnp.zeros((n, pad), F32)], axis=-1)
    return (jnp.concatenate([jnp.ones((n_ctx, LANES), F32), cos], axis=0),
            jnp.concatenate([jnp.zeros((n_ctx, LANES), F32), sin], axis=0))


def _proj_in_kernel(x_ref, g_ref, shift_ref, scale_ref, w_ref, qg_ref, wq_ref, kvg_ref, wkv_ref, cos_ref, sin_ref,
                    u_ref, q_ref, kv_ref, kr_ref, hz_ref):
    h = ((_rms_rows(x_ref[0]) * g_ref[...]) * (1.0 + scale_ref[0]) + shift_ref[0]).astype(BF16)
    p = jnp.dot(h, w_ref[...], preferred_element_type=F32)
    u_ref[...] = p[:, PIN_U:PIN_CQ]
    hz_ref[0] = p[:, PIN_HZ:PIN_KR]
    cos, sin = cos_ref[...], sin_ref[...]
    kr_ref[0] = (p[:, PIN_KR:PIN_KRP] * cos + p[:, PIN_KRP:PIN_COLS] * sin).astype(BF16)
    cq = (_rms_rows(p[:, PIN_CQ:PIN_CKV]) * qg_ref[...]).astype(BF16)
    q2 = jnp.dot(cq, wq_ref[...], preferred_element_type=F32)
    hw = MLA_HEADS * LANES
    for hd in range(MLA_HEADS):
        a, b = q2[:, hd * LANES:(hd + 1) * LANES], q2[:, hw + hd * LANES:hw + (hd + 1) * LANES]
        q_ref[0, :, hd * LANES:(hd + 1) * LANES] = ((a * cos + b * sin) * MLA_SCALE).astype(BF16)
    ckv = (_rms_rows(p[:, PIN_CKV:PIN_HZ]) * kvg_ref[...]).astype(BF16)
    kv_ref[0] = jnp.dot(ckv, wkv_ref[...], preferred_element_type=F32).astype(BF16)


def _proj_in(x, norm_g, shift, scale, w_ext, q_norm_g, w_q, kv_norm_g, w_ukv, cos, sin):
    bsz, n, d = x.shape
    tm = min(MIX_ROWS, n)
    hw = MLA_HEADS * LANES

    def rows(width):
        return pl.BlockSpec((1, tm, width), lambda b, i: (b, i, 0))

    per_batch = pl.BlockSpec((1, 1, d), lambda b, i: (b, 0, 0))
    table = pl.BlockSpec((tm, LANES), lambda b, i: (i, 0))
    return pl.pallas_call(
        _proj_in_kernel,
        grid=(bsz, n // tm),
        in_specs=[rows(d), _const_spec((1, d)), per_batch, per_batch, _const_spec((d, PIN_COLS)),
                  _const_spec((1, MLA_Q_RANK)), _const_spec((MLA_Q_RANK, 2 * hw)),
                  _const_spec((1, MLA_KV_RANK)), _const_spec((MLA_KV_RANK, hw)), table, table],
        out_specs=[pl.BlockSpec((tm, S5_W), lambda b, i: (i, b)), rows(hw), rows(hw), rows(LANES), rows(3 * HY_W)],
        out_shape=[jax.ShapeDtypeStruct((n, bsz * S5_W), F32), jax.ShapeDtypeStruct((bsz, n, hw), BF16),
                   jax.ShapeDtypeStruct((bsz, n, hw), BF16), jax.ShapeDtypeStruct((bsz, n, LANES), BF16),
                   jax.ShapeDtypeStruct((bsz, n, 3 * HY_W), F32)],
        compiler_params=pltpu.CompilerParams(dimension_semantics=("arbitrary", "arbitrary"),
                                             vmem_limit_bytes=VMEM_LIMIT),
        name="proj_in",
    )(x, norm_g.reshape(1, d), shift, scale, w_ext, q_norm_g.reshape(1, -1), w_q, kv_norm_g.reshape(1, -1),
      w_ukv.astype(BF16), cos, sin)


def _short_conv(z, w, b):
    n = z.shape[1]
    zp = jnp.pad(z, ((0, 0), (1, 1), (0, 0)))
    return zp[:, :n] * w[0] + zp[:, 1:n + 1] * w[1] + zp[:, 2:] * w[2] + b


def _hyena_filter_taps(n, w1, b1, w2, b2, w3, b3, freq):
    t = jnp.linspace(0.0, 1.0, n, dtype=F32)[:, None]
    bands = (HY_POS_EMB - 1) // 2
    f = jnp.linspace(1e-4, bands - 1, bands, dtype=F32)
    ang = (2.0 * math.pi * jnp.arange(n, dtype=F32) / n)[:, None] * f
    z = jnp.concatenate([t, jnp.cos(ang), -jnp.sin(ang)], axis=-1)
    h = jnp.sin(freq[0] * (z @ w1 + b1))
    h = jnp.sin(freq[1] * (h @ w2 + b2))
    h = (h @ w3 + b3).astype(F32).reshape(n, 2, HY_ORDER, HY_W)
    deltas = jnp.abs(jnp.linspace(HY_MIN_DECAY, HY_MAX_DECAY, HY_W, dtype=F32))
    h = h * jnp.exp(-t[:, :, None, None] * deltas)
    fwd, bwd = h[:, 0], h[:, 1]
    return jnp.concatenate([fwd, jnp.zeros((1, HY_ORDER, HY_W), F32), bwd[:0:-1]], axis=0)


def _fft_long_conv(y, k_f):
    n = y.shape[1]
    yf = jnp.fft.rfft(y, n=2 * n, axis=1)
    return jnp.fft.irfft(yf * k_f, n=2 * n, axis=1)[:, :n]


def _hyena_mixer(z, short_w, short_b, filt, bias):
    n = z.shape[1]
    z = _short_conv(z, short_w, short_b)
    v, x1, x2 = jnp.split(z, 3, axis=-1)
    y = v.astype(F32)
    if (2 * n) % HY_N2 == 0 and (2 * n) // HY_N2 >= 2 * SUBLANES:
        k_time = _hyena_filter_taps(n, *filt)
        tables = _dft_tables(2 * n)
        kr, ki = _hyena_spectrum(k_time.reshape(2 * n, HY_ORDER * HY_W), tables)
        for o, gate in enumerate((x1, x2)):
            cols = slice(o * HY_W, (o + 1) * HY_W)
            y = gate.astype(F32) * (_hyena_long_conv(y, kr[:, cols], ki[:, cols], tables) + y * bias[o].astype(F32))
    else:
        k_f = jnp.fft.rfft(_hyena_filter_taps(n, *filt), axis=0)
        for o, gate in enumerate((x1, x2)):
            y = gate.astype(F32) * (_fft_long_conv(y, k_f[:, o]) + y * bias[o].astype(F32))
    return y.astype(z.dtype)


HY_N2 = 128
HY_PITCH = 136
HY_UNROLL = 8


def _dft_tables(nfft):
    n1 = nfft // HY_N2
    half = n1 // 2
    k1 = jnp.arange(n1, dtype=jnp.int32)
    n2 = jnp.arange(HY_N2, dtype=jnp.int32)
    t = HY_N2 * k1[None, None, :] + n2[:, None, None]
    ang = (2.0 * math.pi / nfft) * ((k1[None, :, None] * t) % nfft).astype(F32)
    gr, gi = jnp.cos(ang), -jnp.sin(ang)
    g_cplx = jnp.concatenate([jnp.concatenate([gr[..., :half], -gi[..., :half]], -1),
                              jnp.concatenate([gi[..., :half], gr[..., :half]], -1)], axis=1)
    g_real = jnp.concatenate([gr, gi], axis=1)
    hr = jnp.cos(ang).swapaxes(1, 2)[:, :half] / nfft
    hi = jnp.sin(ang).swapaxes(1, 2)[:, :half] / nfft
    g_inv = jnp.concatenate([jnp.concatenate([hr, -hi], -1), jnp.concatenate([hi, hr], -1)], axis=1)
    a2 = (2.0 * math.pi / HY_N2) * ((n2[:, None] * n2[None, :]) % HY_N2).astype(F32)
    fr, fi = jnp.cos(a2), -jnp.sin(a2)
    f2 = jnp.concatenate([jnp.concatenate([fr, -fi], -1), jnp.concatenate([fi, fr], -1)], axis=0)
    f2_inv = jnp.concatenate([jnp.concatenate([fr, fi], -1), jnp.concatenate([-fi, fr], -1)], axis=0)
    return dict(n1=n1, g_cplx=g_cplx.astype(BF16), g_real=g_real.astype(BF16), g_inv=g_inv.astype(BF16),
                f2=f2.astype(BF16), f2_inv=f2_inv.astype(BF16))


def _dft_stage1(load_rows, g_ref, s_re, s_im, n1):
    def body(n2, carry):
        a = jnp.dot(g_ref[n2], load_rows(n2).astype(BF16), preferred_element_type=F32)
        s_re[pl.ds(n2, n1, stride=HY_PITCH), :] = a[:n1]
        s_im[pl.ds(n2, n1, stride=HY_PITCH), :] = a[n1:]
        return carry

    lax.fori_loop(0, HY_N2, body, 0, unroll=HY_UNROLL)


def _dft_stage2(f2_ref, s_re, s_im, k1):
    r0 = pl.multiple_of(k1 * HY_PITCH, SUBLANES)
    sl = jnp.concatenate([s_re[pl.ds(r0, HY_N2), :], s_im[pl.ds(r0, HY_N2), :]], axis=0).astype(BF16)
    x = jnp.dot(f2_ref[...], sl, preferred_element_type=F32)
    return r0, x[:HY_N2], x[HY_N2:]


def _hyena_spectrum_kernel(x_ref, g_ref, f2_ref, kr_ref, ki_ref, s_re, s_im, *, n1):
    _dft_stage1(lambda n2: x_ref[pl.ds(n2, n1, stride=HY_PITCH), :], g_ref, s_re, s_im, n1)

    def body(k1, carry):
        _, xr, xi = _dft_stage2(f2_ref, s_re, s_im, k1)
        q0 = pl.multiple_of(k1 * HY_N2, HY_N2)
        kr_ref[pl.ds(q0, HY_N2), :] = xr
        ki_ref[pl.ds(q0, HY_N2), :] = xi
        return carry

    lax.fori_loop(0, n1, body, 0)


def _hyena_conv_kernel(xr_ref, xi_ref, g_ref, f2_ref, f2i_ref, gi_ref, kr_ref, ki_ref, yr_ref, yi_ref,
                       s_re, s_im, *, n1):
    half = n1 // 2

    def load_rows(n2):
        return jnp.concatenate([xr_ref[0, pl.ds(n2, half, stride=HY_PITCH), :],
                                xi_ref[0, pl.ds(n2, half, stride=HY_PITCH), :]], axis=0)

    _dft_stage1(load_rows, g_ref, s_re, s_im, n1)

    def spectrum_product(k1, carry):
        r0, xr, xi = _dft_stage2(f2_ref, s_re, s_im, k1)
        q0 = pl.multiple_of(k1 * HY_N2, HY_N2)
        kr, ki = kr_ref[pl.ds(q0, HY_N2), :], ki_ref[pl.ds(q0, HY_N2), :]
        y = jnp.concatenate([xr * kr - xi * ki, xr * ki + xi * kr], axis=0).astype(BF16)
        b = jnp.dot(f2i_ref[...], y, preferred_element_type=F32)
        s_re[pl.ds(r0, HY_N2), :] = b[:HY_N2]
        s_im[pl.ds(r0, HY_N2), :] = b[HY_N2:]
        return carry

    lax.fori_loop(0, n1, spectrum_product, 0, unroll=HY_UNROLL // 2)

    yr_ref[...] = jnp.zeros_like(yr_ref)
    yi_ref[...] = jnp.zeros_like(yi_ref)

    def inverse_stage1(n2, carry):
        bs = jnp.concatenate([s_re[pl.ds(n2, n1, stride=HY_PITCH), :],
                              s_im[pl.ds(n2, n1, stride=HY_PITCH), :]], axis=0).astype(BF16)
        y = jnp.dot(gi_ref[n2], bs, preferred_element_type=F32)
        yr_ref[0, pl.ds(n2, half, stride=HY_PITCH), :] = y[:half]
        yi_ref[0, pl.ds(n2, half, stride=HY_PITCH), :] = y[half:]
        return carry

    lax.fori_loop(0, HY_N2, inverse_stage1, 0, unroll=HY_UNROLL)


def _to_strided(a, rows):
    lead, c = a.shape[:-2], a.shape[-1]
    a = a.reshape(lead + (rows, HY_N2, c))
    a = jnp.pad(a, [(0, 0)] * len(lead) + [(0, 0), (0, HY_PITCH - HY_N2), (0, 0)])
    return a.reshape(lead + (rows * HY_PITCH, c))


def _from_strided(a, rows):
    lead, c = a.shape[:-2], a.shape[-1]
    return a.reshape(lead + (rows, HY_PITCH, c))[..., :HY_N2, :].reshape(lead + (rows * HY_N2, c))


def _const_spec(shape):
    return pl.BlockSpec(shape, lambda *_: (0,) * len(shape), pipeline_mode=pl.Buffered(1))


def _hyena_spectrum(k_time, tables):
    nfft, c = k_time.shape
    n1 = tables["n1"]
    spec_out = pl.BlockSpec((nfft, LANES), lambda j: (0, j))
    return pl.pallas_call(
        functools.partial(_hyena_spectrum_kernel, n1=n1),
        grid=(c // LANES,),
        in_specs=[pl.BlockSpec((n1 * HY_PITCH, LANES), lambda j: (0, j)),
                  _const_spec((HY_N2, 2 * n1, n1)), _const_spec((2 * HY_N2, 2 * HY_N2))],
        out_specs=[spec_out, spec_out],
        out_shape=[jax.ShapeDtypeStruct((nfft, c), F32)] * 2,
        scratch_shapes=[pltpu.VMEM((n1 * HY_PITCH, LANES), F32)] * 2,
        compiler_params=pltpu.CompilerParams(dimension_semantics=("arbitrary",), vmem_limit_bytes=VMEM_LIMIT),
        name="hyena_spectrum",
    )(_to_strided(k_time, n1), tables["g_real"], tables["f2"])


def _hyena_long_conv(y, kr, ki, tables):
    bsz, n, c = y.shape
    n1 = tables["n1"]
    half = n1 // 2
    pairs = bsz // 2
    xr, xi = _to_strided(y[0::2], half), _to_strided(y[1::2], half)
    spec_x = pl.BlockSpec((1, half * HY_PITCH, LANES), lambda j, p: (p, 0, j))
    spec_k = pl.BlockSpec((n1 * HY_N2, LANES), lambda j, p: (0, j), pipeline_mode=pl.Buffered(1))
    yr, yi = pl.pallas_call(
        functools.partial(_hyena_conv_kernel, n1=n1),
        grid=(c // LANES, pairs),
        in_specs=[spec_x, spec_x,
                  _const_spec((HY_N2, 2 * n1, n1)), _const_spec((2 * HY_N2, 2 * HY_N2)),
                  _const_spec((2 * HY_N2, 2 * HY_N2)), _const_spec((HY_N2, n1, 2 * n1)),
                  spec_k, spec_k],
        out_specs=[spec_x, spec_x],
        out_shape=[jax.ShapeDtypeStruct((pairs, half * HY_PITCH, c), F32)] * 2,
        scratch_shapes=[pltpu.VMEM((n1 * HY_PITCH, LANES), F32)] * 2,
        compiler_params=pltpu.CompilerParams(dimension_semantics=("arbitrary", "arbitrary"),
                                             vmem_limit_bytes=VMEM_LIMIT),
        name="hyena_conv",
    )(xr, xi, tables["g_cplx"], tables["f2"], tables["f2_inv"], tables["g_inv"], kr, ki)
    out = jnp.stack([_from_strided(yr, half), _from_strided(yi, half)], axis=1)
    return out.reshape(bsz, n, c)


def _rms_rows(v):
    return v * lax.rsqrt(jnp.mean(v * v, axis=-1, keepdims=True) + EPS)


def _mix_out_kernel(s5_ref, mla_ref, hy_ref, x_ref, g_ref, w_ref, gate_ref, n2g_ref, shift_ref, scale_ref, wr_ref,
                    xo_ref, f_ref, lg_ref):
    c1, c2 = S5_W, S5_W + MLA_W
    y = jnp.dot((_rms_rows(s5_ref[...]) * g_ref[:, :c1]).astype(BF16), w_ref[:c1, :], preferred_element_type=F32)
    y += jnp.dot((_rms_rows(mla_ref[0]) * g_ref[:, c1:c2]).astype(BF16), w_ref[c1:c2, :], preferred_element_type=F32)
    y += jnp.dot((_rms_rows(hy_ref[0]) * g_ref[:, c2:]).astype(BF16), w_ref[c2:, :], preferred_element_type=F32)
    x_new = x_ref[0] + gate_ref[0] * y
    xo_ref[0] = x_new
    f = ((_rms_rows(x_new) * n2g_ref[...]) * (1.0 + scale_ref[0]) + shift_ref[0]).astype(BF16)
    f_ref[0] = f
    lg_ref[0] = jnp.dot(f, wr_ref[...], preferred_element_type=F32)


def _mix_out(y_s5, y_mla, y_hy, x, mix_g, w_out, gate, norm2_g, shift, scale, w_route):
    bsz, n, d = x.shape
    tm = min(MIX_ROWS, n)

    def rows(width):
        return pl.BlockSpec((1, tm, width), lambda b, i: (b, i, 0))

    per_batch = pl.BlockSpec((1, 1, d), lambda b, i: (b, 0, 0))
    return pl.pallas_call(
        _mix_out_kernel,
        grid=(bsz, n // tm),
        in_specs=[pl.BlockSpec((tm, S5_W), lambda b, i: (i, b)), rows(MLA_W), rows(HY_W), rows(d),
                  _const_spec((1, d)), _const_spec((d, d)),
                  per_batch, _const_spec((1, d)), per_batch, per_batch, _const_spec((d, LANES))],
        out_specs=[rows(d), rows(d), rows(LANES)],
        out_shape=[jax.ShapeDtypeStruct((bsz, n, d), F32), jax.ShapeDtypeStruct((bsz, n, d), BF16),
                   jax.ShapeDtypeStruct((bsz, n, LANES), F32)],
        compiler_params=pltpu.CompilerParams(dimension_semantics=("arbitrary", "arbitrary"),
                                             vmem_limit_bytes=VMEM_LIMIT),
        name="mix_out",
    )(y_s5, y_mla, y_hy, x, mix_g.reshape(1, d), w_out.astype(BF16), gate, norm2_g.reshape(1, d), shift, scale,
      w_route.astype(BF16))


def _moe_kernel(be_ref, used_ref, x_ref, w1_ref, w3_ref, w2_ref, rw_ref, o_ref, w13_s, w2_s):
    i = pl.program_id(0)
    hid = w2_s.shape[0]

    @pl.when(jnp.logical_or(i == 0, be_ref[i] != be_ref[jnp.maximum(i - 1, 0)]))
    def _():
        w13_s[:, :hid] = w1_ref[0, 0].astype(BF16)
        w13_s[:, hid:] = w3_ref[0, 0].astype(BF16)
        w2_s[...] = w2_ref[0, 0].astype(BF16)

    @pl.when(i < used_ref[0])
    def _():
        ab = jnp.dot(x_ref[...], w13_s[...], preferred_element_type=F32)
        a, b = ab[:, :hid], ab[:, hid:]
        h = (a * jax.nn.sigmoid(a)) * b
        y = jnp.dot(h.astype(BF16), w2_s[...], preferred_element_type=F32)
        o_ref[...] = y * rw_ref[...]

    @pl.when(i >= used_ref[0])
    def _():
        o_ref[...] = jnp.zeros_like(o_ref)


def _moe_experts(xg, block_e, n_used, w1, w3, w2, layer, row_w):
    n_rows, d = xg.shape
    hid = w1.shape[3]
    n_blocks = n_rows // MOE_BLOCK
    return pl.pallas_call(
        _moe_kernel,
        grid_spec=pltpu.PrefetchScalarGridSpec(
            num_scalar_prefetch=2,
            grid=(n_blocks,),
            in_specs=[
                pl.BlockSpec((MOE_BLOCK, d), lambda i, be, nu: (i, 0)),
                pl.BlockSpec((1, 1, d, hid), lambda i, be, nu: (layer, be[i], 0, 0)),
                pl.BlockSpec((1, 1, d, hid), lambda i, be, nu: (layer, be[i], 0, 0)),
                pl.BlockSpec((1, 1, hid, d), lambda i, be, nu: (layer, be[i], 0, 0)),
                pl.BlockSpec((MOE_BLOCK, 1), lambda i, be, nu: (i, 0)),
            ],
            out_specs=pl.BlockSpec((MOE_BLOCK, d), lambda i, be, nu: (i, 0)),
            scratch_shapes=[pltpu.VMEM((d, 2 * hid), BF16), pltpu.VMEM((hid, d), BF16)],
        ),
        out_shape=jax.ShapeDtypeStruct((n_rows, d), F32),
        compiler_params=pltpu.CompilerParams(dimension_semantics=("arbitrary",), vmem_limit_bytes=VMEM_LIMIT),
        name="moe_experts",
    )(block_e, n_used, xg, w1, w3, w2, row_w)


def _hier_moe(h, logits, w1, w3, w2, layer):
    t, d = h.shape
    g_prob = jax.nn.softmax(logits[:, :MOE_GROUPS], axis=-1)
    g_idx = jnp.argmax(g_prob, axis=-1).astype(jnp.int32)
    g_w = jnp.max(g_prob, axis=-1, keepdims=True)
    e_logits = logits[:, MOE_GROUPS:MOE_GROUPS + MOE_EXPERTS].reshape(t, MOE_GROUPS, MOE_PER_GROUP)
    in_group = jnp.take_along_axis(e_logits, g_idx[:, None, None], axis=1)[:, 0]
    i1 = jnp.argmax(in_group, axis=-1).astype(jnp.int32)
    rest = jnp.where(jnp.arange(MOE_PER_GROUP, dtype=jnp.int32)[None, :] == i1[:, None], -jnp.inf, in_group)
    i2 = jnp.argmax(rest, axis=-1).astype(jnp.int32)
    top_v = jnp.stack([jnp.max(in_group, axis=-1), jnp.max(rest, axis=-1)], axis=-1)
    top_i = jnp.stack([i1, i2], axis=-1)
    gate = jax.nn.softmax(top_v, axis=-1) * g_w
    eid = g_idx[:, None] * MOE_PER_GROUP + top_i
    n_assign = t * MOE_TOP_K
    flat_e = eid.reshape(n_assign)
    flat_w = gate.reshape(n_assign)
    se, order = lax.sort((flat_e, jnp.arange(n_assign, dtype=jnp.int32)), num_keys=1)
    counts = jnp.sum(flat_e[None, :] == jnp.arange(MOE_EXPERTS, dtype=jnp.int32)[:, None], axis=1, dtype=jnp.int32)
    padded = (counts + MOE_BLOCK - 1) // MOE_BLOCK * MOE_BLOCK
    start = jnp.cumsum(counts) - counts
    pend = jnp.cumsum(padded)
    pstart = pend - padded
    experts = jnp.arange(MOE_EXPERTS, dtype=jnp.int32)[:, None]
    shift = jnp.sum(jnp.where(se[None, :] == experts, (pstart - start)[:, None], 0), axis=0)
    dest = (jnp.arange(n_assign, dtype=jnp.int32) + shift).astype(jnp.int32)
    n_blocks = -(-n_assign // MOE_BLOCK) + MOE_EXPERTS
    n_rows = n_blocks * MOE_BLOCK
    block_start = jnp.arange(n_blocks, dtype=jnp.int32) * MOE_BLOCK
    block_e = jnp.minimum(jnp.sum(pend[None, :] <= block_start[:, None], axis=1), MOE_EXPERTS - 1).astype(jnp.int32)
    off = (block_start - pstart[block_e])[:, None] + jnp.arange(MOE_BLOCK, dtype=jnp.int32)[None, :]
    valid = (off < counts[block_e][:, None]).reshape(n_rows)
    row_asg = order[jnp.where(valid, (start[block_e][:, None] + off).reshape(n_rows), 0)]
    row_tok = jnp.where(valid, row_asg // MOE_TOP_K, 0)
    row_w = jnp.where(valid, flat_w[row_asg], 0.0)
    xg = h[row_tok]
    n_used = (pend[-1:] // MOE_BLOCK).astype(jnp.int32)
    ys = _moe_experts(xg, block_e, n_used, w1, w3, w2, layer, row_w[:, None])
    _, slot = lax.sort((order, dest), num_keys=1)
    slot = slot.reshape(t, MOE_TOP_K)
    return ys[slot[:, 0]] + ys[slot[:, 1]]


def _final_norm_kernel(x_ref, g_ref, o_ref):
    x = x_ref[...]
    y = x * lax.rsqrt(jnp.mean(x * x, axis=-1, keepdims=True) + EPS)
    o_ref[...] = y * g_ref[...]


def _final_norm(x, g):
    t, d = x.shape
    blk = 1024
    return pl.pallas_call(
        _final_norm_kernel,
        grid=(t // blk,),
        in_specs=[pl.BlockSpec((blk, d), lambda i: (i, 0)), pl.BlockSpec((1, d), lambda i: (0, 0))],
        out_specs=pl.BlockSpec((blk, d), lambda i: (i, 0)),
        out_shape=jax.ShapeDtypeStruct((t, d), F32),
        name="final_norm",
    )(x, g.reshape(1, d))


def kernel(x, c, ctx, c_ctx, ada_w, ada_b, norm1_g, norm2_g, w_in,
           s5_lambda_re, s5_lambda_im, s5_log_dt, s5_b_re, s5_b_im, s5_c_re, s5_c_im, s5_d, s5_glu_w,
           mla_q_norm_g, mla_kv_norm_g, mla_w_uq, mla_w_ukv,
           hy_short_w, hy_short_b, hy_f_w1, hy_f_b1, hy_f_w2, hy_f_b2, hy_f_w3, hy_f_b3, hy_f_freq, hy_bias,
           mix_norm_g, w_out, moe_w_group, moe_w_expert, moe_w1, moe_w3, moe_w2, final_g):
    bsz, n, d = x.shape
    n_ctx = ctx.shape[1]
    xl, xc = x, ctx
    act_l = jax.nn.silu(c)
    act_c = jax.nn.silu(c_ctx)
    rope_cos, rope_sin = _rope_slot_tables(n_ctx, n)
    for i in range(DEPTH):
        ctx_out = i < DEPTH - 1
        mod_l = jnp.split((act_l @ ada_w[i] + ada_b[i])[:, None, :], 6, axis=-1)
        mod_c = jnp.split((act_c @ ada_w[i] + ada_b[i])[None, None, :], 6, axis=-1)
        mod_c = [jnp.broadcast_to(m, (bsz, 1, d)) for m in mod_c]
        w_ext, w_q = _proj_in_weights(w_in[i], mla_w_uq[i])
        proj = functools.partial(_proj_in, w_ext=w_ext, q_norm_g=mla_q_norm_g[i], w_q=w_q,
                                 kv_norm_g=mla_kv_norm_g[i], w_ukv=mla_w_ukv[i])
        u_l, q_l, kv_l, kr_l, hz_l = proj(xl, norm1_g[i], mod_l[0], mod_l[1], cos=rope_cos[n_ctx:], sin=rope_sin[n_ctx:])
        u_c, q_c, kv_c, kr_c, hz_c = proj(xc, norm1_g[i], mod_c[0], mod_c[1], cos=rope_cos[:n_ctx], sin=rope_sin[:n_ctx])
        s5_all = _s5_mixer(jnp.concatenate([u_c, u_l], axis=0), n_ctx, s5_lambda_re[i], s5_lambda_im[i],
                           s5_log_dt[i], s5_b_re[i], s5_b_im[i], s5_c_re[i], s5_c_im[i], s5_d[i], s5_glu_w[i])
        kv_all = jnp.concatenate([kv_c, kv_l], axis=1)
        kr_all = jnp.concatenate([kr_c, kr_l], axis=1)
        mla_l = _attention(q_l, kv_all, kr_all)
        filt = (hy_f_w1[i], hy_f_b1[i], hy_f_w2[i], hy_f_b2[i], hy_f_w3[i], hy_f_b3[i], hy_f_freq[i])
        hy_l = _hyena_mixer(hz_l, hy_short_w[i], hy_short_b[i], filt, hy_bias[i])
        w_route = jnp.pad(jnp.concatenate([moe_w_group[i], moe_w_expert[i]], axis=1),
                          ((0, 0), (0, LANES - MOE_GROUPS - MOE_EXPERTS)))
        xl, fl, lg_l = _mix_out(s5_all[n_ctx:], mla_l, hy_l, xl, mix_norm_g[i], w_out[i], mod_l[2], norm2_g[i],
                                mod_l[3], mod_l[4], w_route)
        fl, lg_l = fl.reshape(bsz * n, d), lg_l.reshape(bsz * n, LANES)
        if ctx_out:
            mla_c = _attention(q_c, kv_c, kr_c)
            hy_c = _hyena_mixer(hz_c, hy_short_w[i], hy_short_b[i], filt, hy_bias[i])
            xc, fc, lg_c = _mix_out(s5_all[:n_ctx], mla_c, hy_c, xc, mix_norm_g[i], w_out[i], mod_c[2], norm2_g[i],
                                    mod_c[3], mod_c[4], w_route)
            y = _hier_moe(jnp.concatenate([fl, fc.reshape(bsz * n_ctx, d)], axis=0),
                          jnp.concatenate([lg_l, lg_c.reshape(bsz * n_ctx, LANES)], axis=0),
                          moe_w1, moe_w3, moe_w2, i)
            xl = xl + mod_l[5] * y[:bsz * n].reshape(bsz, n, d)
            xc = xc + mod_c[5] * y[bsz * n:].reshape(bsz, n_ctx, d)
        else:
            y = _hier_moe(fl, lg_l, moe_w1, moe_w3, moe_w2, i)
            xl = xl + mod_l[5] * y.reshape(bsz, n, d)
    return _final_norm(xl.reshape(bsz * n, d), final_g).reshape(bsz, n, d)
```

```python
import functools
import math

import jax
import jax.numpy as jnp
from jax import lax
from jax.experimental import pallas as pl
from jax.experimental.pallas import tpu as pltpu

D_MODEL = 1024
DEPTH = 4
GRID_W = 64
EPS = 1e-6

MIX_W = D_MODEL
S5_W = D_MODEL // 4
S5_GROUP = 16
S5_GROUPS = S5_W // S5_GROUP
S5_STATE = 64
S5_N = S5_GROUPS * S5_STATE
MLA_V = 64
MLA_W = D_MODEL // 2
MLA_HEADS = MLA_W // MLA_V
MLA_NOPE = 64
MLA_ROPE = 32
MLA_QK = MLA_NOPE + MLA_ROPE
MLA_Q_RANK = 384
MLA_KV_RANK = 256
MLA_SCALE = 1.0 / math.sqrt(MLA_NOPE + MLA_ROPE)
ROPE_BASE = 10000.0
HY_W = D_MODEL // 4
HY_ORDER = 2
HY_POS_EMB = 33
HY_FILTER_W = 64
HY_MIN_DECAY = math.log(1e-2) / 1.5
HY_MAX_DECAY = math.log(1e-2) / 0.3
MOE_GROUPS = 4
MOE_PER_GROUP = 8
MOE_EXPERTS = MOE_GROUPS * MOE_PER_GROUP
MOE_TOP_K = 2
MOE_HIDDEN = 512
MOE_BLOCK = 256

SUBLANES = 8
S5_CHUNK = 128
MM_ROWS = 256
ATTN_Q_BLOCK = 256
MIX_ROWS = 512
LANES = 128
ATTN_HEADS_PER_STEP = 4
VMEM_LIMIT = 48 * 1024 * 1024

F32 = jnp.float32
BF16 = jnp.bfloat16


def _rms_norm(x, g):
    xf = x.astype(F32)
    y = xf * lax.rsqrt(jnp.mean(xf * xf, axis=-1, keepdims=True) + EPS)
    return (y * g.astype(F32)).astype(x.dtype)


def _modulate(x, g, shift, scale):
    return _rms_norm(x, g) * (1.0 + scale) + shift


def _split_projection(p):
    o1 = S5_W
    o2 = o1 + MLA_Q_RANK
    o3 = o2 + MLA_KV_RANK
    o4 = o3 + MLA_ROPE
    return p[..., :o1], p[..., o1:o2], p[..., o2:o3], p[..., o3:o4], p[..., o4:]


def _s5_matrices(lam_re, lam_im, log_dt, b_re, b_im, c_re, c_im):
    g, p, h = S5_GROUPS, S5_STATE, S5_GROUP
    dt = jnp.exp(log_dt)[..., None]
    mag = jnp.exp(lam_re * dt)
    ar, ai = mag * jnp.cos(lam_im * dt), mag * jnp.sin(lam_im * dt)
    den = lam_re * lam_re + lam_im * lam_im
    fr = ((ar - 1.0) * lam_re + ai * lam_im) / den
    fi = (ai * lam_re - (ar - 1.0) * lam_im) / den
    bbr = fr[..., None] * b_re - fi[..., None] * b_im
    bbi = fr[..., None] * b_im + fi[..., None] * b_re
    eye = jnp.eye(g, dtype=F32)

    def block_in(m):
        return jnp.einsum('kgph,gj->kghjp', m, eye).reshape(2, g * h, g * p)

    def block_out(m):
        return jnp.einsum('kghp,gj->kgpjh', m, eye).reshape(2, g * p, g * h)

    bmat = jnp.concatenate([block_in(bbr), block_in(bbi)], axis=-1)
    cmat = jnp.concatenate([block_out(c_re), -block_out(c_im)], axis=1)
    lam = jnp.stack([ar[0], ai[0], ar[1], ai[1]]).reshape(4, 1, g * p)
    lam = jnp.broadcast_to(lam, (4, SUBLANES, g * p)).reshape(4 * SUBLANES, g * p)
    return bmat.astype(BF16), cmat.astype(BF16), lam


def _s5_kernel(uf_ref, ub_ref, bmat_ref, cmat_ref, lam_ref, yf_ref, yb_ref, buf_f, buf_b, state):
    rows = uf_ref.shape[0]
    steps = rows // SUBLANES
    n = S5_N
    s = SUBLANES

    @pl.when(pl.program_id(0) == 0)
    def _():
        state[...] = jnp.zeros_like(state)

    def drive(r, carry):
        r0 = pl.multiple_of(r * MM_ROWS, MM_ROWS)
        buf_f[pl.ds(r0, MM_ROWS), :] = jnp.dot(uf_ref[pl.ds(r0, MM_ROWS), :].astype(BF16), bmat_ref[0],
                                               preferred_element_type=F32)
        buf_b[pl.ds(r0, MM_ROWS), :] = jnp.dot(ub_ref[pl.ds(r0, MM_ROWS), :].astype(BF16), bmat_ref[1],
                                               preferred_element_type=F32)
        return carry

    lax.fori_loop(0, rows // MM_ROWS, drive, 0)

    def step(j, carry):
        fr, fi, br, bi = carry
        rf = pl.multiple_of(j * s, s)
        rb = pl.multiple_of((steps - 1 - j) * s, s)
        lfr, lfi = lam_ref[0:s, :], lam_ref[s:2 * s, :]
        lbr, lbi = lam_ref[2 * s:3 * s, :], lam_ref[3 * s:4 * s, :]
        nfr = lfr * fr - lfi * fi + buf_f[pl.ds(rf, s), 0:n]
        nfi = lfr * fi + lfi * fr + buf_f[pl.ds(rf, s), n:2 * n]
        nbr = lbr * br - lbi * bi + buf_b[pl.ds(rb, s), 0:n]
        nbi = lbr * bi + lbi * br + buf_b[pl.ds(rb, s), n:2 * n]
        buf_f[pl.ds(rf, s), 0:n] = nfr
        buf_f[pl.ds(rf, s), n:2 * n] = nfi
        buf_b[pl.ds(rb, s), 0:n] = nbr
        buf_b[pl.ds(rb, s), n:2 * n] = nbi
        return nfr, nfi, nbr, nbi

    init = (state[0:s, :], state[s:2 * s, :], state[2 * s:3 * s, :], state[3 * s:4 * s, :])
    fr, fi, br, bi = lax.fori_loop(0, steps, step, init, unroll=2)
    state[0:s, :] = fr
    state[s:2 * s, :] = fi
    state[2 * s:3 * s, :] = br
    state[3 * s:4 * s, :] = bi

    def readout(r, carry):
        r0 = pl.multiple_of(r * MM_ROWS, MM_ROWS)
        yf_ref[pl.ds(r0, MM_ROWS), :] = jnp.dot(buf_f[pl.ds(r0, MM_ROWS), :].astype(BF16), cmat_ref[0],
                                                preferred_element_type=F32)
        yb_ref[pl.ds(r0, MM_ROWS), :] = jnp.dot(buf_b[pl.ds(r0, MM_ROWS), :].astype(BF16), cmat_ref[1],
                                                preferred_element_type=F32)
        return carry

    lax.fori_loop(0, rows // MM_ROWS, readout, 0)


def _s5_scan(u_tm, bmat, cmat, lam, n_ctx_chunks, chunk):
    rows_all, w = u_tm.shape
    rows = chunk * SUBLANES
    n_chunks = rows_all // rows
    n2 = 2 * S5_N

    def bwd_chunk(i):
        return jnp.where(i < n_ctx_chunks, n_ctx_chunks - 1 - i, n_chunks - 1 + n_ctx_chunks - i)

    return pl.pallas_call(
        _s5_kernel,
        grid=(n_chunks,),
        in_specs=[
            pl.BlockSpec((rows, w), lambda i: (i, 0)),
            pl.BlockSpec((rows, w), lambda i: (bwd_chunk(i), 0)),
            pl.BlockSpec((2, w, n2), lambda i: (0, 0, 0)),
            pl.BlockSpec((2, n2, w), lambda i: (0, 0, 0)),
            pl.BlockSpec((4 * SUBLANES, S5_N), lambda i: (0, 0)),
        ],
        out_specs=[
            pl.BlockSpec((rows, w), lambda i: (i, 0)),
            pl.BlockSpec((rows, w), lambda i: (bwd_chunk(i), 0)),
        ],
        out_shape=[jax.ShapeDtypeStruct((rows_all, w), F32)] * 2,
        scratch_shapes=[
            pltpu.VMEM((rows, n2), F32),
            pltpu.VMEM((rows, n2), F32),
            pltpu.VMEM((4 * SUBLANES, S5_N), F32),
        ],
        compiler_params=pltpu.CompilerParams(dimension_semantics=("arbitrary",), vmem_limit_bytes=VMEM_LIMIT),
        name="s5_scan",
    )(u_tm, u_tm, bmat, cmat, lam)


def _s5_mixer(u_tm, n_ctx, lam_re, lam_im, log_dt, b_re, b_im, c_re, c_im, d, glu_w):
    n_all, bw = u_tm.shape
    bsz, w = bw // S5_W, S5_W
    assert bsz == SUBLANES and n_all % S5_CHUNK == 0 and n_ctx % S5_CHUNK == 0
    bmat, cmat, lam = _s5_matrices(lam_re, lam_im, log_dt, b_re, b_im, c_re, c_im)
    u_rows = u_tm.reshape(n_all * bsz, w)
    y_f, y_b = _s5_scan(u_rows, bmat, cmat, lam, n_ctx // S5_CHUNK, S5_CHUNK)
    y = y_f + y_b + d * u_rows
    y = jax.nn.gelu(y)
    y = y * jax.nn.sigmoid(y @ glu_w)
    return y.reshape(n_all, bw)


def _axial_rope_tables(n_tokens):
    rows = n_tokens // GRID_W
    row = jnp.repeat(jnp.arange(rows), GRID_W).astype(F32)
    col = jnp.tile(jnp.arange(GRID_W), rows).astype(F32)
    half = MLA_ROPE // 2
    inv = ROPE_BASE ** (-jnp.arange(0, half, 2, dtype=F32) / half)
    ang_r = row[:, None] * inv
    ang_c = col[:, None] * inv
    cos = jnp.concatenate([jnp.cos(ang_r)] * 2 + [jnp.cos(ang_c)] * 2, axis=-1)
    sin = jnp.concatenate([jnp.sin(ang_r)] * 2 + [jnp.sin(ang_c)] * 2, axis=-1)
    return cos, sin


def _rope_partner(w):
    q = MLA_ROPE // 4
    parts = []
    for g in range(2):
        x1, x2 = w[..., 2 * g * q:(2 * g + 1) * q], w[..., (2 * g + 1) * q:(2 * g + 2) * q]
        parts += [-x2, x1]
    return jnp.concatenate(parts, axis=-1)


def _attn_kernel(q_ref, kv_ref, kr_ref, o_ref, kcat, vaug):
    hs, nk = kcat.shape[0], kcat.shape[1]
    dn = MLA_NOPE

    @pl.when(pl.program_id(2) == 0)
    def _():
        lane = lax.broadcasted_iota(jnp.int32, (nk, LANES), 1)
        ones_col = jnp.where(lane == 0, 1.0, 0.0).astype(BF16)
        for h in range(hs):
            kvh = kv_ref[0, :, h * LANES:(h + 1) * LANES]
            kcat[h] = jnp.where(lane < dn, kvh, kr_ref[0])
            vaug[h] = jnp.where(lane >= dn, kvh, ones_col)

    outs = []
    for h in range(hs):
        s = lax.dot_general(q_ref[0, :, h * LANES:(h + 1) * LANES], kcat[h], (((1,), (1,)), ((), ())),
                            preferred_element_type=F32)
        p = jnp.exp(s - jnp.max(s, axis=-1, keepdims=True))
        o = jnp.dot(p.astype(BF16), vaug[h], preferred_element_type=F32)
        outs.append(o[:, dn:] / o[:, 0:1])
    o_ref[0] = jnp.concatenate(outs, axis=-1)


def _attention(q, kv, kr):
    bsz, nq, _ = q.shape
    nk = kv.shape[1]
    heads, dv = MLA_HEADS, MLA_V
    tq = min(ATTN_Q_BLOCK, nq)
    hs = ATTN_HEADS_PER_STEP
    return pl.pallas_call(
        _attn_kernel,
        grid=(bsz, heads // hs, nq // tq),
        in_specs=[
            pl.BlockSpec((1, tq, hs * LANES), lambda b, h, i: (b, i, h)),
            pl.BlockSpec((1, nk, hs * LANES), lambda b, h, i: (b, 0, h), pipeline_mode=pl.Buffered(1)),
            pl.BlockSpec((1, nk, LANES), lambda b, h, i: (b, 0, 0), pipeline_mode=pl.Buffered(1)),
        ],
        out_specs=pl.BlockSpec((1, tq, hs * dv), lambda b, h, i: (b, i, h)),
        out_shape=jax.ShapeDtypeStruct((bsz, nq, heads * dv), F32),
        scratch_shapes=[pltpu.VMEM((hs, nk, LANES), BF16), pltpu.VMEM((hs, nk, LANES), BF16)],
        compiler_params=pltpu.CompilerParams(
            dimension_semantics=("arbitrary", "arbitrary", "arbitrary"), vmem_limit_bytes=VMEM_LIMIT),
        name="mla_attention",
    )(q, kv, kr)


PIN_U = 0
PIN_CQ = PIN_U + S5_W
PIN_CKV = PIN_CQ + MLA_Q_RANK
PIN_HZ = PIN_CKV + MLA_KV_RANK
PIN_KR = PIN_HZ + 3 * HY_W
PIN_KRP = PIN_KR + LANES
PIN_COLS = PIN_KRP + LANES


def _proj_in_weights(w_in, w_uq):
    o1 = S5_W
    o2 = o1 + MLA_Q_RANK
    o3 = o2 + MLA_KV_RANK
    o4 = o3 + MLA_ROPE
    pad = LANES - MLA_QK
    w_kr = w_in[:, o3:o4]
    in_slot = lambda w: jnp.pad(w, ((0, 0), (MLA_NOPE, pad)))
    w_ext = jnp.concatenate([w_in[:, :o3], w_in[:, o4:], in_slot(w_kr), in_slot(_rope_partner(w_kr))], axis=1)
    w_h = w_uq.reshape(MLA_Q_RANK, MLA_HEADS, MLA_QK)
    w_rot = jnp.concatenate([jnp.zeros_like(w_h[..., :MLA_NOPE]), _rope_partner(w_h[..., MLA_NOPE:])], axis=-1)
    w_q = jnp.concatenate([jnp.pad(w, ((0, 0), (0, 0), (0, pad))).reshape(MLA_Q_RANK, MLA_HEADS * LANES)
                           for w in (w_h, w_rot)], axis=-1)
    return w_ext.astype(BF16), w_q.astype(BF16)


def _rope_slot_tables(n_ctx, n):
    cos, sin = _axial_rope_tables(n)
    pad = LANES - MLA_QK
    cos = jnp.concatenate([jnp.ones((n, MLA_NOPE), F32), cos, jnp.ones((n, pad), F32)], axis=-1)
    sin = jnp.concatenate([jnp.zeros((n, MLA_NOPE), F32), sin, jnp.zeros((n, pad), F32)], axis=-1)
    return (jnp.concatenate([jnp.ones((n_ctx, LANES), F32), cos], axis=0),
            jnp.concatenate([jnp.zeros((n_ctx, LANES), F32), sin], axis=0))


def _apply_pending(x_ref, pending_refs):
    y0_ref, y1_ref, gate_ref = pending_refs
    return x_ref[0] + gate_ref[0] * (y0_ref[...] + y1_ref[...])


def _pending_specs(pending, tm, blocks_per_batch, d):
    if pending is None:
        return [], []
    y0, y1, gate, row0 = pending
    blk0 = row0 // tm
    rows = pl.BlockSpec((tm, d), lambda b, i: (blk0 + b * blocks_per_batch + i, 0))
    return [rows, rows, pl.BlockSpec((1, 1, d), lambda b, i: (b, 0, 0))], [y0, y1, gate]


def _proj_in_kernel(*refs, has_pending):
    if has_pending:
        x_ref, *pending_refs = refs[:4]
        refs = refs[4:]
    else:
        x_ref, refs = refs[0], refs[1:]
    (g_ref, shift_ref, scale_ref, w_ref, qg_ref, wq_ref, kvg_ref, wkv_ref, cos_ref, sin_ref,
     u_ref, q_ref, kv_ref, kr_ref, hz_ref, *xo_ref) = refs
    if has_pending:
        x = _apply_pending(x_ref, pending_refs)
        xo_ref[0][0] = x
    else:
        x = x_ref[0]
    h = ((_rms_rows(x) * g_ref[...]) * (1.0 + scale_ref[0]) + shift_ref[0]).astype(BF16)
    p = jnp.dot(h, w_ref[...], preferred_element_type=F32)
    u_ref[...] = p[:, PIN_U:PIN_CQ]
    hz_ref[0] = p[:, PIN_HZ:PIN_KR]
    cos, sin = cos_ref[...], sin_ref[...]
    kr_ref[0] = (p[:, PIN_KR:PIN_KRP] * cos + p[:, PIN_KRP:PIN_COLS] * sin).astype(BF16)
    cq = (_rms_rows(p[:, PIN_CQ:PIN_CKV]) * qg_ref[...]).astype(BF16)
    q2 = jnp.dot(cq, wq_ref[...], preferred_element_type=F32)
    hw = MLA_HEADS * LANES
    for hd in range(MLA_HEADS):
        a, b = q2[:, hd * LANES:(hd + 1) * LANES], q2[:, hw + hd * LANES:hw + (hd + 1) * LANES]
        q_ref[0, :, hd * LANES:(hd + 1) * LANES] = ((a * cos + b * sin) * MLA_SCALE).astype(BF16)
    ckv = (_rms_rows(p[:, PIN_CKV:PIN_HZ]) * kvg_ref[...]).astype(BF16)
    kv_ref[0] = jnp.dot(ckv, wkv_ref[...], preferred_element_type=F32).astype(BF16)


def _proj_in(x, pending, norm_g, shift, scale, w_ext, q_norm_g, w_q, kv_norm_g, w_ukv, cos, sin):
    bsz, n, d = x.shape
    tm = min(MIX_ROWS, n)
    hw = MLA_HEADS * LANES

    def rows(width):
        return pl.BlockSpec((1, tm, width), lambda b, i: (b, i, 0))

    per_batch = pl.BlockSpec((1, 1, d), lambda b, i: (b, 0, 0))
    table = pl.BlockSpec((tm, LANES), lambda b, i: (i, 0))
    pend_specs, pend_args = _pending_specs(pending, tm, n // tm, d)
    has_pending = pending is not None
    out_specs = [pl.BlockSpec((tm, S5_W), lambda b, i: (i, b)), rows(hw), rows(hw), rows(LANES), rows(3 * HY_W)]
    out_shape = [jax.ShapeDtypeStruct((n, bsz * S5_W), F32), jax.ShapeDtypeStruct((bsz, n, hw), BF16),
                 jax.ShapeDtypeStruct((bsz, n, hw), BF16), jax.ShapeDtypeStruct((bsz, n, LANES), BF16),
                 jax.ShapeDtypeStruct((bsz, n, 3 * HY_W), F32)]
    if has_pending:
        out_specs.append(rows(d))
        out_shape.append(jax.ShapeDtypeStruct((bsz, n, d), F32))
    outs = pl.pallas_call(
        functools.partial(_proj_in_kernel, has_pending=has_pending),
        grid=(bsz, n // tm),
        in_specs=[rows(d), *pend_specs, _const_spec((1, d)), per_batch, per_batch, _const_spec((d, PIN_COLS)),
                  _const_spec((1, MLA_Q_RANK)), _const_spec((MLA_Q_RANK, 2 * hw)),
                  _const_spec((1, MLA_KV_RANK)), _const_spec((MLA_KV_RANK, hw)), table, table],
        out_specs=out_specs,
        out_shape=out_shape,
        compiler_params=pltpu.CompilerParams(dimension_semantics=("arbitrary", "arbitrary"),
                                             vmem_limit_bytes=VMEM_LIMIT),
        name="proj_in",
    )(x, *pend_args, norm_g.reshape(1, d), shift, scale, w_ext, q_norm_g.reshape(1, -1), w_q,
      kv_norm_g.reshape(1, -1), w_ukv.astype(BF16), cos, sin)
    return (*outs[:5], outs[5] if has_pending else x)


def _short_conv(z, w, b):
    n = z.shape[1]
    zp = jnp.pad(z, ((0, 0), (1, 1), (0, 0)))
    return zp[:, :n] * w[0] + zp[:, 1:n + 1] * w[1] + zp[:, 2:] * w[2] + b


def _hyena_filter_taps(n, w1, b1, w2, b2, w3, b3, freq):
    t = jnp.linspace(0.0, 1.0, n, dtype=F32)[:, None]
    bands = (HY_POS_EMB - 1) // 2
    f = jnp.linspace(1e-4, bands - 1, bands, dtype=F32)
    ang = (2.0 * math.pi * jnp.arange(n, dtype=F32) / n)[:, None] * f
    z = jnp.concatenate([t, jnp.cos(ang), -jnp.sin(ang)], axis=-1)
    h = jnp.sin(freq[0] * (z @ w1 + b1))
    h = jnp.sin(freq[1] * (h @ w2 + b2))
    h = (h @ w3 + b3).astype(F32).reshape(n, 2, HY_ORDER, HY_W)
    deltas = jnp.abs(jnp.linspace(HY_MIN_DECAY, HY_MAX_DECAY, HY_W, dtype=F32))
    h = h * jnp.exp(-t[:, :, None, None] * deltas)
    fwd, bwd = h[:, 0], h[:, 1]
    return jnp.concatenate([fwd, jnp.zeros((1, HY_ORDER, HY_W), F32), bwd[:0:-1]], axis=0)


def _fft_long_conv(y, k_f):
    n = y.shape[1]
    yf = jnp.fft.rfft(y, n=2 * n, axis=1)
    return jnp.fft.irfft(yf * k_f, n=2 * n, axis=1)[:, :n]


def _hyena_mixer(z, short_w, short_b, filt, bias):
    n = z.shape[1]
    z = _short_conv(z, short_w, short_b)
    v, x1, x2 = jnp.split(z, 3, axis=-1)
    y = v.astype(F32)
    if (2 * n) % HY_N2 == 0 and (2 * n) // HY_N2 >= 2 * SUBLANES:
        k_time = _hyena_filter_taps(n, *filt)
        tables = _dft_tables(2 * n)
        kr, ki = _hyena_spectrum(k_time.reshape(2 * n, HY_ORDER * HY_W), tables)
        for o, gate in enumerate((x1, x2)):
            cols = slice(o * HY_W, (o + 1) * HY_W)
            y = gate.astype(F32) * (_hyena_long_conv(y, kr[:, cols], ki[:, cols], tables) + y * bias[o].astype(F32))
    else:
        k_f = jnp.fft.rfft(_hyena_filter_taps(n, *filt), axis=0)
        for o, gate in enumerate((x1, x2)):
            y = gate.astype(F32) * (_fft_long_conv(y, k_f[:, o]) + y * bias[o].astype(F32))
    return y.astype(z.dtype)


HY_N2 = 128
HY_PITCH = 136
HY_UNROLL = 8


def _dft_tables(nfft):
    n1 = nfft // HY_N2
    half = n1 // 2
    k1 = jnp.arange(n1, dtype=jnp.int32)
    n2 = jnp.arange(HY_N2, dtype=jnp.int32)
    t = HY_N2 * k1[None, None, :] + n2[:, None, None]
    ang = (2.0 * math.pi / nfft) * ((k1[None, :, None] * t) % nfft).astype(F32)
    gr, gi = jnp.cos(ang), -jnp.sin(ang)
    g_cplx = jnp.concatenate([jnp.concatenate([gr[..., :half], -gi[..., :half]], -1),
                              jnp.concatenate([gi[..., :half], gr[..., :half]], -1)], axis=1)
    g_real = jnp.concatenate([gr, gi], axis=1)
    hr = jnp.cos(ang).swapaxes(1, 2)[:, :half] / nfft
    hi = jnp.sin(ang).swapaxes(1, 2)[:, :half] / nfft
    g_inv = jnp.concatenate([jnp.concatenate([hr, -hi], -1), jnp.concatenate([hi, hr], -1)], axis=1)
    a2 = (2.0 * math.pi / HY_N2) * ((n2[:, None] * n2[None, :]) % HY_N2).astype(F32)
    fr, fi = jnp.cos(a2), -jnp.sin(a2)
    f2 = jnp.concatenate([jnp.concatenate([fr, -fi], -1), jnp.concatenate([fi, fr], -1)], axis=0)
    f2_inv = jnp.concatenate([jnp.concatenate([fr, fi], -1), jnp.concatenate([-fi, fr], -1)], axis=0)
    return dict(n1=n1, g_cplx=g_cplx.astype(BF16), g_real=g_real.astype(BF16), g_inv=g_inv.astype(BF16),
                f2=f2.astype(BF16), f2_inv=f2_inv.astype(BF16))


def _dft_stage1(load_rows, g_ref, s_re, s_im, n1):
    def body(n2, carry):
        a = jnp.dot(g_ref[n2], load_rows(n2).astype(BF16), preferred_element_type=F32)
        s_re[pl.ds(n2, n1, stride=HY_PITCH), :] = a[:n1]
        s_im[pl.ds(n2, n1, stride=HY_PITCH), :] = a[n1:]
        return carry

    lax.fori_loop(0, HY_N2, body, 0, unroll=HY_UNROLL)


def _dft_stage2(f2_ref, s_re, s_im, k1):
    r0 = pl.multiple_of(k1 * HY_PITCH, SUBLANES)
    sl = jnp.concatenate([s_re[pl.ds(r0, HY_N2), :], s_im[pl.ds(r0, HY_N2), :]], axis=0).astype(BF16)
    x = jnp.dot(f2_ref[...], sl, preferred_element_type=F32)
    return r0, x[:HY_N2], x[HY_N2:]


def _hyena_spectrum_kernel(x_ref, g_ref, f2_ref, kr_ref, ki_ref, s_re, s_im, *, n1):
    _dft_stage1(lambda n2: x_ref[pl.ds(n2, n1, stride=HY_PITCH), :], g_ref, s_re, s_im, n1)

    def body(k1, carry):
        _, xr, xi = _dft_stage2(f2_ref, s_re, s_im, k1)
        q0 = pl.multiple_of(k1 * HY_N2, HY_N2)
        kr_ref[pl.ds(q0, HY_N2), :] = xr
        ki_ref[pl.ds(q0, HY_N2), :] = xi
        return carry

    lax.fori_loop(0, n1, body, 0)


def _hyena_conv_kernel(xr_ref, xi_ref, g_ref, f2_ref, f2i_ref, gi_ref, kr_ref, ki_ref, yr_ref, yi_ref,
                       s_re, s_im, *, n1):
    half = n1 // 2

    def load_rows(n2):
        return jnp.concatenate([xr_ref[0, pl.ds(n2, half, stride=HY_PITCH), :],
                                xi_ref[0, pl.ds(n2, half, stride=HY_PITCH), :]], axis=0)

    _dft_stage1(load_rows, g_ref, s_re, s_im, n1)

    def spectrum_product(k1, carry):
        r0, xr, xi = _dft_stage2(f2_ref, s_re, s_im, k1)
        q0 = pl.multiple_of(k1 * HY_N2, HY_N2)
        kr, ki = kr_ref[pl.ds(q0, HY_N2), :], ki_ref[pl.ds(q0, HY_N2), :]
        y = jnp.concatenate([xr * kr - xi * ki, xr * ki + xi * kr], axis=0).astype(BF16)
        b = jnp.dot(f2i_ref[...], y, preferred_element_type=F32)
        s_re[pl.ds(r0, HY_N2), :] = b[:HY_N2]
        s_im[pl.ds(r0, HY_N2), :] = b[HY_N2:]
        return carry

    lax.fori_loop(0, n1, spectrum_product, 0, unroll=HY_UNROLL // 2)

    yr_ref[...] = jnp.zeros_like(yr_ref)
    yi_ref[...] = jnp.zeros_like(yi_ref)

    def inverse_stage1(n2, carry):
        bs = jnp.concatenate([s_re[pl.ds(n2, n1, stride=HY_PITCH), :],
                              s_im[pl.ds(n2, n1, stride=HY_PITCH), :]], axis=0).astype(BF16)
        y = jnp.dot(gi_ref[n2], bs, preferred_element_type=F32)
        yr_ref[0, pl.ds(n2, half, stride=HY_PITCH), :] = y[:half]
        yi_ref[0, pl.ds(n2, half, stride=HY_PITCH), :] = y[half:]
        return carry

    lax.fori_loop(0, HY_N2, inverse_stage1, 0, unroll=HY_UNROLL)


def _to_strided(a, rows):
    lead, c = a.shape[:-2], a.shape[-1]
    a = a.reshape(lead + (rows, HY_N2, c))
    a = jnp.pad(a, [(0, 0)] * len(lead) + [(0, 0), (0, HY_PITCH - HY_N2), (0, 0)])
    return a.reshape(lead + (rows * HY_PITCH, c))


def _from_strided(a, rows):
    lead, c = a.shape[:-2], a.shape[-1]
    return a.reshape(lead + (rows, HY_PITCH, c))[..., :HY_N2, :].reshape(lead + (rows * HY_N2, c))


def _const_spec(shape):
    return pl.BlockSpec(shape, lambda *_: (0,) * len(shape), pipeline_mode=pl.Buffered(1))


def _hyena_spectrum(k_time, tables):
    nfft, c = k_time.shape
    n1 = tables["n1"]
    spec_out = pl.BlockSpec((nfft, LANES), lambda j: (0, j))
    return pl.pallas_call(
        functools.partial(_hyena_spectrum_kernel, n1=n1),
        grid=(c // LANES,),
        in_specs=[pl.BlockSpec((n1 * HY_PITCH, LANES), lambda j: (0, j)),
                  _const_spec((HY_N2, 2 * n1, n1)), _const_spec((2 * HY_N2, 2 * HY_N2))],
        out_specs=[spec_out, spec_out],
        out_shape=[jax.ShapeDtypeStruct((nfft, c), F32)] * 2,
        scratch_shapes=[pltpu.VMEM((n1 * HY_PITCH, LANES), F32)] * 2,
        compiler_params=pltpu.CompilerParams(dimension_semantics=("arbitrary",), vmem_limit_bytes=VMEM_LIMIT),
        name="hyena_spectrum",
    )(_to_strided(k_time, n1), tables["g_real"], tables["f2"])


def _hyena_long_conv(y, kr, ki, tables):
    bsz, n, c = y.shape
    n1 = tables["n1"]
    half = n1 // 2
    pairs = bsz // 2
    xr, xi = _to_strided(y[0::2], half), _to_strided(y[1::2], half)
    spec_x = pl.BlockSpec((1, half * HY_PITCH, LANES), lambda j, p: (p, 0, j))
    spec_k = pl.BlockSpec((n1 * HY_N2, LANES), lambda j, p: (0, j), pipeline_mode=pl.Buffered(1))
    yr, yi = pl.pallas_call(
        functools.partial(_hyena_conv_kernel, n1=n1),
        grid=(c // LANES, pairs),
        in_specs=[spec_x, spec_x,
                  _const_spec((HY_N2, 2 * n1, n1)), _const_spec((2 * HY_N2, 2 * HY_N2)),
                  _const_spec((2 * HY_N2, 2 * HY_N2)), _const_spec((HY_N2, n1, 2 * n1)),
                  spec_k, spec_k],
        out_specs=[spec_x, spec_x],
        out_shape=[jax.ShapeDtypeStruct((pairs, half * HY_PITCH, c), F32)] * 2,
        scratch_shapes=[pltpu.VMEM((n1 * HY_PITCH, LANES), F32)] * 2,
        compiler_params=pltpu.CompilerParams(dimension_semantics=("arbitrary", "arbitrary"),
                                             vmem_limit_bytes=VMEM_LIMIT),
        name="hyena_conv",
    )(xr, xi, tables["g_cplx"], tables["f2"], tables["f2_inv"], tables["g_inv"], kr, ki)
    out = jnp.stack([_from_strided(yr, half), _from_strided(yi, half)], axis=1)
    return out.reshape(bsz, n, c)


def _rms_rows(v):
    return v * lax.rsqrt(jnp.mean(v * v, axis=-1, keepdims=True) + EPS)


def _mix_out_kernel(s5_ref, mla_ref, hy_ref, x_ref, g_ref, w_ref, gate_ref, n2g_ref, shift_ref, scale_ref, wr_ref,
                    xo_ref, f_ref, lg_ref):
    c1, c2 = S5_W, S5_W + MLA_W
    y = jnp.dot((_rms_rows(s5_ref[...]) * g_ref[:, :c1]).astype(BF16), w_ref[:c1, :], preferred_element_type=F32)
    y += jnp.dot((_rms_rows(mla_ref[0]) * g_ref[:, c1:c2]).astype(BF16), w_ref[c1:c2, :], preferred_element_type=F32)
    y += jnp.dot((_rms_rows(hy_ref[0]) * g_ref[:, c2:]).astype(BF16), w_ref[c2:, :], preferred_element_type=F32)
    x_new = x_ref[0] + gate_ref[0] * y
    xo_ref[0] = x_new
    f = ((_rms_rows(x_new) * n2g_ref[...]) * (1.0 + scale_ref[0]) + shift_ref[0]).astype(BF16)
    f_ref[0] = f.astype(F32)
    lg_ref[0] = jnp.dot(f, wr_ref[...], preferred_element_type=F32)


def _mix_out(y_s5, y_mla, y_hy, x, mix_g, w_out, gate, norm2_g, shift, scale, w_route):
    bsz, n, d = x.shape
    tm = min(MIX_ROWS, n)

    def rows(width):
        return pl.BlockSpec((1, tm, width), lambda b, i: (b, i, 0))

    per_batch = pl.BlockSpec((1, 1, d), lambda b, i: (b, 0, 0))
    return pl.pallas_call(
        _mix_out_kernel,
        grid=(bsz, n // tm),
        in_specs=[pl.BlockSpec((tm, S5_W), lambda b, i: (i, b)), rows(MLA_W), rows(HY_W), rows(d),
                  _const_spec((1, d)), _const_spec((d, d)),
                  per_batch, _const_spec((1, d)), per_batch, per_batch, _const_spec((d, LANES))],
        out_specs=[rows(d), rows(d), rows(LANES)],
        out_shape=[jax.ShapeDtypeStruct((bsz, n, d), F32), jax.ShapeDtypeStruct((bsz, n, d), F32),
                   jax.ShapeDtypeStruct((bsz, n, LANES), F32)],
        compiler_params=pltpu.CompilerParams(dimension_semantics=("arbitrary", "arbitrary"),
                                             vmem_limit_bytes=VMEM_LIMIT),
        name="mix_out",
    )(y_s5, y_mla, y_hy, x, mix_g.reshape(1, d), w_out.astype(BF16), gate, norm2_g.reshape(1, d), shift, scale,
      w_route.astype(BF16))


def _moe_kernel(be_ref, used_ref, x_ref, w1_ref, w3_ref, w2_ref, rw_ref, o_ref, w13_s, w2_s):
    i = pl.program_id(0)
    hid = w2_s.shape[0]

    @pl.when(jnp.logical_or(i == 0, be_ref[i] != be_ref[jnp.maximum(i - 1, 0)]))
    def _():
        w13_s[:, :hid] = w1_ref[0, 0].astype(BF16)
        w13_s[:, hid:] = w3_ref[0, 0].astype(BF16)
        w2_s[...] = w2_ref[0, 0].astype(BF16)

    @pl.when(i < used_ref[0])
    def _():
        ab = jnp.dot(x_ref[...].astype(BF16), w13_s[...], preferred_element_type=F32)
        a, b = ab[:, :hid], ab[:, hid:]
        h = (a * jax.nn.sigmoid(a)) * b
        y = jnp.dot(h.astype(BF16), w2_s[...], preferred_element_type=F32)
        o_ref[...] = y * rw_ref[...]

    @pl.when(i >= used_ref[0])
    def _():
        o_ref[...] = jnp.zeros_like(o_ref)


def _moe_experts(xg, block_e, n_used, w1, w3, w2, layer, row_w):
    n_rows, d = xg.shape
    hid = w1.shape[3]
    n_blocks = n_rows // MOE_BLOCK
    return pl.pallas_call(
        _moe_kernel,
        grid_spec=pltpu.PrefetchScalarGridSpec(
            num_scalar_prefetch=2,
            grid=(n_blocks,),
            in_specs=[
                pl.BlockSpec((MOE_BLOCK, d), lambda i, be, nu: (i, 0)),
                pl.BlockSpec((1, 1, d, hid), lambda i, be, nu: (layer, be[i], 0, 0)),
                pl.BlockSpec((1, 1, d, hid), lambda i, be, nu: (layer, be[i], 0, 0)),
                pl.BlockSpec((1, 1, hid, d), lambda i, be, nu: (layer, be[i], 0, 0)),
                pl.BlockSpec((MOE_BLOCK, 1), lambda i, be, nu: (i, 0)),
            ],
            out_specs=pl.BlockSpec((MOE_BLOCK, d), lambda i, be, nu: (i, 0)),
            scratch_shapes=[pltpu.VMEM((d, 2 * hid), BF16), pltpu.VMEM((hid, d), BF16)],
        ),
        out_shape=jax.ShapeDtypeStruct((n_rows, d), F32),
        compiler_params=pltpu.CompilerParams(dimension_semantics=("arbitrary",), vmem_limit_bytes=VMEM_LIMIT),
        name="moe_experts",
    )(block_e, n_used, xg, w1, w3, w2, row_w)


def _hier_moe(h, logits, w1, w3, w2, layer):
    t, d = h.shape
    g_prob = jax.nn.softmax(logits[:, :MOE_GROUPS], axis=-1)
    g_idx = jnp.argmax(g_prob, axis=-1).astype(jnp.int32)
    g_w = jnp.max(g_prob, axis=-1, keepdims=True)
    e_logits = logits[:, MOE_GROUPS:MOE_GROUPS + MOE_EXPERTS].reshape(t, MOE_GROUPS, MOE_PER_GROUP)
    in_group = jnp.take_along_axis(e_logits, g_idx[:, None, None], axis=1)[:, 0]
    i1 = jnp.argmax(in_group, axis=-1).astype(jnp.int32)
    rest = jnp.where(jnp.arange(MOE_PER_GROUP, dtype=jnp.int32)[None, :] == i1[:, None], -jnp.inf, in_group)
    i2 = jnp.argmax(rest, axis=-1).astype(jnp.int32)
    top_v = jnp.stack([jnp.max(in_group, axis=-1), jnp.max(rest, axis=-1)], axis=-1)
    top_i = jnp.stack([i1, i2], axis=-1)
    gate = jax.nn.softmax(top_v, axis=-1) * g_w
    eid = g_idx[:, None] * MOE_PER_GROUP + top_i
    n_assign = t * MOE_TOP_K
    flat_e = eid.reshape(n_assign)
    flat_w = gate.reshape(n_assign)
    se, order = lax.sort((flat_e, jnp.arange(n_assign, dtype=jnp.int32)), num_keys=1)
    counts = jnp.sum(flat_e[None, :] == jnp.arange(MOE_EXPERTS, dtype=jnp.int32)[:, None], axis=1, dtype=jnp.int32)
    padded = (counts + MOE_BLOCK - 1) // MOE_BLOCK * MOE_BLOCK
    start = jnp.cumsum(counts) - counts
    pend = jnp.cumsum(padded)
    pstart = pend - padded
    experts = jnp.arange(MOE_EXPERTS, dtype=jnp.int32)[:, None]
    shift = jnp.sum(jnp.where(se[None, :] == experts, (pstart - start)[:, None], 0), axis=0)
    dest = (jnp.arange(n_assign, dtype=jnp.int32) + shift).astype(jnp.int32)
    n_blocks = -(-n_assign // MOE_BLOCK) + MOE_EXPERTS
    n_rows = n_blocks * MOE_BLOCK
    block_start = jnp.arange(n_blocks, dtype=jnp.int32) * MOE_BLOCK
    block_e = jnp.minimum(jnp.sum(pend[None, :] <= block_start[:, None], axis=1), MOE_EXPERTS - 1).astype(jnp.int32)
    off = (block_start - pstart[block_e])[:, None] + jnp.arange(MOE_BLOCK, dtype=jnp.int32)[None, :]
    valid = (off < counts[block_e][:, None]).reshape(n_rows)
    row_asg = order[jnp.where(valid, (start[block_e][:, None] + off).reshape(n_rows), 0)]
    row_tok = jnp.where(valid, row_asg // MOE_TOP_K, 0)
    row_w = jnp.where(valid, flat_w[row_asg], 0.0)
    xg = h[row_tok]
    n_used = (pend[-1:] // MOE_BLOCK).astype(jnp.int32)
    ys = _moe_experts(xg, block_e, n_used, w1, w3, w2, layer, row_w[:, None])
    _, slot = lax.sort((order, dest), num_keys=1)
    slot = slot.reshape(t, MOE_TOP_K)
    return ys[slot[:, 0]], ys[slot[:, 1]]


def _final_norm_kernel(x_ref, y0_ref, y1_ref, gate_ref, g_ref, o_ref):
    o_ref[0] = _rms_rows(_apply_pending(x_ref, (y0_ref, y1_ref, gate_ref))) * g_ref[...]


def _final_norm(x, pending, g):
    bsz, n, d = x.shape
    tm = min(MIX_ROWS, n)
    rows = pl.BlockSpec((1, tm, d), lambda b, i: (b, i, 0))
    pend_specs, pend_args = _pending_specs(pending, tm, n // tm, d)
    return pl.pallas_call(
        _final_norm_kernel,
        grid=(bsz, n // tm),
        in_specs=[rows, *pend_specs, _const_spec((1, d))],
        out_specs=rows,
        out_shape=jax.ShapeDtypeStruct((bsz, n, d), F32),
        compiler_params=pltpu.CompilerParams(dimension_semantics=("arbitrary", "arbitrary")),
        name="final_norm",
    )(x, *pend_args, g.reshape(1, d))


def kernel(x, c, ctx, c_ctx, ada_w, ada_b, norm1_g, norm2_g, w_in,
           s5_lambda_re, s5_lambda_im, s5_log_dt, s5_b_re, s5_b_im, s5_c_re, s5_c_im, s5_d, s5_glu_w,
           mla_q_norm_g, mla_kv_norm_g, mla_w_uq, mla_w_ukv,
           hy_short_w, hy_short_b, hy_f_w1, hy_f_b1, hy_f_w2, hy_f_b2, hy_f_w3, hy_f_b3, hy_f_freq, hy_bias,
           mix_norm_g, w_out, moe_w_group, moe_w_expert, moe_w1, moe_w3, moe_w2, final_g):
    bsz, n, d = x.shape
    n_ctx = ctx.shape[1]
    xl, xc = x, ctx
    act_l = jax.nn.silu(c)
    act_c = jax.nn.silu(c_ctx)
    rope_cos, rope_sin = _rope_slot_tables(n_ctx, n)
    pend_l = pend_c = None
    for i in range(DEPTH):
        ctx_out = i < DEPTH - 1
        mod_l = jnp.split((act_l @ ada_w[i] + ada_b[i])[:, None, :], 6, axis=-1)
        mod_c = jnp.split((act_c @ ada_w[i] + ada_b[i])[None, None, :], 6, axis=-1)
        mod_c = [jnp.broadcast_to(m, (bsz, 1, d)) for m in mod_c]
        w_ext, w_q = _proj_in_weights(w_in[i], mla_w_uq[i])
        proj = functools.partial(_proj_in, w_ext=w_ext, q_norm_g=mla_q_norm_g[i], w_q=w_q,
                                 kv_norm_g=mla_kv_norm_g[i], w_ukv=mla_w_ukv[i])
        u_l, q_l, kv_l, kr_l, hz_l, xl = proj(xl, pend_l, norm1_g[i], mod_l[0], mod_l[1],
                                              cos=rope_cos[n_ctx:], sin=rope_sin[n_ctx:])
        u_c, q_c, kv_c, kr_c, hz_c, xc = proj(xc, pend_c, norm1_g[i], mod_c[0], mod_c[1],
                                              cos=rope_cos[:n_ctx], sin=rope_sin[:n_ctx])
        s5_all = _s5_mixer(jnp.concatenate([u_c, u_l], axis=0), n_ctx, s5_lambda_re[i], s5_lambda_im[i],
                           s5_log_dt[i], s5_b_re[i], s5_b_im[i], s5_c_re[i], s5_c_im[i], s5_d[i], s5_glu_w[i])
        kv_all = jnp.concatenate([kv_c, kv_l], axis=1)
        kr_all = jnp.concatenate([kr_c, kr_l], axis=1)
        mla_l = _attention(q_l, kv_all, kr_all)
        filt = (hy_f_w1[i], hy_f_b1[i], hy_f_w2[i], hy_f_b2[i], hy_f_w3[i], hy_f_b3[i], hy_f_freq[i])
        hy_l = _hyena_mixer(hz_l, hy_short_w[i], hy_short_b[i], filt, hy_bias[i])
        w_route = jnp.pad(jnp.concatenate([moe_w_group[i], moe_w_expert[i]], axis=1),
                          ((0, 0), (0, LANES - MOE_GROUPS - MOE_EXPERTS)))
        xl, fl, lg_l = _mix_out(s5_all[n_ctx:], mla_l, hy_l, xl, mix_norm_g[i], w_out[i], mod_l[2], norm2_g[i],
                                mod_l[3], mod_l[4], w_route)
        fl, lg_l = fl.reshape(bsz * n, d), lg_l.reshape(bsz * n, LANES)
        if ctx_out:
            mla_c = _attention(q_c, kv_c, kr_c)
            hy_c = _hyena_mixer(hz_c, hy_short_w[i], hy_short_b[i], filt, hy_bias[i])
            xc, fc, lg_c = _mix_out(s5_all[:n_ctx], mla_c, hy_c, xc, mix_norm_g[i], w_out[i], mod_c[2], norm2_g[i],
                                    mod_c[3], mod_c[4], w_route)
            y0, y1 = _hier_moe(jnp.concatenate([fl, fc.reshape(bsz * n_ctx, d)], axis=0),
                               jnp.concatenate([lg_l, lg_c.reshape(bsz * n_ctx, LANES)], axis=0),
                               moe_w1, moe_w3, moe_w2, i)
            pend_c = (y0, y1, mod_c[5], bsz * n)
        else:
            y0, y1 = _hier_moe(fl, lg_l, moe_w1, moe_w3, moe_w2, i)
        pend_l = (y0, y1, mod_l[5], 0)
    return _final_norm(xl, pend_l, final_g)
```

```python
import functools
import math

import jax
import jax.numpy as jnp
from jax import lax
from jax.experimental import pallas as pl
from jax.experimental.pallas import tpu as pltpu

D_MODEL = 1024
DEPTH = 4
GRID_W = 64
EPS = 1e-6

MIX_W = D_MODEL
S5_W = D_MODEL // 4
S5_GROUP = 16
S5_GROUPS = S5_W // S5_GROUP
S5_STATE = 64
S5_N = S5_GROUPS * S5_STATE
MLA_V = 64
MLA_W = D_MODEL // 2
MLA_HEADS = MLA_W // MLA_V
MLA_NOPE = 64
MLA_ROPE = 32
MLA_QK = MLA_NOPE + MLA_ROPE
MLA_Q_RANK = 384
MLA_KV_RANK = 256
MLA_SCALE = 1.0 / math.sqrt(MLA_NOPE + MLA_ROPE)
ROPE_BASE = 10000.0
HY_W = D_MODEL // 4
HY_ORDER = 2
HY_POS_EMB = 33
HY_FILTER_W = 64
HY_MIN_DECAY = math.log(1e-2) / 1.5
HY_MAX_DECAY = math.log(1e-2) / 0.3
MOE_GROUPS = 4
MOE_PER_GROUP = 8
MOE_EXPERTS = MOE_GROUPS * MOE_PER_GROUP
MOE_TOP_K = 2
MOE_HIDDEN = 512
MOE_BLOCK = 256

SUBLANES = 8
S5_CHUNK = 128
MM_ROWS = 256
ATTN_Q_BLOCK = 256
MIX_ROWS = 512
LANES = 128
ATTN_HEADS_PER_STEP = 4
VMEM_LIMIT = 48 * 1024 * 1024

F32 = jnp.float32
BF16 = jnp.bfloat16


def _rms_norm(x, g):
    xf = x.astype(F32)
    y = xf * lax.rsqrt(jnp.mean(xf * xf, axis=-1, keepdims=True) + EPS)
    return (y * g.astype(F32)).astype(x.dtype)


def _modulate(x, g, shift, scale):
    return _rms_norm(x, g) * (1.0 + scale) + shift


def _split_projection(p):
    o1 = S5_W
    o2 = o1 + MLA_Q_RANK
    o3 = o2 + MLA_KV_RANK
    o4 = o3 + MLA_ROPE
    return p[..., :o1], p[..., o1:o2], p[..., o2:o3], p[..., o3:o4], p[..., o4:]


def _s5_matrices(lam_re, lam_im, log_dt, b_re, b_im, c_re, c_im):
    g, p, h = S5_GROUPS, S5_STATE, S5_GROUP
    dt = jnp.exp(log_dt)[..., None]
    mag = jnp.exp(lam_re * dt)
    ar, ai = mag * jnp.cos(lam_im * dt), mag * jnp.sin(lam_im * dt)
    den = lam_re * lam_re + lam_im * lam_im
    fr = ((ar - 1.0) * lam_re + ai * lam_im) / den
    fi = (ai * lam_re - (ar - 1.0) * lam_im) / den
    bbr = fr[..., None] * b_re - fi[..., None] * b_im
    bbi = fr[..., None] * b_im + fi[..., None] * b_re
    eye = jnp.eye(g, dtype=F32)

    def block_in(m):
        return jnp.einsum('kgph,gj->kghjp', m, eye).reshape(2, g * h, g * p)

    def block_out(m):
        return jnp.einsum('kghp,gj->kgpjh', m, eye).reshape(2, g * p, g * h)

    bmat = jnp.concatenate([block_in(bbr), block_in(bbi)], axis=-1)
    cmat = jnp.concatenate([block_out(c_re), -block_out(c_im)], axis=1)
    lam = jnp.stack([ar[0], ai[0], ar[1], ai[1]]).reshape(4, 1, g * p)
    lam = jnp.broadcast_to(lam, (4, SUBLANES, g * p)).reshape(4 * SUBLANES, g * p)
    return bmat.astype(BF16), cmat.astype(BF16), lam


def _s5_kernel(uf_ref, ub_ref, bmat_ref, cmat_ref, lam_ref, yf_ref, yb_ref, buf_f, buf_b, state):
    rows = uf_ref.shape[0]
    steps = rows // SUBLANES
    n = S5_N
    s = SUBLANES

    @pl.when(pl.program_id(0) == 0)
    def _():
        state[...] = jnp.zeros_like(state)

    def drive(r, carry):
        r0 = pl.multiple_of(r * MM_ROWS, MM_ROWS)
        buf_f[pl.ds(r0, MM_ROWS), :] = jnp.dot(uf_ref[pl.ds(r0, MM_ROWS), :].astype(BF16), bmat_ref[0],
                                               preferred_element_type=F32)
        buf_b[pl.ds(r0, MM_ROWS), :] = jnp.dot(ub_ref[pl.ds(r0, MM_ROWS), :].astype(BF16), bmat_ref[1],
                                               preferred_element_type=F32)
        return carry

    lax.fori_loop(0, rows // MM_ROWS, drive, 0)

    def step(j, carry):
        fr, fi, br, bi = carry
        rf = pl.multiple_of(j * s, s)
        rb = pl.multiple_of((steps - 1 - j) * s, s)
        lfr, lfi = lam_ref[0:s, :], lam_ref[s:2 * s, :]
        lbr, lbi = lam_ref[2 * s:3 * s, :], lam_ref[3 * s:4 * s, :]
        nfr = lfr * fr - lfi * fi + buf_f[pl.ds(rf, s), 0:n]
        nfi = lfr * fi + lfi * fr + buf_f[pl.ds(rf, s), n:2 * n]
        nbr = lbr * br - lbi * bi + buf_b[pl.ds(rb, s), 0:n]
        nbi = lbr * bi + lbi * br + buf_b[pl.ds(rb, s), n:2 * n]
        buf_f[pl.ds(rf, s), 0:n] = nfr
        buf_f[pl.ds(rf, s), n:2 * n] = nfi
        buf_b[pl.ds(rb, s), 0:n] = nbr
        buf_b[pl.ds(rb, s), n:2 * n] = nbi
        return nfr, nfi, nbr, nbi

    init = (state[0:s, :], state[s:2 * s, :], state[2 * s:3 * s, :], state[3 * s:4 * s, :])
    fr, fi, br, bi = lax.fori_loop(0, steps, step, init, unroll=2)
    state[0:s, :] = fr
    state[s:2 * s, :] = fi
    state[2 * s:3 * s, :] = br
    state[3 * s:4 * s, :] = bi

    def readout(r, carry):
        r0 = pl.multiple_of(r * MM_ROWS, MM_ROWS)
        yf_ref[pl.ds(r0, MM_ROWS), :] = jnp.dot(buf_f[pl.ds(r0, MM_ROWS), :].astype(BF16), cmat_ref[0],
                                                preferred_element_type=F32)
        yb_ref[pl.ds(r0, MM_ROWS), :] = jnp.dot(buf_b[pl.ds(r0, MM_ROWS), :].astype(BF16), cmat_ref[1],
                                                preferred_element_type=F32)
        return carry

    lax.fori_loop(0, rows // MM_ROWS, readout, 0)


def _s5_scan(u_tm, bmat, cmat, lam, n_ctx_chunks, chunk):
    rows_all, w = u_tm.shape
    rows = chunk * SUBLANES
    n_chunks = rows_all // rows
    n_lat_chunks = n_chunks - n_ctx_chunks
    n2 = 2 * S5_N

    def fwd_chunk(i):
        return (i + n_lat_chunks) % n_chunks

    def bwd_chunk(i):
        return n_chunks - 1 - i

    return pl.pallas_call(
        _s5_kernel,
        grid=(n_chunks,),
        in_specs=[
            pl.BlockSpec((rows, w), lambda i: (fwd_chunk(i), 0)),
            pl.BlockSpec((rows, w), lambda i: (bwd_chunk(i), 0)),
            pl.BlockSpec((2, w, n2), lambda i: (0, 0, 0)),
            pl.BlockSpec((2, n2, w), lambda i: (0, 0, 0)),
            pl.BlockSpec((4 * SUBLANES, S5_N), lambda i: (0, 0)),
        ],
        out_specs=[
            pl.BlockSpec((rows, w), lambda i: (fwd_chunk(i), 0)),
            pl.BlockSpec((rows, w), lambda i: (bwd_chunk(i), 0)),
        ],
        out_shape=[jax.ShapeDtypeStruct((rows_all, w), F32)] * 2,
        scratch_shapes=[
            pltpu.VMEM((rows, n2), F32),
            pltpu.VMEM((rows, n2), F32),
            pltpu.VMEM((4 * SUBLANES, S5_N), F32),
        ],
        compiler_params=pltpu.CompilerParams(dimension_semantics=("arbitrary",), vmem_limit_bytes=VMEM_LIMIT),
        name="s5_scan",
    )(u_tm, u_tm, bmat, cmat, lam)


def _s5_mixer(u_tm, n_ctx, lam_re, lam_im, log_dt, b_re, b_im, c_re, c_im, d, glu_w):
    n_all, bw = u_tm.shape
    bsz, w = bw // S5_W, S5_W
    assert bsz == SUBLANES and n_all % S5_CHUNK == 0 and n_ctx % S5_CHUNK == 0
    bmat, cmat, lam = _s5_matrices(lam_re, lam_im, log_dt, b_re, b_im, c_re, c_im)
    u_rows = u_tm.reshape(n_all * bsz, w)
    y_f, y_b = _s5_scan(u_rows, bmat, cmat, lam, n_ctx // S5_CHUNK, S5_CHUNK)
    y = y_f + y_b + d * u_rows
    y = jax.nn.gelu(y)
    y = y * jax.nn.sigmoid(y @ glu_w)
    return y.reshape(n_all, bw)


def _axial_rope_tables(n_tokens):
    rows = n_tokens // GRID_W
    row = jnp.repeat(jnp.arange(rows), GRID_W).astype(F32)
    col = jnp.tile(jnp.arange(GRID_W), rows).astype(F32)
    half = MLA_ROPE // 2
    inv = ROPE_BASE ** (-jnp.arange(0, half, 2, dtype=F32) / half)
    ang_r = row[:, None] * inv
    ang_c = col[:, None] * inv
    cos = jnp.concatenate([jnp.cos(ang_r)] * 2 + [jnp.cos(ang_c)] * 2, axis=-1)
    sin = jnp.concatenate([jnp.sin(ang_r)] * 2 + [jnp.sin(ang_c)] * 2, axis=-1)
    return cos, sin


def _rope_partner(w):
    q = MLA_ROPE // 4
    parts = []
    for g in range(2):
        x1, x2 = w[..., 2 * g * q:(2 * g + 1) * q], w[..., (2 * g + 1) * q:(2 * g + 2) * q]
        parts += [-x2, x1]
    return jnp.concatenate(parts, axis=-1)


def _attn_kernel(q_ref, kv_ref, kr_ref, o_ref, kcat, vaug):
    hs, nk = kcat.shape[0], kcat.shape[1]
    dn = MLA_NOPE

    @pl.when(pl.program_id(2) == 0)
    def _():
        lane = lax.broadcasted_iota(jnp.int32, (nk, LANES), 1)
        ones_col = jnp.where(lane == 0, 1.0, 0.0).astype(BF16)
        for h in range(hs):
            kvh = kv_ref[0, :, h * LANES:(h + 1) * LANES]
            kcat[h] = jnp.where(lane < dn, kvh, kr_ref[0])
            vaug[h] = jnp.where(lane >= dn, kvh, ones_col)

    outs = []
    for h in range(hs):
        s = lax.dot_general(q_ref[0, :, h * LANES:(h + 1) * LANES], kcat[h], (((1,), (1,)), ((), ())),
                            preferred_element_type=F32)
        p = jnp.exp(s - jnp.max(s, axis=-1, keepdims=True))
        o = jnp.dot(p.astype(BF16), vaug[h], preferred_element_type=F32)
        outs.append(o[:, dn:] / o[:, 0:1])
    o_ref[0] = jnp.concatenate(outs, axis=-1)


def _attention(q, kv, kr, key_row0, nk):
    bsz, nq, _ = q.shape
    heads, dv = MLA_HEADS, MLA_V
    tq = min(ATTN_Q_BLOCK, nq)
    hs = ATTN_HEADS_PER_STEP
    kb = key_row0 // nk
    return pl.pallas_call(
        _attn_kernel,
        grid=(bsz, heads // hs, nq // tq),
        in_specs=[
            pl.BlockSpec((1, tq, hs * LANES), lambda b, h, i: (b, i, h)),
            pl.BlockSpec((1, nk, hs * LANES), lambda b, h, i: (b, kb, h), pipeline_mode=pl.Buffered(1)),
            pl.BlockSpec((1, nk, LANES), lambda b, h, i: (b, kb, 0), pipeline_mode=pl.Buffered(1)),
        ],
        out_specs=pl.BlockSpec((1, tq, hs * dv), lambda b, h, i: (b, i, h)),
        out_shape=jax.ShapeDtypeStruct((bsz, nq, heads * dv), F32),
        scratch_shapes=[pltpu.VMEM((hs, nk, LANES), BF16), pltpu.VMEM((hs, nk, LANES), BF16)],
        compiler_params=pltpu.CompilerParams(
            dimension_semantics=("arbitrary", "arbitrary", "arbitrary"), vmem_limit_bytes=VMEM_LIMIT),
        name="mla_attention",
    )(q, kv, kr)


PIN_U = 0
PIN_CQ = PIN_U + S5_W
PIN_CKV = PIN_CQ + MLA_Q_RANK
PIN_HZ = PIN_CKV + MLA_KV_RANK
PIN_KR = PIN_HZ + 3 * HY_W
PIN_KRP = PIN_KR + LANES
PIN_COLS = PIN_KRP + LANES


def _proj_in_weights(w_in, w_uq):
    o1 = S5_W
    o2 = o1 + MLA_Q_RANK
    o3 = o2 + MLA_KV_RANK
    o4 = o3 + MLA_ROPE
    pad = LANES - MLA_QK
    w_kr = w_in[:, o3:o4]
    in_slot = lambda w: jnp.pad(w, ((0, 0), (MLA_NOPE, pad)))
    w_ext = jnp.concatenate([w_in[:, :o3], w_in[:, o4:], in_slot(w_kr), in_slot(_rope_partner(w_kr))], axis=1)
    w_h = w_uq.reshape(MLA_Q_RANK, MLA_HEADS, MLA_QK)
    w_rot = jnp.concatenate([jnp.zeros_like(w_h[..., :MLA_NOPE]), _rope_partner(w_h[..., MLA_NOPE:])], axis=-1)
    w_q = jnp.concatenate([jnp.pad(w, ((0, 0), (0, 0), (0, pad))).reshape(MLA_Q_RANK, MLA_HEADS * LANES)
                           for w in (w_h, w_rot)], axis=-1)
    return w_ext.astype(BF16), w_q.astype(BF16)


def _rope_slot_tables(n, n_ctx):
    cos, sin = _axial_rope_tables(n)
    pad = LANES - MLA_QK
    cos = jnp.concatenate([jnp.ones((n, MLA_NOPE), F32), cos, jnp.ones((n, pad), F32)], axis=-1)
    sin = jnp.concatenate([jnp.zeros((n, MLA_NOPE), F32), sin, jnp.zeros((n, pad), F32)], axis=-1)
    return (jnp.concatenate([cos, jnp.ones((n_ctx, LANES), F32)], axis=0),
            jnp.concatenate([sin, jnp.zeros((n_ctx, LANES), F32)], axis=0))


def _apply_pending(x_ref, pending_refs):
    y0_ref, y1_ref, gate_ref = pending_refs
    return x_ref[0] + gate_ref[0] * (y0_ref[...] + y1_ref[...])


def _pending_specs(pending, tm, blocks_per_batch, d):
    if pending is None:
        return [], []
    y0, y1, gate, row0 = pending
    blk0 = row0 // tm
    rows = pl.BlockSpec((tm, d), lambda b, i: (blk0 + b * blocks_per_batch + i, 0))
    return [rows, rows, pl.BlockSpec((1, 1, d), lambda b, i: (b, 0, 0))], [y0, y1, gate]


def _proj_in_kernel(*refs, has_pending, n_shared):
    if has_pending:
        x_ref, *pending_refs = refs[:4]
        refs = refs[4:]
    else:
        x_ref, refs = refs[0], refs[1:]
    g_ref, shift_ref, scale_ref, w_ref, qg_ref, wq_ref, kvg_ref, wkv_ref, cos_ref, sin_ref = refs[:10]
    u_ref, q_ref, kv_ref, kr_ref, hz_ref, *xo_ref = refs[10 + n_shared:]
    if has_pending:
        x = _apply_pending(x_ref, pending_refs)
        xo_ref[0][0] = x
    else:
        x = x_ref[0]
    h = ((_rms_rows(x) * g_ref[...]) * (1.0 + scale_ref[0]) + shift_ref[0]).astype(BF16)
    p = jnp.dot(h, w_ref[...], preferred_element_type=F32)
    u_ref[...] = p[:, PIN_U:PIN_CQ]
    hz_ref[0] = p[:, PIN_HZ:PIN_KR]
    cos, sin = cos_ref[...], sin_ref[...]
    kr_ref[0] = (p[:, PIN_KR:PIN_KRP] * cos + p[:, PIN_KRP:PIN_COLS] * sin).astype(BF16)
    cq = (_rms_rows(p[:, PIN_CQ:PIN_CKV]) * qg_ref[...]).astype(BF16)
    q2 = jnp.dot(cq, wq_ref[...], preferred_element_type=F32)
    hw = MLA_HEADS * LANES
    for hd in range(MLA_HEADS):
        a, b = q2[:, hd * LANES:(hd + 1) * LANES], q2[:, hw + hd * LANES:hw + (hd + 1) * LANES]
        q_ref[0, :, hd * LANES:(hd + 1) * LANES] = ((a * cos + b * sin) * MLA_SCALE).astype(BF16)
    ckv = (_rms_rows(p[:, PIN_CKV:PIN_HZ]) * kvg_ref[...]).astype(BF16)
    kv_ref[0] = jnp.dot(ckv, wkv_ref[...], preferred_element_type=F32).astype(BF16)


def _proj_in(x, pending, norm_g, shift, scale, w_ext, q_norm_g, w_q, kv_norm_g, w_ukv, cos, sin, row0, shared):
    bsz, n, d = x.shape
    n_all = cos.shape[0]
    tm = min(MIX_ROWS, n)
    blk0 = row0 // tm
    hw = MLA_HEADS * LANES

    def rows(width, first=0):
        return pl.BlockSpec((1, tm, width), lambda b, i: (b, first + i, 0))

    per_batch = pl.BlockSpec((1, 1, d), lambda b, i: (b, 0, 0))
    table = pl.BlockSpec((tm, LANES), lambda b, i: (blk0 + i, 0))
    pend_specs, pend_args = _pending_specs(pending, tm, n // tm, d)
    has_pending = pending is not None
    out_specs = [pl.BlockSpec((tm, S5_W), lambda b, i: (blk0 + i, b)), rows(hw), rows(hw, blk0), rows(LANES, blk0),
                 rows(3 * HY_W)]
    out_shape = [jax.ShapeDtypeStruct((n_all, bsz * S5_W), F32), jax.ShapeDtypeStruct((bsz, n, hw), BF16),
                 jax.ShapeDtypeStruct((bsz, n_all, hw), BF16), jax.ShapeDtypeStruct((bsz, n_all, LANES), BF16),
                 jax.ShapeDtypeStruct((bsz, n, 3 * HY_W), F32)]
    if has_pending:
        out_specs.append(rows(d))
        out_shape.append(jax.ShapeDtypeStruct((bsz, n, d), F32))
    n_in = 1 + len(pend_args) + 10
    shared = list(shared or ())
    aliases = {n_in + k: out for k, out in enumerate((0, 2, 3)[:len(shared)])}
    outs = pl.pallas_call(
        functools.partial(_proj_in_kernel, has_pending=has_pending, n_shared=len(shared)),
        grid=(bsz, n // tm),
        in_specs=[rows(d), *pend_specs, _const_spec((1, d)), per_batch, per_batch, _const_spec((d, PIN_COLS)),
                  _const_spec((1, MLA_Q_RANK)), _const_spec((MLA_Q_RANK, 2 * hw)),
                  _const_spec((1, MLA_KV_RANK)), _const_spec((MLA_KV_RANK, hw)), table, table,
                  *[pl.BlockSpec(memory_space=pl.ANY)] * len(shared)],
        out_specs=out_specs,
        out_shape=out_shape,
        input_output_aliases=aliases,
        compiler_params=pltpu.CompilerParams(dimension_semantics=("arbitrary", "arbitrary"),
                                             vmem_limit_bytes=VMEM_LIMIT),
        name="proj_in",
    )(x, *pend_args, norm_g.reshape(1, d), shift, scale, w_ext, q_norm_g.reshape(1, -1), w_q,
      kv_norm_g.reshape(1, -1), w_ukv.astype(BF16), cos, sin, *shared)
    return (outs[0], outs[2], outs[3]), outs[1], outs[4], (outs[5] if has_pending else x)


def _short_conv(z, w, b):
    n = z.shape[1]
    zp = jnp.pad(z, ((0, 0), (1, 1), (0, 0)))
    return zp[:, :n] * w[0] + zp[:, 1:n + 1] * w[1] + zp[:, 2:] * w[2] + b


def _hyena_filter_taps(n, w1, b1, w2, b2, w3, b3, freq):
    t = jnp.linspace(0.0, 1.0, n, dtype=F32)[:, None]
    bands = (HY_POS_EMB - 1) // 2
    f = jnp.linspace(1e-4, bands - 1, bands, dtype=F32)
    ang = (2.0 * math.pi * jnp.arange(n, dtype=F32) / n)[:, None] * f
    z = jnp.concatenate([t, jnp.cos(ang), -jnp.sin(ang)], axis=-1)
    h = jnp.sin(freq[0] * (z @ w1 + b1))
    h = jnp.sin(freq[1] * (h @ w2 + b2))
    h = (h @ w3 + b3).astype(F32).reshape(n, 2, HY_ORDER, HY_W)
    deltas = jnp.abs(jnp.linspace(HY_MIN_DECAY, HY_MAX_DECAY, HY_W, dtype=F32))
    h = h * jnp.exp(-t[:, :, None, None] * deltas)
    fwd, bwd = h[:, 0], h[:, 1]
    return jnp.concatenate([fwd, jnp.zeros((1, HY_ORDER, HY_W), F32), bwd[:0:-1]], axis=0)


def _fft_long_conv(y, k_f):
    n = y.shape[1]
    yf = jnp.fft.rfft(y, n=2 * n, axis=1)
    return jnp.fft.irfft(yf * k_f, n=2 * n, axis=1)[:, :n]


def _hyena_mixer(z, short_w, short_b, filt, bias):
    n = z.shape[1]
    z = _short_conv(z, short_w, short_b)
    v, x1, x2 = jnp.split(z, 3, axis=-1)
    y = v.astype(F32)
    if (2 * n) % HY_N2 == 0 and (2 * n) // HY_N2 >= 2 * SUBLANES:
        k_time = _hyena_filter_taps(n, *filt)
        tables = _dft_tables(2 * n)
        kr, ki = _hyena_spectrum(k_time.reshape(2 * n, HY_ORDER * HY_W), tables)
        for o, gate in enumerate((x1, x2)):
            cols = slice(o * HY_W, (o + 1) * HY_W)
            y = gate.astype(F32) * (_hyena_long_conv(y, kr[:, cols], ki[:, cols], tables) + y * bias[o].astype(F32))
    else:
        k_f = jnp.fft.rfft(_hyena_filter_taps(n, *filt), axis=0)
        for o, gate in enumerate((x1, x2)):
            y = gate.astype(F32) * (_fft_long_conv(y, k_f[:, o]) + y * bias[o].astype(F32))
    return y.astype(z.dtype)


HY_N2 = 128
HY_PITCH = 136
HY_UNROLL = 8


def _dft_tables(nfft):
    n1 = nfft // HY_N2
    half = n1 // 2
    k1 = jnp.arange(n1, dtype=jnp.int32)
    n2 = jnp.arange(HY_N2, dtype=jnp.int32)
    t = HY_N2 * k1[None, None, :] + n2[:, None, None]
    ang = (2.0 * math.pi / nfft) * ((k1[None, :, None] * t) % nfft).astype(F32)
    gr, gi = jnp.cos(ang), -jnp.sin(ang)
    g_cplx = jnp.concatenate([jnp.concatenate([gr[..., :half], -gi[..., :half]], -1),
                              jnp.concatenate([gi[..., :half], gr[..., :half]], -1)], axis=1)
    g_real = jnp.concatenate([gr, gi], axis=1)
    hr = jnp.cos(ang).swapaxes(1, 2)[:, :half] / nfft
    hi = jnp.sin(ang).swapaxes(1, 2)[:, :half] / nfft
    g_inv = jnp.concatenate([jnp.concatenate([hr, -hi], -1), jnp.concatenate([hi, hr], -1)], axis=1)
    a2 = (2.0 * math.pi / HY_N2) * ((n2[:, None] * n2[None, :]) % HY_N2).astype(F32)
    fr, fi = jnp.cos(a2), -jnp.sin(a2)
    f2 = jnp.concatenate([jnp.concatenate([fr, -fi], -1), jnp.concatenate([fi, fr], -1)], axis=0)
    f2_inv = jnp.concatenate([jnp.concatenate([fr, fi], -1), jnp.concatenate([-fi, fr], -1)], axis=0)
    return dict(n1=n1, g_cplx=g_cplx.astype(BF16), g_real=g_real.astype(BF16), g_inv=g_inv.astype(BF16),
                f2=f2.astype(BF16), f2_inv=f2_inv.astype(BF16))


def _dft_stage1(load_rows, g_ref, s_re, s_im, n1):
    def body(n2, carry):
        a = jnp.dot(g_ref[n2], load_rows(n2).astype(BF16), preferred_element_type=F32)
        s_re[pl.ds(n2, n1, stride=HY_PITCH), :] = a[:n1]
        s_im[pl.ds(n2, n1, stride=HY_PITCH), :] = a[n1:]
        return carry

    lax.fori_loop(0, HY_N2, body, 0, unroll=HY_UNROLL)


def _dft_stage2(f2_ref, s_re, s_im, k1):
    r0 = pl.multiple_of(k1 * HY_PITCH, SUBLANES)
    sl = jnp.concatenate([s_re[pl.ds(r0, HY_N2), :], s_im[pl.ds(r0, HY_N2), :]], axis=0).astype(BF16)
    x = jnp.dot(f2_ref[...], sl, preferred_element_type=F32)
    return r0, x[:HY_N2], x[HY_N2:]


def _hyena_spectrum_kernel(x_ref, g_ref, f2_ref, kr_ref, ki_ref, s_re, s_im, *, n1):
    _dft_stage1(lambda n2: x_ref[pl.ds(n2, n1, stride=HY_PITCH), :], g_ref, s_re, s_im, n1)

    def body(k1, carry):
        _, xr, xi = _dft_stage2(f2_ref, s_re, s_im, k1)
        q0 = pl.multiple_of(k1 * HY_N2, HY_N2)
        kr_ref[pl.ds(q0, HY_N2), :] = xr
        ki_ref[pl.ds(q0, HY_N2), :] = xi
        return carry

    lax.fori_loop(0, n1, body, 0)


def _hyena_conv_kernel(xr_ref, xi_ref, g_ref, f2_ref, f2i_ref, gi_ref, kr_ref, ki_ref, yr_ref, yi_ref,
                       s_re, s_im, *, n1):
    half = n1 // 2

    def load_rows(n2):
        return jnp.concatenate([xr_ref[0, pl.ds(n2, half, stride=HY_PITCH), :],
                                xi_ref[0, pl.ds(n2, half, stride=HY_PITCH), :]], axis=0)

    _dft_stage1(load_rows, g_ref, s_re, s_im, n1)

    def spectrum_product(k1, carry):
        r0, xr, xi = _dft_stage2(f2_ref, s_re, s_im, k1)
        q0 = pl.multiple_of(k1 * HY_N2, HY_N2)
        kr, ki = kr_ref[pl.ds(q0, HY_N2), :], ki_ref[pl.ds(q0, HY_N2), :]
        y = jnp.concatenate([xr * kr - xi * ki, xr * ki + xi * kr], axis=0).astype(BF16)
        b = jnp.dot(f2i_ref[...], y, preferred_element_type=F32)
        s_re[pl.ds(r0, HY_N2), :] = b[:HY_N2]
        s_im[pl.ds(r0, HY_N2), :] = b[HY_N2:]
        return carry

    lax.fori_loop(0, n1, spectrum_product, 0, unroll=HY_UNROLL // 2)

    yr_ref[...] = jnp.zeros_like(yr_ref)
    yi_ref[...] = jnp.zeros_like(yi_ref)

    def inverse_stage1(n2, carry):
        bs = jnp.concatenate([s_re[pl.ds(n2, n1, stride=HY_PITCH), :],
                              s_im[pl.ds(n2, n1, stride=HY_PITCH), :]], axis=0).astype(BF16)
        y = jnp.dot(gi_ref[n2], bs, preferred_element_type=F32)
        yr_ref[0, pl.ds(n2, half, stride=HY_PITCH), :] = y[:half]
        yi_ref[0, pl.ds(n2, half, stride=HY_PITCH), :] = y[half:]
        return carry

    lax.fori_loop(0, HY_N2, inverse_stage1, 0, unroll=HY_UNROLL)


def _to_strided(a, rows):
    lead, c = a.shape[:-2], a.shape[-1]
    a = a.reshape(lead + (rows, HY_N2, c))
    a = jnp.pad(a, [(0, 0)] * len(lead) + [(0, 0), (0, HY_PITCH - HY_N2), (0, 0)])
    return a.reshape(lead + (rows * HY_PITCH, c))


def _from_strided(a, rows):
    lead, c = a.shape[:-2], a.shape[-1]
    return a.reshape(lead + (rows, HY_PITCH, c))[..., :HY_N2, :].reshape(lead + (rows * HY_N2, c))


def _const_spec(shape):
    return pl.BlockSpec(shape, lambda *_: (0,) * len(shape), pipeline_mode=pl.Buffered(1))


def _hyena_spectrum(k_time, tables):
    nfft, c = k_time.shape
    n1 = tables["n1"]
    spec_out = pl.BlockSpec((nfft, LANES), lambda j: (0, j))
    return pl.pallas_call(
        functools.partial(_hyena_spectrum_kernel, n1=n1),
        grid=(c // LANES,),
        in_specs=[pl.BlockSpec((n1 * HY_PITCH, LANES), lambda j: (0, j)),
                  _const_spec((HY_N2, 2 * n1, n1)), _const_spec((2 * HY_N2, 2 * HY_N2))],
        out_specs=[spec_out, spec_out],
        out_shape=[jax.ShapeDtypeStruct((nfft, c), F32)] * 2,
        scratch_shapes=[pltpu.VMEM((n1 * HY_PITCH, LANES), F32)] * 2,
        compiler_params=pltpu.CompilerParams(dimension_semantics=("arbitrary",), vmem_limit_bytes=VMEM_LIMIT),
        name="hyena_spectrum",
    )(_to_strided(k_time, n1), tables["g_real"], tables["f2"])


def _hyena_long_conv(y, kr, ki, tables):
    bsz, n, c = y.shape
    n1 = tables["n1"]
    half = n1 // 2
    pairs = bsz // 2
    xr, xi = _to_strided(y[0::2], half), _to_strided(y[1::2], half)
    spec_x = pl.BlockSpec((1, half * HY_PITCH, LANES), lambda j, p: (p, 0, j))
    spec_k = pl.BlockSpec((n1 * HY_N2, LANES), lambda j, p: (0, j), pipeline_mode=pl.Buffered(1))
    yr, yi = pl.pallas_call(
        functools.partial(_hyena_conv_kernel, n1=n1),
        grid=(c // LANES, pairs),
        in_specs=[spec_x, spec_x,
                  _const_spec((HY_N2, 2 * n1, n1)), _const_spec((2 * HY_N2, 2 * HY_N2)),
                  _const_spec((2 * HY_N2, 2 * HY_N2)), _const_spec((HY_N2, n1, 2 * n1)),
                  spec_k, spec_k],
        out_specs=[spec_x, spec_x],
        out_shape=[jax.ShapeDtypeStruct((pairs, half * HY_PITCH, c), F32)] * 2,
        scratch_shapes=[pltpu.VMEM((n1 * HY_PITCH, LANES), F32)] * 2,
        compiler_params=pltpu.CompilerParams(dimension_semantics=("arbitrary", "arbitrary"),
                                             vmem_limit_bytes=VMEM_LIMIT),
        name="hyena_conv",
    )(xr, xi, tables["g_cplx"], tables["f2"], tables["f2_inv"], tables["g_inv"], kr, ki)
    out = jnp.stack([_from_strided(yr, half), _from_strided(yi, half)], axis=1)
    return out.reshape(bsz, n, c)


def _rms_rows(v):
    return v * lax.rsqrt(jnp.mean(v * v, axis=-1, keepdims=True) + EPS)


def _mix_out_kernel(s5_ref, mla_ref, hy_ref, x_ref, g_ref, w_ref, gate_ref, n2g_ref, shift_ref, scale_ref, wr_ref,
                    *rest):
    xo_ref, f_ref, lg_ref = rest[-3:]
    c1, c2 = S5_W, S5_W + MLA_W
    y = jnp.dot((_rms_rows(s5_ref[...]) * g_ref[:, :c1]).astype(BF16), w_ref[:c1, :], preferred_element_type=F32)
    y += jnp.dot((_rms_rows(mla_ref[0]) * g_ref[:, c1:c2]).astype(BF16), w_ref[c1:c2, :], preferred_element_type=F32)
    y += jnp.dot((_rms_rows(hy_ref[0]) * g_ref[:, c2:]).astype(BF16), w_ref[c2:, :], preferred_element_type=F32)
    x_new = x_ref[0] + gate_ref[0] * y
    xo_ref[0] = x_new
    f = ((_rms_rows(x_new) * n2g_ref[...]) * (1.0 + scale_ref[0]) + shift_ref[0]).astype(BF16)
    f_ref[...] = f.astype(F32)
    lg_ref[...] = jnp.dot(f, wr_ref[...], preferred_element_type=F32)


def _mix_out(y_s5, y_mla, y_hy, x, mix_g, w_out, gate, norm2_g, shift, scale, w_route, row0, tok0, n_tok, shared):
    bsz, n, d = x.shape
    tm = min(MIX_ROWS, n)
    s5_blk0, tok_blk0, per_batch_blocks = row0 // tm, tok0 // tm, n // tm

    def rows(width):
        return pl.BlockSpec((1, tm, width), lambda b, i: (b, i, 0))

    def tokens(width):
        return pl.BlockSpec((tm, width), lambda b, i: (tok_blk0 + b * per_batch_blocks + i, 0))

    per_batch = pl.BlockSpec((1, 1, d), lambda b, i: (b, 0, 0))
    shared = list(shared or ())
    outs = pl.pallas_call(
        _mix_out_kernel,
        grid=(bsz, n // tm),
        in_specs=[pl.BlockSpec((tm, S5_W), lambda b, i: (s5_blk0 + i, b)), rows(MLA_W), rows(HY_W), rows(d),
                  _const_spec((1, d)), _const_spec((d, d)),
                  per_batch, _const_spec((1, d)), per_batch, per_batch, _const_spec((d, LANES)),
                  *[pl.BlockSpec(memory_space=pl.ANY)] * len(shared)],
        out_specs=[rows(d), tokens(d), tokens(LANES)],
        out_shape=[jax.ShapeDtypeStruct((bsz, n, d), F32), jax.ShapeDtypeStruct((n_tok, d), F32),
                   jax.ShapeDtypeStruct((n_tok, LANES), F32)],
        input_output_aliases={11 + k: 1 + k for k in range(len(shared))},
        compiler_params=pltpu.CompilerParams(dimension_semantics=("arbitrary", "arbitrary"),
                                             vmem_limit_bytes=VMEM_LIMIT),
        name="mix_out",
    )(y_s5, y_mla, y_hy, x, mix_g.reshape(1, d), w_out.astype(BF16), gate, norm2_g.reshape(1, d), shift, scale,
      w_route.astype(BF16), *shared)
    return outs[0], (outs[1], outs[2])


def _moe_kernel(be_ref, used_ref, x_ref, w1_ref, w3_ref, w2_ref, rw_ref, o_ref, w13_s, w2_s):
    i = pl.program_id(0)
    hid = w2_s.shape[0]

    @pl.when(jnp.logical_or(i == 0, be_ref[i] != be_ref[jnp.maximum(i - 1, 0)]))
    def _():
        w13_s[:, :hid] = w1_ref[0, 0].astype(BF16)
        w13_s[:, hid:] = w3_ref[0, 0].astype(BF16)
        w2_s[...] = w2_ref[0, 0].astype(BF16)

    @pl.when(i < used_ref[0])
    def _():
        ab = jnp.dot(x_ref[...].astype(BF16), w13_s[...], preferred_element_type=F32)
        a, b = ab[:, :hid], ab[:, hid:]
        h = (a * jax.nn.sigmoid(a)) * b
        y = jnp.dot(h.astype(BF16), w2_s[...], preferred_element_type=F32)
        rw = rw_ref[...]
        for c0 in range(0, y.shape[1], LANES):
            o_ref[:, c0:c0 + LANES] = y[:, c0:c0 + LANES] * rw

    @pl.when(i >= used_ref[0])
    def _():
        o_ref[...] = jnp.zeros_like(o_ref)


def _moe_experts(xg, block_e, n_used, w1, w3, w2, layer, row_w):
    n_rows, d = xg.shape
    hid = w1.shape[3]
    n_blocks = n_rows // MOE_BLOCK
    return pl.pallas_call(
        _moe_kernel,
        grid_spec=pltpu.PrefetchScalarGridSpec(
            num_scalar_prefetch=2,
            grid=(n_blocks,),
            in_specs=[
                pl.BlockSpec((MOE_BLOCK, d), lambda i, be, nu: (i, 0)),
                pl.BlockSpec((1, 1, d, hid), lambda i, be, nu: (layer, be[i], 0, 0)),
                pl.BlockSpec((1, 1, d, hid), lambda i, be, nu: (layer, be[i], 0, 0)),
                pl.BlockSpec((1, 1, hid, d), lambda i, be, nu: (layer, be[i], 0, 0)),
                pl.BlockSpec((MOE_BLOCK, LANES), lambda i, be, nu: (i, 0)),
            ],
            out_specs=pl.BlockSpec((MOE_BLOCK, d), lambda i, be, nu: (i, 0)),
            scratch_shapes=[pltpu.VMEM((d, 2 * hid), BF16), pltpu.VMEM((hid, d), BF16)],
        ),
        out_shape=jax.ShapeDtypeStruct((n_rows, d), F32),
        compiler_params=pltpu.CompilerParams(dimension_semantics=("arbitrary",), vmem_limit_bytes=VMEM_LIMIT),
        name="moe_experts",
    )(block_e, n_used, xg, w1, w3, w2, row_w)


def _hier_moe(h, logits, w1, w3, w2, layer):
    t, d = h.shape
    g_prob = jax.nn.softmax(logits[:, :MOE_GROUPS], axis=-1)
    g_idx = jnp.argmax(g_prob, axis=-1).astype(jnp.int32)
    g_w = jnp.max(g_prob, axis=-1, keepdims=True)
    e_logits = logits[:, MOE_GROUPS:MOE_GROUPS + MOE_EXPERTS].reshape(t, MOE_GROUPS, MOE_PER_GROUP)
    in_group = jnp.take_along_axis(e_logits, g_idx[:, None, None], axis=1)[:, 0]
    i1 = jnp.argmax(in_group, axis=-1).astype(jnp.int32)
    rest = jnp.where(jnp.arange(MOE_PER_GROUP, dtype=jnp.int32)[None, :] == i1[:, None], -jnp.inf, in_group)
    i2 = jnp.argmax(rest, axis=-1).astype(jnp.int32)
    top_v = jnp.stack([jnp.max(in_group, axis=-1), jnp.max(rest, axis=-1)], axis=-1)
    top_i = jnp.stack([i1, i2], axis=-1)
    gate = jax.nn.softmax(top_v, axis=-1) * g_w
    eid = g_idx[:, None] * MOE_PER_GROUP + top_i
    n_assign = t * MOE_TOP_K
    flat_e = eid.reshape(n_assign)
    flat_w = gate.reshape(n_assign)
    se, order = lax.sort((flat_e, jnp.arange(n_assign, dtype=jnp.int32)), num_keys=1)
    counts = jnp.sum(flat_e[None, :] == jnp.arange(MOE_EXPERTS, dtype=jnp.int32)[:, None], axis=1, dtype=jnp.int32)
    padded = (counts + MOE_BLOCK - 1) // MOE_BLOCK * MOE_BLOCK
    start = jnp.cumsum(counts) - counts
    pend = jnp.cumsum(padded)
    pstart = pend - padded
    experts = jnp.arange(MOE_EXPERTS, dtype=jnp.int32)[:, None]
    shift = jnp.sum(jnp.where(se[None, :] == experts, (pstart - start)[:, None], 0), axis=0)
    dest = (jnp.arange(n_assign, dtype=jnp.int32) + shift).astype(jnp.int32)
    n_blocks = -(-n_assign // MOE_BLOCK) + MOE_EXPERTS
    n_rows = n_blocks * MOE_BLOCK
    block_start = jnp.arange(n_blocks, dtype=jnp.int32) * MOE_BLOCK
    block_e = jnp.minimum(jnp.sum(pend[None, :] <= block_start[:, None], axis=1), MOE_EXPERTS - 1).astype(jnp.int32)
    off = (block_start - pstart[block_e])[:, None] + jnp.arange(MOE_BLOCK, dtype=jnp.int32)[None, :]
    valid = (off < counts[block_e][:, None]).reshape(n_rows)
    row_asg = order[jnp.where(valid, (start[block_e][:, None] + off).reshape(n_rows), 0)]
    row_tok = jnp.where(valid, row_asg // MOE_TOP_K, 0)
    row_w = jnp.where(valid, flat_w[row_asg], 0.0)
    xg = h[row_tok]
    n_used = (pend[-1:] // MOE_BLOCK).astype(jnp.int32)
    ys = _moe_experts(xg, block_e, n_used, w1, w3, w2, layer, jnp.broadcast_to(row_w[:, None], (n_rows, LANES)))
    _, slot = lax.sort((order, dest), num_keys=1)
    slot = slot.reshape(t, MOE_TOP_K)
    return ys[slot[:, 0]], ys[slot[:, 1]]


def _final_norm_kernel(x_ref, y0_ref, y1_ref, gate_ref, g_ref, o_ref):
    o_ref[0] = _rms_rows(_apply_pending(x_ref, (y0_ref, y1_ref, gate_ref))) * g_ref[...]


def _final_norm(x, pending, g):
    bsz, n, d = x.shape
    tm = min(MIX_ROWS, n)
    rows = pl.BlockSpec((1, tm, d), lambda b, i: (b, i, 0))
    pend_specs, pend_args = _pending_specs(pending, tm, n // tm, d)
    return pl.pallas_call(
        _final_norm_kernel,
        grid=(bsz, n // tm),
        in_specs=[rows, *pend_specs, _const_spec((1, d))],
        out_specs=rows,
        out_shape=jax.ShapeDtypeStruct((bsz, n, d), F32),
        compiler_params=pltpu.CompilerParams(dimension_semantics=("arbitrary", "arbitrary")),
        name="final_norm",
    )(x, *pend_args, g.reshape(1, d))


def kernel(x, c, ctx, c_ctx, ada_w, ada_b, norm1_g, norm2_g, w_in,
           s5_lambda_re, s5_lambda_im, s5_log_dt, s5_b_re, s5_b_im, s5_c_re, s5_c_im, s5_d, s5_glu_w,
           mla_q_norm_g, mla_kv_norm_g, mla_w_uq, mla_w_ukv,
           hy_short_w, hy_short_b, hy_f_w1, hy_f_b1, hy_f_w2, hy_f_b2, hy_f_w3, hy_f_b3, hy_f_freq, hy_bias,
           mix_norm_g, w_out, moe_w_group, moe_w_expert, moe_w1, moe_w3, moe_w2, final_g):
    bsz, n, d = x.shape
    n_ctx = ctx.shape[1]
    xl, xc = x, ctx
    act_l = jax.nn.silu(c)
    act_c = jax.nn.silu(c_ctx)
    rope_cos, rope_sin = _rope_slot_tables(n, n_ctx)
    pend_l = pend_c = None
    for i in range(DEPTH):
        ctx_out = i < DEPTH - 1
        mod_l = jnp.split((act_l @ ada_w[i] + ada_b[i])[:, None, :], 6, axis=-1)
        mod_c = jnp.split((act_c @ ada_w[i] + ada_b[i])[None, None, :], 6, axis=-1)
        mod_c = [jnp.broadcast_to(m, (bsz, 1, d)) for m in mod_c]
        w_ext, w_q = _proj_in_weights(w_in[i], mla_w_uq[i])
        proj = functools.partial(_proj_in, w_ext=w_ext, q_norm_g=mla_q_norm_g[i], w_q=w_q,
                                 kv_norm_g=mla_kv_norm_g[i], w_ukv=mla_w_ukv[i], cos=rope_cos, sin=rope_sin)
        seq, q_l, hz_l, xl = proj(xl, pend_l, norm1_g[i], mod_l[0], mod_l[1], row0=0, shared=None)
        (u_all, kv_all, kr_all), q_c, hz_c, xc = proj(xc, pend_c, norm1_g[i], mod_c[0], mod_c[1], row0=n, shared=seq)
        s5_all = _s5_mixer(u_all, n_ctx, s5_lambda_re[i], s5_lambda_im[i], s5_log_dt[i], s5_b_re[i], s5_b_im[i],
                           s5_c_re[i], s5_c_im[i], s5_d[i], s5_glu_w[i])
        mla_l = _attention(q_l, kv_all, kr_all, 0, n + n_ctx)
        filt = (hy_f_w1[i], hy_f_b1[i], hy_f_w2[i], hy_f_b2[i], hy_f_w3[i], hy_f_b3[i], hy_f_freq[i])
        hy_l = _hyena_mixer(hz_l, hy_short_w[i], hy_short_b[i], filt, hy_bias[i])
        w_route = jnp.pad(jnp.concatenate([moe_w_group[i], moe_w_expert[i]], axis=1),
                          ((0, 0), (0, LANES - MOE_GROUPS - MOE_EXPERTS)))
        n_tok = bsz * (n + n_ctx) if ctx_out else bsz * n
        xl, moe_in = _mix_out(s5_all, mla_l, hy_l, xl, mix_norm_g[i], w_out[i], mod_l[2], norm2_g[i],
                              mod_l[3], mod_l[4], w_route, row0=0, tok0=0, n_tok=n_tok, shared=None)
        if ctx_out:
            mla_c = _attention(q_c, kv_all, kr_all, n, n_ctx)
            hy_c = _hyena_mixer(hz_c, hy_short_w[i], hy_short_b[i], filt, hy_bias[i])
            xc, moe_in = _mix_out(s5_all, mla_c, hy_c, xc, mix_norm_g[i], w_out[i], mod_c[2], norm2_g[i],
                                  mod_c[3], mod_c[4], w_route, row0=n, tok0=bsz * n, n_tok=n_tok, shared=moe_in)
        y0, y1 = _hier_moe(*moe_in, moe_w1, moe_w3, moe_w2, i)
        pend_l, pend_c = (y0, y1, mod_l[5], 0), (y0, y1, mod_c[5], bsz * n)
    return _final_norm(xl, pend_l, final_g)
```

```python
import functools
import math

import jax
import jax.numpy as jnp
from jax import lax
from jax.experimental import pallas as pl
from jax.experimental.pallas import tpu as pltpu

D_MODEL = 1024
DEPTH = 4
GRID_W = 64
EPS = 1e-6

MIX_W = D_MODEL
S5_W = D_MODEL // 4
S5_GROUP = 16
S5_GROUPS = S5_W // S5_GROUP
S5_STATE = 64
S5_N = S5_GROUPS * S5_STATE
MLA_V = 64
MLA_W = D_MODEL // 2
MLA_HEADS = MLA_W // MLA_V
MLA_NOPE = 64
MLA_ROPE = 32
MLA_QK = MLA_NOPE + MLA_ROPE
MLA_Q_RANK = 384
MLA_KV_RANK = 256
MLA_SCALE = 1.0 / math.sqrt(MLA_NOPE + MLA_ROPE)
ROPE_BASE = 10000.0
HY_W = D_MODEL // 4
HY_ORDER = 2
HY_POS_EMB = 33
HY_FILTER_W = 64
HY_MIN_DECAY = math.log(1e-2) / 1.5
HY_MAX_DECAY = math.log(1e-2) / 0.3
MOE_GROUPS = 4
MOE_PER_GROUP = 8
MOE_EXPERTS = MOE_GROUPS * MOE_PER_GROUP
MOE_TOP_K = 2
MOE_HIDDEN = 512
MOE_BLOCK = 256

SUBLANES = 8
S5_CHUNK = 128
MM_ROWS = 256
ATTN_Q_BLOCK = 256
MIX_ROWS = 512
LANES = 128
ATTN_HEADS_PER_STEP = 4
VMEM_LIMIT = 48 * 1024 * 1024

F32 = jnp.float32
BF16 = jnp.bfloat16


def _rms_norm(x, g):
    xf = x.astype(F32)
    y = xf * lax.rsqrt(jnp.mean(xf * xf, axis=-1, keepdims=True) + EPS)
    return (y * g.astype(F32)).astype(x.dtype)


def _modulate(x, g, shift, scale):
    return _rms_norm(x, g) * (1.0 + scale) + shift


def _split_projection(p):
    o1 = S5_W
    o2 = o1 + MLA_Q_RANK
    o3 = o2 + MLA_KV_RANK
    o4 = o3 + MLA_ROPE
    return p[..., :o1], p[..., o1:o2], p[..., o2:o3], p[..., o3:o4], p[..., o4:]


def _s5_matrices(lam_re, lam_im, log_dt, b_re, b_im, c_re, c_im):
    g, p, h = S5_GROUPS, S5_STATE, S5_GROUP
    dt = jnp.exp(log_dt)[..., None]
    mag = jnp.exp(lam_re * dt)
    ar, ai = mag * jnp.cos(lam_im * dt), mag * jnp.sin(lam_im * dt)
    den = lam_re * lam_re + lam_im * lam_im
    fr = ((ar - 1.0) * lam_re + ai * lam_im) / den
    fi = (ai * lam_re - (ar - 1.0) * lam_im) / den
    bbr = fr[..., None] * b_re - fi[..., None] * b_im
    bbi = fr[..., None] * b_im + fi[..., None] * b_re
    eye = jnp.eye(g, dtype=F32)

    def block_in(m):
        return jnp.einsum('kgph,gj->kghjp', m, eye).reshape(2, g * h, g * p)

    def block_out(m):
        return jnp.einsum('kghp,gj->kgpjh', m, eye).reshape(2, g * p, g * h)

    bmat = jnp.concatenate([block_in(bbr), block_in(bbi)], axis=-1)
    cmat = jnp.concatenate([block_out(c_re), -block_out(c_im)], axis=1)
    lam = jnp.stack([ar[0], ai[0], ar[1], ai[1]]).reshape(4, 1, g * p)
    lam = jnp.broadcast_to(lam, (4, SUBLANES, g * p)).reshape(4 * SUBLANES, g * p)
    return bmat.astype(BF16), cmat.astype(BF16), lam


def _s5_kernel(uf_ref, ub_ref, bmat_ref, cmat_ref, lam_ref, yf_ref, yb_ref, buf_f, buf_b, state):
    rows = uf_ref.shape[0]
    steps = rows // SUBLANES
    n = S5_N
    s = SUBLANES

    @pl.when(pl.program_id(0) == 0)
    def _():
        state[...] = jnp.zeros_like(state)

    def drive(r, carry):
        r0 = pl.multiple_of(r * MM_ROWS, MM_ROWS)
        buf_f[pl.ds(r0, MM_ROWS), :] = jnp.dot(uf_ref[pl.ds(r0, MM_ROWS), :].astype(BF16), bmat_ref[0],
                                               preferred_element_type=F32)
        buf_b[pl.ds(r0, MM_ROWS), :] = jnp.dot(ub_ref[pl.ds(r0, MM_ROWS), :].astype(BF16), bmat_ref[1],
                                               preferred_element_type=F32)
        return carry

    lax.fori_loop(0, rows // MM_ROWS, drive, 0)

    def step(j, carry):
        fr, fi, br, bi = carry
        rf = pl.multiple_of(j * s, s)
        rb = pl.multiple_of((steps - 1 - j) * s, s)
        lfr, lfi = lam_ref[0:s, :], lam_ref[s:2 * s, :]
        lbr, lbi = lam_ref[2 * s:3 * s, :], lam_ref[3 * s:4 * s, :]
        nfr = lfr * fr - lfi * fi + buf_f[pl.ds(rf, s), 0:n]
        nfi = lfr * fi + lfi * fr + buf_f[pl.ds(rf, s), n:2 * n]
        nbr = lbr * br - lbi * bi + buf_b[pl.ds(rb, s), 0:n]
        nbi = lbr * bi + lbi * br + buf_b[pl.ds(rb, s), n:2 * n]
        buf_f[pl.ds(rf, s), 0:n] = nfr
        buf_f[pl.ds(rf, s), n:2 * n] = nfi
        buf_b[pl.ds(rb, s), 0:n] = nbr
        buf_b[pl.ds(rb, s), n:2 * n] = nbi
        return nfr, nfi, nbr, nbi

    init = (state[0:s, :], state[s:2 * s, :], state[2 * s:3 * s, :], state[3 * s:4 * s, :])
    fr, fi, br, bi = lax.fori_loop(0, steps, step, init, unroll=2)
    state[0:s, :] = fr
    state[s:2 * s, :] = fi
    state[2 * s:3 * s, :] = br
    state[3 * s:4 * s, :] = bi

    def readout(r, carry):
        r0 = pl.multiple_of(r * MM_ROWS, MM_ROWS)
        yf_ref[pl.ds(r0, MM_ROWS), :] = jnp.dot(buf_f[pl.ds(r0, MM_ROWS), :].astype(BF16), cmat_ref[0],
                                                preferred_element_type=F32)
        yb_ref[pl.ds(r0, MM_ROWS), :] = jnp.dot(buf_b[pl.ds(r0, MM_ROWS), :].astype(BF16), cmat_ref[1],
                                                preferred_element_type=F32)
        return carry

    lax.fori_loop(0, rows // MM_ROWS, readout, 0)


def _s5_scan(u_tm, bmat, cmat, lam, n_ctx_chunks, chunk):
    rows_all, w = u_tm.shape
    rows = chunk * SUBLANES
    n_chunks = rows_all // rows
    n_lat_chunks = n_chunks - n_ctx_chunks
    n2 = 2 * S5_N

    def fwd_chunk(i):
        return (i + n_lat_chunks) % n_chunks

    def bwd_chunk(i):
        return n_chunks - 1 - i

    return pl.pallas_call(
        _s5_kernel,
        grid=(n_chunks,),
        in_specs=[
            pl.BlockSpec((rows, w), lambda i: (fwd_chunk(i), 0)),
            pl.BlockSpec((rows, w), lambda i: (bwd_chunk(i), 0)),
            pl.BlockSpec((2, w, n2), lambda i: (0, 0, 0)),
            pl.BlockSpec((2, n2, w), lambda i: (0, 0, 0)),
            pl.BlockSpec((4 * SUBLANES, S5_N), lambda i: (0, 0)),
        ],
        out_specs=[
            pl.BlockSpec((rows, w), lambda i: (fwd_chunk(i), 0)),
            pl.BlockSpec((rows, w), lambda i: (bwd_chunk(i), 0)),
        ],
        out_shape=[jax.ShapeDtypeStruct((rows_all, w), F32)] * 2,
        scratch_shapes=[
            pltpu.VMEM((rows, n2), F32),
            pltpu.VMEM((rows, n2), F32),
            pltpu.VMEM((4 * SUBLANES, S5_N), F32),
        ],
        compiler_params=pltpu.CompilerParams(dimension_semantics=("arbitrary",), vmem_limit_bytes=VMEM_LIMIT),
        name="s5_scan",
    )(u_tm, u_tm, bmat, cmat, lam)


def _s5_mixer(u_tm, n_ctx, lam_re, lam_im, log_dt, b_re, b_im, c_re, c_im, d, glu_w):
    n_all, bw = u_tm.shape
    bsz, w = bw // S5_W, S5_W
    assert bsz == SUBLANES and n_all % S5_CHUNK == 0 and n_ctx % S5_CHUNK == 0
    bmat, cmat, lam = _s5_matrices(lam_re, lam_im, log_dt, b_re, b_im, c_re, c_im)
    u_rows = u_tm.reshape(n_all * bsz, w)
    y_f, y_b = _s5_scan(u_rows, bmat, cmat, lam, n_ctx // S5_CHUNK, S5_CHUNK)
    y = y_f + y_b + d * u_rows
    y = jax.nn.gelu(y)
    y = y * jax.nn.sigmoid(y @ glu_w)
    return y.reshape(n_all, bw)


def _axial_rope_tables(n_tokens):
    rows = n_tokens // GRID_W
    row = jnp.repeat(jnp.arange(rows), GRID_W).astype(F32)
    col = jnp.tile(jnp.arange(GRID_W), rows).astype(F32)
    half = MLA_ROPE // 2
    inv = ROPE_BASE ** (-jnp.arange(0, half, 2, dtype=F32) / half)
    ang_r = row[:, None] * inv
    ang_c = col[:, None] * inv
    cos = jnp.concatenate([jnp.cos(ang_r)] * 2 + [jnp.cos(ang_c)] * 2, axis=-1)
    sin = jnp.concatenate([jnp.sin(ang_r)] * 2 + [jnp.sin(ang_c)] * 2, axis=-1)
    return cos, sin


def _rope_partner(w):
    q = MLA_ROPE // 4
    parts = []
    for g in range(2):
        x1, x2 = w[..., 2 * g * q:(2 * g + 1) * q], w[..., (2 * g + 1) * q:(2 * g + 2) * q]
        parts += [-x2, x1]
    return jnp.concatenate(parts, axis=-1)


def _attn_kernel(q_ref, kv_ref, kr_ref, o_ref, kcat, vaug):
    hs, nk = kcat.shape[0], kcat.shape[1]
    dn = MLA_NOPE

    @pl.when(pl.program_id(2) == 0)
    def _():
        lane = lax.broadcasted_iota(jnp.int32, (nk, LANES), 1)
        ones_col = jnp.where(lane == 0, 1.0, 0.0).astype(BF16)
        for h in range(hs):
            kvh = kv_ref[0, :, h * LANES:(h + 1) * LANES]
            kcat[h] = jnp.where(lane < dn, kvh, kr_ref[0])
            vaug[h] = jnp.where(lane >= dn, kvh, ones_col)

    outs = []
    for h in range(hs):
        s = lax.dot_general(q_ref[0, :, h * LANES:(h + 1) * LANES], kcat[h], (((1,), (1,)), ((), ())),
                            preferred_element_type=F32)
        p = jnp.exp(s - jnp.max(s, axis=-1, keepdims=True))
        o = jnp.dot(p.astype(BF16), vaug[h], preferred_element_type=F32)
        outs.append(o[:, dn:] / o[:, 0:1])
    o_ref[0] = jnp.concatenate(outs, axis=-1)


def _attention(q, kv, kr, key_row0, nk):
    bsz, nq, _ = q.shape
    heads, dv = MLA_HEADS, MLA_V
    tq = min(ATTN_Q_BLOCK, nq)
    hs = ATTN_HEADS_PER_STEP
    kb = key_row0 // nk
    return pl.pallas_call(
        _attn_kernel,
        grid=(bsz, heads // hs, nq // tq),
        in_specs=[
            pl.BlockSpec((1, tq, hs * LANES), lambda b, h, i: (b, i, h)),
            pl.BlockSpec((1, nk, hs * LANES), lambda b, h, i: (b, kb, h), pipeline_mode=pl.Buffered(1)),
            pl.BlockSpec((1, nk, LANES), lambda b, h, i: (b, kb, 0), pipeline_mode=pl.Buffered(1)),
        ],
        out_specs=pl.BlockSpec((1, tq, hs * dv), lambda b, h, i: (b, i, h)),
        out_shape=jax.ShapeDtypeStruct((bsz, nq, heads * dv), F32),
        scratch_shapes=[pltpu.VMEM((hs, nk, LANES), BF16), pltpu.VMEM((hs, nk, LANES), BF16)],
        compiler_params=pltpu.CompilerParams(
            dimension_semantics=("arbitrary", "arbitrary", "arbitrary"), vmem_limit_bytes=VMEM_LIMIT),
        name="mla_attention",
    )(q, kv, kr)


PIN_U = 0
PIN_CQ = PIN_U + S5_W
PIN_CKV = PIN_CQ + MLA_Q_RANK
PIN_HZ = PIN_CKV + MLA_KV_RANK
PIN_KR = PIN_HZ + 3 * HY_W
PIN_KRP = PIN_KR + LANES
PIN_COLS = PIN_KRP + LANES


def _proj_in_weights(w_in, w_uq):
    o1 = S5_W
    o2 = o1 + MLA_Q_RANK
    o3 = o2 + MLA_KV_RANK
    o4 = o3 + MLA_ROPE
    pad = LANES - MLA_QK
    w_kr = w_in[:, o3:o4]
    in_slot = lambda w: jnp.pad(w, ((0, 0), (MLA_NOPE, pad)))
    w_ext = jnp.concatenate([w_in[:, :o3], w_in[:, o4:], in_slot(w_kr), in_slot(_rope_partner(w_kr))], axis=1)
    w_h = w_uq.reshape(MLA_Q_RANK, MLA_HEADS, MLA_QK)
    w_rot = jnp.concatenate([jnp.zeros_like(w_h[..., :MLA_NOPE]), _rope_partner(w_h[..., MLA_NOPE:])], axis=-1)
    w_q = jnp.concatenate([jnp.pad(w, ((0, 0), (0, 0), (0, pad))).reshape(MLA_Q_RANK, MLA_HEADS * LANES)
                           for w in (w_h, w_rot)], axis=-1)
    return w_ext.astype(BF16), w_q.astype(BF16)


def _rope_slot_tables(n, n_ctx):
    cos, sin = _axial_rope_tables(n)
    pad = LANES - MLA_QK
    cos = jnp.concatenate([jnp.ones((n, MLA_NOPE), F32), cos, jnp.ones((n, pad), F32)], axis=-1)
    sin = jnp.concatenate([jnp.zeros((n, MLA_NOPE), F32), sin, jnp.zeros((n, pad), F32)], axis=-1)
    return (jnp.concatenate([cos, jnp.ones((n_ctx, LANES), F32)], axis=0),
            jnp.concatenate([sin, jnp.zeros((n_ctx, LANES), F32)], axis=0))


def _apply_pending(x_ref, pending_refs):
    y0_ref, y1_ref, gate_ref = pending_refs
    return x_ref[0] + gate_ref[0] * (y0_ref[...] + y1_ref[...])


def _pending_specs(pending, tm, blocks_per_batch, d):
    if pending is None:
        return [], []
    y0, y1, gate, row0 = pending
    blk0 = row0 // tm
    rows = pl.BlockSpec((tm, d), lambda b, i: (blk0 + b * blocks_per_batch + i, 0))
    return [rows, rows, pl.BlockSpec((1, 1, d), lambda b, i: (b, 0, 0))], [y0, y1, gate]


def _proj_in_kernel(*refs, has_pending, n_shared):
    if has_pending:
        x_ref, *pending_refs = refs[:4]
        refs = refs[4:]
    else:
        x_ref, refs = refs[0], refs[1:]
    g_ref, shift_ref, scale_ref, w_ref, qg_ref, wq_ref, kvg_ref, wkv_ref, cos_ref, sin_ref = refs[:10]
    u_ref, q_ref, kv_ref, kr_ref, hz_ref, *xo_ref = refs[10 + n_shared:]
    if has_pending:
        x = _apply_pending(x_ref, pending_refs)
        xo_ref[0][0] = x
    else:
        x = x_ref[0]
    h = ((_rms_rows(x) * g_ref[...]) * (1.0 + scale_ref[0]) + shift_ref[0]).astype(BF16)
    p = jnp.dot(h, w_ref[...], preferred_element_type=F32)
    u_ref[...] = p[:, PIN_U:PIN_CQ]
    hz_ref[0] = p[:, PIN_HZ:PIN_KR]
    cos, sin = cos_ref[...], sin_ref[...]
    kr_ref[0] = (p[:, PIN_KR:PIN_KRP] * cos + p[:, PIN_KRP:PIN_COLS] * sin).astype(BF16)
    cq = (_rms_rows(p[:, PIN_CQ:PIN_CKV]) * qg_ref[...]).astype(BF16)
    q2 = jnp.dot(cq, wq_ref[...], preferred_element_type=F32)
    hw = MLA_HEADS * LANES
    for hd in range(MLA_HEADS):
        a, b = q2[:, hd * LANES:(hd + 1) * LANES], q2[:, hw + hd * LANES:hw + (hd + 1) * LANES]
        q_ref[0, :, hd * LANES:(hd + 1) * LANES] = ((a * cos + b * sin) * MLA_SCALE).astype(BF16)
    ckv = (_rms_rows(p[:, PIN_CKV:PIN_HZ]) * kvg_ref[...]).astype(BF16)
    kv_ref[0] = jnp.dot(ckv, wkv_ref[...], preferred_element_type=F32).astype(BF16)


def _proj_in(x, pending, norm_g, shift, scale, w_ext, q_norm_g, w_q, kv_norm_g, w_ukv, cos, sin, row0, shared):
    bsz, n, d = x.shape
    n_all = cos.shape[0]
    tm = min(MIX_ROWS, n)
    blk0 = row0 // tm
    hw = MLA_HEADS * LANES

    def rows(width, first=0):
        return pl.BlockSpec((1, tm, width), lambda b, i: (b, first + i, 0))

    per_batch = pl.BlockSpec((1, 1, d), lambda b, i: (b, 0, 0))
    table = pl.BlockSpec((tm, LANES), lambda b, i: (blk0 + i, 0))
    pend_specs, pend_args = _pending_specs(pending, tm, n // tm, d)
    has_pending = pending is not None
    out_specs = [pl.BlockSpec((tm, S5_W), lambda b, i: (blk0 + i, b)), rows(hw), rows(hw, blk0), rows(LANES, blk0),
                 rows(3 * HY_W)]
    out_shape = [jax.ShapeDtypeStruct((n_all, bsz * S5_W), F32), jax.ShapeDtypeStruct((bsz, n, hw), BF16),
                 jax.ShapeDtypeStruct((bsz, n_all, hw), BF16), jax.ShapeDtypeStruct((bsz, n_all, LANES), BF16),
                 jax.ShapeDtypeStruct((bsz, n, 3 * HY_W), F32)]
    if has_pending:
        out_specs.append(rows(d))
        out_shape.append(jax.ShapeDtypeStruct((bsz, n, d), F32))
    n_in = 1 + len(pend_args) + 10
    shared = list(shared or ())
    aliases = {n_in + k: out for k, out in enumerate((0, 2, 3)[:len(shared)])}
    outs = pl.pallas_call(
        functools.partial(_proj_in_kernel, has_pending=has_pending, n_shared=len(shared)),
        grid=(bsz, n // tm),
        in_specs=[rows(d), *pend_specs, _const_spec((1, d)), per_batch, per_batch, _const_spec((d, PIN_COLS)),
                  _const_spec((1, MLA_Q_RANK)), _const_spec((MLA_Q_RANK, 2 * hw)),
                  _const_spec((1, MLA_KV_RANK)), _const_spec((MLA_KV_RANK, hw)), table, table,
                  *[pl.BlockSpec(memory_space=pl.ANY)] * len(shared)],
        out_specs=out_specs,
        out_shape=out_shape,
        input_output_aliases=aliases,
        compiler_params=pltpu.CompilerParams(dimension_semantics=("arbitrary", "arbitrary"),
                                             vmem_limit_bytes=VMEM_LIMIT),
        name="proj_in",
    )(x, *pend_args, norm_g.reshape(1, d), shift, scale, w_ext, q_norm_g.reshape(1, -1), w_q,
      kv_norm_g.reshape(1, -1), w_ukv.astype(BF16), cos, sin, *shared)
    return (outs[0], outs[2], outs[3]), outs[1], outs[4], (outs[5] if has_pending else x)


def _short_conv(z, w, b):
    n = z.shape[1]
    zp = jnp.pad(z, ((0, 0), (1, 1), (0, 0)))
    return zp[:, :n] * w[0] + zp[:, 1:n + 1] * w[1] + zp[:, 2:] * w[2] + b


def _hyena_filter_taps(n, w1, b1, w2, b2, w3, b3, freq):
    t = jnp.linspace(0.0, 1.0, n, dtype=F32)[:, None]
    bands = (HY_POS_EMB - 1) // 2
    f = jnp.linspace(1e-4, bands - 1, bands, dtype=F32)
    ang = (2.0 * math.pi * jnp.arange(n, dtype=F32) / n)[:, None] * f
    z = jnp.concatenate([t, jnp.cos(ang), -jnp.sin(ang)], axis=-1)
    h = jnp.sin(freq[0] * (z @ w1 + b1))
    h = jnp.sin(freq[1] * (h @ w2 + b2))
    h = (h @ w3 + b3).astype(F32).reshape(n, 2, HY_ORDER, HY_W)
    deltas = jnp.abs(jnp.linspace(HY_MIN_DECAY, HY_MAX_DECAY, HY_W, dtype=F32))
    h = h * jnp.exp(-t[:, :, None, None] * deltas)
    fwd, bwd = h[:, 0], h[:, 1]
    return jnp.concatenate([fwd, jnp.zeros((1, HY_ORDER, HY_W), F32), bwd[:0:-1]], axis=0)


def _fft_long_conv(y, k_f):
    n = y.shape[1]
    yf = jnp.fft.rfft(y, n=2 * n, axis=1)
    return jnp.fft.irfft(yf * k_f, n=2 * n, axis=1)[:, :n]


def _hyena_mixer(z, short_w, short_b, filt, bias):
    n = z.shape[1]
    z = _short_conv(z, short_w, short_b)
    v, x1, x2 = jnp.split(z, 3, axis=-1)
    y = v.astype(F32)
    if (2 * n) % HY_N2 == 0 and (2 * n) // HY_N2 >= 2 * SUBLANES:
        k_time = _hyena_filter_taps(n, *filt)
        tables = _dft_tables(2 * n)
        kr, ki = _hyena_spectrum(k_time.reshape(2 * n, HY_ORDER * HY_W), tables)
        for o, gate in enumerate((x1, x2)):
            cols = slice(o * HY_W, (o + 1) * HY_W)
            y = gate.astype(F32) * (_hyena_long_conv(y, kr[:, cols], ki[:, cols], tables) + y * bias[o].astype(F32))
    else:
        k_f = jnp.fft.rfft(_hyena_filter_taps(n, *filt), axis=0)
        for o, gate in enumerate((x1, x2)):
            y = gate.astype(F32) * (_fft_long_conv(y, k_f[:, o]) + y * bias[o].astype(F32))
    return y.astype(z.dtype)


HY_N2 = 128
HY_PITCH = 136
HY_UNROLL = 8


def _dft_tables(nfft):
    n1 = nfft // HY_N2
    half = n1 // 2
    k1 = jnp.arange(n1, dtype=jnp.int32)
    n2 = jnp.arange(HY_N2, dtype=jnp.int32)
    t = HY_N2 * k1[None, None, :] + n2[:, None, None]
    ang = (2.0 * math.pi / nfft) * ((k1[None, :, None] * t) % nfft).astype(F32)
    gr, gi = jnp.cos(ang), -jnp.sin(ang)
    g_cplx = jnp.concatenate([jnp.concatenate([gr[..., :half], -gi[..., :half]], -1),
                              jnp.concatenate([gi[..., :half], gr[..., :half]], -1)], axis=1)
    g_real = jnp.concatenate([gr, gi], axis=1)
    hr = jnp.cos(ang).swapaxes(1, 2)[:, :half] / nfft
    hi = jnp.sin(ang).swapaxes(1, 2)[:, :half] / nfft
    g_inv = jnp.concatenate([jnp.concatenate([hr, -hi], -1), jnp.concatenate([hi, hr], -1)], axis=1)
    a2 = (2.0 * math.pi / HY_N2) * ((n2[:, None] * n2[None, :]) % HY_N2).astype(F32)
    fr, fi = jnp.cos(a2), -jnp.sin(a2)
    f2 = jnp.concatenate([jnp.concatenate([fr, -fi], -1), jnp.concatenate([fi, fr], -1)], axis=0)
    f2_inv = jnp.concatenate([jnp.concatenate([fr, fi], -1), jnp.concatenate([-fi, fr], -1)], axis=0)
    return dict(n1=n1, g_cplx=g_cplx.astype(BF16), g_real=g_real.astype(BF16), g_inv=g_inv.astype(BF16),
                f2=f2.astype(BF16), f2_inv=f2_inv.astype(BF16))


def _dft_stage1(load_rows, g_ref, s_re, s_im, n1):
    def body(n2, carry):
        a = jnp.dot(g_ref[n2], load_rows(n2).astype(BF16), preferred_element_type=F32)
        s_re[pl.ds(n2, n1, stride=HY_PITCH), :] = a[:n1]
        s_im[pl.ds(n2, n1, stride=HY_PITCH), :] = a[n1:]
        return carry

    lax.fori_loop(0, HY_N2, body, 0, unroll=HY_UNROLL)


def _dft_stage2(f2_ref, s_re, s_im, k1):
    r0 = pl.multiple_of(k1 * HY_PITCH, SUBLANES)
    sl = jnp.concatenate([s_re[pl.ds(r0, HY_N2), :], s_im[pl.ds(r0, HY_N2), :]], axis=0).astype(BF16)
    x = jnp.dot(f2_ref[...], sl, preferred_element_type=F32)
    return r0, x[:HY_N2], x[HY_N2:]


def _hyena_spectrum_kernel(x_ref, g_ref, f2_ref, kr_ref, ki_ref, s_re, s_im, *, n1):
    _dft_stage1(lambda n2: x_ref[pl.ds(n2, n1, stride=HY_PITCH), :], g_ref, s_re, s_im, n1)

    def body(k1, carry):
        _, xr, xi = _dft_stage2(f2_ref, s_re, s_im, k1)
        q0 = pl.multiple_of(k1 * HY_N2, HY_N2)
        kr_ref[pl.ds(q0, HY_N2), :] = xr
        ki_ref[pl.ds(q0, HY_N2), :] = xi
        return carry

    lax.fori_loop(0, n1, body, 0)


def _hyena_conv_kernel(xr_ref, xi_ref, g_ref, f2_ref, f2i_ref, gi_ref, kr_ref, ki_ref, yr_ref, yi_ref,
                       s_re, s_im, *, n1):
    half = n1 // 2

    def load_rows(n2):
        return jnp.concatenate([xr_ref[0, pl.ds(n2, half, stride=HY_PITCH), :],
                                xi_ref[0, pl.ds(n2, half, stride=HY_PITCH), :]], axis=0)

    _dft_stage1(load_rows, g_ref, s_re, s_im, n1)

    def spectrum_product(k1, carry):
        r0, xr, xi = _dft_stage2(f2_ref, s_re, s_im, k1)
        q0 = pl.multiple_of(k1 * HY_N2, HY_N2)
        kr, ki = kr_ref[pl.ds(q0, HY_N2), :], ki_ref[pl.ds(q0, HY_N2), :]
        y = jnp.concatenate([xr * kr - xi * ki, xr * ki + xi * kr], axis=0).astype(BF16)
        b = jnp.dot(f2i_ref[...], y, preferred_element_type=F32)
        s_re[pl.ds(r0, HY_N2), :] = b[:HY_N2]
        s_im[pl.ds(r0, HY_N2), :] = b[HY_N2:]
        return carry

    lax.fori_loop(0, n1, spectrum_product, 0, unroll=HY_UNROLL // 2)

    yr_ref[...] = jnp.zeros_like(yr_ref)
    yi_ref[...] = jnp.zeros_like(yi_ref)

    def inverse_stage1(n2, carry):
        bs = jnp.concatenate([s_re[pl.ds(n2, n1, stride=HY_PITCH), :],
                              s_im[pl.ds(n2, n1, stride=HY_PITCH), :]], axis=0).astype(BF16)
        y = jnp.dot(gi_ref[n2], bs, preferred_element_type=F32)
        yr_ref[0, pl.ds(n2, half, stride=HY_PITCH), :] = y[:half]
        yi_ref[0, pl.ds(n2, half, stride=HY_PITCH), :] = y[half:]
        return carry

    lax.fori_loop(0, HY_N2, inverse_stage1, 0, unroll=HY_UNROLL)


def _to_strided(a, rows):
    lead, c = a.shape[:-2], a.shape[-1]
    a = a.reshape(lead + (rows, HY_N2, c))
    a = jnp.pad(a, [(0, 0)] * len(lead) + [(0, 0), (0, HY_PITCH - HY_N2), (0, 0)])
    return a.reshape(lead + (rows * HY_PITCH, c))


def _from_strided(a, rows):
    lead, c = a.shape[:-2], a.shape[-1]
    return a.reshape(lead + (rows, HY_PITCH, c))[..., :HY_N2, :].reshape(lead + (rows * HY_N2, c))


def _const_spec(shape):
    return pl.BlockSpec(shape, lambda *_: (0,) * len(shape), pipeline_mode=pl.Buffered(1))


def _hyena_spectrum(k_time, tables):
    nfft, c = k_time.shape
    n1 = tables["n1"]
    spec_out = pl.BlockSpec((nfft, LANES), lambda j: (0, j))
    return pl.pallas_call(
        functools.partial(_hyena_spectrum_kernel, n1=n1),
        grid=(c // LANES,),
        in_specs=[pl.BlockSpec((n1 * HY_PITCH, LANES), lambda j: (0, j)),
                  _const_spec((HY_N2, 2 * n1, n1)), _const_spec((2 * HY_N2, 2 * HY_N2))],
        out_specs=[spec_out, spec_out],
        out_shape=[jax.ShapeDtypeStruct((nfft, c), F32)] * 2,
        scratch_shapes=[pltpu.VMEM((n1 * HY_PITCH, LANES), F32)] * 2,
        compiler_params=pltpu.CompilerParams(dimension_semantics=("arbitrary",), vmem_limit_bytes=VMEM_LIMIT),
        name="hyena_spectrum",
    )(_to_strided(k_time, n1), tables["g_real"], tables["f2"])


def _hyena_long_conv(y, kr, ki, tables):
    bsz, n, c = y.shape
    n1 = tables["n1"]
    half = n1 // 2
    pairs = bsz // 2
    xr, xi = _to_strided(y[0::2], half), _to_strided(y[1::2], half)
    spec_x = pl.BlockSpec((1, half * HY_PITCH, LANES), lambda j, p: (p, 0, j))
    spec_k = pl.BlockSpec((n1 * HY_N2, LANES), lambda j, p: (0, j), pipeline_mode=pl.Buffered(1))
    yr, yi = pl.pallas_call(
        functools.partial(_hyena_conv_kernel, n1=n1),
        grid=(c // LANES, pairs),
        in_specs=[spec_x, spec_x,
                  _const_spec((HY_N2, 2 * n1, n1)), _const_spec((2 * HY_N2, 2 * HY_N2)),
                  _const_spec((2 * HY_N2, 2 * HY_N2)), _const_spec((HY_N2, n1, 2 * n1)),
                  spec_k, spec_k],
        out_specs=[spec_x, spec_x],
        out_shape=[jax.ShapeDtypeStruct((pairs, half * HY_PITCH, c), F32)] * 2,
        scratch_shapes=[pltpu.VMEM((n1 * HY_PITCH, LANES), F32)] * 2,
        compiler_params=pltpu.CompilerParams(dimension_semantics=("arbitrary", "arbitrary"),
                                             vmem_limit_bytes=VMEM_LIMIT),
        name="hyena_conv",
    )(xr, xi, tables["g_cplx"], tables["f2"], tables["f2_inv"], tables["g_inv"], kr, ki)
    out = jnp.stack([_from_strided(yr, half), _from_strided(yi, half)], axis=1)
    return out.reshape(bsz, n, c)


def _rms_rows(v):
    return v * lax.rsqrt(jnp.mean(v * v, axis=-1, keepdims=True) + EPS)


HIGH_HALF = -(1 << 16)


def _pack_bf16_pairs(f):
    half = f.shape[1] // 2
    bits = lax.bitcast_convert_type(f.astype(F32), jnp.int32)
    return lax.shift_right_logical(bits[:, :half], 16) | (bits[:, half:] & HIGH_HALF)


def _unpack_bf16_pairs(w):
    lo = lax.bitcast_convert_type(lax.shift_left(w, 16), F32)
    hi = lax.bitcast_convert_type(w & HIGH_HALF, F32)
    return jnp.concatenate([lo, hi], axis=-1).astype(BF16)


def _mix_out_kernel(s5_ref, mla_ref, hy_ref, x_ref, g_ref, w_ref, gate_ref, n2g_ref, shift_ref, scale_ref, wr_ref,
                    *rest):
    xo_ref, f_ref, lg_ref = rest[-3:]
    c1, c2 = S5_W, S5_W + MLA_W
    y = jnp.dot((_rms_rows(s5_ref[...]) * g_ref[:, :c1]).astype(BF16), w_ref[:c1, :], preferred_element_type=F32)
    y += jnp.dot((_rms_rows(mla_ref[0]) * g_ref[:, c1:c2]).astype(BF16), w_ref[c1:c2, :], preferred_element_type=F32)
    y += jnp.dot((_rms_rows(hy_ref[0]) * g_ref[:, c2:]).astype(BF16), w_ref[c2:, :], preferred_element_type=F32)
    x_new = x_ref[0] + gate_ref[0] * y
    xo_ref[0] = x_new
    f = ((_rms_rows(x_new) * n2g_ref[...]) * (1.0 + scale_ref[0]) + shift_ref[0]).astype(BF16)
    f_ref[...] = _pack_bf16_pairs(f)
    lg_ref[...] = jnp.dot(f, wr_ref[...], preferred_element_type=F32)


def _mix_out(y_s5, y_mla, y_hy, x, mix_g, w_out, gate, norm2_g, shift, scale, w_route, row0, tok0, n_tok, shared):
    bsz, n, d = x.shape
    tm = min(MIX_ROWS, n)
    s5_blk0, tok_blk0, per_batch_blocks = row0 // tm, tok0 // tm, n // tm

    def rows(width):
        return pl.BlockSpec((1, tm, width), lambda b, i: (b, i, 0))

    def tokens(width):
        return pl.BlockSpec((tm, width), lambda b, i: (tok_blk0 + b * per_batch_blocks + i, 0))

    per_batch = pl.BlockSpec((1, 1, d), lambda b, i: (b, 0, 0))
    shared = list(shared or ())
    outs = pl.pallas_call(
        _mix_out_kernel,
        grid=(bsz, n // tm),
        in_specs=[pl.BlockSpec((tm, S5_W), lambda b, i: (s5_blk0 + i, b)), rows(MLA_W), rows(HY_W), rows(d),
                  _const_spec((1, d)), _const_spec((d, d)),
                  per_batch, _const_spec((1, d)), per_batch, per_batch, _const_spec((d, LANES)),
                  *[pl.BlockSpec(memory_space=pl.ANY)] * len(shared)],
        out_specs=[rows(d), tokens(d // 2), tokens(LANES)],
        out_shape=[jax.ShapeDtypeStruct((bsz, n, d), F32), jax.ShapeDtypeStruct((n_tok, d // 2), jnp.int32),
                   jax.ShapeDtypeStruct((n_tok, LANES), F32)],
        input_output_aliases={11 + k: 1 + k for k in range(len(shared))},
        compiler_params=pltpu.CompilerParams(dimension_semantics=("arbitrary", "arbitrary"),
                                             vmem_limit_bytes=VMEM_LIMIT),
        name="mix_out",
    )(y_s5, y_mla, y_hy, x, mix_g.reshape(1, d), w_out.astype(BF16), gate, norm2_g.reshape(1, d), shift, scale,
      w_route.astype(BF16), *shared)
    return outs[0], (outs[1], outs[2])


def _moe_kernel(be_ref, used_ref, x_ref, w1_ref, w3_ref, w2_ref, rw_ref, o_ref, w13_s, w2_s):
    i = pl.program_id(0)
    hid = w2_s.shape[0]

    @pl.when(jnp.logical_or(i == 0, be_ref[i] != be_ref[jnp.maximum(i - 1, 0)]))
    def _():
        w13_s[:, :hid] = w1_ref[0, 0].astype(BF16)
        w13_s[:, hid:] = w3_ref[0, 0].astype(BF16)
        w2_s[...] = w2_ref[0, 0].astype(BF16)

    @pl.when(i < used_ref[0])
    def _():
        ab = jnp.dot(_unpack_bf16_pairs(x_ref[...]), w13_s[...], preferred_element_type=F32)
        a, b = ab[:, :hid], ab[:, hid:]
        h = (a * jax.nn.sigmoid(a)) * b
        y = jnp.dot(h.astype(BF16), w2_s[...], preferred_element_type=F32)
        rw = rw_ref[...]
        for c0 in range(0, y.shape[1], LANES):
            o_ref[:, c0:c0 + LANES] = y[:, c0:c0 + LANES] * rw

    @pl.when(i >= used_ref[0])
    def _():
        o_ref[...] = jnp.zeros_like(o_ref)


def _moe_experts(xg, block_e, n_used, w1, w3, w2, layer, row_w):
    n_rows = xg.shape[0]
    d, hid = w1.shape[2], w1.shape[3]
    n_blocks = n_rows // MOE_BLOCK
    return pl.pallas_call(
        _moe_kernel,
        grid_spec=pltpu.PrefetchScalarGridSpec(
            num_scalar_prefetch=2,
            grid=(n_blocks,),
            in_specs=[
                pl.BlockSpec((MOE_BLOCK, d // 2), lambda i, be, nu: (i, 0)),
                pl.BlockSpec((1, 1, d, hid), lambda i, be, nu: (layer, be[i], 0, 0)),
                pl.BlockSpec((1, 1, d, hid), lambda i, be, nu: (layer, be[i], 0, 0)),
                pl.BlockSpec((1, 1, hid, d), lambda i, be, nu: (layer, be[i], 0, 0)),
                pl.BlockSpec((MOE_BLOCK, LANES), lambda i, be, nu: (i, 0)),
            ],
            out_specs=pl.BlockSpec((MOE_BLOCK, d), lambda i, be, nu: (i, 0)),
            scratch_shapes=[pltpu.VMEM((d, 2 * hid), BF16), pltpu.VMEM((hid, d), BF16)],
        ),
        out_shape=jax.ShapeDtypeStruct((n_rows, d), F32),
        compiler_params=pltpu.CompilerParams(dimension_semantics=("arbitrary",), vmem_limit_bytes=VMEM_LIMIT),
        name="moe_experts",
    )(block_e, n_used, xg, w1, w3, w2, row_w)


def _hier_moe(h, logits, w1, w3, w2, layer):
    t = h.shape[0]
    g_prob = jax.nn.softmax(logits[:, :MOE_GROUPS], axis=-1)
    g_idx = jnp.argmax(g_prob, axis=-1).astype(jnp.int32)
    g_w = jnp.max(g_prob, axis=-1, keepdims=True)
    e_logits = logits[:, MOE_GROUPS:MOE_GROUPS + MOE_EXPERTS].reshape(t, MOE_GROUPS, MOE_PER_GROUP)
    in_group = jnp.take_along_axis(e_logits, g_idx[:, None, None], axis=1)[:, 0]
    i1 = jnp.argmax(in_group, axis=-1).astype(jnp.int32)
    rest = jnp.where(jnp.arange(MOE_PER_GROUP, dtype=jnp.int32)[None, :] == i1[:, None], -jnp.inf, in_group)
    i2 = jnp.argmax(rest, axis=-1).astype(jnp.int32)
    top_v = jnp.stack([jnp.max(in_group, axis=-1), jnp.max(rest, axis=-1)], axis=-1)
    top_i = jnp.stack([i1, i2], axis=-1)
    gate = jax.nn.softmax(top_v, axis=-1) * g_w
    eid = g_idx[:, None] * MOE_PER_GROUP + top_i
    n_assign = t * MOE_TOP_K
    flat_e = eid.reshape(n_assign)
    flat_w = gate.reshape(n_assign)
    se, order = lax.sort((flat_e, jnp.arange(n_assign, dtype=jnp.int32)), num_keys=1)
    counts = jnp.sum(flat_e[None, :] == jnp.arange(MOE_EXPERTS, dtype=jnp.int32)[:, None], axis=1, dtype=jnp.int32)
    padded = (counts + MOE_BLOCK - 1) // MOE_BLOCK * MOE_BLOCK
    start = jnp.cumsum(counts) - counts
    pend = jnp.cumsum(padded)
    pstart = pend - padded
    experts = jnp.arange(MOE_EXPERTS, dtype=jnp.int32)[:, None]
    shift = jnp.sum(jnp.where(se[None, :] == experts, (pstart - start)[:, None], 0), axis=0)
    dest = (jnp.arange(n_assign, dtype=jnp.int32) + shift).astype(jnp.int32)
    n_blocks = -(-n_assign // MOE_BLOCK) + MOE_EXPERTS
    n_rows = n_blocks * MOE_BLOCK
    block_start = jnp.arange(n_blocks, dtype=jnp.int32) * MOE_BLOCK
    block_e = jnp.minimum(jnp.sum(pend[None, :] <= block_start[:, None], axis=1), MOE_EXPERTS - 1).astype(jnp.int32)
    off = (block_start - pstart[block_e])[:, None] + jnp.arange(MOE_BLOCK, dtype=jnp.int32)[None, :]
    valid = (off < counts[block_e][:, None]).reshape(n_rows)
    row_asg = order[jnp.where(valid, (start[block_e][:, None] + off).reshape(n_rows), 0)]
    row_tok = jnp.where(valid, row_asg // MOE_TOP_K, 0)
    row_w = jnp.where(valid, flat_w[row_asg], 0.0)
    xg = h[row_tok]
    n_used = (pend[-1:] // MOE_BLOCK).astype(jnp.int32)
    ys = _moe_experts(xg, block_e, n_used, w1, w3, w2, layer, jnp.broadcast_to(row_w[:, None], (n_rows, LANES)))
    _, slot = lax.sort((order, dest), num_keys=1)
    slot = slot.reshape(t, MOE_TOP_K)
    return ys[slot[:, 0]], ys[slot[:, 1]]


def _final_norm_kernel(x_ref, y0_ref, y1_ref, gate_ref, g_ref, o_ref):
    o_ref[0] = _rms_rows(_apply_pending(x_ref, (y0_ref, y1_ref, gate_ref))) * g_ref[...]


def _final_norm(x, pending, g):
    bsz, n, d = x.shape
    tm = min(MIX_ROWS, n)
    rows = pl.BlockSpec((1, tm, d), lambda b, i: (b, i, 0))
    pend_specs, pend_args = _pending_specs(pending, tm, n // tm, d)
    return pl.pallas_call(
        _final_norm_kernel,
        grid=(bsz, n // tm),
        in_specs=[rows, *pend_specs, _const_spec((1, d))],
        out_specs=rows,
        out_shape=jax.ShapeDtypeStruct((bsz, n, d), F32),
        compiler_params=pltpu.CompilerParams(dimension_semantics=("arbitrary", "arbitrary")),
        name="final_norm",
    )(x, *pend_args, g.reshape(1, d))


def kernel(x, c, ctx, c_ctx, ada_w, ada_b, norm1_g, norm2_g, w_in,
           s5_lambda_re, s5_lambda_im, s5_log_dt, s5_b_re, s5_b_im, s5_c_re, s5_c_im, s5_d, s5_glu_w,
           mla_q_norm_g, mla_kv_norm_g, mla_w_uq, mla_w_ukv,
           hy_short_w, hy_short_b, hy_f_w1, hy_f_b1, hy_f_w2, hy_f_b2, hy_f_w3, hy_f_b3, hy_f_freq, hy_bias,
           mix_norm_g, w_out, moe_w_group, moe_w_expert, moe_w1, moe_w3, moe_w2, final_g):
    bsz, n, d = x.shape
    n_ctx = ctx.shape[1]
    xl, xc = x, ctx
    act_l = jax.nn.silu(c)
    act_c = jax.nn.silu(c_ctx)
    rope_cos, rope_sin = _rope_slot_tables(n, n_ctx)
    pend_l = pend_c = None
    for i in range(DEPTH):
        ctx_out = i < DEPTH - 1
        mod_l = jnp.split((act_l @ ada_w[i] + ada_b[i])[:, None, :], 6, axis=-1)
        mod_c = jnp.split((act_c @ ada_w[i] + ada_b[i])[None, None, :], 6, axis=-1)
        mod_c = [jnp.broadcast_to(m, (bsz, 1, d)) for m in mod_c]
        w_ext, w_q = _proj_in_weights(w_in[i], mla_w_uq[i])
        proj = functools.partial(_proj_in, w_ext=w_ext, q_norm_g=mla_q_norm_g[i], w_q=w_q,
                                 kv_norm_g=mla_kv_norm_g[i], w_ukv=mla_w_ukv[i], cos=rope_cos, sin=rope_sin)
        seq, q_l, hz_l, xl = proj(xl, pend_l, norm1_g[i], mod_l[0], mod_l[1], row0=0, shared=None)
        (u_all, kv_all, kr_all), q_c, hz_c, xc = proj(xc, pend_c, norm1_g[i], mod_c[0], mod_c[1], row0=n, shared=seq)
        s5_all = _s5_mixer(u_all, n_ctx, s5_lambda_re[i], s5_lambda_im[i], s5_log_dt[i], s5_b_re[i], s5_b_im[i],
                           s5_c_re[i], s5_c_im[i], s5_d[i], s5_glu_w[i])
        mla_l = _attention(q_l, kv_all, kr_all, 0, n + n_ctx)
        filt = (hy_f_w1[i], hy_f_b1[i], hy_f_w2[i], hy_f_b2[i], hy_f_w3[i], hy_f_b3[i], hy_f_freq[i])
        hy_l = _hyena_mixer(hz_l, hy_short_w[i], hy_short_b[i], filt, hy_bias[i])
        w_route = jnp.pad(jnp.concatenate([moe_w_group[i], moe_w_expert[i]], axis=1),
                          ((0, 0), (0, LANES - MOE_GROUPS - MOE_EXPERTS)))
        n_tok = bsz * (n + n_ctx) if ctx_out else bsz * n
        xl, moe_in = _mix_out(s5_all, mla_l, hy_l, xl, mix_norm_g[i], w_out[i], mod_l[2], norm2_g[i],
                              mod_l[3], mod_l[4], w_route, row0=0, tok0=0, n_tok=n_tok, shared=None)
        if ctx_out:
            mla_c = _attention(q_c, kv_all, kr_all, n, n_ctx)
            hy_c = _hyena_mixer(hz_c, hy_short_w[i], hy_short_b[i], filt, hy_bias[i])
            xc, moe_in = _mix_out(s5_all, mla_c, hy_c, xc, mix_norm_g[i], w_out[i], mod_c[2], norm2_g[i],
                                  mod_c[3], mod_c[4], w_route, row0=n, tok0=bsz * n, n_tok=n_tok, shared=moe_in)
        y0, y1 = _hier_moe(*moe_in, moe_w1, moe_w3, moe_w2, i)
        pend_l, pend_c = (y0, y1, mod_l[5], 0), (y0, y1, mod_c[5], bsz * n)
    return _final_norm(xl, pend_l, final_g)
```

```python
import functools
import math

import jax
import jax.numpy as jnp
from jax import lax
from jax.experimental import pallas as pl
from jax.experimental.pallas import tpu as pltpu

D_MODEL = 1024
DEPTH = 4
GRID_W = 64
EPS = 1e-6

MIX_W = D_MODEL
S5_W = D_MODEL // 4
S5_GROUP = 16
S5_GROUPS = S5_W // S5_GROUP
S5_STATE = 64
S5_N = S5_GROUPS * S5_STATE
MLA_V = 64
MLA_W = D_MODEL // 2
MLA_HEADS = MLA_W // MLA_V
MLA_NOPE = 64
MLA_ROPE = 32
MLA_QK = MLA_NOPE + MLA_ROPE
MLA_Q_RANK = 384
MLA_KV_RANK = 256
MLA_SCALE = 1.0 / math.sqrt(MLA_NOPE + MLA_ROPE)
ROPE_BASE = 10000.0
HY_W = D_MODEL // 4
HY_ORDER = 2
HY_POS_EMB = 33
HY_FILTER_W = 64
HY_MIN_DECAY = math.log(1e-2) / 1.5
HY_MAX_DECAY = math.log(1e-2) / 0.3
MOE_GROUPS = 4
MOE_PER_GROUP = 8
MOE_EXPERTS = MOE_GROUPS * MOE_PER_GROUP
MOE_TOP_K = 2
MOE_HIDDEN = 512
MOE_BLOCK = 256

SUBLANES = 8
S5_CHUNK = 128
MM_ROWS = 256
ATTN_Q_BLOCK = 256
MIX_ROWS = 512
LANES = 128
ATTN_HEADS_PER_STEP = 4
VMEM_LIMIT = 48 * 1024 * 1024

F32 = jnp.float32
BF16 = jnp.bfloat16


def _rms_norm(x, g):
    xf = x.astype(F32)
    y = xf * lax.rsqrt(jnp.mean(xf * xf, axis=-1, keepdims=True) + EPS)
    return (y * g.astype(F32)).astype(x.dtype)


def _modulate(x, g, shift, scale):
    return _rms_norm(x, g) * (1.0 + scale) + shift


def _split_projection(p):
    o1 = S5_W
    o2 = o1 + MLA_Q_RANK
    o3 = o2 + MLA_KV_RANK
    o4 = o3 + MLA_ROPE
    return p[..., :o1], p[..., o1:o2], p[..., o2:o3], p[..., o3:o4], p[..., o4:]


def _s5_matrices(lam_re, lam_im, log_dt, b_re, b_im, c_re, c_im):
    g, p, h = S5_GROUPS, S5_STATE, S5_GROUP
    dt = jnp.exp(log_dt)[..., None]
    mag = jnp.exp(lam_re * dt)
    ar, ai = mag * jnp.cos(lam_im * dt), mag * jnp.sin(lam_im * dt)
    den = lam_re * lam_re + lam_im * lam_im
    fr = ((ar - 1.0) * lam_re + ai * lam_im) / den
    fi = (ai * lam_re - (ar - 1.0) * lam_im) / den
    bbr = fr[..., None] * b_re - fi[..., None] * b_im
    bbi = fr[..., None] * b_im + fi[..., None] * b_re
    eye = jnp.eye(g, dtype=F32)

    def block_in(m):
        return jnp.einsum('kgph,gj->kghjp', m, eye).reshape(2, g * h, g * p)

    def block_out(m):
        return jnp.einsum('kghp,gj->kgpjh', m, eye).reshape(2, g * p, g * h)

    bmat = jnp.concatenate([block_in(bbr), block_in(bbi)], axis=-1)
    cmat = jnp.concatenate([block_out(c_re), -block_out(c_im)], axis=1)
    lam = jnp.stack([ar[0], ai[0], ar[1], ai[1]]).reshape(4, 1, g * p)
    lam = jnp.broadcast_to(lam, (4, SUBLANES, g * p)).reshape(4 * SUBLANES, g * p)
    return bmat.astype(BF16), cmat.astype(BF16), lam


def _s5_kernel(uf_ref, ub_ref, bmat_ref, cmat_ref, lam_ref, yf_ref, yb_ref, buf_f, buf_b, state):
    rows = uf_ref.shape[0]
    steps = rows // SUBLANES
    n = S5_N
    s = SUBLANES

    @pl.when(pl.program_id(0) == 0)
    def _():
        state[...] = jnp.zeros_like(state)

    def drive(r, carry):
        r0 = pl.multiple_of(r * MM_ROWS, MM_ROWS)
        buf_f[pl.ds(r0, MM_ROWS), :] = jnp.dot(uf_ref[pl.ds(r0, MM_ROWS), :].astype(BF16), bmat_ref[0],
                                               preferred_element_type=F32)
        buf_b[pl.ds(r0, MM_ROWS), :] = jnp.dot(ub_ref[pl.ds(r0, MM_ROWS), :].astype(BF16), bmat_ref[1],
                                               preferred_element_type=F32)
        return carry

    lax.fori_loop(0, rows // MM_ROWS, drive, 0)

    def step(j, carry):
        fr, fi, br, bi = carry
        rf = pl.multiple_of(j * s, s)
        rb = pl.multiple_of((steps - 1 - j) * s, s)
        lfr, lfi = lam_ref[0:s, :], lam_ref[s:2 * s, :]
        lbr, lbi = lam_ref[2 * s:3 * s, :], lam_ref[3 * s:4 * s, :]
        nfr = lfr * fr - lfi * fi + buf_f[pl.ds(rf, s), 0:n]
        nfi = lfr * fi + lfi * fr + buf_f[pl.ds(rf, s), n:2 * n]
        nbr = lbr * br - lbi * bi + buf_b[pl.ds(rb, s), 0:n]
        nbi = lbr * bi + lbi * br + buf_b[pl.ds(rb, s), n:2 * n]
        buf_f[pl.ds(rf, s), 0:n] = nfr
        buf_f[pl.ds(rf, s), n:2 * n] = nfi
        buf_b[pl.ds(rb, s), 0:n] = nbr
        buf_b[pl.ds(rb, s), n:2 * n] = nbi
        return nfr, nfi, nbr, nbi

    init = (state[0:s, :], state[s:2 * s, :], state[2 * s:3 * s, :], state[3 * s:4 * s, :])
    fr, fi, br, bi = lax.fori_loop(0, steps, step, init, unroll=2)
    state[0:s, :] = fr
    state[s:2 * s, :] = fi
    state[2 * s:3 * s, :] = br
    state[3 * s:4 * s, :] = bi

    def readout(r, carry):
        r0 = pl.multiple_of(r * MM_ROWS, MM_ROWS)
        yf_ref[pl.ds(r0, MM_ROWS), :] = jnp.dot(buf_f[pl.ds(r0, MM_ROWS), :].astype(BF16), cmat_ref[0],
                                                preferred_element_type=F32)
        yb_ref[pl.ds(r0, MM_ROWS), :] = jnp.dot(buf_b[pl.ds(r0, MM_ROWS), :].astype(BF16), cmat_ref[1],
                                                preferred_element_type=F32)
        return carry

    lax.fori_loop(0, rows // MM_ROWS, readout, 0)


def _s5_scan(u_tm, bmat, cmat, lam, n_ctx_chunks, chunk):
    rows_all, w = u_tm.shape
    rows = chunk * SUBLANES
    n_chunks = rows_all // rows
    n_lat_chunks = n_chunks - n_ctx_chunks
    n2 = 2 * S5_N

    def fwd_chunk(i):
        return (i + n_lat_chunks) % n_chunks

    def bwd_chunk(i):
        return n_chunks - 1 - i

    return pl.pallas_call(
        _s5_kernel,
        grid=(n_chunks,),
        in_specs=[
            pl.BlockSpec((rows, w), lambda i: (fwd_chunk(i), 0)),
            pl.BlockSpec((rows, w), lambda i: (bwd_chunk(i), 0)),
            pl.BlockSpec((2, w, n2), lambda i: (0, 0, 0)),
            pl.BlockSpec((2, n2, w), lambda i: (0, 0, 0)),
            pl.BlockSpec((4 * SUBLANES, S5_N), lambda i: (0, 0)),
        ],
        out_specs=[
            pl.BlockSpec((rows, w), lambda i: (fwd_chunk(i), 0)),
            pl.BlockSpec((rows, w), lambda i: (bwd_chunk(i), 0)),
        ],
        out_shape=[jax.ShapeDtypeStruct((rows_all, w), F32)] * 2,
        scratch_shapes=[
            pltpu.VMEM((rows, n2), F32),
            pltpu.VMEM((rows, n2), F32),
            pltpu.VMEM((4 * SUBLANES, S5_N), F32),
        ],
        compiler_params=pltpu.CompilerParams(dimension_semantics=("arbitrary",), vmem_limit_bytes=VMEM_LIMIT),
        name="s5_scan",
    )(u_tm, u_tm, bmat, cmat, lam)


def _s5_mixer(u_tm, n_ctx, lam_re, lam_im, log_dt, b_re, b_im, c_re, c_im, d, glu_w):
    n_all, bw = u_tm.shape
    bsz, w = bw // S5_W, S5_W
    assert bsz == SUBLANES and n_all % S5_CHUNK == 0 and n_ctx % S5_CHUNK == 0
    bmat, cmat, lam = _s5_matrices(lam_re, lam_im, log_dt, b_re, b_im, c_re, c_im)
    u_rows = u_tm.reshape(n_all * bsz, w)
    y_f, y_b = _s5_scan(u_rows, bmat, cmat, lam, n_ctx // S5_CHUNK, S5_CHUNK)
    y = y_f + y_b + d * u_rows
    y = jax.nn.gelu(y)
    y = y * jax.nn.sigmoid(y @ glu_w)
    return y.reshape(n_all, bw)


def _axial_rope_tables(n_tokens):
    rows = n_tokens // GRID_W
    row = jnp.repeat(jnp.arange(rows), GRID_W).astype(F32)
    col = jnp.tile(jnp.arange(GRID_W), rows).astype(F32)
    half = MLA_ROPE // 2
    inv = ROPE_BASE ** (-jnp.arange(0, half, 2, dtype=F32) / half)
    ang_r = row[:, None] * inv
    ang_c = col[:, None] * inv
    cos = jnp.concatenate([jnp.cos(ang_r)] * 2 + [jnp.cos(ang_c)] * 2, axis=-1)
    sin = jnp.concatenate([jnp.sin(ang_r)] * 2 + [jnp.sin(ang_c)] * 2, axis=-1)
    return cos, sin


def _rope_partner(w):
    q = MLA_ROPE // 4
    parts = []
    for g in range(2):
        x1, x2 = w[..., 2 * g * q:(2 * g + 1) * q], w[..., (2 * g + 1) * q:(2 * g + 2) * q]
        parts += [-x2, x1]
    return jnp.concatenate(parts, axis=-1)


def _attn_kernel(q_ref, kv_ref, kr_ref, o_ref, kcat, vaug):
    hs, nk = kcat.shape[0], kcat.shape[1]
    dn = MLA_NOPE

    @pl.when(pl.program_id(2) == 0)
    def _():
        lane = lax.broadcasted_iota(jnp.int32, (nk, LANES), 1)
        ones_col = jnp.where(lane == 0, 1.0, 0.0).astype(BF16)
        for h in range(hs):
            kvh = kv_ref[0, :, h * LANES:(h + 1) * LANES]
            kcat[h] = jnp.where(lane < dn, kvh, kr_ref[0])
            vaug[h] = jnp.where(lane >= dn, kvh, ones_col)

    outs = []
    for h in range(hs):
        s = lax.dot_general(q_ref[0, :, h * LANES:(h + 1) * LANES], kcat[h], (((1,), (1,)), ((), ())),
                            preferred_element_type=F32)
        p = jnp.exp(s - jnp.max(s, axis=-1, keepdims=True))
        o = jnp.dot(p.astype(BF16), vaug[h], preferred_element_type=F32)
        outs.append(o[:, dn:] / o[:, 0:1])
    o_ref[0] = jnp.concatenate(outs, axis=-1)


def _attention(q, kv, kr, key_row0, nk):
    bsz, nq, _ = q.shape
    heads, dv = MLA_HEADS, MLA_V
    tq = min(ATTN_Q_BLOCK, nq)
    hs = ATTN_HEADS_PER_STEP
    kb = key_row0 // nk
    return pl.pallas_call(
        _attn_kernel,
        grid=(bsz, heads // hs, nq // tq),
        in_specs=[
            pl.BlockSpec((1, tq, hs * LANES), lambda b, h, i: (b, i, h)),
            pl.BlockSpec((1, nk, hs * LANES), lambda b, h, i: (b, kb, h), pipeline_mode=pl.Buffered(1)),
            pl.BlockSpec((1, nk, LANES), lambda b, h, i: (b, kb, 0), pipeline_mode=pl.Buffered(1)),
        ],
        out_specs=pl.BlockSpec((1, tq, hs * dv), lambda b, h, i: (b, i, h)),
        out_shape=jax.ShapeDtypeStruct((bsz, nq, heads * dv), F32),
        scratch_shapes=[pltpu.VMEM((hs, nk, LANES), BF16), pltpu.VMEM((hs, nk, LANES), BF16)],
        compiler_params=pltpu.CompilerParams(
            dimension_semantics=("arbitrary", "arbitrary", "arbitrary"), vmem_limit_bytes=VMEM_LIMIT),
        name="mla_attention",
    )(q, kv, kr)


PIN_U = 0
PIN_CQ = PIN_U + S5_W
PIN_CKV = PIN_CQ + MLA_Q_RANK
PIN_HZ = PIN_CKV + MLA_KV_RANK
PIN_KR = PIN_HZ + 3 * HY_W
PIN_KRP = PIN_KR + LANES
PIN_COLS = PIN_KRP + LANES


def _proj_in_weights(w_in, w_uq):
    o1 = S5_W
    o2 = o1 + MLA_Q_RANK
    o3 = o2 + MLA_KV_RANK
    o4 = o3 + MLA_ROPE
    pad = LANES - MLA_QK
    w_kr = w_in[:, o3:o4]
    in_slot = lambda w: jnp.pad(w, ((0, 0), (MLA_NOPE, pad)))
    w_ext = jnp.concatenate([w_in[:, :o3], w_in[:, o4:], in_slot(w_kr), in_slot(_rope_partner(w_kr))], axis=1)
    w_h = w_uq.reshape(MLA_Q_RANK, MLA_HEADS, MLA_QK)
    w_rot = jnp.concatenate([jnp.zeros_like(w_h[..., :MLA_NOPE]), _rope_partner(w_h[..., MLA_NOPE:])], axis=-1)
    w_q = jnp.concatenate([jnp.pad(w, ((0, 0), (0, 0), (0, pad))).reshape(MLA_Q_RANK, MLA_HEADS * LANES)
                           for w in (w_h, w_rot)], axis=-1)
    return w_ext.astype(BF16), w_q.astype(BF16)


def _rope_slot_tables(n, n_ctx):
    cos, sin = _axial_rope_tables(n)
    pad = LANES - MLA_QK
    cos = jnp.concatenate([jnp.ones((n, MLA_NOPE), F32), cos, jnp.ones((n, pad), F32)], axis=-1)
    sin = jnp.concatenate([jnp.zeros((n, MLA_NOPE), F32), sin, jnp.zeros((n, pad), F32)], axis=-1)
    return (jnp.concatenate([cos, jnp.ones((n_ctx, LANES), F32)], axis=0),
            jnp.concatenate([sin, jnp.zeros((n_ctx, LANES), F32)], axis=0))


def _apply_pending(x_ref, pending_refs):
    y0_ref, y1_ref, gate_ref = pending_refs
    return x_ref[0] + gate_ref[0] * (y0_ref[...] + y1_ref[...])


def _pending_specs(pending, tm, blocks_per_batch, d):
    if pending is None:
        return [], []
    y0, y1, gate, row0 = pending
    blk0 = row0 // tm
    rows = pl.BlockSpec((tm, d), lambda b, i: (blk0 + b * blocks_per_batch + i, 0))
    return [rows, rows, pl.BlockSpec((1, 1, d), lambda b, i: (b, 0, 0))], [y0, y1, gate]


def _proj_in_kernel(*refs, has_pending, n_shared):
    if has_pending:
        x_ref, *pending_refs = refs[:4]
        refs = refs[4:]
    else:
        x_ref, refs = refs[0], refs[1:]
    g_ref, shift_ref, scale_ref, w_ref, qg_ref, wq_ref, kvg_ref, wkv_ref, cos_ref, sin_ref = refs[:10]
    u_ref, q_ref, kv_ref, kr_ref, hz_ref, *xo_ref = refs[10 + n_shared:]
    if has_pending:
        x = _apply_pending(x_ref, pending_refs)
        xo_ref[0][0] = x
    else:
        x = x_ref[0]
    h = ((_rms_rows(x) * g_ref[...]) * (1.0 + scale_ref[0]) + shift_ref[0]).astype(BF16)
    p = jnp.dot(h, w_ref[...], preferred_element_type=F32)
    u_ref[...] = p[:, PIN_U:PIN_CQ]
    hz_ref[0] = p[:, PIN_HZ:PIN_KR]
    cos, sin = cos_ref[...], sin_ref[...]
    kr_ref[0] = (p[:, PIN_KR:PIN_KRP] * cos + p[:, PIN_KRP:PIN_COLS] * sin).astype(BF16)
    cq = (_rms_rows(p[:, PIN_CQ:PIN_CKV]) * qg_ref[...]).astype(BF16)
    q2 = jnp.dot(cq, wq_ref[...], preferred_element_type=F32)
    hw = MLA_HEADS * LANES
    for hd in range(MLA_HEADS):
        a, b = q2[:, hd * LANES:(hd + 1) * LANES], q2[:, hw + hd * LANES:hw + (hd + 1) * LANES]
        q_ref[0, :, hd * LANES:(hd + 1) * LANES] = ((a * cos + b * sin) * MLA_SCALE).astype(BF16)
    ckv = (_rms_rows(p[:, PIN_CKV:PIN_HZ]) * kvg_ref[...]).astype(BF16)
    kv_ref[0] = jnp.dot(ckv, wkv_ref[...], preferred_element_type=F32).astype(BF16)


def _proj_in(x, pending, norm_g, shift, scale, w_ext, q_norm_g, w_q, kv_norm_g, w_ukv, cos, sin, row0, shared):
    bsz, n, d = x.shape
    n_all = cos.shape[0]
    tm = min(MIX_ROWS, n)
    blk0 = row0 // tm
    hw = MLA_HEADS * LANES

    def rows(width, first=0):
        return pl.BlockSpec((1, tm, width), lambda b, i: (b, first + i, 0))

    per_batch = pl.BlockSpec((1, 1, d), lambda b, i: (b, 0, 0))
    table = pl.BlockSpec((tm, LANES), lambda b, i: (blk0 + i, 0))
    pend_specs, pend_args = _pending_specs(pending, tm, n // tm, d)
    has_pending = pending is not None
    out_specs = [pl.BlockSpec((tm, S5_W), lambda b, i: (blk0 + i, b)), rows(hw), rows(hw, blk0), rows(LANES, blk0),
                 rows(3 * HY_W)]
    out_shape = [jax.ShapeDtypeStruct((n_all, bsz * S5_W), F32), jax.ShapeDtypeStruct((bsz, n, hw), BF16),
                 jax.ShapeDtypeStruct((bsz, n_all, hw), BF16), jax.ShapeDtypeStruct((bsz, n_all, LANES), BF16),
                 jax.ShapeDtypeStruct((bsz, n, 3 * HY_W), F32)]
    if has_pending:
        out_specs.append(rows(d))
        out_shape.append(jax.ShapeDtypeStruct((bsz, n, d), F32))
    n_in = 1 + len(pend_args) + 10
    shared = list(shared or [jnp.zeros(out_shape[k].shape, out_shape[k].dtype) for k in (0, 2, 3)])
    aliases = {n_in + k: out for k, out in enumerate((0, 2, 3))}
    outs = pl.pallas_call(
        functools.partial(_proj_in_kernel, has_pending=has_pending, n_shared=len(shared)),
        grid=(bsz, n // tm),
        in_specs=[rows(d), *pend_specs, _const_spec((1, d)), per_batch, per_batch, _const_spec((d, PIN_COLS)),
                  _const_spec((1, MLA_Q_RANK)), _const_spec((MLA_Q_RANK, 2 * hw)),
                  _const_spec((1, MLA_KV_RANK)), _const_spec((MLA_KV_RANK, hw)), table, table,
                  *[pl.BlockSpec(memory_space=pl.ANY)] * len(shared)],
        out_specs=out_specs,
        out_shape=out_shape,
        input_output_aliases=aliases,
        compiler_params=pltpu.CompilerParams(dimension_semantics=("arbitrary", "arbitrary"),
                                             vmem_limit_bytes=VMEM_LIMIT),
        name="proj_in",
    )(x, *pend_args, norm_g.reshape(1, d), shift, scale, w_ext, q_norm_g.reshape(1, -1), w_q,
      kv_norm_g.reshape(1, -1), w_ukv.astype(BF16), cos, sin, *shared)
    return (outs[0], outs[2], outs[3]), outs[1], outs[4], (outs[5] if has_pending else x)


def _short_conv(z, w, b):
    n = z.shape[1]
    zp = jnp.pad(z, ((0, 0), (1, 1), (0, 0)))
    return zp[:, :n] * w[0] + zp[:, 1:n + 1] * w[1] + zp[:, 2:] * w[2] + b


def _hyena_filter_taps(n, w1, b1, w2, b2, w3, b3, freq):
    t = jnp.linspace(0.0, 1.0, n, dtype=F32)[:, None]
    bands = (HY_POS_EMB - 1) // 2
    f = jnp.linspace(1e-4, bands - 1, bands, dtype=F32)
    ang = (2.0 * math.pi * jnp.arange(n, dtype=F32) / n)[:, None] * f
    z = jnp.concatenate([t, jnp.cos(ang), -jnp.sin(ang)], axis=-1)
    h = jnp.sin(freq[0] * (z @ w1 + b1))
    h = jnp.sin(freq[1] * (h @ w2 + b2))
    h = (h @ w3 + b3).astype(F32).reshape(n, 2, HY_ORDER, HY_W)
    deltas = jnp.abs(jnp.linspace(HY_MIN_DECAY, HY_MAX_DECAY, HY_W, dtype=F32))
    h = h * jnp.exp(-t[:, :, None, None] * deltas)
    fwd, bwd = h[:, 0], h[:, 1]
    return jnp.concatenate([fwd, jnp.zeros((1, HY_ORDER, HY_W), F32), bwd[:0:-1]], axis=0)


def _dense_conv_kernel(x_ref, k_ref, ff_ref, g_ref, o_ref):
    n = x_ref.shape[1]
    k = k_ref[...].astype(BF16)
    kr = jnp.dot(ff_ref[0], k, preferred_element_type=F32)
    ki = jnp.dot(ff_ref[1], k, preferred_element_type=F32)
    x = x_ref[0].astype(BF16)
    yr = jnp.dot(ff_ref[0, :, :n], x, preferred_element_type=F32)
    yi = jnp.dot(ff_ref[1, :, :n], x, preferred_element_type=F32)
    zr, zi = (yr * kr - yi * ki).astype(BF16), (yr * ki + yi * kr).astype(BF16)
    o_ref[0] = jnp.dot(g_ref[0], zr, preferred_element_type=F32) + jnp.dot(g_ref[1], zi, preferred_element_type=F32)


def _dense_long_conv(y, k_time):
    bsz, n, c = y.shape
    nfft = 2 * n
    idx = jnp.arange(nfft, dtype=jnp.int32)
    ang = (2.0 * math.pi / nfft) * ((idx[:, None] * idx[None, :]) % nfft).astype(F32)
    ff = jnp.stack([jnp.cos(ang), -jnp.sin(ang)]).astype(BF16)
    g = (jnp.stack([jnp.cos(ang), -jnp.sin(ang)])[:, :n, :] / nfft).astype(BF16)
    return pl.pallas_call(
        _dense_conv_kernel,
        grid=(bsz,),
        in_specs=[pl.BlockSpec((1, n, c), lambda b: (b, 0, 0)), _const_spec((nfft, c)),
                  _const_spec((2, nfft, nfft)), _const_spec((2, n, nfft))],
        out_specs=pl.BlockSpec((1, n, c), lambda b: (b, 0, 0)),
        out_shape=jax.ShapeDtypeStruct((bsz, n, c), F32),
        compiler_params=pltpu.CompilerParams(dimension_semantics=("arbitrary",)),
        name="hyena_dense_conv",
    )(y, k_time, ff, g)


def _hyena_mixer(z, short_w, short_b, filt, bias):
    n = z.shape[1]
    z = _short_conv(z, short_w, short_b)
    v, x1, x2 = jnp.split(z, 3, axis=-1)
    y = v.astype(F32)
    if (2 * n) % HY_N2 == 0 and (2 * n) // HY_N2 >= 2 * SUBLANES:
        k_time = _hyena_filter_taps(n, *filt)
        tables = _dft_tables(2 * n)
        kr, ki = _hyena_spectrum(k_time.reshape(2 * n, HY_ORDER * HY_W), tables)
        for o, gate in enumerate((x1, x2)):
            cols = slice(o * HY_W, (o + 1) * HY_W)
            y = gate.astype(F32) * (_hyena_long_conv(y, kr[:, cols], ki[:, cols], tables) + y * bias[o].astype(F32))
    else:
        k_time = _hyena_filter_taps(n, *filt)
        for o, gate in enumerate((x1, x2)):
            y = gate.astype(F32) * (_dense_long_conv(y, k_time[:, o]) + y * bias[o].astype(F32))
    return y.astype(z.dtype)


HY_N2 = 128
HY_PITCH = 136
HY_UNROLL = 8


def _dft_tables(nfft):
    n1 = nfft // HY_N2
    half = n1 // 2
    k1 = jnp.arange(n1, dtype=jnp.int32)
    n2 = jnp.arange(HY_N2, dtype=jnp.int32)
    t = HY_N2 * k1[None, None, :] + n2[:, None, None]
    ang = (2.0 * math.pi / nfft) * ((k1[None, :, None] * t) % nfft).astype(F32)
    gr, gi = jnp.cos(ang), -jnp.sin(ang)
    g_cplx = jnp.concatenate([jnp.concatenate([gr[..., :half], -gi[..., :half]], -1),
                              jnp.concatenate([gi[..., :half], gr[..., :half]], -1)], axis=1)
    g_real = jnp.concatenate([gr, gi], axis=1)
    hr = jnp.cos(ang).swapaxes(1, 2)[:, :half] / nfft
    hi = jnp.sin(ang).swapaxes(1, 2)[:, :half] / nfft
    g_inv = jnp.concatenate([jnp.concatenate([hr, -hi], -1), jnp.concatenate([hi, hr], -1)], axis=1)
    a2 = (2.0 * math.pi / HY_N2) * ((n2[:, None] * n2[None, :]) % HY_N2).astype(F32)
    fr, fi = jnp.cos(a2), -jnp.sin(a2)
    f2 = jnp.concatenate([jnp.concatenate([fr, -fi], -1), jnp.concatenate([fi, fr], -1)], axis=0)
    f2_inv = jnp.concatenate([jnp.concatenate([fr, fi], -1), jnp.concatenate([-fi, fr], -1)], axis=0)
    return dict(n1=n1, g_cplx=g_cplx.astype(BF16), g_real=g_real.astype(BF16), g_inv=g_inv.astype(BF16),
                f2=f2.astype(BF16), f2_inv=f2_inv.astype(BF16))


def _dft_stage1(load_rows, g_ref, s_re, s_im, n1):
    def body(n2, carry):
        a = jnp.dot(g_ref[n2], load_rows(n2).astype(BF16), preferred_element_type=F32)
        s_re[pl.ds(n2, n1, stride=HY_PITCH), :] = a[:n1]
        s_im[pl.ds(n2, n1, stride=HY_PITCH), :] = a[n1:]
        return carry

    lax.fori_loop(0, HY_N2, body, 0, unroll=HY_UNROLL)


def _dft_stage2(f2_ref, s_re, s_im, k1):
    r0 = pl.multiple_of(k1 * HY_PITCH, SUBLANES)
    sl = jnp.concatenate([s_re[pl.ds(r0, HY_N2), :], s_im[pl.ds(r0, HY_N2), :]], axis=0).astype(BF16)
    x = jnp.dot(f2_ref[...], sl, preferred_element_type=F32)
    return r0, x[:HY_N2], x[HY_N2:]


def _hyena_spectrum_kernel(x_ref, g_ref, f2_ref, kr_ref, ki_ref, s_re, s_im, *, n1):
    _dft_stage1(lambda n2: x_ref[pl.ds(n2, n1, stride=HY_PITCH), :], g_ref, s_re, s_im, n1)

    def body(k1, carry):
        _, xr, xi = _dft_stage2(f2_ref, s_re, s_im, k1)
        q0 = pl.multiple_of(k1 * HY_N2, HY_N2)
        kr_ref[pl.ds(q0, HY_N2), :] = xr
        ki_ref[pl.ds(q0, HY_N2), :] = xi
        return carry

    lax.fori_loop(0, n1, body, 0)


def _hyena_conv_kernel(xr_ref, xi_ref, g_ref, f2_ref, f2i_ref, gi_ref, kr_ref, ki_ref, yr_ref, yi_ref,
                       s_re, s_im, *, n1):
    half = n1 // 2

    def load_rows(n2):
        return jnp.concatenate([xr_ref[0, pl.ds(n2, half, stride=HY_PITCH), :],
                                xi_ref[0, pl.ds(n2, half, stride=HY_PITCH), :]], axis=0)

    _dft_stage1(load_rows, g_ref, s_re, s_im, n1)

    def spectrum_product(k1, carry):
        r0, xr, xi = _dft_stage2(f2_ref, s_re, s_im, k1)
        q0 = pl.multiple_of(k1 * HY_N2, HY_N2)
        kr, ki = kr_ref[pl.ds(q0, HY_N2), :], ki_ref[pl.ds(q0, HY_N2), :]
        y = jnp.concatenate([xr * kr - xi * ki, xr * ki + xi * kr], axis=0).astype(BF16)
        b = jnp.dot(f2i_ref[...], y, preferred_element_type=F32)
        s_re[pl.ds(r0, HY_N2), :] = b[:HY_N2]
        s_im[pl.ds(r0, HY_N2), :] = b[HY_N2:]
        return carry

    lax.fori_loop(0, n1, spectrum_product, 0, unroll=HY_UNROLL // 2)

    yr_ref[...] = jnp.zeros_like(yr_ref)
    yi_ref[...] = jnp.zeros_like(yi_ref)

    def inverse_stage1(n2, carry):
        bs = jnp.concatenate([s_re[pl.ds(n2, n1, stride=HY_PITCH), :],
                              s_im[pl.ds(n2, n1, stride=HY_PITCH), :]], axis=0).astype(BF16)
        y = jnp.dot(gi_ref[n2], bs, preferred_element_type=F32)
        yr_ref[0, pl.ds(n2, half, stride=HY_PITCH), :] = y[:half]
        yi_ref[0, pl.ds(n2, half, stride=HY_PITCH), :] = y[half:]
        return carry

    lax.fori_loop(0, HY_N2, inverse_stage1, 0, unroll=HY_UNROLL)


def _to_strided(a, rows):
    lead, c = a.shape[:-2], a.shape[-1]
    a = a.reshape(lead + (rows, HY_N2, c))
    a = jnp.pad(a, [(0, 0)] * len(lead) + [(0, 0), (0, HY_PITCH - HY_N2), (0, 0)])
    return a.reshape(lead + (rows * HY_PITCH, c))


def _from_strided(a, rows):
    lead, c = a.shape[:-2], a.shape[-1]
    return a.reshape(lead + (rows, HY_PITCH, c))[..., :HY_N2, :].reshape(lead + (rows * HY_N2, c))


def _const_spec(shape):
    return pl.BlockSpec(shape, lambda *_: (0,) * len(shape), pipeline_mode=pl.Buffered(1))


def _hyena_spectrum(k_time, tables):
    nfft, c = k_time.shape
    n1 = tables["n1"]
    spec_out = pl.BlockSpec((nfft, LANES), lambda j: (0, j))
    return pl.pallas_call(
        functools.partial(_hyena_spectrum_kernel, n1=n1),
        grid=(c // LANES,),
        in_specs=[pl.BlockSpec((n1 * HY_PITCH, LANES), lambda j: (0, j)),
                  _const_spec((HY_N2, 2 * n1, n1)), _const_spec((2 * HY_N2, 2 * HY_N2))],
        out_specs=[spec_out, spec_out],
        out_shape=[jax.ShapeDtypeStruct((nfft, c), F32)] * 2,
        scratch_shapes=[pltpu.VMEM((n1 * HY_PITCH, LANES), F32)] * 2,
        compiler_params=pltpu.CompilerParams(dimension_semantics=("arbitrary",), vmem_limit_bytes=VMEM_LIMIT),
        name="hyena_spectrum",
    )(_to_strided(k_time, n1), tables["g_real"], tables["f2"])


def _hyena_long_conv(y, kr, ki, tables):
    bsz, n, c = y.shape
    n1 = tables["n1"]
    half = n1 // 2
    pairs = bsz // 2
    xr, xi = _to_strided(y[0::2], half), _to_strided(y[1::2], half)
    spec_x = pl.BlockSpec((1, half * HY_PITCH, LANES), lambda j, p: (p, 0, j))
    spec_k = pl.BlockSpec((n1 * HY_N2, LANES), lambda j, p: (0, j), pipeline_mode=pl.Buffered(1))
    yr, yi = pl.pallas_call(
        functools.partial(_hyena_conv_kernel, n1=n1),
        grid=(c // LANES, pairs),
        in_specs=[spec_x, spec_x,
                  _const_spec((HY_N2, 2 * n1, n1)), _const_spec((2 * HY_N2, 2 * HY_N2)),
                  _const_spec((2 * HY_N2, 2 * HY_N2)), _const_spec((HY_N2, n1, 2 * n1)),
                  spec_k, spec_k],
        out_specs=[spec_x, spec_x],
        out_shape=[jax.ShapeDtypeStruct((pairs, half * HY_PITCH, c), F32)] * 2,
        scratch_shapes=[pltpu.VMEM((n1 * HY_PITCH, LANES), F32)] * 2,
        compiler_params=pltpu.CompilerParams(dimension_semantics=("arbitrary", "arbitrary"),
                                             vmem_limit_bytes=VMEM_LIMIT),
        name="hyena_conv",
    )(xr, xi, tables["g_cplx"], tables["f2"], tables["f2_inv"], tables["g_inv"], kr, ki)
    out = jnp.stack([_from_strided(yr, half), _from_strided(yi, half)], axis=1)
    return out.reshape(bsz, n, c)


def _rms_rows(v):
    return v * lax.rsqrt(jnp.mean(v * v, axis=-1, keepdims=True) + EPS)


HIGH_HALF = -(1 << 16)


def _pack_bf16_pairs(f):
    half = f.shape[1] // 2
    bits = lax.bitcast_convert_type(f.astype(F32), jnp.int32)
    return lax.shift_right_logical(bits[:, :half], 16) | (bits[:, half:] & HIGH_HALF)


def _unpack_bf16_pairs(w):
    lo = lax.bitcast_convert_type(lax.shift_left(w, 16), F32)
    hi = lax.bitcast_convert_type(w & HIGH_HALF, F32)
    return jnp.concatenate([lo, hi], axis=-1).astype(BF16)


def _mix_out_kernel(s5_ref, mla_ref, hy_ref, x_ref, g_ref, w_ref, gate_ref, n2g_ref, shift_ref, scale_ref, wr_ref,
                    *rest):
    xo_ref, f_ref, lg_ref = rest[-3:]
    c1, c2 = S5_W, S5_W + MLA_W
    y = jnp.dot((_rms_rows(s5_ref[...]) * g_ref[:, :c1]).astype(BF16), w_ref[:c1, :], preferred_element_type=F32)
    y += jnp.dot((_rms_rows(mla_ref[0]) * g_ref[:, c1:c2]).astype(BF16), w_ref[c1:c2, :], preferred_element_type=F32)
    y += jnp.dot((_rms_rows(hy_ref[0]) * g_ref[:, c2:]).astype(BF16), w_ref[c2:, :], preferred_element_type=F32)
    x_new = x_ref[0] + gate_ref[0] * y
    xo_ref[0] = x_new
    f = ((_rms_rows(x_new) * n2g_ref[...]) * (1.0 + scale_ref[0]) + shift_ref[0]).astype(BF16)
    f_ref[...] = _pack_bf16_pairs(f)
    lg_ref[...] = jnp.dot(f, wr_ref[...], preferred_element_type=F32)


def _mix_out(y_s5, y_mla, y_hy, x, mix_g, w_out, gate, norm2_g, shift, scale, w_route, row0, tok0, n_tok, shared):
    bsz, n, d = x.shape
    tm = min(MIX_ROWS, n)
    s5_blk0, tok_blk0, per_batch_blocks = row0 // tm, tok0 // tm, n // tm

    def rows(width):
        return pl.BlockSpec((1, tm, width), lambda b, i: (b, i, 0))

    def tokens(width):
        return pl.BlockSpec((tm, width), lambda b, i: (tok_blk0 + b * per_batch_blocks + i, 0))

    per_batch = pl.BlockSpec((1, 1, d), lambda b, i: (b, 0, 0))
    if shared is None:
        shared = () if n_tok == bsz * n else (jnp.zeros((n_tok, d // 2), jnp.int32), jnp.zeros((n_tok, LANES), F32))
    shared = list(shared)
    outs = pl.pallas_call(
        _mix_out_kernel,
        grid=(bsz, n // tm),
        in_specs=[pl.BlockSpec((tm, S5_W), lambda b, i: (s5_blk0 + i, b)), rows(MLA_W), rows(HY_W), rows(d),
                  _const_spec((1, d)), _const_spec((d, d)),
                  per_batch, _const_spec((1, d)), per_batch, per_batch, _const_spec((d, LANES)),
                  *[pl.BlockSpec(memory_space=pl.ANY)] * len(shared)],
        out_specs=[rows(d), tokens(d // 2), tokens(LANES)],
        out_shape=[jax.ShapeDtypeStruct((bsz, n, d), F32), jax.ShapeDtypeStruct((n_tok, d // 2), jnp.int32),
                   jax.ShapeDtypeStruct((n_tok, LANES), F32)],
        input_output_aliases={11 + k: 1 + k for k in range(len(shared))},
        compiler_params=pltpu.CompilerParams(dimension_semantics=("arbitrary", "arbitrary"),
                                             vmem_limit_bytes=VMEM_LIMIT),
        name="mix_out",
    )(y_s5, y_mla, y_hy, x, mix_g.reshape(1, d), w_out.astype(BF16), gate, norm2_g.reshape(1, d), shift, scale,
      w_route.astype(BF16), *shared)
    return outs[0], (outs[1], outs[2])


def _moe_kernel(be_ref, used_ref, x_ref, w1_ref, w3_ref, w2_ref, rw_ref, o_ref, w13_s, w2_s):
    i = pl.program_id(0)
    hid = w2_s.shape[0]

    @pl.when(jnp.logical_or(i == 0, be_ref[i] != be_ref[jnp.maximum(i - 1, 0)]))
    def _():
        w13_s[:, :hid] = w1_ref[0, 0].astype(BF16)
        w13_s[:, hid:] = w3_ref[0, 0].astype(BF16)
        w2_s[...] = w2_ref[0, 0].astype(BF16)

    @pl.when(i < used_ref[0])
    def _():
        ab = jnp.dot(_unpack_bf16_pairs(x_ref[...]), w13_s[...], preferred_element_type=F32)
        a, b = ab[:, :hid], ab[:, hid:]
        h = (a * jax.nn.sigmoid(a)) * b
        y = jnp.dot(h.astype(BF16), w2_s[...], preferred_element_type=F32)
        rw = rw_ref[...]
        for c0 in range(0, y.shape[1], LANES):
            o_ref[:, c0:c0 + LANES] = y[:, c0:c0 + LANES] * rw

    @pl.when(i >= used_ref[0])
    def _():
        o_ref[...] = jnp.zeros_like(o_ref)


def _moe_experts(xg, block_e, n_used, w1, w3, w2, layer, row_w):
    n_rows = xg.shape[0]
    d, hid = w1.shape[2], w1.shape[3]
    n_blocks = n_rows // MOE_BLOCK
    return pl.pallas_call(
        _moe_kernel,
        grid_spec=pltpu.PrefetchScalarGridSpec(
            num_scalar_prefetch=2,
            grid=(n_blocks,),
            in_specs=[
                pl.BlockSpec((MOE_BLOCK, d // 2), lambda i, be, nu: (i, 0)),
                pl.BlockSpec((1, 1, d, hid), lambda i, be, nu: (layer, be[i], 0, 0)),
                pl.BlockSpec((1, 1, d, hid), lambda i, be, nu: (layer, be[i], 0, 0)),
                pl.BlockSpec((1, 1, hid, d), lambda i, be, nu: (layer, be[i], 0, 0)),
                pl.BlockSpec((MOE_BLOCK, LANES), lambda i, be, nu: (i, 0)),
            ],
            out_specs=pl.BlockSpec((MOE_BLOCK, d), lambda i, be, nu: (i, 0)),
            scratch_shapes=[pltpu.VMEM((d, 2 * hid), BF16), pltpu.VMEM((hid, d), BF16)],
        ),
        out_shape=jax.ShapeDtypeStruct((n_rows, d), F32),
        compiler_params=pltpu.CompilerParams(dimension_semantics=("arbitrary",), vmem_limit_bytes=VMEM_LIMIT),
        name="moe_experts",
    )(block_e, n_used, xg, w1, w3, w2, row_w)


def _hier_moe(h, logits, w1, w3, w2, layer):
    t = h.shape[0]
    g_prob = jax.nn.softmax(logits[:, :MOE_GROUPS], axis=-1)
    g_idx = jnp.argmax(g_prob, axis=-1).astype(jnp.int32)
    g_w = jnp.max(g_prob, axis=-1, keepdims=True)
    e_logits = logits[:, MOE_GROUPS:MOE_GROUPS + MOE_EXPERTS].reshape(t, MOE_GROUPS, MOE_PER_GROUP)
    in_group = jnp.take_along_axis(e_logits, g_idx[:, None, None], axis=1)[:, 0]
    i1 = jnp.argmax(in_group, axis=-1).astype(jnp.int32)
    rest = jnp.where(jnp.arange(MOE_PER_GROUP, dtype=jnp.int32)[None, :] == i1[:, None], -jnp.inf, in_group)
    i2 = jnp.argmax(rest, axis=-1).astype(jnp.int32)
    top_v = jnp.stack([jnp.max(in_group, axis=-1), jnp.max(rest, axis=-1)], axis=-1)
    top_i = jnp.stack([i1, i2], axis=-1)
    gate = jax.nn.softmax(top_v, axis=-1) * g_w
    eid = g_idx[:, None] * MOE_PER_GROUP + top_i
    n_assign = t * MOE_TOP_K
    flat_e = eid.reshape(n_assign)
    flat_w = gate.reshape(n_assign)
    se, order = lax.sort((flat_e, jnp.arange(n_assign, dtype=jnp.int32)), num_keys=1)
    counts = jnp.sum(flat_e[None, :] == jnp.arange(MOE_EXPERTS, dtype=jnp.int32)[:, None], axis=1, dtype=jnp.int32)
    padded = (counts + MOE_BLOCK - 1) // MOE_BLOCK * MOE_BLOCK
    start = jnp.cumsum(counts) - counts
    pend = jnp.cumsum(padded)
    pstart = pend - padded
    experts = jnp.arange(MOE_EXPERTS, dtype=jnp.int32)[:, None]
    shift = jnp.sum(jnp.where(se[None, :] == experts, (pstart - start)[:, None], 0), axis=0)
    dest = (jnp.arange(n_assign, dtype=jnp.int32) + shift).astype(jnp.int32)
    n_blocks = -(-n_assign // MOE_BLOCK) + MOE_EXPERTS
    n_rows = n_blocks * MOE_BLOCK
    block_start = jnp.arange(n_blocks, dtype=jnp.int32) * MOE_BLOCK
    block_e = jnp.minimum(jnp.sum(pend[None, :] <= block_start[:, None], axis=1), MOE_EXPERTS - 1).astype(jnp.int32)
    off = (block_start - pstart[block_e])[:, None] + jnp.arange(MOE_BLOCK, dtype=jnp.int32)[None, :]
    valid = (off < counts[block_e][:, None]).reshape(n_rows)
    row_asg = order[jnp.where(valid, (start[block_e][:, None] + off).reshape(n_rows), 0)]
    row_tok = jnp.where(valid, row_asg // MOE_TOP_K, 0)
    row_w = jnp.where(valid, flat_w[row_asg], 0.0)
    xg = h[row_tok]
    n_used = (pend[-1:] // MOE_BLOCK).astype(jnp.int32)
    ys = _moe_experts(xg, block_e, n_used, w1, w3, w2, layer, jnp.broadcast_to(row_w[:, None], (n_rows, LANES)))
    _, slot = lax.sort((order, dest), num_keys=1)
    slot = slot.reshape(t, MOE_TOP_K)
    return ys[slot[:, 0]], ys[slot[:, 1]]


def _final_norm_kernel(x_ref, y0_ref, y1_ref, gate_ref, g_ref, o_ref):
    o_ref[0] = _rms_rows(_apply_pending(x_ref, (y0_ref, y1_ref, gate_ref))) * g_ref[...]


def _final_norm(x, pending, g):
    bsz, n, d = x.shape
    tm = min(MIX_ROWS, n)
    rows = pl.BlockSpec((1, tm, d), lambda b, i: (b, i, 0))
    pend_specs, pend_args = _pending_specs(pending, tm, n // tm, d)
    return pl.pallas_call(
        _final_norm_kernel,
        grid=(bsz, n // tm),
        in_specs=[rows, *pend_specs, _const_spec((1, d))],
        out_specs=rows,
        out_shape=jax.ShapeDtypeStruct((bsz, n, d), F32),
        compiler_params=pltpu.CompilerParams(dimension_semantics=("arbitrary", "arbitrary")),
        name="final_norm",
    )(x, *pend_args, g.reshape(1, d))


def kernel(x, c, ctx, c_ctx, ada_w, ada_b, norm1_g, norm2_g, w_in,
           s5_lambda_re, s5_lambda_im, s5_log_dt, s5_b_re, s5_b_im, s5_c_re, s5_c_im, s5_d, s5_glu_w,
           mla_q_norm_g, mla_kv_norm_g, mla_w_uq, mla_w_ukv,
           hy_short_w, hy_short_b, hy_f_w1, hy_f_b1, hy_f_w2, hy_f_b2, hy_f_w3, hy_f_b3, hy_f_freq, hy_bias,
           mix_norm_g, w_out, moe_w_group, moe_w_expert, moe_w1, moe_w3, moe_w2, final_g):
    bsz, n, d = x.shape
    n_ctx = ctx.shape[1]
    xl, xc = x, ctx
    act_l = jax.nn.silu(c)
    act_c = jax.nn.silu(c_ctx)
    rope_cos, rope_sin = _rope_slot_tables(n, n_ctx)
    pend_l = pend_c = None
    for i in range(DEPTH):
        ctx_out = i < DEPTH - 1
        mod_l = jnp.split((act_l @ ada_w[i] + ada_b[i])[:, None, :], 6, axis=-1)
        mod_c = jnp.split((act_c @ ada_w[i] + ada_b[i])[None, None, :], 6, axis=-1)
        mod_c = [jnp.broadcast_to(m, (bsz, 1, d)) for m in mod_c]
        w_ext, w_q = _proj_in_weights(w_in[i], mla_w_uq[i])
        proj = functools.partial(_proj_in, w_ext=w_ext, q_norm_g=mla_q_norm_g[i], w_q=w_q,
                                 kv_norm_g=mla_kv_norm_g[i], w_ukv=mla_w_ukv[i], cos=rope_cos, sin=rope_sin)
        seq, q_l, hz_l, xl = proj(xl, pend_l, norm1_g[i], mod_l[0], mod_l[1], row0=0, shared=None)
        (u_all, kv_all, kr_all), q_c, hz_c, xc = proj(xc, pend_c, norm1_g[i], mod_c[0], mod_c[1], row0=n, shared=seq)
        s5_all = _s5_mixer(u_all, n_ctx, s5_lambda_re[i], s5_lambda_im[i], s5_log_dt[i], s5_b_re[i], s5_b_im[i],
                           s5_c_re[i], s5_c_im[i], s5_d[i], s5_glu_w[i])
        mla_l = _attention(q_l, kv_all, kr_all, 0, n + n_ctx)
        filt = (hy_f_w1[i], hy_f_b1[i], hy_f_w2[i], hy_f_b2[i], hy_f_w3[i], hy_f_b3[i], hy_f_freq[i])
        hy_l = _hyena_mixer(hz_l, hy_short_w[i], hy_short_b[i], filt, hy_bias[i])
        w_route = jnp.pad(jnp.concatenate([moe_w_group[i], moe_w_expert[i]], axis=1),
                          ((0, 0), (0, LANES - MOE_GROUPS - MOE_EXPERTS)))
        n_tok = bsz * (n + n_ctx) if ctx_out else bsz * n
        xl, moe_in = _mix_out(s5_all, mla_l, hy_l, xl, mix_norm_g[i], w_out[i], mod_l[2], norm2_g[i],
                              mod_l[3], mod_l[4], w_route, row0=0, tok0=0, n_tok=n_tok, shared=None)
        if ctx_out:
            mla_c = _attention(q_c, kv_all, kr_all, n, n_ctx)
            hy_c = _hyena_mixer(hz_c, hy_short_w[i], hy_short_b[i], filt, hy_bias[i])
            xc, moe_in = _mix_out(s5_all, mla_c, hy_c, xc, mix_norm_g[i], w_out[i], mod_c[2], norm2_g[i],
                                  mod_c[3], mod_c[4], w_route, row0=n, tok0=bsz * n, n_tok=n_tok, shared=moe_in)
        y0, y1 = _hier_moe(*moe_in, moe_w1, moe_w3, moe_w2, i)
        pend_l, pend_c = (y0, y1, mod_l[5], 0), (y0, y1, mod_c[5], bsz * n)
    return _final_norm(xl, pend_l, final_g)
```

```python
import functools
import math

import jax
import jax.numpy as jnp
from jax import lax
from jax.experimental import pallas as pl
from jax.experimental.pallas import tpu as pltpu

D_MODEL = 1024
DEPTH = 4
GRID_W = 64
EPS = 1e-6

MIX_W = D_MODEL
S5_W = D_MODEL // 4
S5_GROUP = 16
S5_GROUPS = S5_W // S5_GROUP
S5_STATE = 64
S5_N = S5_GROUPS * S5_STATE
MLA_V = 64
MLA_W = D_MODEL // 2
MLA_HEADS = MLA_W // MLA_V
MLA_NOPE = 64
MLA_ROPE = 32
MLA_QK = MLA_NOPE + MLA_ROPE
MLA_Q_RANK = 384
MLA_KV_RANK = 256
MLA_SCALE = 1.0 / math.sqrt(MLA_NOPE + MLA_ROPE)
ROPE_BASE = 10000.0
HY_W = D_MODEL // 4
HY_ORDER = 2
HY_POS_EMB = 33
HY_FILTER_W = 64
HY_MIN_DECAY = math.log(1e-2) / 1.5
HY_MAX_DECAY = math.log(1e-2) / 0.3
MOE_GROUPS = 4
MOE_PER_GROUP = 8
MOE_EXPERTS = MOE_GROUPS * MOE_PER_GROUP
MOE_TOP_K = 2
MOE_HIDDEN = 512
MOE_BLOCK = 256

SUBLANES = 8
S5_CHUNK = 128
MM_ROWS = 256
ATTN_Q_BLOCK = 256
MIX_ROWS = 512
LANES = 128
ATTN_HEADS_PER_STEP = 4
VMEM_LIMIT = 48 * 1024 * 1024

F32 = jnp.float32
BF16 = jnp.bfloat16


def _rms_rows(v):
    return v * lax.rsqrt(jnp.mean(v * v, axis=-1, keepdims=True) + EPS)


def _s5_matrices(lam_re, lam_im, log_dt, b_re, b_im, c_re, c_im):
    g, p, h = S5_GROUPS, S5_STATE, S5_GROUP
    dt = jnp.exp(log_dt)[..., None]
    mag = jnp.exp(lam_re * dt)
    ar, ai = mag * jnp.cos(lam_im * dt), mag * jnp.sin(lam_im * dt)
    den = lam_re * lam_re + lam_im * lam_im
    fr = ((ar - 1.0) * lam_re + ai * lam_im) / den
    fi = (ai * lam_re - (ar - 1.0) * lam_im) / den
    bbr = fr[..., None] * b_re - fi[..., None] * b_im
    bbi = fr[..., None] * b_im + fi[..., None] * b_re
    eye = jnp.eye(g, dtype=F32)

    def block_in(m):
        return jnp.einsum('kgph,gj->kghjp', m, eye).reshape(2, g * h, g * p)

    def block_out(m):
        return jnp.einsum('kghp,gj->kgpjh', m, eye).reshape(2, g * p, g * h)

    bmat = jnp.concatenate([block_in(bbr), block_in(bbi)], axis=-1)
    cmat = jnp.concatenate([block_out(c_re), -block_out(c_im)], axis=1)
    lam = jnp.stack([ar[0], ai[0], ar[1], ai[1]]).reshape(4, 1, g * p)
    lam = jnp.broadcast_to(lam, (4, SUBLANES, g * p)).reshape(4 * SUBLANES, g * p)
    return bmat.astype(BF16), cmat.astype(BF16), lam


def _s5_kernel(uf_ref, ub_ref, bmat_ref, cmat_ref, lam_ref, yf_ref, yb_ref, buf_f, buf_b, state):
    rows = uf_ref.shape[0]
    steps = rows // SUBLANES
    n = S5_N
    s = SUBLANES

    @pl.when(pl.program_id(0) == 0)
    def _():
        state[...] = jnp.zeros_like(state)

    def drive(r, carry):
        r0 = pl.multiple_of(r * MM_ROWS, MM_ROWS)
        buf_f[pl.ds(r0, MM_ROWS), :] = jnp.dot(uf_ref[pl.ds(r0, MM_ROWS), :].astype(BF16), bmat_ref[0],
                                               preferred_element_type=F32)
        buf_b[pl.ds(r0, MM_ROWS), :] = jnp.dot(ub_ref[pl.ds(r0, MM_ROWS), :].astype(BF16), bmat_ref[1],
                                               preferred_element_type=F32)
        return carry

    lax.fori_loop(0, rows // MM_ROWS, drive, 0)

    def step(j, carry):
        fr, fi, br, bi = carry
        rf = pl.multiple_of(j * s, s)
        rb = pl.multiple_of((steps - 1 - j) * s, s)
        lfr, lfi = lam_ref[0:s, :], lam_ref[s:2 * s, :]
        lbr, lbi = lam_ref[2 * s:3 * s, :], lam_ref[3 * s:4 * s, :]
        nfr = lfr * fr - lfi * fi + buf_f[pl.ds(rf, s), 0:n]
        nfi = lfr * fi + lfi * fr + buf_f[pl.ds(rf, s), n:2 * n]
        nbr = lbr * br - lbi * bi + buf_b[pl.ds(rb, s), 0:n]
        nbi = lbr * bi + lbi * br + buf_b[pl.ds(rb, s), n:2 * n]
        buf_f[pl.ds(rf, s), 0:n] = nfr
        buf_f[pl.ds(rf, s), n:2 * n] = nfi
        buf_b[pl.ds(rb, s), 0:n] = nbr
        buf_b[pl.ds(rb, s), n:2 * n] = nbi
        return nfr, nfi, nbr, nbi

    init = (state[0:s, :], state[s:2 * s, :], state[2 * s:3 * s, :], state[3 * s:4 * s, :])
    fr, fi, br, bi = lax.fori_loop(0, steps, step, init, unroll=2)
    state[0:s, :] = fr
    state[s:2 * s, :] = fi
    state[2 * s:3 * s, :] = br
    state[3 * s:4 * s, :] = bi

    def readout(r, carry):
        r0 = pl.multiple_of(r * MM_ROWS, MM_ROWS)
        yf_ref[pl.ds(r0, MM_ROWS), :] = jnp.dot(buf_f[pl.ds(r0, MM_ROWS), :].astype(BF16), cmat_ref[0],
                                                preferred_element_type=F32)
        yb_ref[pl.ds(r0, MM_ROWS), :] = jnp.dot(buf_b[pl.ds(r0, MM_ROWS), :].astype(BF16), cmat_ref[1],
                                                preferred_element_type=F32)
        return carry

    lax.fori_loop(0, rows // MM_ROWS, readout, 0)


def _s5_scan(u_tm, bmat, cmat, lam, n_ctx_chunks, chunk):
    rows_all, w = u_tm.shape
    rows = chunk * SUBLANES
    n_chunks = rows_all // rows
    n_lat_chunks = n_chunks - n_ctx_chunks
    n2 = 2 * S5_N

    def fwd_chunk(i):
        return (i + n_lat_chunks) % n_chunks

    def bwd_chunk(i):
        return n_chunks - 1 - i

    return pl.pallas_call(
        _s5_kernel,
        grid=(n_chunks,),
        in_specs=[
            pl.BlockSpec((rows, w), lambda i: (fwd_chunk(i), 0)),
            pl.BlockSpec((rows, w), lambda i: (bwd_chunk(i), 0)),
            pl.BlockSpec((2, w, n2), lambda i: (0, 0, 0)),
            pl.BlockSpec((2, n2, w), lambda i: (0, 0, 0)),
            pl.BlockSpec((4 * SUBLANES, S5_N), lambda i: (0, 0)),
        ],
        out_specs=[
            pl.BlockSpec((rows, w), lambda i: (fwd_chunk(i), 0)),
            pl.BlockSpec((rows, w), lambda i: (bwd_chunk(i), 0)),
        ],
        out_shape=[jax.ShapeDtypeStruct((rows_all, w), F32)] * 2,
        scratch_shapes=[
            pltpu.VMEM((rows, n2), F32),
            pltpu.VMEM((rows, n2), F32),
            pltpu.VMEM((4 * SUBLANES, S5_N), F32),
        ],
        compiler_params=pltpu.CompilerParams(dimension_semantics=("arbitrary",), vmem_limit_bytes=VMEM_LIMIT),
        name="s5_scan",
    )(u_tm, u_tm, bmat, cmat, lam)


def _s5_mixer(u_tm, n_ctx, lam_re, lam_im, log_dt, b_re, b_im, c_re, c_im, d, glu_w):
    n_all, bw = u_tm.shape
    bsz, w = bw // S5_W, S5_W
    assert bsz == SUBLANES and n_all % S5_CHUNK == 0 and n_ctx % S5_CHUNK == 0
    bmat, cmat, lam = _s5_matrices(lam_re, lam_im, log_dt, b_re, b_im, c_re, c_im)
    u_rows = u_tm.reshape(n_all * bsz, w)
    y_f, y_b = _s5_scan(u_rows, bmat, cmat, lam, n_ctx // S5_CHUNK, S5_CHUNK)
    y = y_f + y_b + d * u_rows
    y = jax.nn.gelu(y)
    y = y * jax.nn.sigmoid(y @ glu_w)
    return y.reshape(n_all, bw)


def _axial_rope_tables(n_tokens):
    rows = n_tokens // GRID_W
    row = jnp.repeat(jnp.arange(rows), GRID_W).astype(F32)
    col = jnp.tile(jnp.arange(GRID_W), rows).astype(F32)
    half = MLA_ROPE // 2
    inv = ROPE_BASE ** (-jnp.arange(0, half, 2, dtype=F32) / half)
    ang_r = row[:, None] * inv
    ang_c = col[:, None] * inv
    cos = jnp.concatenate([jnp.cos(ang_r)] * 2 + [jnp.cos(ang_c)] * 2, axis=-1)
    sin = jnp.concatenate([jnp.sin(ang_r)] * 2 + [jnp.sin(ang_c)] * 2, axis=-1)
    return cos, sin


def _rope_partner(w):
    q = MLA_ROPE // 4
    parts = []
    for g in range(2):
        x1, x2 = w[..., 2 * g * q:(2 * g + 1) * q], w[..., (2 * g + 1) * q:(2 * g + 2) * q]
        parts += [-x2, x1]
    return jnp.concatenate(parts, axis=-1)


def _attn_kernel(q_ref, kv_ref, kr_ref, o_ref, kcat, vaug):
    hs, nk = kcat.shape[0], kcat.shape[1]
    dn = MLA_NOPE

    @pl.when(pl.program_id(2) == 0)
    def _():
        lane = lax.broadcasted_iota(jnp.int32, (nk, LANES), 1)
        ones_col = jnp.where(lane == 0, 1.0, 0.0).astype(BF16)
        for h in range(hs):
            kvh = kv_ref[0, :, h * LANES:(h + 1) * LANES]
            kcat[h] = jnp.where(lane < dn, kvh, kr_ref[0])
            vaug[h] = jnp.where(lane >= dn, kvh, ones_col)

    outs = []
    for h in range(hs):
        s = lax.dot_general(q_ref[0, :, h * LANES:(h + 1) * LANES], kcat[h], (((1,), (1,)), ((), ())),
                            preferred_element_type=F32)
        p = jnp.exp(s - jnp.max(s, axis=-1, keepdims=True))
        o = jnp.dot(p.astype(BF16), vaug[h], preferred_element_type=F32)
        outs.append(o[:, dn:] / o[:, 0:1])
    o_ref[0] = jnp.concatenate(outs, axis=-1)


def _attention(q, kv, kr, key_row0, nk):
    bsz, nq, _ = q.shape
    heads, dv = MLA_HEADS, MLA_V
    tq = min(ATTN_Q_BLOCK, nq)
    hs = ATTN_HEADS_PER_STEP
    kb = key_row0 // nk
    return pl.pallas_call(
        _attn_kernel,
        grid=(bsz, heads // hs, nq // tq),
        in_specs=[
            pl.BlockSpec((1, tq, hs * LANES), lambda b, h, i: (b, i, h)),
            pl.BlockSpec((1, nk, hs * LANES), lambda b, h, i: (b, kb, h), pipeline_mode=pl.Buffered(1)),
            pl.BlockSpec((1, nk, LANES), lambda b, h, i: (b, kb, 0), pipeline_mode=pl.Buffered(1)),
        ],
        out_specs=pl.BlockSpec((1, tq, hs * dv), lambda b, h, i: (b, i, h)),
        out_shape=jax.ShapeDtypeStruct((bsz, nq, heads * dv), F32),
        scratch_shapes=[pltpu.VMEM((hs, nk, LANES), BF16), pltpu.VMEM((hs, nk, LANES), BF16)],
        compiler_params=pltpu.CompilerParams(
            dimension_semantics=("arbitrary", "arbitrary", "arbitrary"), vmem_limit_bytes=VMEM_LIMIT),
        name="mla_attention",
    )(q, kv, kr)


PIN_U = 0
PIN_CQ = PIN_U + S5_W
PIN_CKV = PIN_CQ + MLA_Q_RANK
PIN_HZ = PIN_CKV + MLA_KV_RANK
PIN_KR = PIN_HZ + 3 * HY_W
PIN_KRP = PIN_KR + LANES
PIN_COLS = PIN_KRP + LANES


def _proj_in_weights(w_in, w_uq):
    o1 = S5_W
    o2 = o1 + MLA_Q_RANK
    o3 = o2 + MLA_KV_RANK
    o4 = o3 + MLA_ROPE
    pad = LANES - MLA_QK
    w_kr = w_in[:, o3:o4]
    in_slot = lambda w: jnp.pad(w, ((0, 0), (MLA_NOPE, pad)))
    w_ext = jnp.concatenate([w_in[:, :o3], w_in[:, o4:], in_slot(w_kr), in_slot(_rope_partner(w_kr))], axis=1)
    w_h = w_uq.reshape(MLA_Q_RANK, MLA_HEADS, MLA_QK)
    w_rot = jnp.concatenate([jnp.zeros_like(w_h[..., :MLA_NOPE]), _rope_partner(w_h[..., MLA_NOPE:])], axis=-1)
    w_q = jnp.concatenate([jnp.pad(w, ((0, 0), (0, 0), (0, pad))).reshape(MLA_Q_RANK, MLA_HEADS * LANES)
                           for w in (w_h, w_rot)], axis=-1)
    return w_ext.astype(BF16), w_q.astype(BF16)


def _rope_slot_tables(n, n_ctx):
    cos, sin = _axial_rope_tables(n)
    pad = LANES - MLA_QK
    cos = jnp.concatenate([jnp.ones((n, MLA_NOPE), F32), cos, jnp.ones((n, pad), F32)], axis=-1)
    sin = jnp.concatenate([jnp.zeros((n, MLA_NOPE), F32), sin, jnp.zeros((n, pad), F32)], axis=-1)
    return (jnp.concatenate([cos, jnp.ones((n_ctx, LANES), F32)], axis=0),
            jnp.concatenate([sin, jnp.zeros((n_ctx, LANES), F32)], axis=0))


def _apply_pending(x_ref, pending_refs):
    y0_ref, y1_ref, gate_ref = pending_refs
    return x_ref[0] + gate_ref[0] * (y0_ref[...] + y1_ref[...])


def _pending_specs(pending, tm, blocks_per_batch, d):
    if pending is None:
        return [], []
    y0, y1, gate, row0 = pending
    blk0 = row0 // tm
    rows = pl.BlockSpec((tm, d), lambda b, i: (blk0 + b * blocks_per_batch + i, 0))
    return [rows, rows, pl.BlockSpec((1, 1, d), lambda b, i: (b, 0, 0))], [y0, y1, gate]


def _proj_in_kernel(*refs, has_pending, n_shared):
    if has_pending:
        x_ref, *pending_refs = refs[:4]
        refs = refs[4:]
    else:
        x_ref, refs = refs[0], refs[1:]
    g_ref, shift_ref, scale_ref, w_ref, qg_ref, wq_ref, kvg_ref, wkv_ref, cos_ref, sin_ref = refs[:10]
    u_ref, q_ref, kv_ref, kr_ref, hz_ref, *xo_ref = refs[10 + n_shared:]
    if has_pending:
        x = _apply_pending(x_ref, pending_refs)
        xo_ref[0][0] = x
    else:
        x = x_ref[0]
    h = ((_rms_rows(x) * g_ref[...]) * (1.0 + scale_ref[0]) + shift_ref[0]).astype(BF16)
    p = jnp.dot(h, w_ref[...], preferred_element_type=F32)
    u_ref[...] = p[:, PIN_U:PIN_CQ]
    hz_ref[0] = p[:, PIN_HZ:PIN_KR]
    cos, sin = cos_ref[...], sin_ref[...]
    kr_ref[0] = (p[:, PIN_KR:PIN_KRP] * cos + p[:, PIN_KRP:PIN_COLS] * sin).astype(BF16)
    cq = (_rms_rows(p[:, PIN_CQ:PIN_CKV]) * qg_ref[...]).astype(BF16)
    q2 = jnp.dot(cq, wq_ref[...], preferred_element_type=F32)
    hw = MLA_HEADS * LANES
    for hd in range(MLA_HEADS):
        a, b = q2[:, hd * LANES:(hd + 1) * LANES], q2[:, hw + hd * LANES:hw + (hd + 1) * LANES]
        q_ref[0, :, hd * LANES:(hd + 1) * LANES] = ((a * cos + b * sin) * MLA_SCALE).astype(BF16)
    ckv = (_rms_rows(p[:, PIN_CKV:PIN_HZ]) * kvg_ref[...]).astype(BF16)
    kv_ref[0] = jnp.dot(ckv, wkv_ref[...], preferred_element_type=F32).astype(BF16)


def _proj_in(x, pending, norm_g, shift, scale, w_ext, q_norm_g, w_q, kv_norm_g, w_ukv, cos, sin, row0, shared):
    bsz, n, d = x.shape
    n_all = cos.shape[0]
    tm = min(MIX_ROWS, n)
    blk0 = row0 // tm
    hw = MLA_HEADS * LANES

    def rows(width, first=0):
        return pl.BlockSpec((1, tm, width), lambda b, i: (b, first + i, 0))

    per_batch = pl.BlockSpec((1, 1, d), lambda b, i: (b, 0, 0))
    table = pl.BlockSpec((tm, LANES), lambda b, i: (blk0 + i, 0))
    pend_specs, pend_args = _pending_specs(pending, tm, n // tm, d)
    has_pending = pending is not None
    out_specs = [pl.BlockSpec((tm, S5_W), lambda b, i: (blk0 + i, b)), rows(hw), rows(hw, blk0), rows(LANES, blk0),
                 rows(3 * HY_W)]
    out_shape = [jax.ShapeDtypeStruct((n_all, bsz * S5_W), F32), jax.ShapeDtypeStruct((bsz, n, hw), BF16),
                 jax.ShapeDtypeStruct((bsz, n_all, hw), BF16), jax.ShapeDtypeStruct((bsz, n_all, LANES), BF16),
                 jax.ShapeDtypeStruct((bsz, n, 3 * HY_W), F32)]
    if has_pending:
        out_specs.append(rows(d))
        out_shape.append(jax.ShapeDtypeStruct((bsz, n, d), F32))
    n_in = 1 + len(pend_args) + 10
    shared = list(shared or [jnp.zeros(out_shape[k].shape, out_shape[k].dtype) for k in (0, 2, 3)])
    aliases = {n_in + k: out for k, out in enumerate((0, 2, 3))}
    outs = pl.pallas_call(
        functools.partial(_proj_in_kernel, has_pending=has_pending, n_shared=len(shared)),
        grid=(bsz, n // tm),
        in_specs=[rows(d), *pend_specs, _const_spec((1, d)), per_batch, per_batch, _const_spec((d, PIN_COLS)),
                  _const_spec((1, MLA_Q_RANK)), _const_spec((MLA_Q_RANK, 2 * hw)),
                  _const_spec((1, MLA_KV_RANK)), _const_spec((MLA_KV_RANK, hw)), table, table,
                  *[pl.BlockSpec(memory_space=pl.ANY)] * len(shared)],
        out_specs=out_specs,
        out_shape=out_shape,
        input_output_aliases=aliases,
        compiler_params=pltpu.CompilerParams(dimension_semantics=("arbitrary", "arbitrary"),
                                             vmem_limit_bytes=VMEM_LIMIT),
        name="proj_in",
    )(x, *pend_args, norm_g.reshape(1, d), shift, scale, w_ext, q_norm_g.reshape(1, -1), w_q,
      kv_norm_g.reshape(1, -1), w_ukv.astype(BF16), cos, sin, *shared)
    return (outs[0], outs[2], outs[3]), outs[1], outs[4], (outs[5] if has_pending else x)


def _short_conv(z, w, b):
    n = z.shape[1]
    zp = jnp.pad(z, ((0, 0), (1, 1), (0, 0)))
    return zp[:, :n] * w[0] + zp[:, 1:n + 1] * w[1] + zp[:, 2:] * w[2] + b


def _hyena_filter_taps(n, w1, b1, w2, b2, w3, b3, freq):
    t = jnp.linspace(0.0, 1.0, n, dtype=F32)[:, None]
    bands = (HY_POS_EMB - 1) // 2
    f = jnp.linspace(1e-4, bands - 1, bands, dtype=F32)
    ang = (2.0 * math.pi * jnp.arange(n, dtype=F32) / n)[:, None] * f
    z = jnp.concatenate([t, jnp.cos(ang), -jnp.sin(ang)], axis=-1)
    h = jnp.sin(freq[0] * (z @ w1 + b1))
    h = jnp.sin(freq[1] * (h @ w2 + b2))
    h = (h @ w3 + b3).astype(F32)
    deltas = jnp.abs(jnp.linspace(HY_MIN_DECAY, HY_MAX_DECAY, HY_W, dtype=F32))
    h = h * jnp.exp(-t * jnp.tile(deltas, 2 * HY_ORDER))
    half = HY_ORDER * HY_W
    fwd, bwd = h[:, :half], h[:, half:]
    return jnp.concatenate([fwd, jnp.zeros((1, half), F32), bwd[:0:-1]], axis=0)


def _dense_conv_kernel(x_ref, k_ref, ff_ref, g_ref, o_ref):
    n = x_ref.shape[1]
    k = k_ref[...].astype(BF16)
    kr = jnp.dot(ff_ref[0], k, preferred_element_type=F32)
    ki = jnp.dot(ff_ref[1], k, preferred_element_type=F32)
    x = x_ref[0].astype(BF16)
    yr = jnp.dot(ff_ref[0, :, :n], x, preferred_element_type=F32)
    yi = jnp.dot(ff_ref[1, :, :n], x, preferred_element_type=F32)
    zr, zi = (yr * kr - yi * ki).astype(BF16), (yr * ki + yi * kr).astype(BF16)
    o_ref[0] = jnp.dot(g_ref[0], zr, preferred_element_type=F32) + jnp.dot(g_ref[1], zi, preferred_element_type=F32)


def _dense_long_conv(y, k_time):
    bsz, n, c = y.shape
    nfft = 2 * n
    idx = jnp.arange(nfft, dtype=jnp.int32)
    ang = (2.0 * math.pi / nfft) * ((idx[:, None] * idx[None, :]) % nfft).astype(F32)
    ff = jnp.stack([jnp.cos(ang), -jnp.sin(ang)]).astype(BF16)
    g = (jnp.stack([jnp.cos(ang), -jnp.sin(ang)])[:, :n, :] / nfft).astype(BF16)
    return pl.pallas_call(
        _dense_conv_kernel,
        grid=(bsz,),
        in_specs=[pl.BlockSpec((1, n, c), lambda b: (b, 0, 0)), _const_spec((nfft, c)),
                  _const_spec((2, nfft, nfft)), _const_spec((2, n, nfft))],
        out_specs=pl.BlockSpec((1, n, c), lambda b: (b, 0, 0)),
        out_shape=jax.ShapeDtypeStruct((bsz, n, c), F32),
        compiler_params=pltpu.CompilerParams(dimension_semantics=("arbitrary",)),
        name="hyena_dense_conv",
    )(y, k_time, ff, g)


def _hyena_mixer(z, short_w, short_b, filt, bias):
    n = z.shape[1]
    z = _short_conv(z, short_w, short_b)
    k_time = _hyena_filter_taps(n, *filt)
    if (2 * n) % HY_N2 == 0 and (2 * n) // HY_N2 >= 2 * SUBLANES:
        tables = _dft_tables(2 * n)
        kr, ki = _hyena_spectrum(k_time, tables)
        rows = n // HY_N2
        z = _to_strided(z, rows)
        y = z[..., :HY_W]
        for o in range(HY_ORDER):
            cols = slice(o * HY_W, (o + 1) * HY_W)
            gate = z[..., (o + 1) * HY_W:(o + 2) * HY_W]
            y = gate * (_hyena_long_conv(y, kr[:, cols], ki[:, cols], tables) + y * bias[o])
        return _from_strided(y, rows)
    y = z[..., :HY_W]
    for o in range(HY_ORDER):
        gate = z[..., (o + 1) * HY_W:(o + 2) * HY_W]
        y = gate * (_dense_long_conv(y, k_time[:, o * HY_W:(o + 1) * HY_W]) + y * bias[o])
    return y


HY_N2 = 128
HY_PITCH = 136
HY_UNROLL = 8


def _dft_tables(nfft):
    n1 = nfft // HY_N2
    half = n1 // 2
    k1 = jnp.arange(n1, dtype=jnp.int32)
    n2 = jnp.arange(HY_N2, dtype=jnp.int32)
    t = HY_N2 * k1[None, None, :] + n2[:, None, None]
    ang = (2.0 * math.pi / nfft) * ((k1[None, :, None] * t) % nfft).astype(F32)
    gr, gi = jnp.cos(ang), -jnp.sin(ang)
    g_cplx = jnp.concatenate([jnp.concatenate([gr[..., :half], -gi[..., :half]], -1),
                              jnp.concatenate([gi[..., :half], gr[..., :half]], -1)], axis=1)
    g_real = jnp.concatenate([gr, gi], axis=1)
    hr = jnp.cos(ang).swapaxes(1, 2)[:, :half] / nfft
    hi = jnp.sin(ang).swapaxes(1, 2)[:, :half] / nfft
    g_inv = jnp.concatenate([jnp.concatenate([hr, -hi], -1), jnp.concatenate([hi, hr], -1)], axis=1)
    a2 = (2.0 * math.pi / HY_N2) * ((n2[:, None] * n2[None, :]) % HY_N2).astype(F32)
    fr, fi = jnp.cos(a2), -jnp.sin(a2)
    f2 = jnp.concatenate([jnp.concatenate([fr, -fi], -1), jnp.concatenate([fi, fr], -1)], axis=0)
    f2_inv = jnp.concatenate([jnp.concatenate([fr, fi], -1), jnp.concatenate([-fi, fr], -1)], axis=0)
    return dict(n1=n1, g_cplx=g_cplx.astype(BF16), g_real=g_real.astype(BF16), g_inv=g_inv.astype(BF16),
                f2=f2.astype(BF16), f2_inv=f2_inv.astype(BF16))


def _dft_stage1(load_rows, g_ref, s_re, s_im, n1):
    def body(n2, carry):
        a = jnp.dot(g_ref[n2], load_rows(n2).astype(BF16), preferred_element_type=F32)
        s_re[pl.ds(n2, n1, stride=HY_PITCH), :] = a[:n1]
        s_im[pl.ds(n2, n1, stride=HY_PITCH), :] = a[n1:]
        return carry

    lax.fori_loop(0, HY_N2, body, 0, unroll=HY_UNROLL)


def _dft_stage2(f2_ref, s_re, s_im, k1):
    r0 = pl.multiple_of(k1 * HY_PITCH, SUBLANES)
    sl = jnp.concatenate([s_re[pl.ds(r0, HY_N2), :], s_im[pl.ds(r0, HY_N2), :]], axis=0).astype(BF16)
    x = jnp.dot(f2_ref[...], sl, preferred_element_type=F32)
    return r0, x[:HY_N2], x[HY_N2:]


def _hyena_spectrum_kernel(x_ref, g_ref, f2_ref, kr_ref, ki_ref, s_re, s_im, *, n1):
    _dft_stage1(lambda n2: x_ref[pl.ds(n2, n1, stride=HY_PITCH), :], g_ref, s_re, s_im, n1)

    def body(k1, carry):
        _, xr, xi = _dft_stage2(f2_ref, s_re, s_im, k1)
        q0 = pl.multiple_of(k1 * HY_N2, HY_N2)
        kr_ref[pl.ds(q0, HY_N2), :] = xr
        ki_ref[pl.ds(q0, HY_N2), :] = xi
        return carry

    lax.fori_loop(0, n1, body, 0)


def _hyena_conv_kernel(x_ref, g_ref, f2_ref, f2i_ref, gi_ref, kr_ref, ki_ref, y_ref, s_re, s_im, *, n1):
    half = n1 // 2

    def load_rows(n2):
        return jnp.concatenate([x_ref[0, pl.ds(n2, half, stride=HY_PITCH), :],
                                x_ref[1, pl.ds(n2, half, stride=HY_PITCH), :]], axis=0)

    _dft_stage1(load_rows, g_ref, s_re, s_im, n1)

    def spectrum_product(k1, carry):
        r0, xr, xi = _dft_stage2(f2_ref, s_re, s_im, k1)
        q0 = pl.multiple_of(k1 * HY_N2, HY_N2)
        kr, ki = kr_ref[pl.ds(q0, HY_N2), :], ki_ref[pl.ds(q0, HY_N2), :]
        y = jnp.concatenate([xr * kr - xi * ki, xr * ki + xi * kr], axis=0).astype(BF16)
        b = jnp.dot(f2i_ref[...], y, preferred_element_type=F32)
        s_re[pl.ds(r0, HY_N2), :] = b[:HY_N2]
        s_im[pl.ds(r0, HY_N2), :] = b[HY_N2:]
        return carry

    lax.fori_loop(0, n1, spectrum_product, 0, unroll=HY_UNROLL // 2)

    y_ref[...] = jnp.zeros_like(y_ref)

    def inverse_stage1(n2, carry):
        bs = jnp.concatenate([s_re[pl.ds(n2, n1, stride=HY_PITCH), :],
                              s_im[pl.ds(n2, n1, stride=HY_PITCH), :]], axis=0).astype(BF16)
        y = jnp.dot(gi_ref[n2], bs, preferred_element_type=F32)
        y_ref[0, pl.ds(n2, half, stride=HY_PITCH), :] = y[:half]
        y_ref[1, pl.ds(n2, half, stride=HY_PITCH), :] = y[half:]
        return carry

    lax.fori_loop(0, HY_N2, inverse_stage1, 0, unroll=HY_UNROLL)


def _to_strided(a, rows):
    lead, c = a.shape[:-2], a.shape[-1]
    a = a.reshape(lead + (rows, HY_N2, c))
    a = jnp.pad(a, [(0, 0)] * len(lead) + [(0, 0), (0, HY_PITCH - HY_N2), (0, 0)])
    return a.reshape(lead + (rows * HY_PITCH, c))


def _from_strided(a, rows):
    lead, c = a.shape[:-2], a.shape[-1]
    return a.reshape(lead + (rows, HY_PITCH, c))[..., :HY_N2, :].reshape(lead + (rows * HY_N2, c))


def _const_spec(shape):
    return pl.BlockSpec(shape, lambda *_: (0,) * len(shape), pipeline_mode=pl.Buffered(1))


def _hyena_spectrum(k_time, tables):
    nfft, c = k_time.shape
    n1 = tables["n1"]
    spec_out = pl.BlockSpec((nfft, LANES), lambda j: (0, j))
    return pl.pallas_call(
        functools.partial(_hyena_spectrum_kernel, n1=n1),
        grid=(c // LANES,),
        in_specs=[pl.BlockSpec((n1 * HY_PITCH, LANES), lambda j: (0, j)),
                  _const_spec((HY_N2, 2 * n1, n1)), _const_spec((2 * HY_N2, 2 * HY_N2))],
        out_specs=[spec_out, spec_out],
        out_shape=[jax.ShapeDtypeStruct((nfft, c), F32)] * 2,
        scratch_shapes=[pltpu.VMEM((n1 * HY_PITCH, LANES), F32)] * 2,
        compiler_params=pltpu.CompilerParams(dimension_semantics=("arbitrary",), vmem_limit_bytes=VMEM_LIMIT),
        name="hyena_spectrum",
    )(_to_strided(k_time, n1), tables["g_real"], tables["f2"])


def _hyena_long_conv(y, kr, ki, tables):
    bsz, r, c = y.shape
    n1 = tables["n1"]
    pairs = bsz // 2
    spec_x = pl.BlockSpec((None, 2, r, LANES), lambda j, p: (p, 0, 0, j))
    spec_k = pl.BlockSpec((n1 * HY_N2, LANES), lambda j, p: (0, j), pipeline_mode=pl.Buffered(1))
    out = pl.pallas_call(
        functools.partial(_hyena_conv_kernel, n1=n1),
        grid=(c // LANES, pairs),
        in_specs=[spec_x,
                  _const_spec((HY_N2, 2 * n1, n1)), _const_spec((2 * HY_N2, 2 * HY_N2)),
                  _const_spec((2 * HY_N2, 2 * HY_N2)), _const_spec((HY_N2, n1, 2 * n1)),
                  spec_k, spec_k],
        out_specs=spec_x,
        out_shape=jax.ShapeDtypeStruct((pairs, 2, r, c), F32),
        scratch_shapes=[pltpu.VMEM((n1 * HY_PITCH, LANES), F32)] * 2,
        compiler_params=pltpu.CompilerParams(dimension_semantics=("arbitrary", "arbitrary"),
                                             vmem_limit_bytes=VMEM_LIMIT),
        name="hyena_conv",
    )(y.reshape(pairs, 2, r, c), tables["g_cplx"], tables["f2"], tables["f2_inv"], tables["g_inv"], kr, ki)
    return out.reshape(bsz, r, c)


HIGH_HALF = -(1 << 16)


def _pack_bf16_pairs(f):
    half = f.shape[1] // 2
    bits = lax.bitcast_convert_type(f.astype(F32), jnp.int32)
    return lax.shift_right_logical(bits[:, :half], 16) | (bits[:, half:] & HIGH_HALF)


def _unpack_bf16_pairs(w):
    lo = lax.bitcast_convert_type(lax.shift_left(w, 16), F32)
    hi = lax.bitcast_convert_type(w & HIGH_HALF, F32)
    return jnp.concatenate([lo, hi], axis=-1).astype(BF16)


def _mix_out_kernel(s5_ref, mla_ref, hy_ref, x_ref, g_ref, w_ref, gate_ref, n2g_ref, shift_ref, scale_ref, wr_ref,
                    *rest):
    xo_ref, f_ref, lg_ref = rest[-3:]
    c1, c2 = S5_W, S5_W + MLA_W
    y = jnp.dot((_rms_rows(s5_ref[...]) * g_ref[:, :c1]).astype(BF16), w_ref[:c1, :], preferred_element_type=F32)
    y += jnp.dot((_rms_rows(mla_ref[0]) * g_ref[:, c1:c2]).astype(BF16), w_ref[c1:c2, :], preferred_element_type=F32)
    y += jnp.dot((_rms_rows(hy_ref[0]) * g_ref[:, c2:]).astype(BF16), w_ref[c2:, :], preferred_element_type=F32)
    x_new = x_ref[0] + gate_ref[0] * y
    xo_ref[0] = x_new
    f = ((_rms_rows(x_new) * n2g_ref[...]) * (1.0 + scale_ref[0]) + shift_ref[0]).astype(BF16)
    f_ref[...] = _pack_bf16_pairs(f)
    lg_ref[...] = jnp.dot(f, wr_ref[...], preferred_element_type=F32)


def _mix_out(y_s5, y_mla, y_hy, x, mix_g, w_out, gate, norm2_g, shift, scale, w_route, row0, tok0, n_tok, shared):
    bsz, n, d = x.shape
    tm = min(MIX_ROWS, n)
    s5_blk0, tok_blk0, per_batch_blocks = row0 // tm, tok0 // tm, n // tm

    def rows(width):
        return pl.BlockSpec((1, tm, width), lambda b, i: (b, i, 0))

    def tokens(width):
        return pl.BlockSpec((tm, width), lambda b, i: (tok_blk0 + b * per_batch_blocks + i, 0))

    per_batch = pl.BlockSpec((1, 1, d), lambda b, i: (b, 0, 0))
    if shared is None:
        shared = () if n_tok == bsz * n else (jnp.zeros((n_tok, d // 2), jnp.int32), jnp.zeros((n_tok, LANES), F32))
    shared = list(shared)
    outs = pl.pallas_call(
        _mix_out_kernel,
        grid=(bsz, n // tm),
        in_specs=[pl.BlockSpec((tm, S5_W), lambda b, i: (s5_blk0 + i, b)), rows(MLA_W), rows(HY_W), rows(d),
                  _const_spec((1, d)), _const_spec((d, d)),
                  per_batch, _const_spec((1, d)), per_batch, per_batch, _const_spec((d, LANES)),
                  *[pl.BlockSpec(memory_space=pl.ANY)] * len(shared)],
        out_specs=[rows(d), tokens(d // 2), tokens(LANES)],
        out_shape=[jax.ShapeDtypeStruct((bsz, n, d), F32), jax.ShapeDtypeStruct((n_tok, d // 2), jnp.int32),
                   jax.ShapeDtypeStruct((n_tok, LANES), F32)],
        input_output_aliases={11 + k: 1 + k for k in range(len(shared))},
        compiler_params=pltpu.CompilerParams(dimension_semantics=("arbitrary", "arbitrary"),
                                             vmem_limit_bytes=VMEM_LIMIT),
        name="mix_out",
    )(y_s5, y_mla, y_hy, x, mix_g.reshape(1, d), w_out.astype(BF16), gate, norm2_g.reshape(1, d), shift, scale,
      w_route.astype(BF16), *shared)
    return outs[0], (outs[1], outs[2])


def _moe_kernel(be_ref, used_ref, x_ref, w1_ref, w3_ref, w2_ref, rw_ref, o_ref, w13_s, w2_s):
    i = pl.program_id(0)
    hid = w2_s.shape[0]

    @pl.when(jnp.logical_or(i == 0, be_ref[i] != be_ref[jnp.maximum(i - 1, 0)]))
    def _():
        w13_s[:, :hid] = w1_ref[0, 0].astype(BF16)
        w13_s[:, hid:] = w3_ref[0, 0].astype(BF16)
        w2_s[...] = w2_ref[0, 0].astype(BF16)

    @pl.when(i < used_ref[0])
    def _():
        ab = jnp.dot(_unpack_bf16_pairs(x_ref[...]), w13_s[...], preferred_element_type=F32)
        a, b = ab[:, :hid], ab[:, hid:]
        h = (a * jax.nn.sigmoid(a)) * b
        y = jnp.dot(h.astype(BF16), w2_s[...], preferred_element_type=F32)
        rw = rw_ref[...]
        for c0 in range(0, y.shape[1], LANES):
            o_ref[:, c0:c0 + LANES] = y[:, c0:c0 + LANES] * rw

    @pl.when(i >= used_ref[0])
    def _():
        o_ref[...] = jnp.zeros_like(o_ref)


def _moe_experts(xg, block_e, n_used, w1, w3, w2, layer, row_w):
    n_rows = xg.shape[0]
    d, hid = w1.shape[2], w1.shape[3]
    n_blocks = n_rows // MOE_BLOCK
    return pl.pallas_call(
        _moe_kernel,
        grid_spec=pltpu.PrefetchScalarGridSpec(
            num_scalar_prefetch=2,
            grid=(n_blocks,),
            in_specs=[
                pl.BlockSpec((MOE_BLOCK, d // 2), lambda i, be, nu: (i, 0)),
                pl.BlockSpec((1, 1, d, hid), lambda i, be, nu: (layer, be[i], 0, 0)),
                pl.BlockSpec((1, 1, d, hid), lambda i, be, nu: (layer, be[i], 0, 0)),
                pl.BlockSpec((1, 1, hid, d), lambda i, be, nu: (layer, be[i], 0, 0)),
                pl.BlockSpec((MOE_BLOCK, LANES), lambda i, be, nu: (i, 0)),
            ],
            out_specs=pl.BlockSpec((MOE_BLOCK, d), lambda i, be, nu: (i, 0)),
            scratch_shapes=[pltpu.VMEM((d, 2 * hid), BF16), pltpu.VMEM((hid, d), BF16)],
        ),
        out_shape=jax.ShapeDtypeStruct((n_rows, d), F32),
        compiler_params=pltpu.CompilerParams(dimension_semantics=("arbitrary",), vmem_limit_bytes=VMEM_LIMIT),
        name="moe_experts",
    )(block_e, n_used, xg, w1, w3, w2, row_w)


def _hier_moe(h, logits, w1, w3, w2, layer):
    t = h.shape[0]
    g_prob = jax.nn.softmax(logits[:, :MOE_GROUPS], axis=-1)
    g_idx = jnp.argmax(g_prob, axis=-1).astype(jnp.int32)
    g_w = jnp.max(g_prob, axis=-1, keepdims=True)
    e_logits = logits[:, MOE_GROUPS:MOE_GROUPS + MOE_EXPERTS].reshape(t, MOE_GROUPS, MOE_PER_GROUP)
    in_group = jnp.take_along_axis(e_logits, g_idx[:, None, None], axis=1)[:, 0]
    i1 = jnp.argmax(in_group, axis=-1).astype(jnp.int32)
    rest = jnp.where(jnp.arange(MOE_PER_GROUP, dtype=jnp.int32)[None, :] == i1[:, None], -jnp.inf, in_group)
    i2 = jnp.argmax(rest, axis=-1).astype(jnp.int32)
    top_v = jnp.stack([jnp.max(in_group, axis=-1), jnp.max(rest, axis=-1)], axis=-1)
    top_i = jnp.stack([i1, i2], axis=-1)
    gate = jax.nn.softmax(top_v, axis=-1) * g_w
    eid = g_idx[:, None] * MOE_PER_GROUP + top_i
    n_assign = t * MOE_TOP_K
    flat_e = eid.reshape(n_assign)
    flat_w = gate.reshape(n_assign)
    se, order = lax.sort((flat_e, jnp.arange(n_assign, dtype=jnp.int32)), num_keys=1)
    counts = jnp.sum(flat_e[None, :] == jnp.arange(MOE_EXPERTS, dtype=jnp.int32)[:, None], axis=1, dtype=jnp.int32)
    padded = (counts + MOE_BLOCK - 1) // MOE_BLOCK * MOE_BLOCK
    start = jnp.cumsum(counts) - counts
    pend = jnp.cumsum(padded)
    pstart = pend - padded
    experts = jnp.arange(MOE_EXPERTS, dtype=jnp.int32)[:, None]
    shift = jnp.sum(jnp.where(se[None, :] == experts, (pstart - start)[:, None], 0), axis=0)
    dest = (jnp.arange(n_assign, dtype=jnp.int32) + shift).astype(jnp.int32)
    n_blocks = -(-n_assign // MOE_BLOCK) + MOE_EXPERTS
    n_rows = n_blocks * MOE_BLOCK
    block_start = jnp.arange(n_blocks, dtype=jnp.int32) * MOE_BLOCK
    block_e = jnp.minimum(jnp.sum(pend[None, :] <= block_start[:, None], axis=1), MOE_EXPERTS - 1).astype(jnp.int32)
    off = (block_start - pstart[block_e])[:, None] + jnp.arange(MOE_BLOCK, dtype=jnp.int32)[None, :]
    valid = (off < counts[block_e][:, None]).reshape(n_rows)
    row_asg = order[jnp.where(valid, (start[block_e][:, None] + off).reshape(n_rows), 0)]
    row_tok = jnp.where(valid, row_asg // MOE_TOP_K, 0)
    row_w = jnp.where(valid, flat_w[row_asg], 0.0)
    xg = h[row_tok]
    n_used = (pend[-1:] // MOE_BLOCK).astype(jnp.int32)
    ys = _moe_experts(xg, block_e, n_used, w1, w3, w2, layer, jnp.broadcast_to(row_w[:, None], (n_rows, LANES)))
    _, slot = lax.sort((order, dest), num_keys=1)
    slot = slot.reshape(t, MOE_TOP_K)
    return ys[slot[:, 0]], ys[slot[:, 1]]


def _final_norm_kernel(x_ref, y0_ref, y1_ref, gate_ref, g_ref, o_ref):
    o_ref[0] = _rms_rows(_apply_pending(x_ref, (y0_ref, y1_ref, gate_ref))) * g_ref[...]


def _final_norm(x, pending, g):
    bsz, n, d = x.shape
    tm = min(MIX_ROWS, n)
    rows = pl.BlockSpec((1, tm, d), lambda b, i: (b, i, 0))
    pend_specs, pend_args = _pending_specs(pending, tm, n // tm, d)
    return pl.pallas_call(
        _final_norm_kernel,
        grid=(bsz, n // tm),
        in_specs=[rows, *pend_specs, _const_spec((1, d))],
        out_specs=rows,
        out_shape=jax.ShapeDtypeStruct((bsz, n, d), F32),
        compiler_params=pltpu.CompilerParams(dimension_semantics=("arbitrary", "arbitrary")),
        name="final_norm",
    )(x, *pend_args, g.reshape(1, d))


def kernel(x, c, ctx, c_ctx, ada_w, ada_b, norm1_g, norm2_g, w_in,
           s5_lambda_re, s5_lambda_im, s5_log_dt, s5_b_re, s5_b_im, s5_c_re, s5_c_im, s5_d, s5_glu_w,
           mla_q_norm_g, mla_kv_norm_g, mla_w_uq, mla_w_ukv,
           hy_short_w, hy_short_b, hy_f_w1, hy_f_b1, hy_f_w2, hy_f_b2, hy_f_w3, hy_f_b3, hy_f_freq, hy_bias,
           mix_norm_g, w_out, moe_w_group, moe_w_expert, moe_w1, moe_w3, moe_w2, final_g):
    bsz, n, d = x.shape
    n_ctx = ctx.shape[1]
    xl, xc = x, ctx
    act_l = jax.nn.silu(c)
    act_c = jax.nn.silu(c_ctx)
    rope_cos, rope_sin = _rope_slot_tables(n, n_ctx)
    pend_l = pend_c = None
    for i in range(DEPTH):
        ctx_out = i < DEPTH - 1
        mod_l = jnp.split((act_l @ ada_w[i] + ada_b[i])[:, None, :], 6, axis=-1)
        mod_c = jnp.split((act_c @ ada_w[i] + ada_b[i])[None, None, :], 6, axis=-1)
        mod_c = [jnp.broadcast_to(m, (bsz, 1, d)) for m in mod_c]
        w_ext, w_q = _proj_in_weights(w_in[i], mla_w_uq[i])
        proj = functools.partial(_proj_in, w_ext=w_ext, q_norm_g=mla_q_norm_g[i], w_q=w_q,
                                 kv_norm_g=mla_kv_norm_g[i], w_ukv=mla_w_ukv[i], cos=rope_cos, sin=rope_sin)
        seq, q_l, hz_l, xl = proj(xl, pend_l, norm1_g[i], mod_l[0], mod_l[1], row0=0, shared=None)
        (u_all, kv_all, kr_all), q_c, hz_c, xc = proj(xc, pend_c, norm1_g[i], mod_c[0], mod_c[1], row0=n, shared=seq)
        s5_all = _s5_mixer(u_all, n_ctx, s5_lambda_re[i], s5_lambda_im[i], s5_log_dt[i], s5_b_re[i], s5_b_im[i],
                           s5_c_re[i], s5_c_im[i], s5_d[i], s5_glu_w[i])
        mla_l = _attention(q_l, kv_all, kr_all, 0, n + n_ctx)
        filt = (hy_f_w1[i], hy_f_b1[i], hy_f_w2[i], hy_f_b2[i], hy_f_w3[i], hy_f_b3[i], hy_f_freq[i])
        hy_l = _hyena_mixer(hz_l, hy_short_w[i], hy_short_b[i], filt, hy_bias[i])
        w_route = jnp.pad(jnp.concatenate([moe_w_group[i], moe_w_expert[i]], axis=1),
                          ((0, 0), (0, LANES - MOE_GROUPS - MOE_EXPERTS)))
        n_tok = bsz * (n + n_ctx) if ctx_out else bsz * n
        xl, moe_in = _mix_out(s5_all, mla_l, hy_l, xl, mix_norm_g[i], w_out[i], mod_l[2], norm2_g[i],
                              mod_l[3], mod_l[4], w_route, row0=0, tok0=0, n_tok=n_tok, shared=None)
        if ctx_out:
            mla_c = _attention(q_c, kv_all, kr_all, n, n_ctx)
            hy_c = _hyena_mixer(hz_c, hy_short_w[i], hy_short_b[i], filt, hy_bias[i])
            xc, moe_in = _mix_out(s5_all, mla_c, hy_c, xc, mix_norm_g[i], w_out[i], mod_c[2], norm2_g[i],
                                  mod_c[3], mod_c[4], w_route, row0=n, tok0=bsz * n, n_tok=n_tok, shared=moe_in)
        y0, y1 = _hier_moe(*moe_in, moe_w1, moe_w3, moe_w2, i)
        pend_l, pend_c = (y0, y1, mod_l[5], 0), (y0, y1, mod_c[5], bsz * n)
    return _final_norm(xl, pend_l, final_g)
```

```python
import functools
import math

import jax
import jax.numpy as jnp
from jax import lax
from jax.experimental import pallas as pl
from jax.experimental.pallas import tpu as pltpu

D_MODEL = 1024
DEPTH = 4
GRID_W = 64
EPS = 1e-6

MIX_W = D_MODEL
S5_W = D_MODEL // 4
S5_GROUP = 16
S5_GROUPS = S5_W // S5_GROUP
S5_STATE = 64
S5_N = S5_GROUPS * S5_STATE
MLA_V = 64
MLA_W = D_MODEL // 2
MLA_HEADS = MLA_W // MLA_V
MLA_NOPE = 64
MLA_ROPE = 32
MLA_QK = MLA_NOPE + MLA_ROPE
MLA_Q_RANK = 384
MLA_KV_RANK = 256
MLA_SCALE = 1.0 / math.sqrt(MLA_NOPE + MLA_ROPE)
ROPE_BASE = 10000.0
HY_W = D_MODEL // 4
HY_ORDER = 2
HY_POS_EMB = 33
HY_FILTER_W = 64
HY_MIN_DECAY = math.log(1e-2) / 1.5
HY_MAX_DECAY = math.log(1e-2) / 0.3
MOE_GROUPS = 4
MOE_PER_GROUP = 8
MOE_EXPERTS = MOE_GROUPS * MOE_PER_GROUP
MOE_TOP_K = 2
MOE_HIDDEN = 512
MOE_BLOCK = 256

SUBLANES = 8
S5_CHUNK = 128
MM_ROWS = 256
ATTN_Q_BLOCK = 256
MIX_ROWS = 512
LANES = 128
ATTN_HEADS_PER_STEP = 4
VMEM_LIMIT = 48 * 1024 * 1024

F32 = jnp.float32
BF16 = jnp.bfloat16


def _rms_rows(v):
    return v * lax.rsqrt(jnp.mean(v * v, axis=-1, keepdims=True) + EPS)


def _s5_matrices(lam_re, lam_im, log_dt, b_re, b_im, c_re, c_im):
    g, p, h = S5_GROUPS, S5_STATE, S5_GROUP
    dt = jnp.exp(log_dt)[..., None]
    mag = jnp.exp(lam_re * dt)
    ar, ai = mag * jnp.cos(lam_im * dt), mag * jnp.sin(lam_im * dt)
    den = lam_re * lam_re + lam_im * lam_im
    fr = ((ar - 1.0) * lam_re + ai * lam_im) / den
    fi = (ai * lam_re - (ar - 1.0) * lam_im) / den
    bbr = fr[..., None] * b_re - fi[..., None] * b_im
    bbi = fr[..., None] * b_im + fi[..., None] * b_re
    eye = jnp.eye(g, dtype=F32)

    def block_in(m):
        return jnp.einsum('kgph,gj->kghjp', m, eye).reshape(2, g * h, g * p)

    def block_out(m):
        return jnp.einsum('kghp,gj->kgpjh', m, eye).reshape(2, g * p, g * h)

    bmat = jnp.concatenate([block_in(bbr), block_in(bbi)], axis=-1)
    cmat = jnp.concatenate([block_out(c_re), -block_out(c_im)], axis=1)
    lam = jnp.stack([ar[0], ai[0], ar[1], ai[1]]).reshape(4, 1, g * p)
    lam = jnp.broadcast_to(lam, (4, SUBLANES, g * p)).reshape(4 * SUBLANES, g * p)
    return bmat.astype(BF16), cmat.astype(BF16), lam


def _s5_kernel(uf_ref, ub_ref, bmat_ref, cmat_ref, lam_ref, yf_ref, yb_ref, buf_f, buf_b, state):
    rows = uf_ref.shape[0]
    steps = rows // SUBLANES
    n = S5_N
    s = SUBLANES

    @pl.when(pl.program_id(0) == 0)
    def _():
        state[...] = jnp.zeros_like(state)

    def drive(r, carry):
        r0 = pl.multiple_of(r * MM_ROWS, MM_ROWS)
        buf_f[pl.ds(r0, MM_ROWS), :] = jnp.dot(uf_ref[pl.ds(r0, MM_ROWS), :].astype(BF16), bmat_ref[0],
                                               preferred_element_type=F32)
        buf_b[pl.ds(r0, MM_ROWS), :] = jnp.dot(ub_ref[pl.ds(r0, MM_ROWS), :].astype(BF16), bmat_ref[1],
                                               preferred_element_type=F32)
        return carry

    lax.fori_loop(0, rows // MM_ROWS, drive, 0)

    def step(j, carry):
        fr, fi, br, bi = carry
        rf = pl.multiple_of(j * s, s)
        rb = pl.multiple_of((steps - 1 - j) * s, s)
        lfr, lfi = lam_ref[0:s, :], lam_ref[s:2 * s, :]
        lbr, lbi = lam_ref[2 * s:3 * s, :], lam_ref[3 * s:4 * s, :]
        nfr = lfr * fr - lfi * fi + buf_f[pl.ds(rf, s), 0:n]
        nfi = lfr * fi + lfi * fr + buf_f[pl.ds(rf, s), n:2 * n]
        nbr = lbr * br - lbi * bi + buf_b[pl.ds(rb, s), 0:n]
        nbi = lbr * bi + lbi * br + buf_b[pl.ds(rb, s), n:2 * n]
        buf_f[pl.ds(rf, s), 0:n] = nfr
        buf_f[pl.ds(rf, s), n:2 * n] = nfi
        buf_b[pl.ds(rb, s), 0:n] = nbr
        buf_b[pl.ds(rb, s), n:2 * n] = nbi
        return nfr, nfi, nbr, nbi

    init = (state[0:s, :], state[s:2 * s, :], state[2 * s:3 * s, :], state[3 * s:4 * s, :])
    fr, fi, br, bi = lax.fori_loop(0, steps, step, init, unroll=2)
    state[0:s, :] = fr
    state[s:2 * s, :] = fi
    state[2 * s:3 * s, :] = br
    state[3 * s:4 * s, :] = bi

    def readout(r, carry):
        r0 = pl.multiple_of(r * MM_ROWS, MM_ROWS)
        yf_ref[pl.ds(r0, MM_ROWS), :] = jnp.dot(buf_f[pl.ds(r0, MM_ROWS), :].astype(BF16), cmat_ref[0],
                                                preferred_element_type=F32)
        yb_ref[pl.ds(r0, MM_ROWS), :] = jnp.dot(buf_b[pl.ds(r0, MM_ROWS), :].astype(BF16), cmat_ref[1],
                                                preferred_element_type=F32)
        return carry

    lax.fori_loop(0, rows // MM_ROWS, readout, 0)


def _s5_scan(u_tm, bmat, cmat, lam, n_ctx_chunks, chunk):
    rows_all, w = u_tm.shape
    rows = chunk * SUBLANES
    n_chunks = rows_all // rows
    n_lat_chunks = n_chunks - n_ctx_chunks
    n2 = 2 * S5_N

    def fwd_chunk(i):
        return (i + n_lat_chunks) % n_chunks

    def bwd_chunk(i):
        return n_chunks - 1 - i

    return pl.pallas_call(
        _s5_kernel,
        grid=(n_chunks,),
        in_specs=[
            pl.BlockSpec((rows, w), lambda i: (fwd_chunk(i), 0)),
            pl.BlockSpec((rows, w), lambda i: (bwd_chunk(i), 0)),
            pl.BlockSpec((2, w, n2), lambda i: (0, 0, 0)),
            pl.BlockSpec((2, n2, w), lambda i: (0, 0, 0)),
            pl.BlockSpec((4 * SUBLANES, S5_N), lambda i: (0, 0)),
        ],
        out_specs=[
            pl.BlockSpec((rows, w), lambda i: (fwd_chunk(i), 0)),
            pl.BlockSpec((rows, w), lambda i: (bwd_chunk(i), 0)),
        ],
        out_shape=[jax.ShapeDtypeStruct((rows_all, w), F32)] * 2,
        scratch_shapes=[
            pltpu.VMEM((rows, n2), F32),
            pltpu.VMEM((rows, n2), F32),
            pltpu.VMEM((4 * SUBLANES, S5_N), F32),
        ],
        compiler_params=pltpu.CompilerParams(dimension_semantics=("arbitrary",), vmem_limit_bytes=VMEM_LIMIT),
        name="s5_scan",
    )(u_tm, u_tm, bmat, cmat, lam)


def _s5_mixer(u_tm, n_ctx, lam_re, lam_im, log_dt, b_re, b_im, c_re, c_im, d, glu_w):
    n_all, bw = u_tm.shape
    bsz, w = bw // S5_W, S5_W
    assert bsz == SUBLANES and n_all % S5_CHUNK == 0 and n_ctx % S5_CHUNK == 0
    bmat, cmat, lam = _s5_matrices(lam_re, lam_im, log_dt, b_re, b_im, c_re, c_im)
    u_rows = u_tm.reshape(n_all * bsz, w)
    y_f, y_b = _s5_scan(u_rows, bmat, cmat, lam, n_ctx // S5_CHUNK, S5_CHUNK)
    y = y_f + y_b + d * u_rows
    y = jax.nn.gelu(y)
    y = y * jax.nn.sigmoid(y @ glu_w)
    return y.reshape(n_all, bw)


def _axial_rope_tables(n_tokens):
    rows = n_tokens // GRID_W
    row = jnp.repeat(jnp.arange(rows), GRID_W).astype(F32)
    col = jnp.tile(jnp.arange(GRID_W), rows).astype(F32)
    half = MLA_ROPE // 2
    inv = ROPE_BASE ** (-jnp.arange(0, half, 2, dtype=F32) / half)
    ang_r = row[:, None] * inv
    ang_c = col[:, None] * inv
    cos = jnp.concatenate([jnp.cos(ang_r)] * 2 + [jnp.cos(ang_c)] * 2, axis=-1)
    sin = jnp.concatenate([jnp.sin(ang_r)] * 2 + [jnp.sin(ang_c)] * 2, axis=-1)
    return cos, sin


def _rope_partner(w):
    q = MLA_ROPE // 4
    parts = []
    for g in range(2):
        x1, x2 = w[..., 2 * g * q:(2 * g + 1) * q], w[..., (2 * g + 1) * q:(2 * g + 2) * q]
        parts += [-x2, x1]
    return jnp.concatenate(parts, axis=-1)


def _attn_kernel(q_ref, kv_ref, kr_ref, o_ref, kcat, vaug):
    hs, nk = kcat.shape[0], kcat.shape[1]
    dn = MLA_NOPE

    @pl.when(pl.program_id(2) == 0)
    def _():
        lane = lax.broadcasted_iota(jnp.int32, (nk, LANES), 1)
        ones_col = jnp.where(lane == 0, 1.0, 0.0).astype(BF16)
        for h in range(hs):
            kvh = kv_ref[0, :, h * LANES:(h + 1) * LANES]
            kcat[h] = jnp.where(lane < dn, kvh, kr_ref[0])
            vaug[h] = jnp.where(lane >= dn, kvh, ones_col)

    outs = []
    for h in range(hs):
        s = lax.dot_general(q_ref[0, :, h * LANES:(h + 1) * LANES], kcat[h], (((1,), (1,)), ((), ())),
                            preferred_element_type=F32)
        p = jnp.exp(s - jnp.max(s, axis=-1, keepdims=True))
        o = jnp.dot(p.astype(BF16), vaug[h], preferred_element_type=F32)
        outs.append(o[:, dn:] / o[:, 0:1])
    o_ref[0] = jnp.concatenate(outs, axis=-1)


def _attention(q, kv, kr, key_row0, nk):
    bsz, nq, _ = q.shape
    heads, dv = MLA_HEADS, MLA_V
    tq = min(ATTN_Q_BLOCK, nq)
    hs = ATTN_HEADS_PER_STEP
    kb = key_row0 // nk
    return pl.pallas_call(
        _attn_kernel,
        grid=(bsz, heads // hs, nq // tq),
        in_specs=[
            pl.BlockSpec((1, tq, hs * LANES), lambda b, h, i: (b, i, h)),
            pl.BlockSpec((1, nk, hs * LANES), lambda b, h, i: (b, kb, h), pipeline_mode=pl.Buffered(1)),
            pl.BlockSpec((1, nk, LANES), lambda b, h, i: (b, kb, 0), pipeline_mode=pl.Buffered(1)),
        ],
        out_specs=pl.BlockSpec((1, tq, hs * dv), lambda b, h, i: (b, i, h)),
        out_shape=jax.ShapeDtypeStruct((bsz, nq, heads * dv), F32),
        scratch_shapes=[pltpu.VMEM((hs, nk, LANES), BF16), pltpu.VMEM((hs, nk, LANES), BF16)],
        compiler_params=pltpu.CompilerParams(
            dimension_semantics=("arbitrary", "arbitrary", "arbitrary"), vmem_limit_bytes=VMEM_LIMIT),
        name="mla_attention",
    )(q, kv, kr)


PIN_U = 0
PIN_CQ = PIN_U + S5_W
PIN_CKV = PIN_CQ + MLA_Q_RANK
PIN_HZ = PIN_CKV + MLA_KV_RANK
PIN_KR = PIN_HZ + 3 * HY_W
PIN_KRP = PIN_KR + LANES
PIN_COLS = PIN_KRP + LANES


def _proj_in_weights(w_in, w_uq):
    o1 = S5_W
    o2 = o1 + MLA_Q_RANK
    o3 = o2 + MLA_KV_RANK
    o4 = o3 + MLA_ROPE
    pad = LANES - MLA_QK
    w_kr = w_in[:, o3:o4]
    in_slot = lambda w: jnp.pad(w, ((0, 0), (MLA_NOPE, pad)))
    w_ext = jnp.concatenate([w_in[:, :o3], w_in[:, o4:], in_slot(w_kr), in_slot(_rope_partner(w_kr))], axis=1)
    w_h = w_uq.reshape(MLA_Q_RANK, MLA_HEADS, MLA_QK)
    w_rot = jnp.concatenate([jnp.zeros_like(w_h[..., :MLA_NOPE]), _rope_partner(w_h[..., MLA_NOPE:])], axis=-1)
    w_q = jnp.concatenate([jnp.pad(w, ((0, 0), (0, 0), (0, pad))).reshape(MLA_Q_RANK, MLA_HEADS * LANES)
                           for w in (w_h, w_rot)], axis=-1)
    return w_ext.astype(BF16), w_q.astype(BF16)


def _rope_slot_tables(n, n_ctx):
    cos, sin = _axial_rope_tables(n)
    pad = LANES - MLA_QK
    cos = jnp.concatenate([jnp.ones((n, MLA_NOPE), F32), cos, jnp.ones((n, pad), F32)], axis=-1)
    sin = jnp.concatenate([jnp.zeros((n, MLA_NOPE), F32), sin, jnp.zeros((n, pad), F32)], axis=-1)
    return (jnp.concatenate([cos, jnp.ones((n_ctx, LANES), F32)], axis=0),
            jnp.concatenate([sin, jnp.zeros((n_ctx, LANES), F32)], axis=0))


def _apply_pending(x_ref, pending_refs):
    y0_ref, y1_ref, gate_ref = pending_refs
    return x_ref[0] + gate_ref[0] * (y0_ref[...] + y1_ref[...])


def _pending_specs(pending, tm, blocks_per_batch, d):
    if pending is None:
        return [], []
    y0, y1, gate, row0 = pending
    blk0 = row0 // tm
    rows = pl.BlockSpec((tm, d), lambda b, i: (blk0 + b * blocks_per_batch + i, 0))
    return [rows, rows, pl.BlockSpec((1, 1, d), lambda b, i: (b, 0, 0))], [y0, y1, gate]


def _proj_in_kernel(*refs, has_pending, n_shared):
    if has_pending:
        x_ref, *pending_refs = refs[:4]
        refs = refs[4:]
    else:
        x_ref, refs = refs[0], refs[1:]
    g_ref, shift_ref, scale_ref, w_ref, qg_ref, wq_ref, kvg_ref, wkv_ref, cos_ref, sin_ref = refs[:10]
    u_ref, q_ref, kv_ref, kr_ref, hz_ref, *xo_ref = refs[10 + n_shared:]
    if has_pending:
        x = _apply_pending(x_ref, pending_refs)
        xo_ref[0][0] = x
    else:
        x = x_ref[0]
    h = ((_rms_rows(x) * g_ref[...]) * (1.0 + scale_ref[0]) + shift_ref[0]).astype(BF16)
    p = jnp.dot(h, w_ref[...], preferred_element_type=F32)
    u_ref[...] = p[:, PIN_U:PIN_CQ]
    hz_ref[0] = p[:, PIN_HZ:PIN_KR]
    cos, sin = cos_ref[...], sin_ref[...]
    kr_ref[0] = (p[:, PIN_KR:PIN_KRP] * cos + p[:, PIN_KRP:PIN_COLS] * sin).astype(BF16)
    cq = (_rms_rows(p[:, PIN_CQ:PIN_CKV]) * qg_ref[...]).astype(BF16)
    q2 = jnp.dot(cq, wq_ref[...], preferred_element_type=F32)
    hw = MLA_HEADS * LANES
    for hd in range(MLA_HEADS):
        a, b = q2[:, hd * LANES:(hd + 1) * LANES], q2[:, hw + hd * LANES:hw + (hd + 1) * LANES]
        q_ref[0, :, hd * LANES:(hd + 1) * LANES] = ((a * cos + b * sin) * MLA_SCALE).astype(BF16)
    ckv = (_rms_rows(p[:, PIN_CKV:PIN_HZ]) * kvg_ref[...]).astype(BF16)
    kv_ref[0] = jnp.dot(ckv, wkv_ref[...], preferred_element_type=F32).astype(BF16)


def _proj_in(x, pending, norm_g, shift, scale, w_ext, q_norm_g, w_q, kv_norm_g, w_ukv, cos, sin, row0, shared):
    bsz, n, d = x.shape
    n_all = cos.shape[0]
    tm = min(MIX_ROWS, n)
    blk0 = row0 // tm
    hw = MLA_HEADS * LANES

    def rows(width, first=0):
        return pl.BlockSpec((1, tm, width), lambda b, i: (b, first + i, 0))

    per_batch = pl.BlockSpec((1, 1, d), lambda b, i: (b, 0, 0))
    table = pl.BlockSpec((tm, LANES), lambda b, i: (blk0 + i, 0))
    pend_specs, pend_args = _pending_specs(pending, tm, n // tm, d)
    has_pending = pending is not None
    out_specs = [pl.BlockSpec((tm, S5_W), lambda b, i: (blk0 + i, b)), rows(hw), rows(hw, blk0), rows(LANES, blk0),
                 rows(3 * HY_W)]
    out_shape = [jax.ShapeDtypeStruct((n_all, bsz * S5_W), F32), jax.ShapeDtypeStruct((bsz, n, hw), BF16),
                 jax.ShapeDtypeStruct((bsz, n_all, hw), BF16), jax.ShapeDtypeStruct((bsz, n_all, LANES), BF16),
                 jax.ShapeDtypeStruct((bsz, n, 3 * HY_W), F32)]
    if has_pending:
        out_specs.append(rows(d))
        out_shape.append(jax.ShapeDtypeStruct((bsz, n, d), F32))
    n_in = 1 + len(pend_args) + 10
    shared = list(shared or [jnp.zeros(out_shape[k].shape, out_shape[k].dtype) for k in (0, 2, 3)])
    aliases = {n_in + k: out for k, out in enumerate((0, 2, 3))}
    outs = pl.pallas_call(
        functools.partial(_proj_in_kernel, has_pending=has_pending, n_shared=len(shared)),
        grid=(bsz, n // tm),
        in_specs=[rows(d), *pend_specs, _const_spec((1, d)), per_batch, per_batch, _const_spec((d, PIN_COLS)),
                  _const_spec((1, MLA_Q_RANK)), _const_spec((MLA_Q_RANK, 2 * hw)),
                  _const_spec((1, MLA_KV_RANK)), _const_spec((MLA_KV_RANK, hw)), table, table,
                  *[pl.BlockSpec(memory_space=pl.ANY)] * len(shared)],
        out_specs=out_specs,
        out_shape=out_shape,
        input_output_aliases=aliases,
        compiler_params=pltpu.CompilerParams(dimension_semantics=("arbitrary", "arbitrary"),
                                             vmem_limit_bytes=VMEM_LIMIT),
        name="proj_in",
    )(x, *pend_args, norm_g.reshape(1, d), shift, scale, w_ext, q_norm_g.reshape(1, -1), w_q,
      kv_norm_g.reshape(1, -1), w_ukv.astype(BF16), cos, sin, *shared)
    return (outs[0], outs[2], outs[3]), outs[1], outs[4], (outs[5] if has_pending else x)


def _short_conv(z, w, b):
    n = z.shape[1]
    zp = jnp.pad(z, ((0, 0), (1, 1), (0, 0)))
    return zp[:, :n] * w[0] + zp[:, 1:n + 1] * w[1] + zp[:, 2:] * w[2] + b


def _hyena_filter_taps(n, w1, b1, w2, b2, w3, b3, freq):
    t = jnp.linspace(0.0, 1.0, n, dtype=F32)[:, None]
    bands = (HY_POS_EMB - 1) // 2
    f = jnp.linspace(1e-4, bands - 1, bands, dtype=F32)
    ang = (2.0 * math.pi * jnp.arange(n, dtype=F32) / n)[:, None] * f
    z = jnp.concatenate([t, jnp.cos(ang), -jnp.sin(ang)], axis=-1)
    h = jnp.sin(freq[0] * (z @ w1 + b1))
    h = jnp.sin(freq[1] * (h @ w2 + b2))
    half = HY_ORDER * HY_W
    deltas = jnp.tile(jnp.abs(jnp.linspace(HY_MIN_DECAY, HY_MAX_DECAY, HY_W, dtype=F32)), HY_ORDER)
    fwd = (h @ w3[:, :half] + b3[:half]) * jnp.exp(-t * deltas)
    bwd_rev = (h[::-1] @ w3[:, half:] + b3[half:]) * jnp.exp(-t[::-1] * deltas)
    return jnp.concatenate([fwd, jnp.zeros((1, half), F32), bwd_rev[:n - 1]], axis=0)


def _dense_conv_kernel(x_ref, k_ref, ff_ref, g_ref, o_ref):
    n = x_ref.shape[1]
    k = k_ref[...].astype(BF16)
    kr = jnp.dot(ff_ref[0], k, preferred_element_type=F32)
    ki = jnp.dot(ff_ref[1], k, preferred_element_type=F32)
    x = x_ref[0].astype(BF16)
    yr = jnp.dot(ff_ref[0, :, :n], x, preferred_element_type=F32)
    yi = jnp.dot(ff_ref[1, :, :n], x, preferred_element_type=F32)
    zr, zi = (yr * kr - yi * ki).astype(BF16), (yr * ki + yi * kr).astype(BF16)
    o_ref[0] = jnp.dot(g_ref[0], zr, preferred_element_type=F32) + jnp.dot(g_ref[1], zi, preferred_element_type=F32)


def _dense_long_conv(y, k_time):
    bsz, n, c = y.shape
    nfft = 2 * n
    idx = jnp.arange(nfft, dtype=jnp.int32)
    ang = (2.0 * math.pi / nfft) * ((idx[:, None] * idx[None, :]) % nfft).astype(F32)
    ff = jnp.stack([jnp.cos(ang), -jnp.sin(ang)]).astype(BF16)
    g = (jnp.stack([jnp.cos(ang), -jnp.sin(ang)])[:, :n, :] / nfft).astype(BF16)
    return pl.pallas_call(
        _dense_conv_kernel,
        grid=(bsz,),
        in_specs=[pl.BlockSpec((1, n, c), lambda b: (b, 0, 0)), _const_spec((nfft, c)),
                  _const_spec((2, nfft, nfft)), _const_spec((2, n, nfft))],
        out_specs=pl.BlockSpec((1, n, c), lambda b: (b, 0, 0)),
        out_shape=jax.ShapeDtypeStruct((bsz, n, c), F32),
        compiler_params=pltpu.CompilerParams(dimension_semantics=("arbitrary",)),
        name="hyena_dense_conv",
    )(y, k_time, ff, g)


def _hyena_mixer(z, short_w, short_b, filt, bias):
    n = z.shape[1]
    z = _short_conv(z, short_w, short_b)
    k_time = _hyena_filter_taps(n, *filt)
    if (2 * n) % HY_N2 == 0 and (2 * n) // HY_N2 >= 2 * SUBLANES:
        tables = _dft_tables(2 * n)
        kr, ki = _hyena_spectrum(k_time, tables)
        rows = n // HY_N2
        z = _to_strided(z, rows)
        y = z[..., :HY_W]
        for o in range(HY_ORDER):
            cols = slice(o * HY_W, (o + 1) * HY_W)
            gate = z[..., (o + 1) * HY_W:(o + 2) * HY_W]
            y = gate * (_hyena_long_conv(y, kr[:, cols], ki[:, cols], tables) + y * bias[o])
        return _from_strided(y, rows)
    y = z[..., :HY_W]
    for o in range(HY_ORDER):
        gate = z[..., (o + 1) * HY_W:(o + 2) * HY_W]
        y = gate * (_dense_long_conv(y, k_time[:, o * HY_W:(o + 1) * HY_W]) + y * bias[o])
    return y


HY_N2 = 128
HY_PITCH = 136
HY_UNROLL = 8


def _dft_tables(nfft):
    n1 = nfft // HY_N2
    half = n1 // 2
    k1 = jnp.arange(n1, dtype=jnp.int32)
    n2 = jnp.arange(HY_N2, dtype=jnp.int32)
    t = HY_N2 * k1[None, None, :] + n2[:, None, None]
    ang = (2.0 * math.pi / nfft) * ((k1[None, :, None] * t) % nfft).astype(F32)
    gr, gi = jnp.cos(ang), -jnp.sin(ang)
    g_cplx = jnp.concatenate([jnp.concatenate([gr[..., :half], -gi[..., :half]], -1),
                              jnp.concatenate([gi[..., :half], gr[..., :half]], -1)], axis=1)
    g_real = jnp.concatenate([gr, gi], axis=1)
    hr = jnp.cos(ang).swapaxes(1, 2)[:, :half] / nfft
    hi = jnp.sin(ang).swapaxes(1, 2)[:, :half] / nfft
    g_inv = jnp.concatenate([jnp.concatenate([hr, -hi], -1), jnp.concatenate([hi, hr], -1)], axis=1)
    a2 = (2.0 * math.pi / HY_N2) * ((n2[:, None] * n2[None, :]) % HY_N2).astype(F32)
    fr, fi = jnp.cos(a2), -jnp.sin(a2)
    f2 = jnp.concatenate([jnp.concatenate([fr, -fi], -1), jnp.concatenate([fi, fr], -1)], axis=0)
    f2_inv = jnp.concatenate([jnp.concatenate([fr, fi], -1), jnp.concatenate([-fi, fr], -1)], axis=0)
    return dict(n1=n1, g_cplx=g_cplx.astype(BF16), g_real=g_real.astype(BF16), g_inv=g_inv.astype(BF16),
                f2=f2.astype(BF16), f2_inv=f2_inv.astype(BF16))


def _dft_stage1(load_rows, g_ref, s_re, s_im, n1):
    def body(n2, carry):
        a = jnp.dot(g_ref[n2], load_rows(n2).astype(BF16), preferred_element_type=F32)
        s_re[pl.ds(n2, n1, stride=HY_PITCH), :] = a[:n1]
        s_im[pl.ds(n2, n1, stride=HY_PITCH), :] = a[n1:]
        return carry

    lax.fori_loop(0, HY_N2, body, 0, unroll=HY_UNROLL)


def _dft_stage2(f2_ref, s_re, s_im, k1):
    r0 = pl.multiple_of(k1 * HY_PITCH, SUBLANES)
    sl = jnp.concatenate([s_re[pl.ds(r0, HY_N2), :], s_im[pl.ds(r0, HY_N2), :]], axis=0).astype(BF16)
    x = jnp.dot(f2_ref[...], sl, preferred_element_type=F32)
    return r0, x[:HY_N2], x[HY_N2:]


def _hyena_spectrum_kernel(x_ref, g_ref, f2_ref, kr_ref, ki_ref, s_re, s_im, *, n1):
    _dft_stage1(lambda n2: x_ref[pl.ds(n2, n1, stride=HY_PITCH), :], g_ref, s_re, s_im, n1)

    def body(k1, carry):
        _, xr, xi = _dft_stage2(f2_ref, s_re, s_im, k1)
        q0 = pl.multiple_of(k1 * HY_N2, HY_N2)
        kr_ref[pl.ds(q0, HY_N2), :] = xr
        ki_ref[pl.ds(q0, HY_N2), :] = xi
        return carry

    lax.fori_loop(0, n1, body, 0)


def _hyena_conv_kernel(x_ref, g_ref, f2_ref, f2i_ref, gi_ref, kr_ref, ki_ref, y_ref, s_re, s_im, *, n1):
    half = n1 // 2

    def load_rows(n2):
        return jnp.concatenate([x_ref[0, pl.ds(n2, half, stride=HY_PITCH), :],
                                x_ref[1, pl.ds(n2, half, stride=HY_PITCH), :]], axis=0)

    _dft_stage1(load_rows, g_ref, s_re, s_im, n1)

    def spectrum_product(k1, carry):
        r0, xr, xi = _dft_stage2(f2_ref, s_re, s_im, k1)
        q0 = pl.multiple_of(k1 * HY_N2, HY_N2)
        kr, ki = kr_ref[pl.ds(q0, HY_N2), :], ki_ref[pl.ds(q0, HY_N2), :]
        y = jnp.concatenate([xr * kr - xi * ki, xr * ki + xi * kr], axis=0).astype(BF16)
        b = jnp.dot(f2i_ref[...], y, preferred_element_type=F32)
        s_re[pl.ds(r0, HY_N2), :] = b[:HY_N2]
        s_im[pl.ds(r0, HY_N2), :] = b[HY_N2:]
        return carry

    lax.fori_loop(0, n1, spectrum_product, 0, unroll=HY_UNROLL // 2)

    y_ref[...] = jnp.zeros_like(y_ref)

    def inverse_stage1(n2, carry):
        bs = jnp.concatenate([s_re[pl.ds(n2, n1, stride=HY_PITCH), :],
                              s_im[pl.ds(n2, n1, stride=HY_PITCH), :]], axis=0).astype(BF16)
        y = jnp.dot(gi_ref[n2], bs, preferred_element_type=F32)
        y_ref[0, pl.ds(n2, half, stride=HY_PITCH), :] = y[:half]
        y_ref[1, pl.ds(n2, half, stride=HY_PITCH), :] = y[half:]
        return carry

    lax.fori_loop(0, HY_N2, inverse_stage1, 0, unroll=HY_UNROLL)


def _to_strided(a, rows):
    lead, c = a.shape[:-2], a.shape[-1]
    a = a.reshape(lead + (rows, HY_N2, c))
    a = jnp.pad(a, [(0, 0)] * len(lead) + [(0, 0), (0, HY_PITCH - HY_N2), (0, 0)])
    return a.reshape(lead + (rows * HY_PITCH, c))


def _from_strided(a, rows):
    lead, c = a.shape[:-2], a.shape[-1]
    return a.reshape(lead + (rows, HY_PITCH, c))[..., :HY_N2, :].reshape(lead + (rows * HY_N2, c))


def _const_spec(shape):
    return pl.BlockSpec(shape, lambda *_: (0,) * len(shape), pipeline_mode=pl.Buffered(1))


def _hyena_spectrum(k_time, tables):
    nfft, c = k_time.shape
    n1 = tables["n1"]
    spec_out = pl.BlockSpec((nfft, LANES), lambda j: (0, j))
    return pl.pallas_call(
        functools.partial(_hyena_spectrum_kernel, n1=n1),
        grid=(c // LANES,),
        in_specs=[pl.BlockSpec((n1 * HY_PITCH, LANES), lambda j: (0, j)),
                  _const_spec((HY_N2, 2 * n1, n1)), _const_spec((2 * HY_N2, 2 * HY_N2))],
        out_specs=[spec_out, spec_out],
        out_shape=[jax.ShapeDtypeStruct((nfft, c), F32)] * 2,
        scratch_shapes=[pltpu.VMEM((n1 * HY_PITCH, LANES), F32)] * 2,
        compiler_params=pltpu.CompilerParams(dimension_semantics=("arbitrary",), vmem_limit_bytes=VMEM_LIMIT),
        name="hyena_spectrum",
    )(_to_strided(k_time, n1), tables["g_real"], tables["f2"])


def _hyena_long_conv(y, kr, ki, tables):
    bsz, r, c = y.shape
    n1 = tables["n1"]
    pairs = bsz // 2
    spec_x = pl.BlockSpec((None, 2, r, LANES), lambda j, p: (p, 0, 0, j))
    spec_k = pl.BlockSpec((n1 * HY_N2, LANES), lambda j, p: (0, j), pipeline_mode=pl.Buffered(1))
    out = pl.pallas_call(
        functools.partial(_hyena_conv_kernel, n1=n1),
        grid=(c // LANES, pairs),
        in_specs=[spec_x,
                  _const_spec((HY_N2, 2 * n1, n1)), _const_spec((2 * HY_N2, 2 * HY_N2)),
                  _const_spec((2 * HY_N2, 2 * HY_N2)), _const_spec((HY_N2, n1, 2 * n1)),
                  spec_k, spec_k],
        out_specs=spec_x,
        out_shape=jax.ShapeDtypeStruct((pairs, 2, r, c), F32),
        scratch_shapes=[pltpu.VMEM((n1 * HY_PITCH, LANES), F32)] * 2,
        compiler_params=pltpu.CompilerParams(dimension_semantics=("arbitrary", "arbitrary"),
                                             vmem_limit_bytes=VMEM_LIMIT),
        name="hyena_conv",
    )(y.reshape(pairs, 2, r, c), tables["g_cplx"], tables["f2"], tables["f2_inv"], tables["g_inv"], kr, ki)
    return out.reshape(bsz, r, c)


HIGH_HALF = -(1 << 16)


def _pack_bf16_pairs(f):
    half = f.shape[1] // 2
    bits = lax.bitcast_convert_type(f.astype(F32), jnp.int32)
    return lax.shift_right_logical(bits[:, :half], 16) | (bits[:, half:] & HIGH_HALF)


def _unpack_bf16_pairs(w):
    lo = lax.bitcast_convert_type(lax.shift_left(w, 16), F32)
    hi = lax.bitcast_convert_type(w & HIGH_HALF, F32)
    return jnp.concatenate([lo, hi], axis=-1).astype(BF16)


def _mix_out_kernel(s5_ref, mla_ref, hy_ref, x_ref, g_ref, w_ref, gate_ref, n2g_ref, shift_ref, scale_ref, wr_ref,
                    *rest):
    xo_ref, f_ref, lg_ref = rest[-3:]
    c1, c2 = S5_W, S5_W + MLA_W
    y = jnp.dot((_rms_rows(s5_ref[...]) * g_ref[:, :c1]).astype(BF16), w_ref[:c1, :], preferred_element_type=F32)
    y += jnp.dot((_rms_rows(mla_ref[0]) * g_ref[:, c1:c2]).astype(BF16), w_ref[c1:c2, :], preferred_element_type=F32)
    y += jnp.dot((_rms_rows(hy_ref[0]) * g_ref[:, c2:]).astype(BF16), w_ref[c2:, :], preferred_element_type=F32)
    x_new = x_ref[0] + gate_ref[0] * y
    xo_ref[0] = x_new
    f = ((_rms_rows(x_new) * n2g_ref[...]) * (1.0 + scale_ref[0]) + shift_ref[0]).astype(BF16)
    f_ref[...] = _pack_bf16_pairs(f)
    lg_ref[...] = jnp.dot(f, wr_ref[...], preferred_element_type=F32)


def _mix_out(y_s5, y_mla, y_hy, x, mix_g, w_out, gate, norm2_g, shift, scale, w_route, row0, tok0, n_tok, shared):
    bsz, n, d = x.shape
    tm = min(MIX_ROWS, n)
    s5_blk0, tok_blk0, per_batch_blocks = row0 // tm, tok0 // tm, n // tm

    def rows(width):
        return pl.BlockSpec((1, tm, width), lambda b, i: (b, i, 0))

    def tokens(width):
        return pl.BlockSpec((tm, width), lambda b, i: (tok_blk0 + b * per_batch_blocks + i, 0))

    per_batch = pl.BlockSpec((1, 1, d), lambda b, i: (b, 0, 0))
    if shared is None:
        shared = () if n_tok == bsz * n else (jnp.zeros((n_tok, d // 2), jnp.int32), jnp.zeros((n_tok, LANES), F32))
    shared = list(shared)
    outs = pl.pallas_call(
        _mix_out_kernel,
        grid=(bsz, n // tm),
        in_specs=[pl.BlockSpec((tm, S5_W), lambda b, i: (s5_blk0 + i, b)), rows(MLA_W), rows(HY_W), rows(d),
                  _const_spec((1, d)), _const_spec((d, d)),
                  per_batch, _const_spec((1, d)), per_batch, per_batch, _const_spec((d, LANES)),
                  *[pl.BlockSpec(memory_space=pl.ANY)] * len(shared)],
        out_specs=[rows(d), tokens(d // 2), tokens(LANES)],
        out_shape=[jax.ShapeDtypeStruct((bsz, n, d), F32), jax.ShapeDtypeStruct((n_tok, d // 2), jnp.int32),
                   jax.ShapeDtypeStruct((n_tok, LANES), F32)],
        input_output_aliases={11 + k: 1 + k for k in range(len(shared))},
        compiler_params=pltpu.CompilerParams(dimension_semantics=("arbitrary", "arbitrary"),
                                             vmem_limit_bytes=VMEM_LIMIT),
        name="mix_out",
    )(y_s5, y_mla, y_hy, x, mix_g.reshape(1, d), w_out.astype(BF16), gate, norm2_g.reshape(1, d), shift, scale,
      w_route.astype(BF16), *shared)
    return outs[0], (outs[1], outs[2])


def _moe_kernel(be_ref, used_ref, x_ref, w1_ref, w3_ref, w2_ref, rw_ref, o_ref, w13_s, w2_s):
    i = pl.program_id(0)
    hid = w2_s.shape[0]

    @pl.when(jnp.logical_or(i == 0, be_ref[i] != be_ref[jnp.maximum(i - 1, 0)]))
    def _():
        w13_s[:, :hid] = w1_ref[0, 0].astype(BF16)
        w13_s[:, hid:] = w3_ref[0, 0].astype(BF16)
        w2_s[...] = w2_ref[0, 0].astype(BF16)

    @pl.when(i < used_ref[0])
    def _():
        ab = jnp.dot(_unpack_bf16_pairs(x_ref[...]), w13_s[...], preferred_element_type=F32)
        a, b = ab[:, :hid], ab[:, hid:]
        h = (a * jax.nn.sigmoid(a)) * b
        y = jnp.dot(h.astype(BF16), w2_s[...], preferred_element_type=F32)
        rw = rw_ref[...]
        for c0 in range(0, y.shape[1], LANES):
            o_ref[:, c0:c0 + LANES] = y[:, c0:c0 + LANES] * rw

    @pl.when(i >= used_ref[0])
    def _():
        o_ref[...] = jnp.zeros_like(o_ref)


def _moe_experts(xg, block_e, n_used, w1, w3, w2, layer, row_w):
    n_rows = xg.shape[0]
    d, hid = w1.shape[2], w1.shape[3]
    n_blocks = n_rows // MOE_BLOCK
    return pl.pallas_call(
        _moe_kernel,
        grid_spec=pltpu.PrefetchScalarGridSpec(
            num_scalar_prefetch=2,
            grid=(n_blocks,),
            in_specs=[
                pl.BlockSpec((MOE_BLOCK, d // 2), lambda i, be, nu: (i, 0)),
                pl.BlockSpec((1, 1, d, hid), lambda i, be, nu: (layer, be[i], 0, 0)),
                pl.BlockSpec((1, 1, d, hid), lambda i, be, nu: (layer, be[i], 0, 0)),
                pl.BlockSpec((1, 1, hid, d), lambda i, be, nu: (layer, be[i], 0, 0)),
                pl.BlockSpec((MOE_BLOCK, LANES), lambda i, be, nu: (i, 0)),
            ],
            out_specs=pl.BlockSpec((MOE_BLOCK, d), lambda i, be, nu: (i, 0)),
            scratch_shapes=[pltpu.VMEM((d, 2 * hid), BF16), pltpu.VMEM((hid, d), BF16)],
        ),
        out_shape=jax.ShapeDtypeStruct((n_rows, d), F32),
        compiler_params=pltpu.CompilerParams(dimension_semantics=("arbitrary",), vmem_limit_bytes=VMEM_LIMIT),
        name="moe_experts",
    )(block_e, n_used, xg, w1, w3, w2, row_w)


def _hier_moe(h, logits, w1, w3, w2, layer):
    t = h.shape[0]
    g_prob = jax.nn.softmax(logits[:, :MOE_GROUPS], axis=-1)
    g_idx = jnp.argmax(g_prob, axis=-1).astype(jnp.int32)
    g_w = jnp.max(g_prob, axis=-1, keepdims=True)
    e_logits = logits[:, MOE_GROUPS:MOE_GROUPS + MOE_EXPERTS].reshape(t, MOE_GROUPS, MOE_PER_GROUP)
    in_group = jnp.take_along_axis(e_logits, g_idx[:, None, None], axis=1)[:, 0]
    i1 = jnp.argmax(in_group, axis=-1).astype(jnp.int32)
    rest = jnp.where(jnp.arange(MOE_PER_GROUP, dtype=jnp.int32)[None, :] == i1[:, None], -jnp.inf, in_group)
    i2 = jnp.argmax(rest, axis=-1).astype(jnp.int32)
    top_v = jnp.stack([jnp.max(in_group, axis=-1), jnp.max(rest, axis=-1)], axis=-1)
    top_i = jnp.stack([i1, i2], axis=-1)
    gate = jax.nn.softmax(top_v, axis=-1) * g_w
    eid = g_idx[:, None] * MOE_PER_GROUP + top_i
    n_assign = t * MOE_TOP_K
    flat_e = eid.reshape(n_assign)
    flat_w = gate.reshape(n_assign)
    se, order = lax.sort((flat_e, jnp.arange(n_assign, dtype=jnp.int32)), num_keys=1)
    counts = jnp.sum(flat_e[None, :] == jnp.arange(MOE_EXPERTS, dtype=jnp.int32)[:, None], axis=1, dtype=jnp.int32)
    padded = (counts + MOE_BLOCK - 1) // MOE_BLOCK * MOE_BLOCK
    start = jnp.cumsum(counts) - counts
    pend = jnp.cumsum(padded)
    pstart = pend - padded
    experts = jnp.arange(MOE_EXPERTS, dtype=jnp.int32)[:, None]
    shift = jnp.sum(jnp.where(se[None, :] == experts, (pstart - start)[:, None], 0), axis=0)
    dest = (jnp.arange(n_assign, dtype=jnp.int32) + shift).astype(jnp.int32)
    n_blocks = -(-n_assign // MOE_BLOCK) + MOE_EXPERTS
    n_rows = n_blocks * MOE_BLOCK
    block_start = jnp.arange(n_blocks, dtype=jnp.int32) * MOE_BLOCK
    block_e = jnp.minimum(jnp.sum(pend[None, :] <= block_start[:, None], axis=1), MOE_EXPERTS - 1).astype(jnp.int32)
    off = (block_start - pstart[block_e])[:, None] + jnp.arange(MOE_BLOCK, dtype=jnp.int32)[None, :]
    valid = (off < counts[block_e][:, None]).reshape(n_rows)
    row_asg = order[jnp.where(valid, (start[block_e][:, None] + off).reshape(n_rows), 0)]
    row_tok = jnp.where(valid, row_asg // MOE_TOP_K, 0)
    row_w = jnp.where(valid, flat_w[row_asg], 0.0)
    xg = h[row_tok]
    n_used = (pend[-1:] // MOE_BLOCK).astype(jnp.int32)
    ys = _moe_experts(xg, block_e, n_used, w1, w3, w2, layer, jnp.broadcast_to(row_w[:, None], (n_rows, LANES)))
    _, slot = lax.sort((order, dest), num_keys=1)
    slot = slot.reshape(t, MOE_TOP_K)
    return ys[slot[:, 0]], ys[slot[:, 1]]


def _final_norm_kernel(x_ref, y0_ref, y1_ref, gate_ref, g_ref, o_ref):
    o_ref[0] = _rms_rows(_apply_pending(x_ref, (y0_ref, y1_ref, gate_ref))) * g_ref[...]


def _final_norm(x, pending, g):
    bsz, n, d = x.shape
    tm = min(MIX_ROWS, n)
    rows = pl.BlockSpec((1, tm, d), lambda b, i: (b, i, 0))
    pend_specs, pend_args = _pending_specs(pending, tm, n // tm, d)
    return pl.pallas_call(
        _final_norm_kernel,
        grid=(bsz, n // tm),
        in_specs=[rows, *pend_specs, _const_spec((1, d))],
        out_specs=rows,
        out_shape=jax.ShapeDtypeStruct((bsz, n, d), F32),
        compiler_params=pltpu.CompilerParams(dimension_semantics=("arbitrary", "arbitrary")),
        name="final_norm",
    )(x, *pend_args, g.reshape(1, d))


def kernel(x, c, ctx, c_ctx, ada_w, ada_b, norm1_g, norm2_g, w_in,
           s5_lambda_re, s5_lambda_im, s5_log_dt, s5_b_re, s5_b_im, s5_c_re, s5_c_im, s5_d, s5_glu_w,
           mla_q_norm_g, mla_kv_norm_g, mla_w_uq, mla_w_ukv,
           hy_short_w, hy_short_b, hy_f_w1, hy_f_b1, hy_f_w2, hy_f_b2, hy_f_w3, hy_f_b3, hy_f_freq, hy_bias,
           mix_norm_g, w_out, moe_w_group, moe_w_expert, moe_w1, moe_w3, moe_w2, final_g):
    bsz, n, d = x.shape
    n_ctx = ctx.shape[1]
    xl, xc = x, ctx
    act_l = jax.nn.silu(c)
    act_c = jax.nn.silu(c_ctx)
    rope_cos, rope_sin = _rope_slot_tables(n, n_ctx)
    pend_l = pend_c = None
    for i in range(DEPTH):
        ctx_out = i < DEPTH - 1
        mod_l = jnp.split((act_l @ ada_w[i] + ada_b[i])[:, None, :], 6, axis=-1)
        mod_c = jnp.split((act_c @ ada_w[i] + ada_b[i])[None, None, :], 6, axis=-1)
        mod_c = [jnp.broadcast_to(m, (bsz, 1, d)) for m in mod_c]
        w_ext, w_q = _proj_in_weights(w_in[i], mla_w_uq[i])
        proj = functools.partial(_proj_in, w_ext=w_ext, q_norm_g=mla_q_norm_g[i], w_q=w_q,
                                 kv_norm_g=mla_kv_norm_g[i], w_ukv=mla_w_ukv[i], cos=rope_cos, sin=rope_sin)
        seq, q_l, hz_l, xl = proj(xl, pend_l, norm1_g[i], mod_l[0], mod_l[1], row0=0, shared=None)
        (u_all, kv_all, kr_all), q_c, hz_c, xc = proj(xc, pend_c, norm1_g[i], mod_c[0], mod_c[1], row0=n, shared=seq)
        s5_all = _s5_mixer(u_all, n_ctx, s5_lambda_re[i], s5_lambda_im[i], s5_log_dt[i], s5_b_re[i], s5_b_im[i],
                           s5_c_re[i], s5_c_im[i], s5_d[i], s5_glu_w[i])
        mla_l = _attention(q_l, kv_all, kr_all, 0, n + n_ctx)
        filt = (hy_f_w1[i], hy_f_b1[i], hy_f_w2[i], hy_f_b2[i], hy_f_w3[i], hy_f_b3[i], hy_f_freq[i])
        hy_l = _hyena_mixer(hz_l, hy_short_w[i], hy_short_b[i], filt, hy_bias[i])
        w_route = jnp.pad(jnp.concatenate([moe_w_group[i], moe_w_expert[i]], axis=1),
                          ((0, 0), (0, LANES - MOE_GROUPS - MOE_EXPERTS)))
        n_tok = bsz * (n + n_ctx) if ctx_out else bsz * n
        xl, moe_in = _mix_out(s5_all, mla_l, hy_l, xl, mix_norm_g[i], w_out[i], mod_l[2], norm2_g[i],
                              mod_l[3], mod_l[4], w_route, row0=0, tok0=0, n_tok=n_tok, shared=None)
        if ctx_out:
            mla_c = _attention(q_c, kv_all, kr_all, n, n_ctx)
            hy_c = _hyena_mixer(hz_c, hy_short_w[i], hy_short_b[i], filt, hy_bias[i])
            xc, moe_in = _mix_out(s5_all, mla_c, hy_c, xc, mix_norm_g[i], w_out[i], mod_c[2], norm2_g[i],
                                  mod_c[3], mod_c[4], w_route, row0=n, tok0=bsz * n, n_tok=n_tok, shared=moe_in)
        y0, y1 = _hier_moe(*moe_in, moe_w1, moe_w3, moe_w2, i)
        pend_l, pend_c = (y0, y1, mod_l[5], 0), (y0, y1, mod_c[5], bsz * n)
    return _final_norm(xl, pend_l, final_g)
```

```python
import functools
import math

import jax
import jax.numpy as jnp
from jax import lax
from jax.experimental import pallas as pl
from jax.experimental.pallas import tpu as pltpu

D_MODEL = 1024
DEPTH = 4
GRID_W = 64
EPS = 1e-6

MIX_W = D_MODEL
S5_W = D_MODEL // 4
S5_GROUP = 16
S5_GROUPS = S5_W // S5_GROUP
S5_STATE = 64
S5_N = S5_GROUPS * S5_STATE
MLA_V = 64
MLA_W = D_MODEL // 2
MLA_HEADS = MLA_W // MLA_V
MLA_NOPE = 64
MLA_ROPE = 32
MLA_QK = MLA_NOPE + MLA_ROPE
MLA_Q_RANK = 384
MLA_KV_RANK = 256
MLA_SCALE = 1.0 / math.sqrt(MLA_NOPE + MLA_ROPE)
ROPE_BASE = 10000.0
HY_W = D_MODEL // 4
HY_ORDER = 2
HY_POS_EMB = 33
HY_FILTER_W = 64
HY_MIN_DECAY = math.log(1e-2) / 1.5
HY_MAX_DECAY = math.log(1e-2) / 0.3
MOE_GROUPS = 4
MOE_PER_GROUP = 8
MOE_EXPERTS = MOE_GROUPS * MOE_PER_GROUP
MOE_TOP_K = 2
MOE_HIDDEN = 512
MOE_BLOCK = 256

SUBLANES = 8
S5_CHUNK = 128
MM_ROWS = 256
ATTN_Q_BLOCK = 256
MIX_ROWS = 512
LANES = 128
ATTN_HEADS_PER_STEP = 4
VMEM_LIMIT = 48 * 1024 * 1024

F32 = jnp.float32
BF16 = jnp.bfloat16


def _rms_rows(v):
    return v * lax.rsqrt(jnp.mean(v * v, axis=-1, keepdims=True) + EPS)


def _s5_matrices(lam_re, lam_im, log_dt, b_re, b_im, c_re, c_im):
    g, p, h = S5_GROUPS, S5_STATE, S5_GROUP
    dt = jnp.exp(log_dt)[..., None]
    mag = jnp.exp(lam_re * dt)
    ar, ai = mag * jnp.cos(lam_im * dt), mag * jnp.sin(lam_im * dt)
    den = lam_re * lam_re + lam_im * lam_im
    fr = ((ar - 1.0) * lam_re + ai * lam_im) / den
    fi = (ai * lam_re - (ar - 1.0) * lam_im) / den
    bbr = fr[..., None] * b_re - fi[..., None] * b_im
    bbi = fr[..., None] * b_im + fi[..., None] * b_re
    eye = jnp.eye(g, dtype=F32)

    def block_in(m):
        return jnp.einsum('kgph,gj->kghjp', m, eye).reshape(2, g * h, g * p)

    def block_out(m):
        return jnp.einsum('kghp,gj->kgpjh', m, eye).reshape(2, g * p, g * h)

    bmat = jnp.concatenate([block_in(bbr), block_in(bbi)], axis=-1)
    cmat = jnp.concatenate([block_out(c_re), -block_out(c_im)], axis=1)
    lam = jnp.stack([ar[0], ai[0], ar[1], ai[1]]).reshape(4, 1, g * p)
    lam = jnp.broadcast_to(lam, (4, SUBLANES, g * p)).reshape(4 * SUBLANES, g * p)
    return bmat.astype(BF16), cmat.astype(BF16), lam


def _s5_kernel(uf_ref, ub_ref, bmat_ref, cmat_ref, lam_ref, yf_ref, yb_ref, buf_f, buf_b, state):
    rows = uf_ref.shape[0]
    steps = rows // SUBLANES
    n = S5_N
    s = SUBLANES

    @pl.when(pl.program_id(0) == 0)
    def _():
        state[...] = jnp.zeros_like(state)

    def drive(r, carry):
        r0 = pl.multiple_of(r * MM_ROWS, MM_ROWS)
        buf_f[pl.ds(r0, MM_ROWS), :] = jnp.dot(uf_ref[pl.ds(r0, MM_ROWS), :].astype(BF16), bmat_ref[0],
                                               preferred_element_type=F32)
        buf_b[pl.ds(r0, MM_ROWS), :] = jnp.dot(ub_ref[pl.ds(r0, MM_ROWS), :].astype(BF16), bmat_ref[1],
                                               preferred_element_type=F32)
        return carry

    lax.fori_loop(0, rows // MM_ROWS, drive, 0)

    def step(j, carry):
        fr, fi, br, bi = carry
        rf = pl.multiple_of(j * s, s)
        rb = pl.multiple_of((steps - 1 - j) * s, s)
        lfr, lfi = lam_ref[0:s, :], lam_ref[s:2 * s, :]
        lbr, lbi = lam_ref[2 * s:3 * s, :], lam_ref[3 * s:4 * s, :]
        nfr = lfr * fr - lfi * fi + buf_f[pl.ds(rf, s), 0:n]
        nfi = lfr * fi + lfi * fr + buf_f[pl.ds(rf, s), n:2 * n]
        nbr = lbr * br - lbi * bi + buf_b[pl.ds(rb, s), 0:n]
        nbi = lbr * bi + lbi * br + buf_b[pl.ds(rb, s), n:2 * n]
        buf_f[pl.ds(rf, s), 0:n] = nfr
        buf_f[pl.ds(rf, s), n:2 * n] = nfi
        buf_b[pl.ds(rb, s), 0:n] = nbr
        buf_b[pl.ds(rb, s), n:2 * n] = nbi
        return nfr, nfi, nbr, nbi

    init = (state[0:s, :], state[s:2 * s, :], state[2 * s:3 * s, :], state[3 * s:4 * s, :])
    fr, fi, br, bi = lax.fori_loop(0, steps, step, init, unroll=2)
    state[0:s, :] = fr
    state[s:2 * s, :] = fi
    state[2 * s:3 * s, :] = br
    state[3 * s:4 * s, :] = bi

    def readout(r, carry):
        r0 = pl.multiple_of(r * MM_ROWS, MM_ROWS)
        yf_ref[pl.ds(r0, MM_ROWS), :] = jnp.dot(buf_f[pl.ds(r0, MM_ROWS), :].astype(BF16), cmat_ref[0],
                                                preferred_element_type=F32)
        yb_ref[pl.ds(r0, MM_ROWS), :] = jnp.dot(buf_b[pl.ds(r0, MM_ROWS), :].astype(BF16), cmat_ref[1],
                                                preferred_element_type=F32)
        return carry

    lax.fori_loop(0, rows // MM_ROWS, readout, 0)


def _s5_scan(u_tm, bmat, cmat, lam, n_ctx_chunks, chunk):
    rows_all, w = u_tm.shape
    rows = chunk * SUBLANES
    n_chunks = rows_all // rows
    n_lat_chunks = n_chunks - n_ctx_chunks
    n2 = 2 * S5_N

    def fwd_chunk(i):
        return (i + n_lat_chunks) % n_chunks

    def bwd_chunk(i):
        return n_chunks - 1 - i

    return pl.pallas_call(
        _s5_kernel,
        grid=(n_chunks,),
        in_specs=[
            pl.BlockSpec((rows, w), lambda i: (fwd_chunk(i), 0)),
            pl.BlockSpec((rows, w), lambda i: (bwd_chunk(i), 0)),
            pl.BlockSpec((2, w, n2), lambda i: (0, 0, 0)),
            pl.BlockSpec((2, n2, w), lambda i: (0, 0, 0)),
            pl.BlockSpec((4 * SUBLANES, S5_N), lambda i: (0, 0)),
        ],
        out_specs=[
            pl.BlockSpec((rows, w), lambda i: (fwd_chunk(i), 0)),
            pl.BlockSpec((rows, w), lambda i: (bwd_chunk(i), 0)),
        ],
        out_shape=[jax.ShapeDtypeStruct((rows_all, w), F32)] * 2,
        scratch_shapes=[
            pltpu.VMEM((rows, n2), F32),
            pltpu.VMEM((rows, n2), F32),
            pltpu.VMEM((4 * SUBLANES, S5_N), F32),
        ],
        compiler_params=pltpu.CompilerParams(dimension_semantics=("arbitrary",), vmem_limit_bytes=VMEM_LIMIT),
        name="s5_scan",
    )(u_tm, u_tm, bmat, cmat, lam)


def _s5_mixer(u_tm, n_ctx, lam_re, lam_im, log_dt, b_re, b_im, c_re, c_im, d, glu_w):
    n_all, bw = u_tm.shape
    bsz, w = bw // S5_W, S5_W
    assert bsz == SUBLANES and n_all % S5_CHUNK == 0 and n_ctx % S5_CHUNK == 0
    bmat, cmat, lam = _s5_matrices(lam_re, lam_im, log_dt, b_re, b_im, c_re, c_im)
    u_rows = u_tm.reshape(n_all * bsz, w)
    y_f, y_b = _s5_scan(u_rows, bmat, cmat, lam, n_ctx // S5_CHUNK, S5_CHUNK)
    y = y_f + y_b + d * u_rows
    y = jax.nn.gelu(y)
    y = y * jax.nn.sigmoid(y @ glu_w)
    return y.reshape(n_all, bw)


def _axial_rope_tables(n_tokens):
    rows = n_tokens // GRID_W
    row = jnp.repeat(jnp.arange(rows), GRID_W).astype(F32)
    col = jnp.tile(jnp.arange(GRID_W), rows).astype(F32)
    half = MLA_ROPE // 2
    inv = ROPE_BASE ** (-jnp.arange(0, half, 2, dtype=F32) / half)
    ang_r = row[:, None] * inv
    ang_c = col[:, None] * inv
    cos = jnp.concatenate([jnp.cos(ang_r)] * 2 + [jnp.cos(ang_c)] * 2, axis=-1)
    sin = jnp.concatenate([jnp.sin(ang_r)] * 2 + [jnp.sin(ang_c)] * 2, axis=-1)
    return cos, sin


def _rope_partner(w):
    q = MLA_ROPE // 4
    parts = []
    for g in range(2):
        x1, x2 = w[..., 2 * g * q:(2 * g + 1) * q], w[..., (2 * g + 1) * q:(2 * g + 2) * q]
        parts += [-x2, x1]
    return jnp.concatenate(parts, axis=-1)


def _attn_kernel(q_ref, kv_ref, kr_ref, o_ref, kcat, vaug):
    hs, nk = kcat.shape[0], kcat.shape[1]
    dn = MLA_NOPE

    @pl.when(pl.program_id(2) == 0)
    def _():
        lane = lax.broadcasted_iota(jnp.int32, (nk, LANES), 1)
        ones_col = jnp.where(lane == 0, 1.0, 0.0).astype(BF16)
        for h in range(hs):
            kvh = kv_ref[0, :, h * LANES:(h + 1) * LANES]
            kcat[h] = jnp.where(lane < dn, kvh, kr_ref[0])
            vaug[h] = jnp.where(lane >= dn, kvh, ones_col)

    outs = []
    for h in range(hs):
        s = lax.dot_general(q_ref[0, :, h * LANES:(h + 1) * LANES], kcat[h], (((1,), (1,)), ((), ())),
                            preferred_element_type=F32)
        p = jnp.exp(s - jnp.max(s, axis=-1, keepdims=True))
        o = jnp.dot(p.astype(BF16), vaug[h], preferred_element_type=F32)
        outs.append(o[:, dn:] / o[:, 0:1])
    o_ref[0] = jnp.concatenate(outs, axis=-1)


def _attention(q, kv, kr, key_row0, nk):
    bsz, nq, _ = q.shape
    heads, dv = MLA_HEADS, MLA_V
    tq = min(ATTN_Q_BLOCK, nq)
    hs = ATTN_HEADS_PER_STEP
    kb = key_row0 // nk
    return pl.pallas_call(
        _attn_kernel,
        grid=(bsz, heads // hs, nq // tq),
        in_specs=[
            pl.BlockSpec((1, tq, hs * LANES), lambda b, h, i: (b, i, h)),
            pl.BlockSpec((1, nk, hs * LANES), lambda b, h, i: (b, kb, h), pipeline_mode=pl.Buffered(1)),
            pl.BlockSpec((1, nk, LANES), lambda b, h, i: (b, kb, 0), pipeline_mode=pl.Buffered(1)),
        ],
        out_specs=pl.BlockSpec((1, tq, hs * dv), lambda b, h, i: (b, i, h)),
        out_shape=jax.ShapeDtypeStruct((bsz, nq, heads * dv), F32),
        scratch_shapes=[pltpu.VMEM((hs, nk, LANES), BF16), pltpu.VMEM((hs, nk, LANES), BF16)],
        compiler_params=pltpu.CompilerParams(
            dimension_semantics=("arbitrary", "arbitrary", "arbitrary"), vmem_limit_bytes=VMEM_LIMIT),
        name="mla_attention",
    )(q, kv, kr)


PIN_U = 0
PIN_CQ = PIN_U + S5_W
PIN_CKV = PIN_CQ + MLA_Q_RANK
PIN_HZ = PIN_CKV + MLA_KV_RANK
PIN_KR = PIN_HZ + 3 * HY_W
PIN_KRP = PIN_KR + LANES
PIN_COLS = PIN_KRP + LANES


def _proj_in_weights(w_in, w_uq):
    o1 = S5_W
    o2 = o1 + MLA_Q_RANK
    o3 = o2 + MLA_KV_RANK
    o4 = o3 + MLA_ROPE
    pad = LANES - MLA_QK
    w_kr = w_in[:, o3:o4]
    in_slot = lambda w: jnp.pad(w, ((0, 0), (MLA_NOPE, pad)))
    w_ext = jnp.concatenate([w_in[:, :o3], w_in[:, o4:], in_slot(w_kr), in_slot(_rope_partner(w_kr))], axis=1)
    w_h = w_uq.reshape(MLA_Q_RANK, MLA_HEADS, MLA_QK)
    w_rot = jnp.concatenate([jnp.zeros_like(w_h[..., :MLA_NOPE]), _rope_partner(w_h[..., MLA_NOPE:])], axis=-1)
    w_q = jnp.concatenate([jnp.pad(w, ((0, 0), (0, 0), (0, pad))).reshape(MLA_Q_RANK, MLA_HEADS * LANES)
                           for w in (w_h, w_rot)], axis=-1)
    return w_ext.astype(BF16), w_q.astype(BF16)


def _rope_slot_tables(n, n_ctx):
    cos, sin = _axial_rope_tables(n)
    pad = LANES - MLA_QK
    cos = jnp.concatenate([jnp.ones((n, MLA_NOPE), F32), cos, jnp.ones((n, pad), F32)], axis=-1)
    sin = jnp.concatenate([jnp.zeros((n, MLA_NOPE), F32), sin, jnp.zeros((n, pad), F32)], axis=-1)
    return (jnp.concatenate([cos, jnp.ones((n_ctx, LANES), F32)], axis=0),
            jnp.concatenate([sin, jnp.zeros((n_ctx, LANES), F32)], axis=0))


def _apply_pending(x_ref, pending_refs):
    y0_ref, y1_ref, gate_ref = pending_refs
    return x_ref[0] + gate_ref[0] * (y0_ref[...] + y1_ref[...])


def _pending_specs(pending, tm, blocks_per_batch, d):
    if pending is None:
        return [], []
    y0, y1, gate, row0 = pending
    blk0 = row0 // tm
    rows = pl.BlockSpec((tm, d), lambda b, i: (blk0 + b * blocks_per_batch + i, 0))
    return [rows, rows, pl.BlockSpec((1, 1, d), lambda b, i: (b, 0, 0))], [y0, y1, gate]


def _proj_in_kernel(*refs, has_pending, n_shared):
    if has_pending:
        x_ref, *pending_refs = refs[:4]
        refs = refs[4:]
    else:
        x_ref, refs = refs[0], refs[1:]
    g_ref, shift_ref, scale_ref, w_ref, qg_ref, wq_ref, kvg_ref, wkv_ref, cos_ref, sin_ref = refs[:10]
    u_ref, q_ref, kv_ref, kr_ref, hz_ref, *xo_ref = refs[10 + n_shared:]
    if has_pending:
        x = _apply_pending(x_ref, pending_refs)
        xo_ref[0][0] = x
    else:
        x = x_ref[0]
    h = ((_rms_rows(x) * g_ref[...]) * (1.0 + scale_ref[0]) + shift_ref[0]).astype(BF16)
    p = jnp.dot(h, w_ref[...], preferred_element_type=F32)
    u_ref[...] = p[:, PIN_U:PIN_CQ]
    hz_ref[0] = p[:, PIN_HZ:PIN_KR]
    cos, sin = cos_ref[...], sin_ref[...]
    kr_ref[0] = (p[:, PIN_KR:PIN_KRP] * cos + p[:, PIN_KRP:PIN_COLS] * sin).astype(BF16)
    cq = (_rms_rows(p[:, PIN_CQ:PIN_CKV]) * qg_ref[...]).astype(BF16)
    q2 = jnp.dot(cq, wq_ref[...], preferred_element_type=F32)
    hw = MLA_HEADS * LANES
    for hd in range(MLA_HEADS):
        a, b = q2[:, hd * LANES:(hd + 1) * LANES], q2[:, hw + hd * LANES:hw + (hd + 1) * LANES]
        q_ref[0, :, hd * LANES:(hd + 1) * LANES] = ((a * cos + b * sin) * MLA_SCALE).astype(BF16)
    ckv = (_rms_rows(p[:, PIN_CKV:PIN_HZ]) * kvg_ref[...]).astype(BF16)
    kv_ref[0] = jnp.dot(ckv, wkv_ref[...], preferred_element_type=F32).astype(BF16)


def _proj_in(x, pending, norm_g, shift, scale, w_ext, q_norm_g, w_q, kv_norm_g, w_ukv, cos, sin, row0, shared):
    bsz, n, d = x.shape
    n_all = cos.shape[0]
    tm = min(MIX_ROWS, n)
    blk0 = row0 // tm
    hw = MLA_HEADS * LANES

    def rows(width, first=0):
        return pl.BlockSpec((1, tm, width), lambda b, i: (b, first + i, 0))

    per_batch = pl.BlockSpec((1, 1, d), lambda b, i: (b, 0, 0))
    table = pl.BlockSpec((tm, LANES), lambda b, i: (blk0 + i, 0))
    pend_specs, pend_args = _pending_specs(pending, tm, n // tm, d)
    has_pending = pending is not None
    out_specs = [pl.BlockSpec((tm, S5_W), lambda b, i: (blk0 + i, b)), rows(hw), rows(hw, blk0), rows(LANES, blk0),
                 rows(3 * HY_W)]
    out_shape = [jax.ShapeDtypeStruct((n_all, bsz * S5_W), F32), jax.ShapeDtypeStruct((bsz, n, hw), BF16),
                 jax.ShapeDtypeStruct((bsz, n_all, hw), BF16), jax.ShapeDtypeStruct((bsz, n_all, LANES), BF16),
                 jax.ShapeDtypeStruct((bsz, n, 3 * HY_W), F32)]
    if has_pending:
        out_specs.append(rows(d))
        out_shape.append(jax.ShapeDtypeStruct((bsz, n, d), F32))
    n_in = 1 + len(pend_args) + 10
    shared = list(shared or [jnp.zeros(out_shape[k].shape, out_shape[k].dtype) for k in (0, 2, 3)])
    aliases = {n_in + k: out for k, out in enumerate((0, 2, 3))}
    outs = pl.pallas_call(
        functools.partial(_proj_in_kernel, has_pending=has_pending, n_shared=len(shared)),
        grid=(bsz, n // tm),
        in_specs=[rows(d), *pend_specs, _const_spec((1, d)), per_batch, per_batch, _const_spec((d, PIN_COLS)),
                  _const_spec((1, MLA_Q_RANK)), _const_spec((MLA_Q_RANK, 2 * hw)),
                  _const_spec((1, MLA_KV_RANK)), _const_spec((MLA_KV_RANK, hw)), table, table,
                  *[pl.BlockSpec(memory_space=pl.ANY)] * len(shared)],
        out_specs=out_specs,
        out_shape=out_shape,
        input_output_aliases=aliases,
        compiler_params=pltpu.CompilerParams(dimension_semantics=("arbitrary", "arbitrary"),
                                             vmem_limit_bytes=VMEM_LIMIT),
        name="proj_in",
    )(x, *pend_args, norm_g.reshape(1, d), shift, scale, w_ext, q_norm_g.reshape(1, -1), w_q,
      kv_norm_g.reshape(1, -1), w_ukv.astype(BF16), cos, sin, *shared)
    return (outs[0], outs[2], outs[3]), outs[1], outs[4], (outs[5] if has_pending else x)


def _short_conv(z, w, b):
    n = z.shape[1]
    zp = jnp.pad(z, ((0, 0), (1, 1), (0, 0)))
    return zp[:, :n] * w[0] + zp[:, 1:n + 1] * w[1] + zp[:, 2:] * w[2] + b


def _hyena_filter_taps(n, w1, b1, w2, b2, w3, b3, freq):
    t = jnp.linspace(0.0, 1.0, n, dtype=F32)[:, None]
    bands = (HY_POS_EMB - 1) // 2
    f = jnp.linspace(1e-4, bands - 1, bands, dtype=F32)
    ang = (2.0 * math.pi * jnp.arange(n, dtype=F32) / n)[:, None] * f
    z = jnp.concatenate([t, jnp.cos(ang), -jnp.sin(ang)], axis=-1)
    h = jnp.sin(freq[0] * (z @ w1 + b1))
    h = jnp.sin(freq[1] * (h @ w2 + b2))
    half = HY_ORDER * HY_W
    deltas = jnp.tile(jnp.abs(jnp.linspace(HY_MIN_DECAY, HY_MAX_DECAY, HY_W, dtype=F32)), HY_ORDER)
    fwd = (h @ w3[:, :half] + b3[:half]) * jnp.exp(-t * deltas)
    bwd_rev = (h[::-1] @ w3[:, half:] + b3[half:]) * jnp.exp(-t[::-1] * deltas)
    return jnp.concatenate([fwd, jnp.zeros((1, half), F32), bwd_rev[:n - 1]], axis=0)


def _dense_conv_kernel(x_ref, k_ref, ff_ref, g_ref, o_ref):
    n = x_ref.shape[1]
    k = k_ref[...].astype(BF16)
    kr = jnp.dot(ff_ref[0], k, preferred_element_type=F32)
    ki = jnp.dot(ff_ref[1], k, preferred_element_type=F32)
    x = x_ref[0].astype(BF16)
    yr = jnp.dot(ff_ref[0, :, :n], x, preferred_element_type=F32)
    yi = jnp.dot(ff_ref[1, :, :n], x, preferred_element_type=F32)
    zr, zi = (yr * kr - yi * ki).astype(BF16), (yr * ki + yi * kr).astype(BF16)
    o_ref[0] = jnp.dot(g_ref[0], zr, preferred_element_type=F32) + jnp.dot(g_ref[1], zi, preferred_element_type=F32)


def _dense_long_conv(y, k_time):
    bsz, n, c = y.shape
    nfft = 2 * n
    idx = jnp.arange(nfft, dtype=jnp.int32)
    ang = (2.0 * math.pi / nfft) * ((idx[:, None] * idx[None, :]) % nfft).astype(F32)
    ff = jnp.stack([jnp.cos(ang), -jnp.sin(ang)]).astype(BF16)
    g = (jnp.stack([jnp.cos(ang), -jnp.sin(ang)])[:, :n, :] / nfft).astype(BF16)
    return pl.pallas_call(
        _dense_conv_kernel,
        grid=(bsz,),
        in_specs=[pl.BlockSpec((1, n, c), lambda b: (b, 0, 0)), _const_spec((nfft, c)),
                  _const_spec((2, nfft, nfft)), _const_spec((2, n, nfft))],
        out_specs=pl.BlockSpec((1, n, c), lambda b: (b, 0, 0)),
        out_shape=jax.ShapeDtypeStruct((bsz, n, c), F32),
        compiler_params=pltpu.CompilerParams(dimension_semantics=("arbitrary",)),
        name="hyena_dense_conv",
    )(y, k_time, ff, g)


def _hyena_mixer(z, short_w, short_b, filt, bias):
    n = z.shape[1]
    z = _short_conv(z, short_w, short_b)
    k_time = _hyena_filter_taps(n, *filt)
    if (2 * n) % HY_N2 == 0 and (2 * n) // HY_N2 >= 2 * SUBLANES:
        tables = _dft_tables(2 * n)
        kr, ki = _hyena_spectrum(k_time, tables)
        rows = n // HY_N2
        z = _to_strided(z, rows)
        y = z[..., :HY_W]
        for o in range(HY_ORDER):
            cols = slice(o * HY_W, (o + 1) * HY_W)
            gate = z[..., (o + 1) * HY_W:(o + 2) * HY_W]
            y = gate * (_hyena_long_conv(y, kr[:, cols], ki[:, cols], tables) + y * bias[o])
        return _from_strided(y, rows)
    y = z[..., :HY_W]
    for o in range(HY_ORDER):
        gate = z[..., (o + 1) * HY_W:(o + 2) * HY_W]
        y = gate * (_dense_long_conv(y, k_time[:, o * HY_W:(o + 1) * HY_W]) + y * bias[o])
    return y


HY_N2 = 128
HY_PITCH = 136
HY_UNROLL = 8


def _dft_tables(nfft):
    n1 = nfft // HY_N2
    half = n1 // 2
    k1 = jnp.arange(n1, dtype=jnp.int32)
    n2 = jnp.arange(HY_N2, dtype=jnp.int32)
    t = HY_N2 * k1[None, None, :] + n2[:, None, None]
    ang = (2.0 * math.pi / nfft) * ((k1[None, :, None] * t) % nfft).astype(F32)
    gr, gi = jnp.cos(ang), -jnp.sin(ang)
    g_cplx = jnp.concatenate([jnp.concatenate([gr[..., :half], -gi[..., :half]], -1),
                              jnp.concatenate([gi[..., :half], gr[..., :half]], -1)], axis=1)
    g_real = jnp.concatenate([gr, gi], axis=1)
    hr = jnp.cos(ang).swapaxes(1, 2)[:, :half] / nfft
    hi = jnp.sin(ang).swapaxes(1, 2)[:, :half] / nfft
    g_inv = jnp.concatenate([jnp.concatenate([hr, -hi], -1), jnp.concatenate([hi, hr], -1)], axis=1)
    a2 = (2.0 * math.pi / HY_N2) * ((n2[:, None] * n2[None, :]) % HY_N2).astype(F32)
    fr, fi = jnp.cos(a2), -jnp.sin(a2)
    f2 = jnp.concatenate([jnp.concatenate([fr, -fi], -1), jnp.concatenate([fi, fr], -1)], axis=0)
    f2_inv = jnp.concatenate([jnp.concatenate([fr, fi], -1), jnp.concatenate([-fi, fr], -1)], axis=0)
    return dict(n1=n1, g_cplx=g_cplx.astype(BF16), g_real=g_real.astype(BF16), g_inv=g_inv.astype(BF16),
                f2=f2.astype(BF16), f2_inv=f2_inv.astype(BF16))


def _dft_stage1(load_rows, g_ref, s_re, s_im, n1):
    def body(n2, carry):
        a = jnp.dot(g_ref[n2], load_rows(n2).astype(BF16), preferred_element_type=F32)
        s_re[pl.ds(n2, n1, stride=HY_PITCH), :] = a[:n1]
        s_im[pl.ds(n2, n1, stride=HY_PITCH), :] = a[n1:]
        return carry

    lax.fori_loop(0, HY_N2, body, 0, unroll=HY_UNROLL)


def _dft_stage2(f2_ref, s_re, s_im, k1):
    r0 = pl.multiple_of(k1 * HY_PITCH, SUBLANES)
    sl = jnp.concatenate([s_re[pl.ds(r0, HY_N2), :], s_im[pl.ds(r0, HY_N2), :]], axis=0).astype(BF16)
    x = jnp.dot(f2_ref[...], sl, preferred_element_type=F32)
    return r0, x[:HY_N2], x[HY_N2:]


def _hyena_spectrum_kernel(x_ref, g_ref, f2_ref, kr_ref, ki_ref, s_re, s_im, *, n1):
    _dft_stage1(lambda n2: x_ref[pl.ds(n2, n1, stride=HY_PITCH), :], g_ref, s_re, s_im, n1)

    def body(k1, carry):
        _, xr, xi = _dft_stage2(f2_ref, s_re, s_im, k1)
        q0 = pl.multiple_of(k1 * HY_N2, HY_N2)
        kr_ref[pl.ds(q0, HY_N2), :] = xr
        ki_ref[pl.ds(q0, HY_N2), :] = xi
        return carry

    lax.fori_loop(0, n1, body, 0)


def _hyena_conv_kernel(x_ref, g_ref, f2_ref, f2i_ref, gi_ref, kr_ref, ki_ref, y_ref, s_re, s_im, *, n1):
    half = n1 // 2

    def load_rows(n2):
        return jnp.concatenate([x_ref[0, pl.ds(n2, half, stride=HY_PITCH), :],
                                x_ref[1, pl.ds(n2, half, stride=HY_PITCH), :]], axis=0)

    _dft_stage1(load_rows, g_ref, s_re, s_im, n1)

    def spectrum_product(k1, carry):
        r0, xr, xi = _dft_stage2(f2_ref, s_re, s_im, k1)
        q0 = pl.multiple_of(k1 * HY_N2, HY_N2)
        kr, ki = kr_ref[pl.ds(q0, HY_N2), :], ki_ref[pl.ds(q0, HY_N2), :]
        y = jnp.concatenate([xr * kr - xi * ki, xr * ki + xi * kr], axis=0).astype(BF16)
        b = jnp.dot(f2i_ref[...], y, preferred_element_type=F32)
        s_re[pl.ds(r0, HY_N2), :] = b[:HY_N2]
        s_im[pl.ds(r0, HY_N2), :] = b[HY_N2:]
        return carry

    lax.fori_loop(0, n1, spectrum_product, 0, unroll=HY_UNROLL // 2)

    y_ref[...] = jnp.zeros_like(y_ref)

    def inverse_stage1(n2, carry):
        bs = jnp.concatenate([s_re[pl.ds(n2, n1, stride=HY_PITCH), :],
                              s_im[pl.ds(n2, n1, stride=HY_PITCH), :]], axis=0).astype(BF16)
        y = jnp.dot(gi_ref[n2], bs, preferred_element_type=F32)
        y_ref[0, pl.ds(n2, half, stride=HY_PITCH), :] = y[:half]
        y_ref[1, pl.ds(n2, half, stride=HY_PITCH), :] = y[half:]
        return carry

    lax.fori_loop(0, HY_N2, inverse_stage1, 0, unroll=HY_UNROLL)


def _to_strided(a, rows):
    lead, c = a.shape[:-2], a.shape[-1]
    a = a.reshape(lead + (rows, HY_N2, c))
    a = jnp.pad(a, [(0, 0)] * len(lead) + [(0, 0), (0, HY_PITCH - HY_N2), (0, 0)])
    return a.reshape(lead + (rows * HY_PITCH, c))


def _from_strided(a, rows):
    lead, c = a.shape[:-2], a.shape[-1]
    return a.reshape(lead + (rows, HY_PITCH, c))[..., :HY_N2, :].reshape(lead + (rows * HY_N2, c))


def _const_spec(shape):
    return pl.BlockSpec(shape, lambda *_: (0,) * len(shape), pipeline_mode=pl.Buffered(1))


def _hyena_spectrum(k_time, tables):
    nfft, c = k_time.shape
    n1 = tables["n1"]
    spec_out = pl.BlockSpec((nfft, LANES), lambda j: (0, j))
    return pl.pallas_call(
        functools.partial(_hyena_spectrum_kernel, n1=n1),
        grid=(c // LANES,),
        in_specs=[pl.BlockSpec((n1 * HY_PITCH, LANES), lambda j: (0, j)),
                  _const_spec((HY_N2, 2 * n1, n1)), _const_spec((2 * HY_N2, 2 * HY_N2))],
        out_specs=[spec_out, spec_out],
        out_shape=[jax.ShapeDtypeStruct((nfft, c), F32)] * 2,
        scratch_shapes=[pltpu.VMEM((n1 * HY_PITCH, LANES), F32)] * 2,
        compiler_params=pltpu.CompilerParams(dimension_semantics=("arbitrary",), vmem_limit_bytes=VMEM_LIMIT),
        name="hyena_spectrum",
    )(_to_strided(k_time, n1), tables["g_real"], tables["f2"])


def _hyena_long_conv(y, kr, ki, tables):
    bsz, r, c = y.shape
    n1 = tables["n1"]
    pairs = bsz // 2
    spec_x = pl.BlockSpec((None, 2, r, LANES), lambda j, p: (p, 0, 0, j))
    spec_k = pl.BlockSpec((n1 * HY_N2, LANES), lambda j, p: (0, j), pipeline_mode=pl.Buffered(1))
    out = pl.pallas_call(
        functools.partial(_hyena_conv_kernel, n1=n1),
        grid=(c // LANES, pairs),
        in_specs=[spec_x,
                  _const_spec((HY_N2, 2 * n1, n1)), _const_spec((2 * HY_N2, 2 * HY_N2)),
                  _const_spec((2 * HY_N2, 2 * HY_N2)), _const_spec((HY_N2, n1, 2 * n1)),
                  spec_k, spec_k],
        out_specs=spec_x,
        out_shape=jax.ShapeDtypeStruct((pairs, 2, r, c), F32),
        scratch_shapes=[pltpu.VMEM((n1 * HY_PITCH, LANES), F32)] * 2,
        compiler_params=pltpu.CompilerParams(dimension_semantics=("arbitrary", "arbitrary"),
                                             vmem_limit_bytes=VMEM_LIMIT),
        name="hyena_conv",
    )(y.reshape(pairs, 2, r, c), tables["g_cplx"], tables["f2"], tables["f2_inv"], tables["g_inv"], kr, ki)
    return out.reshape(bsz, r, c)


HIGH_HALF = -(1 << 16)


def _pack_bf16_pairs(f):
    half = f.shape[1] // 2
    bits = lax.bitcast_convert_type(f.astype(F32), jnp.int32)
    return lax.shift_right_logical(bits[:, :half], 16) | (bits[:, half:] & HIGH_HALF)


def _unpack_bf16_pairs(w):
    lo = lax.bitcast_convert_type(lax.shift_left(w, 16), F32)
    hi = lax.bitcast_convert_type(w & HIGH_HALF, F32)
    return jnp.concatenate([lo, hi], axis=-1).astype(BF16)


def _mix_out_kernel(s5_ref, mla_ref, hy_ref, x_ref, g_ref, w_ref, gate_ref, n2g_ref, shift_ref, scale_ref, wr_ref,
                    *rest):
    xo_ref, f_ref, lg_ref = rest[-3:]
    c1, c2 = S5_W, S5_W + MLA_W
    y = jnp.dot((_rms_rows(s5_ref[...]) * g_ref[:, :c1]).astype(BF16), w_ref[:c1, :], preferred_element_type=F32)
    y += jnp.dot((_rms_rows(mla_ref[0]) * g_ref[:, c1:c2]).astype(BF16), w_ref[c1:c2, :], preferred_element_type=F32)
    y += jnp.dot((_rms_rows(hy_ref[0]) * g_ref[:, c2:]).astype(BF16), w_ref[c2:, :], preferred_element_type=F32)
    x_new = x_ref[0] + gate_ref[0] * y
    xo_ref[0] = x_new
    f = ((_rms_rows(x_new) * n2g_ref[...]) * (1.0 + scale_ref[0]) + shift_ref[0]).astype(BF16)
    f_ref[...] = _pack_bf16_pairs(f)
    lg_ref[...] = jnp.dot(f, wr_ref[...], preferred_element_type=F32)


def _mix_out(y_s5, y_mla, y_hy, x, mix_g, w_out, gate, norm2_g, shift, scale, w_route, row0, tok0, n_tok, shared):
    bsz, n, d = x.shape
    tm = min(MIX_ROWS, n)
    s5_blk0, tok_blk0, per_batch_blocks = row0 // tm, tok0 // tm, n // tm

    def rows(width):
        return pl.BlockSpec((1, tm, width), lambda b, i: (b, i, 0))

    def tokens(width):
        return pl.BlockSpec((tm, width), lambda b, i: (tok_blk0 + b * per_batch_blocks + i, 0))

    per_batch = pl.BlockSpec((1, 1, d), lambda b, i: (b, 0, 0))
    if shared is None:
        shared = () if n_tok == bsz * n else (jnp.zeros((n_tok, d // 2), jnp.int32), jnp.zeros((n_tok, LANES), F32))
    shared = list(shared)
    outs = pl.pallas_call(
        _mix_out_kernel,
        grid=(bsz, n // tm),
        in_specs=[pl.BlockSpec((tm, S5_W), lambda b, i: (s5_blk0 + i, b)), rows(MLA_W), rows(HY_W), rows(d),
                  _const_spec((1, d)), _const_spec((d, d)),
                  per_batch, _const_spec((1, d)), per_batch, per_batch, _const_spec((d, LANES)),
                  *[pl.BlockSpec(memory_space=pl.ANY)] * len(shared)],
        out_specs=[rows(d), tokens(d // 2), tokens(LANES)],
        out_shape=[jax.ShapeDtypeStruct((bsz, n, d), F32), jax.ShapeDtypeStruct((n_tok, d // 2), jnp.int32),
                   jax.ShapeDtypeStruct((n_tok, LANES), F32)],
        input_output_aliases={11 + k: 1 + k for k in range(len(shared))},
        compiler_params=pltpu.CompilerParams(dimension_semantics=("arbitrary", "arbitrary"),
                                             vmem_limit_bytes=VMEM_LIMIT),
        name="mix_out",
    )(y_s5, y_mla, y_hy, x, mix_g.reshape(1, d), w_out.astype(BF16), gate, norm2_g.reshape(1, d), shift, scale,
      w_route.astype(BF16), *shared)
    return outs[0], (outs[1], outs[2])


def _moe_kernel(be_ref, used_ref, x_ref, w1_ref, w3_ref, w2_ref, rw_ref, o_ref, w13_s, w2_s):
    i = pl.program_id(0)
    hid = w2_s.shape[0]

    @pl.when(jnp.logical_or(i == 0, be_ref[i] != be_ref[jnp.maximum(i - 1, 0)]))
    def _():
        w13_s[:, :hid] = w1_ref[0, 0].astype(BF16)
        w13_s[:, hid:] = w3_ref[0, 0].astype(BF16)
        w2_s[...] = w2_ref[0, 0].astype(BF16)

    @pl.when(i < used_ref[0])
    def _():
        ab = jnp.dot(_unpack_bf16_pairs(x_ref[...]), w13_s[...], preferred_element_type=F32)
        a, b = ab[:, :hid], ab[:, hid:]
        h = (a * jax.nn.sigmoid(a)) * b
        y = jnp.dot(h.astype(BF16), w2_s[...], preferred_element_type=F32)
        rw = rw_ref[...]
        for c0 in range(0, y.shape[1], LANES):
            o_ref[:, c0:c0 + LANES] = y[:, c0:c0 + LANES] * rw

    @pl.when(i >= used_ref[0])
    def _():
        o_ref[...] = jnp.zeros_like(o_ref)


def _moe_experts(xg, block_e, n_used, w1, w3, w2, layer, row_w):
    n_rows = xg.shape[0]
    d, hid = w1.shape[2], w1.shape[3]
    n_blocks = n_rows // MOE_BLOCK
    return pl.pallas_call(
        _moe_kernel,
        grid_spec=pltpu.PrefetchScalarGridSpec(
            num_scalar_prefetch=2,
            grid=(n_blocks,),
            in_specs=[
                pl.BlockSpec((MOE_BLOCK, d // 2), lambda i, be, nu: (i, 0)),
                pl.BlockSpec((1, 1, d, hid), lambda i, be, nu: (layer, be[i], 0, 0)),
                pl.BlockSpec((1, 1, d, hid), lambda i, be, nu: (layer, be[i], 0, 0)),
                pl.BlockSpec((1, 1, hid, d), lambda i, be, nu: (layer, be[i], 0, 0)),
                pl.BlockSpec((MOE_BLOCK, LANES), lambda i, be, nu: (i, 0)),
            ],
            out_specs=pl.BlockSpec((MOE_BLOCK, d), lambda i, be, nu: (i, 0)),
            scratch_shapes=[pltpu.VMEM((d, 2 * hid), BF16), pltpu.VMEM((hid, d), BF16)],
        ),
        out_shape=jax.ShapeDtypeStruct((n_rows, d), F32),
        compiler_params=pltpu.CompilerParams(dimension_semantics=("arbitrary",), vmem_limit_bytes=VMEM_LIMIT),
        name="moe_experts",
    )(block_e, n_used, xg, w1, w3, w2, row_w)


def _hier_moe(h, logits, w1, w3, w2, layer):
    t = h.shape[0]
    g_prob = jax.nn.softmax(logits[:, :MOE_GROUPS], axis=-1)
    g_idx = jnp.argmax(g_prob, axis=-1).astype(jnp.int32)
    g_w = jnp.max(g_prob, axis=-1, keepdims=True)
    e_logits = logits[:, MOE_GROUPS:MOE_GROUPS + MOE_EXPERTS].reshape(t, MOE_GROUPS, MOE_PER_GROUP)
    in_group = jnp.take_along_axis(e_logits, g_idx[:, None, None], axis=1)[:, 0]
    i1 = jnp.argmax(in_group, axis=-1).astype(jnp.int32)
    rest = jnp.where(jnp.arange(MOE_PER_GROUP, dtype=jnp.int32)[None, :] == i1[:, None], -jnp.inf, in_group)
    i2 = jnp.argmax(rest, axis=-1).astype(jnp.int32)
    top_v = jnp.stack([jnp.max(in_group, axis=-1), jnp.max(rest, axis=-1)], axis=-1)
    top_i = jnp.stack([i1, i2], axis=-1)
    gate = jax.nn.softmax(top_v, axis=-1) * g_w
    eid = g_idx[:, None] * MOE_PER_GROUP + top_i
    n_assign = t * MOE_TOP_K
    flat_e = eid.reshape(n_assign)
    flat_w = gate.reshape(n_assign)
    idx_bits = max(n_assign - 1, 1).bit_length()
    assert (MOE_EXPERTS << idx_bits) < 2 ** 31
    packed = jnp.sort((flat_e << idx_bits) | jnp.arange(n_assign, dtype=jnp.int32))
    se, order = packed >> idx_bits, packed & ((1 << idx_bits) - 1)
    counts = jnp.sum(flat_e[None, :] == jnp.arange(MOE_EXPERTS, dtype=jnp.int32)[:, None], axis=1, dtype=jnp.int32)
    padded = (counts + MOE_BLOCK - 1) // MOE_BLOCK * MOE_BLOCK
    start = jnp.cumsum(counts) - counts
    pend = jnp.cumsum(padded)
    pstart = pend - padded
    experts = jnp.arange(MOE_EXPERTS, dtype=jnp.int32)[:, None]
    shift = jnp.sum(jnp.where(se[None, :] == experts, (pstart - start)[:, None], 0), axis=0)
    dest = (jnp.arange(n_assign, dtype=jnp.int32) + shift).astype(jnp.int32)
    n_blocks = -(-n_assign // MOE_BLOCK) + MOE_EXPERTS
    n_rows = n_blocks * MOE_BLOCK
    block_start = jnp.arange(n_blocks, dtype=jnp.int32) * MOE_BLOCK
    block_e = jnp.minimum(jnp.sum(pend[None, :] <= block_start[:, None], axis=1), MOE_EXPERTS - 1).astype(jnp.int32)
    off = (block_start - pstart[block_e])[:, None] + jnp.arange(MOE_BLOCK, dtype=jnp.int32)[None, :]
    valid = (off < counts[block_e][:, None]).reshape(n_rows)
    row_asg = order[jnp.where(valid, (start[block_e][:, None] + off).reshape(n_rows), 0)]
    row_tok = jnp.where(valid, row_asg // MOE_TOP_K, 0)
    row_w = jnp.where(valid, flat_w[row_asg], 0.0)
    xg = h[row_tok]
    n_used = (pend[-1:] // MOE_BLOCK).astype(jnp.int32)
    ys = _moe_experts(xg, block_e, n_used, w1, w3, w2, layer, jnp.broadcast_to(row_w[:, None], (n_rows, LANES)))
    _, slot = lax.sort((order, dest), num_keys=1)
    slot = slot.reshape(t, MOE_TOP_K)
    return ys[slot[:, 0]], ys[slot[:, 1]]


def _final_norm_kernel(x_ref, y0_ref, y1_ref, gate_ref, g_ref, o_ref):
    o_ref[0] = _rms_rows(_apply_pending(x_ref, (y0_ref, y1_ref, gate_ref))) * g_ref[...]


def _final_norm(x, pending, g):
    bsz, n, d = x.shape
    tm = min(MIX_ROWS, n)
    rows = pl.BlockSpec((1, tm, d), lambda b, i: (b, i, 0))
    pend_specs, pend_args = _pending_specs(pending, tm, n // tm, d)
    return pl.pallas_call(
        _final_norm_kernel,
        grid=(bsz, n // tm),
        in_specs=[rows, *pend_specs, _const_spec((1, d))],
        out_specs=rows,
        out_shape=jax.ShapeDtypeStruct((bsz, n, d), F32),
        compiler_params=pltpu.CompilerParams(dimension_semantics=("arbitrary", "arbitrary")),
        name="final_norm",
    )(x, *pend_args, g.reshape(1, d))


def kernel(x, c, ctx, c_ctx, ada_w, ada_b, norm1_g, norm2_g, w_in,
           s5_lambda_re, s5_lambda_im, s5_log_dt, s5_b_re, s5_b_im, s5_c_re, s5_c_im, s5_d, s5_glu_w,
           mla_q_norm_g, mla_kv_norm_g, mla_w_uq, mla_w_ukv,
           hy_short_w, hy_short_b, hy_f_w1, hy_f_b1, hy_f_w2, hy_f_b2, hy_f_w3, hy_f_b3, hy_f_freq, hy_bias,
           mix_norm_g, w_out, moe_w_group, moe_w_expert, moe_w1, moe_w3, moe_w2, final_g):
    bsz, n, d = x.shape
    n_ctx = ctx.shape[1]
    xl, xc = x, ctx
    act_l = jax.nn.silu(c)
    act_c = jax.nn.silu(c_ctx)
    rope_cos, rope_sin = _rope_slot_tables(n, n_ctx)
    pend_l = pend_c = None
    for i in range(DEPTH):
        ctx_out = i < DEPTH - 1
        mod_l = jnp.split((act_l @ ada_w[i] + ada_b[i])[:, None, :], 6, axis=-1)
        mod_c = jnp.split((act_c @ ada_w[i] + ada_b[i])[None, None, :], 6, axis=-1)
        mod_c = [jnp.broadcast_to(m, (bsz, 1, d)) for m in mod_c]
        w_ext, w_q = _proj_in_weights(w_in[i], mla_w_uq[i])
        proj = functools.partial(_proj_in, w_ext=w_ext, q_norm_g=mla_q_norm_g[i], w_q=w_q,
                                 kv_norm_g=mla_kv_norm_g[i], w_ukv=mla_w_ukv[i], cos=rope_cos, sin=rope_sin)
        seq, q_l, hz_l, xl = proj(xl, pend_l, norm1_g[i], mod_l[0], mod_l[1], row0=0, shared=None)
        (u_all, kv_all, kr_all), q_c, hz_c, xc = proj(xc, pend_c, norm1_g[i], mod_c[0], mod_c[1], row0=n, shared=seq)
        s5_all = _s5_mixer(u_all, n_ctx, s5_lambda_re[i], s5_lambda_im[i], s5_log_dt[i], s5_b_re[i], s5_b_im[i],
                           s5_c_re[i], s5_c_im[i], s5_d[i], s5_glu_w[i])
        mla_l = _attention(q_l, kv_all, kr_all, 0, n + n_ctx)
        filt = (hy_f_w1[i], hy_f_b1[i], hy_f_w2[i], hy_f_b2[i], hy_f_w3[i], hy_f_b3[i], hy_f_freq[i])
        hy_l = _hyena_mixer(hz_l, hy_short_w[i], hy_short_b[i], filt, hy_bias[i])
        w_route = jnp.pad(jnp.concatenate([moe_w_group[i], moe_w_expert[i]], axis=1),
                          ((0, 0), (0, LANES - MOE_GROUPS - MOE_EXPERTS)))
        n_tok = bsz * (n + n_ctx) if ctx_out else bsz * n
        xl, moe_in = _mix_out(s5_all, mla_l, hy_l, xl, mix_norm_g[i], w_out[i], mod_l[2], norm2_g[i],
                              mod_l[3], mod_l[4], w_route, row0=0, tok0=0, n_tok=n_tok, shared=None)
        if ctx_out:
            mla_c = _attention(q_c, kv_all, kr_all, n, n_ctx)
            hy_c = _hyena_mixer(hz_c, hy_short_w[i], hy_short_b[i], filt, hy_bias[i])
            xc, moe_in = _mix_out(s5_all, mla_c, hy_c, xc, mix_norm_g[i], w_out[i], mod_c[2], norm2_g[i],
                                  mod_c[3], mod_c[4], w_route, row0=n, tok0=bsz * n, n_tok=n_tok, shared=moe_in)
        y0, y1 = _hier_moe(*moe_in, moe_w1, moe_w3, moe_w2, i)
        pend_l, pend_c = (y0, y1, mod_l[5], 0), (y0, y1, mod_c[5], bsz * n)
    return _final_norm(xl, pend_l, final_g)
```

```python
import functools
import math

import jax
import jax.numpy as jnp
from jax import lax
from jax.experimental import pallas as pl
from jax.experimental.pallas import tpu as pltpu

D_MODEL = 1024
DEPTH = 4
GRID_W = 64
EPS = 1e-6

MIX_W = D_MODEL
S5_W = D_MODEL // 4
S5_GROUP = 16
S5_GROUPS = S5_W // S5_GROUP
S5_STATE = 64
S5_N = S5_GROUPS * S5_STATE
MLA_V = 64
MLA_W = D_MODEL // 2
MLA_HEADS = MLA_W // MLA_V
MLA_NOPE = 64
MLA_ROPE = 32
MLA_QK = MLA_NOPE + MLA_ROPE
MLA_Q_RANK = 384
MLA_KV_RANK = 256
MLA_SCALE = 1.0 / math.sqrt(MLA_NOPE + MLA_ROPE)
ROPE_BASE = 10000.0
HY_W = D_MODEL // 4
HY_ORDER = 2
HY_POS_EMB = 33
HY_FILTER_W = 64
HY_MIN_DECAY = math.log(1e-2) / 1.5
HY_MAX_DECAY = math.log(1e-2) / 0.3
MOE_GROUPS = 4
MOE_PER_GROUP = 8
MOE_EXPERTS = MOE_GROUPS * MOE_PER_GROUP
MOE_TOP_K = 2
MOE_HIDDEN = 512
MOE_BLOCK = 256

SUBLANES = 8
S5_CHUNK = 128
MM_ROWS = 256
ATTN_Q_BLOCK = 256
MIX_ROWS = 512
LANES = 128
ATTN_HEADS_PER_STEP = 4
VMEM_LIMIT = 48 * 1024 * 1024

F32 = jnp.float32
BF16 = jnp.bfloat16


def _rms_rows(v):
    return v * lax.rsqrt(jnp.mean(v * v, axis=-1, keepdims=True) + EPS)


def _s5_matrices(lam_re, lam_im, log_dt, b_re, b_im, c_re, c_im):
    g, p, h = S5_GROUPS, S5_STATE, S5_GROUP
    dt = jnp.exp(log_dt)[..., None]
    mag = jnp.exp(lam_re * dt)
    ar, ai = mag * jnp.cos(lam_im * dt), mag * jnp.sin(lam_im * dt)
    den = lam_re * lam_re + lam_im * lam_im
    fr = ((ar - 1.0) * lam_re + ai * lam_im) / den
    fi = (ai * lam_re - (ar - 1.0) * lam_im) / den
    bbr = fr[..., None] * b_re - fi[..., None] * b_im
    bbi = fr[..., None] * b_im + fi[..., None] * b_re
    eye = jnp.eye(g, dtype=F32)

    def block_in(m):
        return jnp.einsum('kgph,gj->kghjp', m, eye).reshape(2, g * h, g * p)

    def block_out(m):
        return jnp.einsum('kghp,gj->kgpjh', m, eye).reshape(2, g * p, g * h)

    bmat = jnp.concatenate([block_in(bbr), block_in(bbi)], axis=-1)
    cmat = jnp.concatenate([block_out(c_re), -block_out(c_im)], axis=1)
    lam = jnp.stack([ar[0], ai[0], ar[1], ai[1]]).reshape(4, 1, g * p)
    lam = jnp.broadcast_to(lam, (4, SUBLANES, g * p)).reshape(4 * SUBLANES, g * p)
    return bmat.astype(BF16), cmat.astype(BF16), lam


def _s5_kernel(uf_ref, ub_ref, bmat_ref, cmat_ref, lam_ref, yf_ref, yb_ref, buf_f, buf_b, state):
    rows = uf_ref.shape[0]
    steps = rows // SUBLANES
    n = S5_N
    s = SUBLANES

    @pl.when(pl.program_id(0) == 0)
    def _():
        state[...] = jnp.zeros_like(state)

    def drive(r, carry):
        r0 = pl.multiple_of(r * MM_ROWS, MM_ROWS)
        buf_f[pl.ds(r0, MM_ROWS), :] = jnp.dot(uf_ref[pl.ds(r0, MM_ROWS), :].astype(BF16), bmat_ref[0],
                                               preferred_element_type=F32)
        buf_b[pl.ds(r0, MM_ROWS), :] = jnp.dot(ub_ref[pl.ds(r0, MM_ROWS), :].astype(BF16), bmat_ref[1],
                                               preferred_element_type=F32)
        return carry

    lax.fori_loop(0, rows // MM_ROWS, drive, 0)

    def step(j, carry):
        fr, fi, br, bi = carry
        rf = pl.multiple_of(j * s, s)
        rb = pl.multiple_of((steps - 1 - j) * s, s)
        lfr, lfi = lam_ref[0:s, :], lam_ref[s:2 * s, :]
        lbr, lbi = lam_ref[2 * s:3 * s, :], lam_ref[3 * s:4 * s, :]
        nfr = lfr * fr - lfi * fi + buf_f[pl.ds(rf, s), 0:n]
        nfi = lfr * fi + lfi * fr + buf_f[pl.ds(rf, s), n:2 * n]
        nbr = lbr * br - lbi * bi + buf_b[pl.ds(rb, s), 0:n]
        nbi = lbr * bi + lbi * br + buf_b[pl.ds(rb, s), n:2 * n]
        buf_f[pl.ds(rf, s), 0:n] = nfr
        buf_f[pl.ds(rf, s), n:2 * n] = nfi
        buf_b[pl.ds(rb, s), 0:n] = nbr
        buf_b[pl.ds(rb, s), n:2 * n] = nbi
        return nfr, nfi, nbr, nbi

    init = (state[0:s, :], state[s:2 * s, :], state[2 * s:3 * s, :], state[3 * s:4 * s, :])
    fr, fi, br, bi = lax.fori_loop(0, steps, step, init, unroll=2)
    state[0:s, :] = fr
    state[s:2 * s, :] = fi
    state[2 * s:3 * s, :] = br
    state[3 * s:4 * s, :] = bi

    def readout(r, carry):
        r0 = pl.multiple_of(r * MM_ROWS, MM_ROWS)
        yf_ref[pl.ds(r0, MM_ROWS), :] = jnp.dot(buf_f[pl.ds(r0, MM_ROWS), :].astype(BF16), cmat_ref[0],
                                                preferred_element_type=F32)
        yb_ref[pl.ds(r0, MM_ROWS), :] = jnp.dot(buf_b[pl.ds(r0, MM_ROWS), :].astype(BF16), cmat_ref[1],
                                                preferred_element_type=F32)
        return carry

    lax.fori_loop(0, rows // MM_ROWS, readout, 0)


def _s5_scan(u_tm, bmat, cmat, lam, n_ctx_chunks, chunk):
    rows_all, w = u_tm.shape
    rows = chunk * SUBLANES
    n_chunks = rows_all // rows
    n_lat_chunks = n_chunks - n_ctx_chunks
    n2 = 2 * S5_N

    def fwd_chunk(i):
        return (i + n_lat_chunks) % n_chunks

    def bwd_chunk(i):
        return n_chunks - 1 - i

    return pl.pallas_call(
        _s5_kernel,
        grid=(n_chunks,),
        in_specs=[
            pl.BlockSpec((rows, w), lambda i: (fwd_chunk(i), 0)),
            pl.BlockSpec((rows, w), lambda i: (bwd_chunk(i), 0)),
            pl.BlockSpec((2, w, n2), lambda i: (0, 0, 0)),
            pl.BlockSpec((2, n2, w), lambda i: (0, 0, 0)),
            pl.BlockSpec((4 * SUBLANES, S5_N), lambda i: (0, 0)),
        ],
        out_specs=[
            pl.BlockSpec((rows, w), lambda i: (fwd_chunk(i), 0)),
            pl.BlockSpec((rows, w), lambda i: (bwd_chunk(i), 0)),
        ],
        out_shape=[jax.ShapeDtypeStruct((rows_all, w), F32)] * 2,
        scratch_shapes=[
            pltpu.VMEM((rows, n2), F32),
            pltpu.VMEM((rows, n2), F32),
            pltpu.VMEM((4 * SUBLANES, S5_N), F32),
        ],
        compiler_params=pltpu.CompilerParams(dimension_semantics=("arbitrary",), vmem_limit_bytes=VMEM_LIMIT),
        name="s5_scan",
    )(u_tm, u_tm, bmat, cmat, lam)


def _s5_mixer(u_tm, n_ctx, lam_re, lam_im, log_dt, b_re, b_im, c_re, c_im, d, glu_w):
    n_all, bw = u_tm.shape
    bsz, w = bw // S5_W, S5_W
    assert bsz == SUBLANES and n_all % S5_CHUNK == 0 and n_ctx % S5_CHUNK == 0
    bmat, cmat, lam = _s5_matrices(lam_re, lam_im, log_dt, b_re, b_im, c_re, c_im)
    u_rows = u_tm.reshape(n_all * bsz, w)
    y_f, y_b = _s5_scan(u_rows, bmat, cmat, lam, n_ctx // S5_CHUNK, S5_CHUNK)
    y = y_f + y_b + d * u_rows
    y = jax.nn.gelu(y)
    y = y * jax.nn.sigmoid(y @ glu_w)
    return y.reshape(n_all, bw)


def _axial_rope_tables(n_tokens):
    rows = n_tokens // GRID_W
    row = jnp.repeat(jnp.arange(rows), GRID_W).astype(F32)
    col = jnp.tile(jnp.arange(GRID_W), rows).astype(F32)
    half = MLA_ROPE // 2
    inv = ROPE_BASE ** (-jnp.arange(0, half, 2, dtype=F32) / half)
    ang_r = row[:, None] * inv
    ang_c = col[:, None] * inv
    cos = jnp.concatenate([jnp.cos(ang_r)] * 2 + [jnp.cos(ang_c)] * 2, axis=-1)
    sin = jnp.concatenate([jnp.sin(ang_r)] * 2 + [jnp.sin(ang_c)] * 2, axis=-1)
    return cos, sin


def _rope_partner(w):
    q = MLA_ROPE // 4
    parts = []
    for g in range(2):
        x1, x2 = w[..., 2 * g * q:(2 * g + 1) * q], w[..., (2 * g + 1) * q:(2 * g + 2) * q]
        parts += [-x2, x1]
    return jnp.concatenate(parts, axis=-1)


def _attn_kernel(q_ref, kv_ref, kr_ref, o_ref, kcat, vaug):
    hs, nk = kcat.shape[0], kcat.shape[1]
    dn = MLA_NOPE

    @pl.when(pl.program_id(2) == 0)
    def _():
        lane = lax.broadcasted_iota(jnp.int32, (nk, LANES), 1)
        ones_col = jnp.where(lane == 0, 1.0, 0.0).astype(BF16)
        for h in range(hs):
            kvh = kv_ref[0, :, h * LANES:(h + 1) * LANES]
            kcat[h] = jnp.where(lane < dn, kvh, kr_ref[0])
            vaug[h] = jnp.where(lane >= dn, kvh, ones_col)

    outs = []
    for h in range(hs):
        s = lax.dot_general(q_ref[0, :, h * LANES:(h + 1) * LANES], kcat[h], (((1,), (1,)), ((), ())),
                            preferred_element_type=F32)
        p = jnp.exp(s - jnp.max(s, axis=-1, keepdims=True))
        o = jnp.dot(p.astype(BF16), vaug[h], preferred_element_type=F32)
        outs.append(o[:, dn:] / o[:, 0:1])
    o_ref[0] = jnp.concatenate(outs, axis=-1)


def _attention(q, kv, kr, key_row0, nk):
    bsz, nq, _ = q.shape
    heads, dv = MLA_HEADS, MLA_V
    tq = min(ATTN_Q_BLOCK, nq)
    hs = ATTN_HEADS_PER_STEP
    kb = key_row0 // nk
    return pl.pallas_call(
        _attn_kernel,
        grid=(bsz, heads // hs, nq // tq),
        in_specs=[
            pl.BlockSpec((1, tq, hs * LANES), lambda b, h, i: (b, i, h)),
            pl.BlockSpec((1, nk, hs * LANES), lambda b, h, i: (b, kb, h), pipeline_mode=pl.Buffered(1)),
            pl.BlockSpec((1, nk, LANES), lambda b, h, i: (b, kb, 0), pipeline_mode=pl.Buffered(1)),
        ],
        out_specs=pl.BlockSpec((1, tq, hs * dv), lambda b, h, i: (b, i, h)),
        out_shape=jax.ShapeDtypeStruct((bsz, nq, heads * dv), F32),
        scratch_shapes=[pltpu.VMEM((hs, nk, LANES), BF16), pltpu.VMEM((hs, nk, LANES), BF16)],
        compiler_params=pltpu.CompilerParams(
            dimension_semantics=("arbitrary", "arbitrary", "arbitrary"), vmem_limit_bytes=VMEM_LIMIT),
        name="mla_attention",
    )(q, kv, kr)


PIN_U = 0
PIN_CQ = PIN_U + S5_W
PIN_CKV = PIN_CQ + MLA_Q_RANK
PIN_HZ = PIN_CKV + MLA_KV_RANK
PIN_KR = PIN_HZ + 3 * HY_W
PIN_KRP = PIN_KR + LANES
PIN_COLS = PIN_KRP + LANES


def _proj_in_weights(w_in, w_uq):
    o1 = S5_W
    o2 = o1 + MLA_Q_RANK
    o3 = o2 + MLA_KV_RANK
    o4 = o3 + MLA_ROPE
    pad = LANES - MLA_QK
    w_kr = w_in[:, o3:o4]
    in_slot = lambda w: jnp.pad(w, ((0, 0), (MLA_NOPE, pad)))
    w_ext = jnp.concatenate([w_in[:, :o3], w_in[:, o4:], in_slot(w_kr), in_slot(_rope_partner(w_kr))], axis=1)
    w_h = w_uq.reshape(MLA_Q_RANK, MLA_HEADS, MLA_QK)
    w_rot = jnp.concatenate([jnp.zeros_like(w_h[..., :MLA_NOPE]), _rope_partner(w_h[..., MLA_NOPE:])], axis=-1)
    w_q = jnp.concatenate([jnp.pad(w, ((0, 0), (0, 0), (0, pad))).reshape(MLA_Q_RANK, MLA_HEADS * LANES)
                           for w in (w_h, w_rot)], axis=-1)
    return w_ext.astype(BF16), w_q.astype(BF16)


def _rope_slot_tables(n, n_ctx):
    cos, sin = _axial_rope_tables(n)
    pad = LANES - MLA_QK
    cos = jnp.concatenate([jnp.ones((n, MLA_NOPE), F32), cos, jnp.ones((n, pad), F32)], axis=-1)
    sin = jnp.concatenate([jnp.zeros((n, MLA_NOPE), F32), sin, jnp.zeros((n, pad), F32)], axis=-1)
    return (jnp.concatenate([cos, jnp.ones((n_ctx, LANES), F32)], axis=0),
            jnp.concatenate([sin, jnp.zeros((n_ctx, LANES), F32)], axis=0))


def _apply_pending(x_ref, pending_refs):
    y0_ref, y1_ref, gate_ref = pending_refs
    return x_ref[0] + gate_ref[0] * (y0_ref[...] + y1_ref[...])


def _pending_specs(pending, tm, blocks_per_batch, d):
    if pending is None:
        return [], []
    y0, y1, gate, row0 = pending
    blk0 = row0 // tm
    rows = pl.BlockSpec((tm, d), lambda b, i: (blk0 + b * blocks_per_batch + i, 0))
    return [rows, rows, pl.BlockSpec((1, 1, d), lambda b, i: (b, 0, 0))], [y0, y1, gate]


def _proj_in_kernel(*refs, has_pending, n_shared):
    if has_pending:
        x_ref, *pending_refs = refs[:4]
        refs = refs[4:]
    else:
        x_ref, refs = refs[0], refs[1:]
    g_ref, shift_ref, scale_ref, w_ref, qg_ref, wq_ref, kvg_ref, wkv_ref, cos_ref, sin_ref = refs[:10]
    u_ref, q_ref, kv_ref, kr_ref, hz_ref, *xo_ref = refs[10 + n_shared:]
    if has_pending:
        x = _apply_pending(x_ref, pending_refs)
        xo_ref[0][0] = x
    else:
        x = x_ref[0]
    h = ((_rms_rows(x) * g_ref[...]) * (1.0 + scale_ref[0]) + shift_ref[0]).astype(BF16)
    p = jnp.dot(h, w_ref[...], preferred_element_type=F32)
    u_ref[...] = p[:, PIN_U:PIN_CQ]
    hz_ref[0] = p[:, PIN_HZ:PIN_KR]
    cos, sin = cos_ref[...], sin_ref[...]
    kr_ref[0] = (p[:, PIN_KR:PIN_KRP] * cos + p[:, PIN_KRP:PIN_COLS] * sin).astype(BF16)
    cq = (_rms_rows(p[:, PIN_CQ:PIN_CKV]) * qg_ref[...]).astype(BF16)
    q2 = jnp.dot(cq, wq_ref[...], preferred_element_type=F32)
    hw = MLA_HEADS * LANES
    for hd in range(MLA_HEADS):
        a, b = q2[:, hd * LANES:(hd + 1) * LANES], q2[:, hw + hd * LANES:hw + (hd + 1) * LANES]
        q_ref[0, :, hd * LANES:(hd + 1) * LANES] = ((a * cos + b * sin) * MLA_SCALE).astype(BF16)
    ckv = (_rms_rows(p[:, PIN_CKV:PIN_HZ]) * kvg_ref[...]).astype(BF16)
    kv_ref[0] = jnp.dot(ckv, wkv_ref[...], preferred_element_type=F32).astype(BF16)


def _proj_in(x, pending, norm_g, shift, scale, w_ext, q_norm_g, w_q, kv_norm_g, w_ukv, cos, sin, row0, shared):
    bsz, n, d = x.shape
    n_all = cos.shape[0]
    tm = min(MIX_ROWS, n)
    blk0 = row0 // tm
    hw = MLA_HEADS * LANES

    def rows(width, first=0):
        return pl.BlockSpec((1, tm, width), lambda b, i: (b, first + i, 0))

    per_batch = pl.BlockSpec((1, 1, d), lambda b, i: (b, 0, 0))
    table = pl.BlockSpec((tm, LANES), lambda b, i: (blk0 + i, 0))
    pend_specs, pend_args = _pending_specs(pending, tm, n // tm, d)
    has_pending = pending is not None
    out_specs = [pl.BlockSpec((tm, S5_W), lambda b, i: (blk0 + i, b)), rows(hw), rows(hw, blk0), rows(LANES, blk0),
                 rows(3 * HY_W)]
    out_shape = [jax.ShapeDtypeStruct((n_all, bsz * S5_W), F32), jax.ShapeDtypeStruct((bsz, n, hw), BF16),
                 jax.ShapeDtypeStruct((bsz, n_all, hw), BF16), jax.ShapeDtypeStruct((bsz, n_all, LANES), BF16),
                 jax.ShapeDtypeStruct((bsz, n, 3 * HY_W), F32)]
    if has_pending:
        out_specs.append(rows(d))
        out_shape.append(jax.ShapeDtypeStruct((bsz, n, d), F32))
    n_in = 1 + len(pend_args) + 10
    shared = list(shared or [jnp.zeros(out_shape[k].shape, out_shape[k].dtype) for k in (0, 2, 3)])
    aliases = {n_in + k: out for k, out in enumerate((0, 2, 3))}
    outs = pl.pallas_call(
        functools.partial(_proj_in_kernel, has_pending=has_pending, n_shared=len(shared)),
        grid=(bsz, n // tm),
        in_specs=[rows(d), *pend_specs, _const_spec((1, d)), per_batch, per_batch, _const_spec((d, PIN_COLS)),
                  _const_spec((1, MLA_Q_RANK)), _const_spec((MLA_Q_RANK, 2 * hw)),
                  _const_spec((1, MLA_KV_RANK)), _const_spec((MLA_KV_RANK, hw)), table, table,
                  *[pl.BlockSpec(memory_space=pl.ANY)] * len(shared)],
        out_specs=out_specs,
        out_shape=out_shape,
        input_output_aliases=aliases,
        compiler_params=pltpu.CompilerParams(dimension_semantics=("arbitrary", "arbitrary"),
                                             vmem_limit_bytes=VMEM_LIMIT),
        name="proj_in",
    )(x, *pend_args, norm_g.reshape(1, d), shift, scale, w_ext, q_norm_g.reshape(1, -1), w_q,
      kv_norm_g.reshape(1, -1), w_ukv.astype(BF16), cos, sin, *shared)
    return (outs[0], outs[2], outs[3]), outs[1], outs[4], (outs[5] if has_pending else x)


def _short_conv(z, w, b):
    n = z.shape[1]
    zp = jnp.pad(z, ((0, 0), (1, 1), (0, 0)))
    return zp[:, :n] * w[0] + zp[:, 1:n + 1] * w[1] + zp[:, 2:] * w[2] + b


def _hyena_filter_taps(n, w1, b1, w2, b2, w3, b3, freq):
    t = jnp.linspace(0.0, 1.0, n, dtype=F32)[:, None]
    bands = (HY_POS_EMB - 1) // 2
    f = jnp.linspace(1e-4, bands - 1, bands, dtype=F32)
    ang = (2.0 * math.pi * jnp.arange(n, dtype=F32) / n)[:, None] * f
    z = jnp.concatenate([t, jnp.cos(ang), -jnp.sin(ang)], axis=-1)
    h = jnp.sin(freq[0] * (z @ w1 + b1))
    h = jnp.sin(freq[1] * (h @ w2 + b2))
    half = HY_ORDER * HY_W
    deltas = jnp.tile(jnp.abs(jnp.linspace(HY_MIN_DECAY, HY_MAX_DECAY, HY_W, dtype=F32)), HY_ORDER)
    fwd = (h @ w3[:, :half] + b3[:half]) * jnp.exp(-t * deltas)
    bwd_rev = (h[::-1] @ w3[:, half:] + b3[half:]) * jnp.exp(-t[::-1] * deltas)
    return jnp.concatenate([fwd, jnp.zeros((1, half), F32), bwd_rev[:n - 1]], axis=0)


def _dense_conv_kernel(x_ref, k_ref, ff_ref, g_ref, o_ref):
    n = x_ref.shape[1]
    k = k_ref[...].astype(BF16)
    kr = jnp.dot(ff_ref[0], k, preferred_element_type=F32)
    ki = jnp.dot(ff_ref[1], k, preferred_element_type=F32)
    x = x_ref[0].astype(BF16)
    yr = jnp.dot(ff_ref[0, :, :n], x, preferred_element_type=F32)
    yi = jnp.dot(ff_ref[1, :, :n], x, preferred_element_type=F32)
    zr, zi = (yr * kr - yi * ki).astype(BF16), (yr * ki + yi * kr).astype(BF16)
    o_ref[0] = jnp.dot(g_ref[0], zr, preferred_element_type=F32) + jnp.dot(g_ref[1], zi, preferred_element_type=F32)


def _dense_long_conv(y, k_time):
    bsz, n, c = y.shape
    nfft = 2 * n
    idx = jnp.arange(nfft, dtype=jnp.int32)
    ang = (2.0 * math.pi / nfft) * ((idx[:, None] * idx[None, :]) % nfft).astype(F32)
    ff = jnp.stack([jnp.cos(ang), -jnp.sin(ang)]).astype(BF16)
    g = (jnp.stack([jnp.cos(ang), -jnp.sin(ang)])[:, :n, :] / nfft).astype(BF16)
    return pl.pallas_call(
        _dense_conv_kernel,
        grid=(bsz,),
        in_specs=[pl.BlockSpec((1, n, c), lambda b: (b, 0, 0)), _const_spec((nfft, c)),
                  _const_spec((2, nfft, nfft)), _const_spec((2, n, nfft))],
        out_specs=pl.BlockSpec((1, n, c), lambda b: (b, 0, 0)),
        out_shape=jax.ShapeDtypeStruct((bsz, n, c), F32),
        compiler_params=pltpu.CompilerParams(dimension_semantics=("arbitrary",)),
        name="hyena_dense_conv",
    )(y, k_time, ff, g)


def _hyena_mixer(z, short_w, short_b, filt, bias):
    n = z.shape[1]
    z = _short_conv(z, short_w, short_b)
    k_time = _hyena_filter_taps(n, *filt)
    if (2 * n) % HY_N2 == 0 and (2 * n) // HY_N2 >= 2 * SUBLANES:
        tables = _dft_tables(2 * n)
        kr, ki = _hyena_spectrum(k_time, tables)
        rows = n // HY_N2
        z = _to_strided(z, rows)
        y = z[..., :HY_W]
        for o in range(HY_ORDER):
            cols = slice(o * HY_W, (o + 1) * HY_W)
            gate = z[..., (o + 1) * HY_W:(o + 2) * HY_W]
            y = gate * (_hyena_long_conv(y, kr[:, cols], ki[:, cols], tables) + y * bias[o])
        return _from_strided(y, rows)
    y = z[..., :HY_W]
    for o in range(HY_ORDER):
        gate = z[..., (o + 1) * HY_W:(o + 2) * HY_W]
        y = gate * (_dense_long_conv(y, k_time[:, o * HY_W:(o + 1) * HY_W]) + y * bias[o])
    return y


HY_N2 = 128
HY_PITCH = 136
HY_UNROLL = 16


def _dft_tables(nfft):
    n1 = nfft // HY_N2
    half = n1 // 2
    k1 = jnp.arange(n1, dtype=jnp.int32)
    n2 = jnp.arange(HY_N2, dtype=jnp.int32)
    t = HY_N2 * k1[None, None, :] + n2[:, None, None]
    ang = (2.0 * math.pi / nfft) * ((k1[None, :, None] * t) % nfft).astype(F32)
    gr, gi = jnp.cos(ang), -jnp.sin(ang)
    g_cplx = jnp.concatenate([jnp.concatenate([gr[..., :half], -gi[..., :half]], -1),
                              jnp.concatenate([gi[..., :half], gr[..., :half]], -1)], axis=1)
    g_real = jnp.concatenate([gr, gi], axis=1)
    hr = jnp.cos(ang).swapaxes(1, 2)[:, :half] / nfft
    hi = jnp.sin(ang).swapaxes(1, 2)[:, :half] / nfft
    g_inv = jnp.concatenate([jnp.concatenate([hr, -hi], -1), jnp.concatenate([hi, hr], -1)], axis=1)
    a2 = (2.0 * math.pi / HY_N2) * ((n2[:, None] * n2[None, :]) % HY_N2).astype(F32)
    fr, fi = jnp.cos(a2), -jnp.sin(a2)
    f2 = jnp.concatenate([jnp.concatenate([fr, -fi], -1), jnp.concatenate([fi, fr], -1)], axis=0)
    f2_inv = jnp.concatenate([jnp.concatenate([fr, fi], -1), jnp.concatenate([-fi, fr], -1)], axis=0)
    return dict(n1=n1, g_cplx=g_cplx.astype(BF16), g_real=g_real.astype(BF16), g_inv=g_inv.astype(BF16),
                f2=f2.astype(BF16), f2_inv=f2_inv.astype(BF16))


def _dft_stage1(load_rows, g_ref, s_re, s_im, n1):
    def body(n2, carry):
        a = jnp.dot(g_ref[n2], load_rows(n2).astype(BF16), preferred_element_type=F32)
        s_re[pl.ds(n2, n1, stride=HY_PITCH), :] = a[:n1]
        s_im[pl.ds(n2, n1, stride=HY_PITCH), :] = a[n1:]
        return carry

    lax.fori_loop(0, HY_N2, body, 0, unroll=HY_UNROLL)


def _dft_stage2(f2_ref, s_re, s_im, k1):
    r0 = pl.multiple_of(k1 * HY_PITCH, SUBLANES)
    sl = jnp.concatenate([s_re[pl.ds(r0, HY_N2), :], s_im[pl.ds(r0, HY_N2), :]], axis=0).astype(BF16)
    x = jnp.dot(f2_ref[...], sl, preferred_element_type=F32)
    return r0, x[:HY_N2], x[HY_N2:]


def _hyena_spectrum_kernel(x_ref, g_ref, f2_ref, kr_ref, ki_ref, s_re, s_im, *, n1):
    _dft_stage1(lambda n2: x_ref[pl.ds(n2, n1, stride=HY_PITCH), :], g_ref, s_re, s_im, n1)

    def body(k1, carry):
        _, xr, xi = _dft_stage2(f2_ref, s_re, s_im, k1)
        q0 = pl.multiple_of(k1 * HY_N2, HY_N2)
        kr_ref[pl.ds(q0, HY_N2), :] = xr
        ki_ref[pl.ds(q0, HY_N2), :] = xi
        return carry

    lax.fori_loop(0, n1, body, 0)


def _hyena_conv_kernel(x_ref, g_ref, f2_ref, f2i_ref, gi_ref, kr_ref, ki_ref, y_ref, s_re, s_im, *, n1):
    half = n1 // 2

    def load_rows(n2):
        return jnp.concatenate([x_ref[0, pl.ds(n2, half, stride=HY_PITCH), :],
                                x_ref[1, pl.ds(n2, half, stride=HY_PITCH), :]], axis=0)

    _dft_stage1(load_rows, g_ref, s_re, s_im, n1)

    def spectrum_product(k1, carry):
        r0, xr, xi = _dft_stage2(f2_ref, s_re, s_im, k1)
        q0 = pl.multiple_of(k1 * HY_N2, HY_N2)
        kr, ki = kr_ref[pl.ds(q0, HY_N2), :], ki_ref[pl.ds(q0, HY_N2), :]
        y = jnp.concatenate([xr * kr - xi * ki, xr * ki + xi * kr], axis=0).astype(BF16)
        b = jnp.dot(f2i_ref[...], y, preferred_element_type=F32)
        s_re[pl.ds(r0, HY_N2), :] = b[:HY_N2]
        s_im[pl.ds(r0, HY_N2), :] = b[HY_N2:]
        return carry

    lax.fori_loop(0, n1, spectrum_product, 0, unroll=HY_UNROLL // 2)

    y_ref[...] = jnp.zeros_like(y_ref)

    def inverse_stage1(n2, carry):
        bs = jnp.concatenate([s_re[pl.ds(n2, n1, stride=HY_PITCH), :],
                              s_im[pl.ds(n2, n1, stride=HY_PITCH), :]], axis=0).astype(BF16)
        y = jnp.dot(gi_ref[n2], bs, preferred_element_type=F32)
        y_ref[0, pl.ds(n2, half, stride=HY_PITCH), :] = y[:half]
        y_ref[1, pl.ds(n2, half, stride=HY_PITCH), :] = y[half:]
        return carry

    lax.fori_loop(0, HY_N2, inverse_stage1, 0, unroll=HY_UNROLL)


def _to_strided(a, rows):
    lead, c = a.shape[:-2], a.shape[-1]
    a = a.reshape(lead + (rows, HY_N2, c))
    a = jnp.pad(a, [(0, 0)] * len(lead) + [(0, 0), (0, HY_PITCH - HY_N2), (0, 0)])
    return a.reshape(lead + (rows * HY_PITCH, c))


def _from_strided(a, rows):
    lead, c = a.shape[:-2], a.shape[-1]
    return a.reshape(lead + (rows, HY_PITCH, c))[..., :HY_N2, :].reshape(lead + (rows * HY_N2, c))


def _const_spec(shape):
    return pl.BlockSpec(shape, lambda *_: (0,) * len(shape), pipeline_mode=pl.Buffered(1))


def _hyena_spectrum(k_time, tables):
    nfft, c = k_time.shape
    n1 = tables["n1"]
    spec_out = pl.BlockSpec((nfft, LANES), lambda j: (0, j))
    return pl.pallas_call(
        functools.partial(_hyena_spectrum_kernel, n1=n1),
        grid=(c // LANES,),
        in_specs=[pl.BlockSpec((n1 * HY_PITCH, LANES), lambda j: (0, j)),
                  _const_spec((HY_N2, 2 * n1, n1)), _const_spec((2 * HY_N2, 2 * HY_N2))],
        out_specs=[spec_out, spec_out],
        out_shape=[jax.ShapeDtypeStruct((nfft, c), F32)] * 2,
        scratch_shapes=[pltpu.VMEM((n1 * HY_PITCH, LANES), F32)] * 2,
        compiler_params=pltpu.CompilerParams(dimension_semantics=("arbitrary",), vmem_limit_bytes=VMEM_LIMIT),
        name="hyena_spectrum",
    )(_to_strided(k_time, n1), tables["g_real"], tables["f2"])


def _hyena_long_conv(y, kr, ki, tables):
    bsz, r, c = y.shape
    n1 = tables["n1"]
    pairs = bsz // 2
    spec_x = pl.BlockSpec((None, 2, r, LANES), lambda j, p: (p, 0, 0, j))
    spec_k = pl.BlockSpec((n1 * HY_N2, LANES), lambda j, p: (0, j), pipeline_mode=pl.Buffered(1))
    out = pl.pallas_call(
        functools.partial(_hyena_conv_kernel, n1=n1),
        grid=(c // LANES, pairs),
        in_specs=[spec_x,
                  _const_spec((HY_N2, 2 * n1, n1)), _const_spec((2 * HY_N2, 2 * HY_N2)),
                  _const_spec((2 * HY_N2, 2 * HY_N2)), _const_spec((HY_N2, n1, 2 * n1)),
                  spec_k, spec_k],
        out_specs=spec_x,
        out_shape=jax.ShapeDtypeStruct((pairs, 2, r, c), F32),
        scratch_shapes=[pltpu.VMEM((n1 * HY_PITCH, LANES), F32)] * 2,
        compiler_params=pltpu.CompilerParams(dimension_semantics=("arbitrary", "arbitrary"),
                                             vmem_limit_bytes=VMEM_LIMIT),
        name="hyena_conv",
    )(y.reshape(pairs, 2, r, c), tables["g_cplx"], tables["f2"], tables["f2_inv"], tables["g_inv"], kr, ki)
    return out.reshape(bsz, r, c)


HIGH_HALF = -(1 << 16)


def _pack_bf16_pairs(f):
    half = f.shape[1] // 2
    bits = lax.bitcast_convert_type(f.astype(F32), jnp.int32)
    return lax.shift_right_logical(bits[:, :half], 16) | (bits[:, half:] & HIGH_HALF)


def _unpack_bf16_pairs(w):
    lo = lax.bitcast_convert_type(lax.shift_left(w, 16), F32)
    hi = lax.bitcast_convert_type(w & HIGH_HALF, F32)
    return jnp.concatenate([lo, hi], axis=-1).astype(BF16)


def _mix_out_kernel(s5_ref, mla_ref, hy_ref, x_ref, g_ref, w_ref, gate_ref, n2g_ref, shift_ref, scale_ref, wr_ref,
                    *rest):
    xo_ref, f_ref, lg_ref = rest[-3:]
    c1, c2 = S5_W, S5_W + MLA_W
    y = jnp.dot((_rms_rows(s5_ref[...]) * g_ref[:, :c1]).astype(BF16), w_ref[:c1, :], preferred_element_type=F32)
    y += jnp.dot((_rms_rows(mla_ref[0]) * g_ref[:, c1:c2]).astype(BF16), w_ref[c1:c2, :], preferred_element_type=F32)
    y += jnp.dot((_rms_rows(hy_ref[0]) * g_ref[:, c2:]).astype(BF16), w_ref[c2:, :], preferred_element_type=F32)
    x_new = x_ref[0] + gate_ref[0] * y
    xo_ref[0] = x_new
    f = ((_rms_rows(x_new) * n2g_ref[...]) * (1.0 + scale_ref[0]) + shift_ref[0]).astype(BF16)
    f_ref[...] = _pack_bf16_pairs(f)
    lg_ref[...] = jnp.dot(f, wr_ref[...], preferred_element_type=F32)


def _mix_out(y_s5, y_mla, y_hy, x, mix_g, w_out, gate, norm2_g, shift, scale, w_route, row0, tok0, n_tok, shared):
    bsz, n, d = x.shape
    tm = min(MIX_ROWS, n)
    s5_blk0, tok_blk0, per_batch_blocks = row0 // tm, tok0 // tm, n // tm

    def rows(width):
        return pl.BlockSpec((1, tm, width), lambda b, i: (b, i, 0))

    def tokens(width):
        return pl.BlockSpec((tm, width), lambda b, i: (tok_blk0 + b * per_batch_blocks + i, 0))

    per_batch = pl.BlockSpec((1, 1, d), lambda b, i: (b, 0, 0))
    if shared is None:
        shared = () if n_tok == bsz * n else (jnp.zeros((n_tok, d // 2), jnp.int32), jnp.zeros((n_tok, LANES), F32))
    shared = list(shared)
    outs = pl.pallas_call(
        _mix_out_kernel,
        grid=(bsz, n // tm),
        in_specs=[pl.BlockSpec((tm, S5_W), lambda b, i: (s5_blk0 + i, b)), rows(MLA_W), rows(HY_W), rows(d),
                  _const_spec((1, d)), _const_spec((d, d)),
                  per_batch, _const_spec((1, d)), per_batch, per_batch, _const_spec((d, LANES)),
                  *[pl.BlockSpec(memory_space=pl.ANY)] * len(shared)],
        out_specs=[rows(d), tokens(d // 2), tokens(LANES)],
        out_shape=[jax.ShapeDtypeStruct((bsz, n, d), F32), jax.ShapeDtypeStruct((n_tok, d // 2), jnp.int32),
                   jax.ShapeDtypeStruct((n_tok, LANES), F32)],
        input_output_aliases={11 + k: 1 + k for k in range(len(shared))},
        compiler_params=pltpu.CompilerParams(dimension_semantics=("arbitrary", "arbitrary"),
                                             vmem_limit_bytes=VMEM_LIMIT),
        name="mix_out",
    )(y_s5, y_mla, y_hy, x, mix_g.reshape(1, d), w_out.astype(BF16), gate, norm2_g.reshape(1, d), shift, scale,
      w_route.astype(BF16), *shared)
    return outs[0], (outs[1], outs[2])


def _moe_kernel(be_ref, used_ref, x_ref, w1_ref, w3_ref, w2_ref, rw_ref, o_ref, w13_s, w2_s):
    i = pl.program_id(0)
    hid = w2_s.shape[0]

    @pl.when(jnp.logical_or(i == 0, be_ref[i] != be_ref[jnp.maximum(i - 1, 0)]))
    def _():
        w13_s[:, :hid] = w1_ref[0, 0].astype(BF16)
        w13_s[:, hid:] = w3_ref[0, 0].astype(BF16)
        w2_s[...] = w2_ref[0, 0].astype(BF16)

    @pl.when(i < used_ref[0])
    def _():
        ab = jnp.dot(_unpack_bf16_pairs(x_ref[...]), w13_s[...], preferred_element_type=F32)
        a, b = ab[:, :hid], ab[:, hid:]
        h = (a * jax.nn.sigmoid(a)) * b
        y = jnp.dot(h.astype(BF16), w2_s[...], preferred_element_type=F32)
        rw = rw_ref[...]
        for c0 in range(0, y.shape[1], LANES):
            o_ref[:, c0:c0 + LANES] = y[:, c0:c0 + LANES] * rw

    @pl.when(i >= used_ref[0])
    def _():
        o_ref[...] = jnp.zeros_like(o_ref)


def _moe_experts(xg, block_e, n_used, w1, w3, w2, layer, row_w):
    n_rows = xg.shape[0]
    d, hid = w1.shape[2], w1.shape[3]
    n_blocks = n_rows // MOE_BLOCK
    return pl.pallas_call(
        _moe_kernel,
        grid_spec=pltpu.PrefetchScalarGridSpec(
            num_scalar_prefetch=2,
            grid=(n_blocks,),
            in_specs=[
                pl.BlockSpec((MOE_BLOCK, d // 2), lambda i, be, nu: (i, 0)),
                pl.BlockSpec((1, 1, d, hid), lambda i, be, nu: (layer, be[i], 0, 0)),
                pl.BlockSpec((1, 1, d, hid), lambda i, be, nu: (layer, be[i], 0, 0)),
                pl.BlockSpec((1, 1, hid, d), lambda i, be, nu: (layer, be[i], 0, 0)),
                pl.BlockSpec((MOE_BLOCK, LANES), lambda i, be, nu: (i, 0)),
            ],
            out_specs=pl.BlockSpec((MOE_BLOCK, d), lambda i, be, nu: (i, 0)),
            scratch_shapes=[pltpu.VMEM((d, 2 * hid), BF16), pltpu.VMEM((hid, d), BF16)],
        ),
        out_shape=jax.ShapeDtypeStruct((n_rows, d), F32),
        compiler_params=pltpu.CompilerParams(dimension_semantics=("arbitrary",), vmem_limit_bytes=VMEM_LIMIT),
        name="moe_experts",
    )(block_e, n_used, xg, w1, w3, w2, row_w)


def _hier_moe(h, logits, w1, w3, w2, layer):
    t = h.shape[0]
    g_prob = jax.nn.softmax(logits[:, :MOE_GROUPS], axis=-1)
    g_idx = jnp.argmax(g_prob, axis=-1).astype(jnp.int32)
    g_w = jnp.max(g_prob, axis=-1, keepdims=True)
    e_logits = logits[:, MOE_GROUPS:MOE_GROUPS + MOE_EXPERTS].reshape(t, MOE_GROUPS, MOE_PER_GROUP)
    in_group = jnp.take_along_axis(e_logits, g_idx[:, None, None], axis=1)[:, 0]
    i1 = jnp.argmax(in_group, axis=-1).astype(jnp.int32)
    rest = jnp.where(jnp.arange(MOE_PER_GROUP, dtype=jnp.int32)[None, :] == i1[:, None], -jnp.inf, in_group)
    i2 = jnp.argmax(rest, axis=-1).astype(jnp.int32)
    top_v = jnp.stack([jnp.max(in_group, axis=-1), jnp.max(rest, axis=-1)], axis=-1)
    top_i = jnp.stack([i1, i2], axis=-1)
    gate = jax.nn.softmax(top_v, axis=-1) * g_w
    eid = g_idx[:, None] * MOE_PER_GROUP + top_i
    n_assign = t * MOE_TOP_K
    flat_e = eid.reshape(n_assign)
    flat_w = gate.reshape(n_assign)
    se, order = lax.sort((flat_e, jnp.arange(n_assign, dtype=jnp.int32)), num_keys=1)
    counts = jnp.sum(flat_e[None, :] == jnp.arange(MOE_EXPERTS, dtype=jnp.int32)[:, None], axis=1, dtype=jnp.int32)
    padded = (counts + MOE_BLOCK - 1) // MOE_BLOCK * MOE_BLOCK
    start = jnp.cumsum(counts) - counts
    pend = jnp.cumsum(padded)
    pstart = pend - padded
    experts = jnp.arange(MOE_EXPERTS, dtype=jnp.int32)[:, None]
    shift = jnp.sum(jnp.where(se[None, :] == experts, (pstart - start)[:, None], 0), axis=0)
    dest = (jnp.arange(n_assign, dtype=jnp.int32) + shift).astype(jnp.int32)
    n_blocks = -(-n_assign // MOE_BLOCK) + MOE_EXPERTS
    n_rows = n_blocks * MOE_BLOCK
    block_start = jnp.arange(n_blocks, dtype=jnp.int32) * MOE_BLOCK
    block_e = jnp.minimum(jnp.sum(pend[None, :] <= block_start[:, None], axis=1), MOE_EXPERTS - 1).astype(jnp.int32)
    off = (block_start - pstart[block_e])[:, None] + jnp.arange(MOE_BLOCK, dtype=jnp.int32)[None, :]
    valid = (off < counts[block_e][:, None]).reshape(n_rows)
    row_asg = order[jnp.where(valid, (start[block_e][:, None] + off).reshape(n_rows), 0)]
    row_tok = jnp.where(valid, row_asg // MOE_TOP_K, 0)
    row_w = jnp.where(valid, flat_w[row_asg], 0.0)
    xg = h[row_tok]
    n_used = (pend[-1:] // MOE_BLOCK).astype(jnp.int32)
    ys = _moe_experts(xg, block_e, n_used, w1, w3, w2, layer, jnp.broadcast_to(row_w[:, None], (n_rows, LANES)))
    _, slot = lax.sort((order, dest), num_keys=1)
    slot = slot.reshape(t, MOE_TOP_K)
    return ys[slot[:, 0]], ys[slot[:, 1]]


def _final_norm_kernel(x_ref, y0_ref, y1_ref, gate_ref, g_ref, o_ref):
    o_ref[0] = _rms_rows(_apply_pending(x_ref, (y0_ref, y1_ref, gate_ref))) * g_ref[...]


def _final_norm(x, pending, g):
    bsz, n, d = x.shape
    tm = min(MIX_ROWS, n)
    rows = pl.BlockSpec((1, tm, d), lambda b, i: (b, i, 0))
    pend_specs, pend_args = _pending_specs(pending, tm, n // tm, d)
    return pl.pallas_call(
        _final_norm_kernel,
        grid=(bsz, n // tm),
        in_specs=[rows, *pend_specs, _const_spec((1, d))],
        out_specs=rows,
        out_shape=jax.ShapeDtypeStruct((bsz, n, d), F32),
        compiler_params=pltpu.CompilerParams(dimension_semantics=("arbitrary", "arbitrary")),
        name="final_norm",
    )(x, *pend_args, g.reshape(1, d))


def kernel(x, c, ctx, c_ctx, ada_w, ada_b, norm1_g, norm2_g, w_in,
           s5_lambda_re, s5_lambda_im, s5_log_dt, s5_b_re, s5_b_im, s5_c_re, s5_c_im, s5_d, s5_glu_w,
           mla_q_norm_g, mla_kv_norm_g, mla_w_uq, mla_w_ukv,
           hy_short_w, hy_short_b, hy_f_w1, hy_f_b1, hy_f_w2, hy_f_b2, hy_f_w3, hy_f_b3, hy_f_freq, hy_bias,
           mix_norm_g, w_out, moe_w_group, moe_w_expert, moe_w1, moe_w3, moe_w2, final_g):
    bsz, n, d = x.shape
    n_ctx = ctx.shape[1]
    xl, xc = x, ctx
    act_l = jax.nn.silu(c)
    act_c = jax.nn.silu(c_ctx)
    rope_cos, rope_sin = _rope_slot_tables(n, n_ctx)
    pend_l = pend_c = None
    for i in range(DEPTH):
        ctx_out = i < DEPTH - 1
        mod_l = jnp.split((act_l @ ada_w[i] + ada_b[i])[:, None, :], 6, axis=-1)
        mod_c = jnp.split((act_c @ ada_w[i] + ada_b[i])[None, None, :], 6, axis=-1)
        mod_c = [jnp.broadcast_to(m, (bsz, 1, d)) for m in mod_c]
        w_ext, w_q = _proj_in_weights(w_in[i], mla_w_uq[i])
        proj = functools.partial(_proj_in, w_ext=w_ext, q_norm_g=mla_q_norm_g[i], w_q=w_q,
                                 kv_norm_g=mla_kv_norm_g[i], w_ukv=mla_w_ukv[i], cos=rope_cos, sin=rope_sin)
        seq, q_l, hz_l, xl = proj(xl, pend_l, norm1_g[i], mod_l[0], mod_l[1], row0=0, shared=None)
        (u_all, kv_all, kr_all), q_c, hz_c, xc = proj(xc, pend_c, norm1_g[i], mod_c[0], mod_c[1], row0=n, shared=seq)
        s5_all = _s5_mixer(u_all, n_ctx, s5_lambda_re[i], s5_lambda_im[i], s5_log_dt[i], s5_b_re[i], s5_b_im[i],
                           s5_c_re[i], s5_c_im[i], s5_d[i], s5_glu_w[i])
        mla_l = _attention(q_l, kv_all, kr_all, 0, n + n_ctx)
        filt = (hy_f_w1[i], hy_f_b1[i], hy_f_w2[i], hy_f_b2[i], hy_f_w3[i], hy_f_b3[i], hy_f_freq[i])
        hy_l = _hyena_mixer(hz_l, hy_short_w[i], hy_short_b[i], filt, hy_bias[i])
        w_route = jnp.pad(jnp.concatenate([moe_w_group[i], moe_w_expert[i]], axis=1),
                          ((0, 0), (0, LANES - MOE_GROUPS - MOE_EXPERTS)))
        n_tok = bsz * (n + n_ctx) if ctx_out else bsz * n
        xl, moe_in = _mix_out(s5_all, mla_l, hy_l, xl, mix_norm_g[i], w_out[i], mod_l[2], norm2_g[i],
                              mod_l[3], mod_l[4], w_route, row0=0, tok0=0, n_tok=n_tok, shared=None)
        if ctx_out:
            mla_c = _attention(q_c, kv_all, kr_all, n, n_ctx)
            hy_c = _hyena_mixer(hz_c, hy_short_w[i], hy_short_b[i], filt, hy_bias[i])
            xc, moe_in = _mix_out(s5_all, mla_c, hy_c, xc, mix_norm_g[i], w_out[i], mod_c[2], norm2_g[i],
                                  mod_c[3], mod_c[4], w_route, row0=n, tok0=bsz * n, n_tok=n_tok, shared=moe_in)
        y0, y1 = _hier_moe(*moe_in, moe_w1, moe_w3, moe_w2, i)
        pend_l, pend_c = (y0, y1, mod_l[5], 0), (y0, y1, mod_c[5], bsz * n)
    return _final_norm(xl, pend_l, final_g)
```
